```python
import jax, jax.numpy as jnp
from jax import lax
import numpy as np

D_MODEL = 2048
BATCH = 8
SEQ = 2048
DEPTH = 2

CHUNK = 64
N_MEM = 256
SB_HEAD_DIM = 128
SB_WIDTH = D_MODEL // 2
SB_HEADS = SB_WIDTH // SB_HEAD_DIM
SB_QBLOCK = 128
GM_GROUP_DIM = 128
GM_WIDTH = D_MODEL // 2
GM_GROUPS = GM_WIDTH // GM_GROUP_DIM
GM_BLOCK = 128
XA_HEADS = 4
XA_WIDTH = D_MODEL // 2
XA_HEAD_DIM = XA_WIDTH // XA_HEADS
N_BRANCH = 3
IN_WIDTH = 3 * SB_WIDTH + 2 * GM_WIDTH + XA_WIDTH
D_FF = 5632
CONV_W = 3
EPS = 1e-6

kernel_name = "hybrid_stickbreak_gmlp_memxattn_convffn"


def rmsnorm(x, g):
    xf = x.astype(jnp.float32)
    xf = xf * lax.rsqrt(jnp.mean(xf * xf, axis=-1, keepdims=True) + EPS)
    return xf.astype(x.dtype) * g


def stick_breaking_attention(q, k, v):
    B, S, H, Dh = q.shape
    scale = Dh ** -0.5
    outs = []
    for i in range(S // SB_QBLOCK):
        q0 = i * SB_QBLOCK
        q1 = q0 + SB_QBLOCK
        qb = q[:, q0:q1]
        kb = k[:, :q1]
        vb = v[:, :q1]
        z = jnp.einsum('bthd,bshd->bhts', qb, kb).astype(jnp.float32) * scale
        t_pos = q0 + jnp.arange(SB_QBLOCK)[:, None]
        s_pos = jnp.arange(q1)[None, :]
        strict = s_pos < t_pos
        log_keep = jnp.where(strict, jax.nn.log_sigmoid(-z), 0.0)
        suffix = lax.cumsum(log_keep, axis=3, reverse=True) - log_keep
        log_a = jax.nn.log_sigmoid(z) + suffix
        a = jnp.where(strict, jnp.exp(log_a), 0.0)
        outs.append(jnp.einsum('bhts,bshd->bthd', a.astype(v.dtype), vb))
    return jnp.concatenate(outs, axis=1)


def spatial_gating(u, v, g_vnorm, w_s, b_s):
    B, S, _ = u.shape
    u = jax.nn.gelu(u)
    v = rmsnorm(jax.nn.gelu(v), g_vnorm)
    pos = jnp.arange(GM_BLOCK)
    mask = (pos[None, :] // CHUNK) <= (pos[:, None] // CHUNK)
    w = jnp.where(mask[None], w_s, jnp.zeros_like(w_s))
    vb = v.reshape(B, S // GM_BLOCK, GM_BLOCK, GM_GROUPS, GM_GROUP_DIM)
    mixed = jnp.einsum('gts,bcsgd->bctgd', w, vb) + b_s.T[None, None, :, :, None]
    return u * mixed.reshape(B, S, GM_WIDTH)


def memory_cross_attention(q, mem_kv):
    B, M, _ = mem_kv.shape
    k, v = jnp.split(mem_kv, 2, axis=-1)
    k = k.reshape(B, M, XA_HEADS, XA_HEAD_DIM)
    v = v.reshape(B, M, XA_HEADS, XA_HEAD_DIM)
    z = jnp.einsum('bthd,bmhd->bhtm', q, k).astype(jnp.float32) * (XA_HEAD_DIM ** -0.5)
    p = jax.nn.softmax(z, axis=-1)
    return jnp.einsum('bhtm,bmhd->bthd', p.astype(v.dtype), v)


def conv_ffn(h, w_up, conv_w, conv_b, w_down):
    S = h.shape[1]
    up = h @ w_up
    gate, val = jnp.split(up, 2, axis=-1)
    gp = jnp.pad(gate, ((0, 0), (CONV_W - 1, 0), (0, 0)))
    conv = conv_b + sum(conv_w[i] * gp[:, i:i + S] for i in range(CONV_W))
    return (jax.nn.gelu(conv) * val) @ w_down


def _fwd_setup_inputs(seed: int = 0) -> dict:
    key = jax.random.key(seed)
    ks = jax.random.split(key, 24)
    f32 = jnp.float32

    def nrm(k, shape, fan_in):
        return jax.random.normal(k, shape, f32) * (fan_in ** -0.5)

    def gain(k, n):
        return 1.0 + 0.05 * jax.random.normal(k, (DEPTH, n), f32)

    L = DEPTH
    return {
        "x": jax.random.normal(ks[0], (BATCH, SEQ, D_MODEL), f32),
        "mem": jax.random.normal(ks[1], (BATCH, N_MEM, D_MODEL), f32),
        "g_mix_pre": gain(ks[2], D_MODEL),
        "w_in": nrm(ks[3], (L, D_MODEL, IN_WIDTH), D_MODEL),
        "g_vnorm": gain(ks[4], GM_WIDTH),
        "w_s": nrm(ks[5], (L, GM_GROUPS, GM_BLOCK, GM_BLOCK), GM_BLOCK),
        "b_s": 1.0 + 0.01 * jax.random.normal(ks[6], (L, GM_GROUPS, GM_BLOCK), f32),
        "g_mem": gain(ks[7], D_MODEL),
        "w_mem_kv": nrm(ks[8], (L, D_MODEL, 2 * XA_WIDTH), D_MODEL),
        "w_gate": nrm(ks[9], (L, D_MODEL, N_BRANCH * D_MODEL), D_MODEL),
        "b_gate": 0.01 * jax.random.normal(ks[10], (L, N_BRANCH * D_MODEL), f32),
        "w_br_sb": nrm(ks[11], (L, SB_WIDTH, D_MODEL), SB_WIDTH),
        "w_br_gm": nrm(ks[12], (L, GM_WIDTH, D_MODEL), GM_WIDTH),
        "w_br_xa": nrm(ks[13], (L, XA_WIDTH, D_MODEL), XA_WIDTH),
        "w_out": nrm(ks[14], (L, D_MODEL, D_MODEL), D_MODEL),
        "g_mix_post": gain(ks[15], D_MODEL),
        "g_ffn_pre": gain(ks[16], D_MODEL),
        "w_up": nrm(ks[17], (L, D_MODEL, 2 * D_FF), D_MODEL),
        "conv_w": nrm(ks[18], (L, CONV_W, D_FF), CONV_W),
        "conv_b": 0.01 * jax.random.normal(ks[19], (L, D_FF), f32),
        "w_down": nrm(ks[20], (L, D_FF, D_MODEL), D_FF),
        "g_ffn_post": gain(ks[21], D_MODEL),
    }


def _fwd_reference(x, mem, g_mix_pre, w_in, g_vnorm, w_s, b_s, g_mem, w_mem_kv, w_gate, b_gate,
              w_br_sb, w_br_gm, w_br_xa, w_out, g_mix_post, g_ffn_pre, w_up, conv_w, conv_b,
              w_down, g_ffn_post):
    B, S, D = x.shape
    splits = [SB_WIDTH, 2 * SB_WIDTH, 3 * SB_WIDTH,
              3 * SB_WIDTH + GM_WIDTH, 3 * SB_WIDTH + 2 * GM_WIDTH]
    for l in range(DEPTH):
        h = rmsnorm(x, g_mix_pre[l])
        proj = h @ w_in[l]
        q_sb, k_sb, v_sb, u_gm, v_gm, q_xa = jnp.split(proj, splits, axis=-1)
        hs = (B, S, SB_HEADS, SB_HEAD_DIM)
        o_sb = stick_breaking_attention(q_sb.reshape(hs), k_sb.reshape(hs),
                                        v_sb.reshape(hs)).reshape(B, S, SB_WIDTH)
        o_gm = spatial_gating(u_gm, v_gm, g_vnorm[l], w_s[l], b_s[l])
        mem_kv = rmsnorm(mem, g_mem[l]) @ w_mem_kv[l]
        o_xa = memory_cross_attention(q_xa.reshape(B, S, XA_HEADS, XA_HEAD_DIM),
                                      mem_kv).reshape(B, S, XA_WIDTH)
        gates = jax.nn.sigmoid(h @ w_gate[l] + b_gate[l]).reshape(B, S, N_BRANCH, D)
        merged = (gates[:, :, 0] * (o_sb @ w_br_sb[l])
                  + gates[:, :, 1] * (o_gm @ w_br_gm[l])
                  + gates[:, :, 2] * (o_xa @ w_br_xa[l]))
        x = x + rmsnorm(merged @ w_out[l], g_mix_post[l])
        h = rmsnorm(x, g_ffn_pre[l])
        x = x + rmsnorm(conv_ffn(h, w_up[l], conv_w[l], conv_b[l], w_down[l]), g_ffn_post[l])
    return x


import jax as _jax
import jax.numpy as _jnp

TWIN_FORMAT = 'train_step'
FWD_PARAMS = ['x', 'mem', 'g_mix_pre', 'w_in', 'g_vnorm', 'w_s', 'b_s', 'g_mem', 'w_mem_kv', 'w_gate', 'b_gate', 'w_br_sb', 'w_br_gm', 'w_br_xa', 'w_out', 'g_mix_post', 'g_ffn_pre', 'w_up', 'conv_w', 'conv_b', 'w_down', 'g_ffn_post']
TWIN_WEIGHTS = ['g_mix_pre', 'w_in', 'g_vnorm', 'w_s', 'b_s', 'g_mem', 'w_mem_kv', 'w_gate', 'b_gate', 'w_br_sb', 'w_br_gm', 'w_br_xa', 'w_out', 'g_mix_post', 'g_ffn_pre', 'w_up', 'conv_w', 'conv_b', 'w_down', 'g_ffn_post']
TWIN_DIFF_INPUT = 'x'
TWIN_INPUTS = ['x', 'mem', 'g_mix_pre', 'w_in', 'g_vnorm', 'w_s', 'b_s', 'g_mem', 'w_mem_kv', 'w_gate', 'b_gate', 'w_br_sb', 'w_br_gm', 'w_br_xa', 'w_out', 'g_mix_post', 'g_ffn_pre', 'w_up', 'conv_w', 'conv_b', 'w_down', 'g_ffn_post', 'loss_target', 'm_g_mix_pre', 'm_w_in', 'm_g_vnorm', 'm_w_s', 'm_b_s', 'm_g_mem', 'm_w_mem_kv', 'm_w_gate', 'm_b_gate', 'm_w_br_sb', 'm_w_br_gm', 'm_w_br_xa', 'm_w_out', 'm_g_mix_post', 'm_g_ffn_pre', 'm_w_up', 'm_conv_w', 'm_conv_b', 'm_w_down', 'm_g_ffn_post', 'v_g_mix_pre', 'v_w_in', 'v_g_vnorm', 'v_w_s', 'v_b_s', 'v_g_mem', 'v_w_mem_kv', 'v_w_gate', 'v_b_gate', 'v_w_br_sb', 'v_w_br_gm', 'v_w_br_xa', 'v_w_out', 'v_g_mix_post', 'v_g_ffn_pre', 'v_w_up', 'v_conv_w', 'v_conv_b', 'v_w_down', 'v_g_ffn_post']
TWIN_OUTPUTS = ['loss', 'grad_x', 'grad_g_mix_pre', 'grad_w_in', 'grad_g_vnorm', 'grad_w_s', 'grad_b_s', 'grad_g_mem', 'grad_w_mem_kv', 'grad_w_gate', 'grad_b_gate', 'grad_w_br_sb', 'grad_w_br_gm', 'grad_w_br_xa', 'grad_w_out', 'grad_g_mix_post', 'grad_g_ffn_pre', 'grad_w_up', 'grad_conv_w', 'grad_conv_b', 'grad_w_down', 'grad_g_ffn_post', 'delta_g_mix_pre', 'delta_w_in', 'delta_g_vnorm', 'delta_w_s', 'delta_b_s', 'delta_g_mem', 'delta_w_mem_kv', 'delta_w_gate', 'delta_b_gate', 'delta_w_br_sb', 'delta_w_br_gm', 'delta_w_br_xa', 'delta_w_out', 'delta_g_mix_post', 'delta_g_ffn_pre', 'delta_w_up', 'delta_conv_w', 'delta_conv_b', 'delta_w_down', 'delta_g_ffn_post', 'new_m_g_mix_pre', 'new_m_w_in', 'new_m_g_vnorm', 'new_m_w_s', 'new_m_b_s', 'new_m_g_mem', 'new_m_w_mem_kv', 'new_m_w_gate', 'new_m_b_gate', 'new_m_w_br_sb', 'new_m_w_br_gm', 'new_m_w_br_xa', 'new_m_w_out', 'new_m_g_mix_post', 'new_m_g_ffn_pre', 'new_m_w_up', 'new_m_conv_w', 'new_m_conv_b', 'new_m_w_down', 'new_m_g_ffn_post', 'new_v_g_mix_pre', 'new_v_w_in', 'new_v_g_vnorm', 'new_v_w_s', 'new_v_b_s', 'new_v_g_mem', 'new_v_w_mem_kv', 'new_v_w_gate', 'new_v_b_gate', 'new_v_w_br_sb', 'new_v_w_br_gm', 'new_v_w_br_xa', 'new_v_w_out', 'new_v_g_mix_post', 'new_v_g_ffn_pre', 'new_v_w_up', 'new_v_conv_w', 'new_v_conv_b', 'new_v_w_down', 'new_v_g_ffn_post']
TWIN_LEAF_KINDS = {'loss': 'loss', 'grad_x': 'grad_x', 'grad_g_mix_pre': 'grad_w', 'grad_w_in': 'grad_w', 'grad_g_vnorm': 'grad_w', 'grad_w_s': 'grad_w', 'grad_b_s': 'grad_w', 'grad_g_mem': 'grad_w', 'grad_w_mem_kv': 'grad_w', 'grad_w_gate': 'grad_w', 'grad_b_gate': 'grad_w', 'grad_w_br_sb': 'grad_w', 'grad_w_br_gm': 'grad_w', 'grad_w_br_xa': 'grad_w', 'grad_w_out': 'grad_w', 'grad_g_mix_post': 'grad_w', 'grad_g_ffn_pre': 'grad_w', 'grad_w_up': 'grad_w', 'grad_conv_w': 'grad_w', 'grad_conv_b': 'grad_w', 'grad_w_down': 'grad_w', 'grad_g_ffn_post': 'grad_w', 'delta_g_mix_pre': 'delta_w', 'delta_w_in': 'delta_w', 'delta_g_vnorm': 'delta_w', 'delta_w_s': 'delta_w', 'delta_b_s': 'delta_w', 'delta_g_mem': 'delta_w', 'delta_w_mem_kv': 'delta_w', 'delta_w_gate': 'delta_w', 'delta_b_gate': 'delta_w', 'delta_w_br_sb': 'delta_w', 'delta_w_br_gm': 'delta_w', 'delta_w_br_xa': 'delta_w', 'delta_w_out': 'delta_w', 'delta_g_mix_post': 'delta_w', 'delta_g_ffn_pre': 'delta_w', 'delta_w_up': 'delta_w', 'delta_conv_w': 'delta_w', 'delta_conv_b': 'delta_w', 'delta_w_down': 'delta_w', 'delta_g_ffn_post': 'delta_w', 'new_m_g_mix_pre': 'new_m', 'new_m_w_in': 'new_m', 'new_m_g_vnorm': 'new_m', 'new_m_w_s': 'new_m', 'new_m_b_s': 'new_m', 'new_m_g_mem': 'new_m', 'new_m_w_mem_kv': 'new_m', 'new_m_w_gate': 'new_m', 'new_m_b_gate': 'new_m', 'new_m_w_br_sb': 'new_m', 'new_m_w_br_gm': 'new_m', 'new_m_w_br_xa': 'new_m', 'new_m_w_out': 'new_m', 'new_m_g_mix_post': 'new_m', 'new_m_g_ffn_pre': 'new_m', 'new_m_w_up': 'new_m', 'new_m_conv_w': 'new_m', 'new_m_conv_b': 'new_m', 'new_m_w_down': 'new_m', 'new_m_g_ffn_post': 'new_m', 'new_v_g_mix_pre': 'new_v', 'new_v_w_in': 'new_v', 'new_v_g_vnorm': 'new_v', 'new_v_w_s': 'new_v', 'new_v_b_s': 'new_v', 'new_v_g_mem': 'new_v', 'new_v_w_mem_kv': 'new_v', 'new_v_w_gate': 'new_v', 'new_v_b_gate': 'new_v', 'new_v_w_br_sb': 'new_v', 'new_v_w_br_gm': 'new_v', 'new_v_w_br_xa': 'new_v', 'new_v_w_out': 'new_v', 'new_v_g_mix_post': 'new_v', 'new_v_g_ffn_pre': 'new_v', 'new_v_w_up': 'new_v', 'new_v_conv_w': 'new_v', 'new_v_conv_b': 'new_v', 'new_v_w_down': 'new_v', 'new_v_g_ffn_post': 'new_v'}


def _forward(args):
    return _fwd_reference(*[args[k] for k in FWD_PARAMS])


def _output_shape():
    out = _jax.eval_shape(lambda: _forward(_fwd_setup_inputs(0)))
    return out.shape, out.dtype

N_MICROBATCH = 1
ADAM_LR = 0.001
ADAM_B1 = 0.9
ADAM_B2 = 0.999
ADAM_EPS = 1e-08
ADAM_WD = 0.01
ADAM_STEP = 10
PER_EXAMPLE_BATCH_AXIS = {'x': 0, 'mem': 0, 'loss_target': 0}
SHARED_INPUTS = []
_WEIGHT_DTYPES = {'g_mix_pre': _jnp.float32, 'w_in': _jnp.float32, 'g_vnorm': _jnp.float32, 'w_s': _jnp.float32, 'b_s': _jnp.float32, 'g_mem': _jnp.float32, 'w_mem_kv': _jnp.float32, 'w_gate': _jnp.float32, 'b_gate': _jnp.float32, 'w_br_sb': _jnp.float32, 'w_br_gm': _jnp.float32, 'w_br_xa': _jnp.float32, 'w_out': _jnp.float32, 'g_mix_post': _jnp.float32, 'g_ffn_pre': _jnp.float32, 'w_up': _jnp.float32, 'conv_w': _jnp.float32, 'conv_b': _jnp.float32, 'w_down': _jnp.float32, 'g_ffn_post': _jnp.float32}
MOMENT_SCALE = {'g_mix_pre': 3.838070e-01, 'w_in': 2.087300e-01, 'g_vnorm': 1.853351e-01, 'w_s': 1.810228e-01, 'b_s': 2.079281e-01, 'g_mem': 6.964759e-02, 'w_mem_kv': 7.168249e-02, 'w_gate': 5.926497e-02, 'b_gate': 1.532017e-01, 'w_br_sb': 2.266907e-01, 'w_br_gm': 6.290532e-01, 'w_br_xa': 7.264118e-02, 'w_out': 6.660589e-01, 'g_mix_post': 8.061902e+00, 'g_ffn_pre': 3.798312e-01, 'w_up': 1.594479e-01, 'conv_w': 1.675409e-01, 'conv_b': 2.995090e-01, 'w_down': 3.237891e-01, 'g_ffn_post': 8.014222e+00}


def _to_microbatches(a, axis):
    t = _jnp.moveaxis(a, axis, 0)
    t = t.reshape((N_MICROBATCH, t.shape[0] // N_MICROBATCH) + t.shape[1:])
    return _jnp.moveaxis(t, 1, axis + 1)


def setup_inputs(seed: int = 0) -> dict:
    inp = _fwd_setup_inputs(seed)
    key = _jax.random.fold_in(_jax.random.key(seed), 7919)
    shape, _ = _output_shape()
    out = dict(inp)
    out["loss_target"] = _jax.random.normal(_jax.random.fold_in(key, 0), shape, _jnp.float32)
    for i, name in enumerate(TWIN_WEIGHTS):
        w = inp[name].astype(_jnp.float32)
        if MOMENT_SCALE is None:
            s = _jnp.sqrt(_jnp.mean(_jnp.square(w)) + 1e-30)
        else:
            s = MOMENT_SCALE[name]
        km, kv = _jax.random.split(_jax.random.fold_in(key, i + 1))
        out[name] = w
        out["m_" + name] = s * _jax.random.normal(km, w.shape, _jnp.float32)
        out["v_" + name] = (s * s) * _jax.random.uniform(kv, w.shape, _jnp.float32, 0.5, 1.5)
    if N_MICROBATCH > 1:
        for name, axis in PER_EXAMPLE_BATCH_AXIS.items():
            out[name] = _to_microbatches(out[name], axis)
    return {'x': out['x'], 'mem': out['mem'], 'g_mix_pre': out['g_mix_pre'], 'w_in': out['w_in'], 'g_vnorm': out['g_vnorm'], 'w_s': out['w_s'], 'b_s': out['b_s'], 'g_mem': out['g_mem'], 'w_mem_kv': out['w_mem_kv'], 'w_gate': out['w_gate'], 'b_gate': out['b_gate'], 'w_br_sb': out['w_br_sb'], 'w_br_gm': out['w_br_gm'], 'w_br_xa': out['w_br_xa'], 'w_out': out['w_out'], 'g_mix_post': out['g_mix_post'], 'g_ffn_pre': out['g_ffn_pre'], 'w_up': out['w_up'], 'conv_w': out['conv_w'], 'conv_b': out['conv_b'], 'w_down': out['w_down'], 'g_ffn_post': out['g_ffn_post'], 'loss_target': out['loss_target'], 'm_g_mix_pre': out['m_g_mix_pre'], 'm_w_in': out['m_w_in'], 'm_g_vnorm': out['m_g_vnorm'], 'm_w_s': out['m_w_s'], 'm_b_s': out['m_b_s'], 'm_g_mem': out['m_g_mem'], 'm_w_mem_kv': out['m_w_mem_kv'], 'm_w_gate': out['m_w_gate'], 'm_b_gate': out['m_b_gate'], 'm_w_br_sb': out['m_w_br_sb'], 'm_w_br_gm': out['m_w_br_gm'], 'm_w_br_xa': out['m_w_br_xa'], 'm_w_out': out['m_w_out'], 'm_g_mix_post': out['m_g_mix_post'], 'm_g_ffn_pre': out['m_g_ffn_pre'], 'm_w_up': out['m_w_up'], 'm_conv_w': out['m_conv_w'], 'm_conv_b': out['m_conv_b'], 'm_w_down': out['m_w_down'], 'm_g_ffn_post': out['m_g_ffn_post'], 'v_g_mix_pre': out['v_g_mix_pre'], 'v_w_in': out['v_w_in'], 'v_g_vnorm': out['v_g_vnorm'], 'v_w_s': out['v_w_s'], 'v_b_s': out['v_b_s'], 'v_g_mem': out['v_g_mem'], 'v_w_mem_kv': out['v_w_mem_kv'], 'v_w_gate': out['v_w_gate'], 'v_b_gate': out['v_b_gate'], 'v_w_br_sb': out['v_w_br_sb'], 'v_w_br_gm': out['v_w_br_gm'], 'v_w_br_xa': out['v_w_br_xa'], 'v_w_out': out['v_w_out'], 'v_g_mix_post': out['v_g_mix_post'], 'v_g_ffn_pre': out['v_g_ffn_pre'], 'v_w_up': out['v_w_up'], 'v_conv_w': out['v_conv_w'], 'v_conv_b': out['v_conv_b'], 'v_w_down': out['v_w_down'], 'v_g_ffn_post': out['v_g_ffn_post']}


def _loss(weights, diff, rest, loss_target):
    with _jax.named_scope("forward"):
        args = {**rest, TWIN_DIFF_INPUT: diff, **{k: w.astype(_WEIGHT_DTYPES[k]) for k, w in weights.items()}}
        y = _forward(args)
    with _jax.named_scope("loss_head"):
        err = _jnp.square(y.astype(_jnp.float32) - loss_target)
        return 0.5 * _jnp.sum(_jnp.mean(err, axis=-1)) if err.ndim else 0.5 * err


def _adamw(w, g, m, v):
    m = ADAM_B1 * m + (1.0 - ADAM_B1) * g
    v = ADAM_B2 * v + (1.0 - ADAM_B2) * _jnp.square(g)
    m_hat = m / (1.0 - ADAM_B1 ** ADAM_STEP)
    v_hat = v / (1.0 - ADAM_B2 ** ADAM_STEP)
    delta = -ADAM_LR * (m_hat / (_jnp.sqrt(v_hat) + ADAM_EPS) + ADAM_WD * w)
    return delta, m, v


def reference(x, mem, g_mix_pre, w_in, g_vnorm, w_s, b_s, g_mem, w_mem_kv, w_gate, b_gate, w_br_sb, w_br_gm, w_br_xa, w_out, g_mix_post, g_ffn_pre, w_up, conv_w, conv_b, w_down, g_ffn_post, loss_target, m_g_mix_pre, m_w_in, m_g_vnorm, m_w_s, m_b_s, m_g_mem, m_w_mem_kv, m_w_gate, m_b_gate, m_w_br_sb, m_w_br_gm, m_w_br_xa, m_w_out, m_g_mix_post, m_g_ffn_pre, m_w_up, m_conv_w, m_conv_b, m_w_down, m_g_ffn_post, v_g_mix_pre, v_w_in, v_g_vnorm, v_w_s, v_b_s, v_g_mem, v_w_mem_kv, v_w_gate, v_b_gate, v_w_br_sb, v_w_br_gm, v_w_br_xa, v_w_out, v_g_mix_post, v_g_ffn_pre, v_w_up, v_conv_w, v_conv_b, v_w_down, v_g_ffn_post):
    given = dict(x=x, mem=mem, g_mix_pre=g_mix_pre, w_in=w_in, g_vnorm=g_vnorm, w_s=w_s, b_s=b_s, g_mem=g_mem, w_mem_kv=w_mem_kv, w_gate=w_gate, b_gate=b_gate, w_br_sb=w_br_sb, w_br_gm=w_br_gm, w_br_xa=w_br_xa, w_out=w_out, g_mix_post=g_mix_post, g_ffn_pre=g_ffn_pre, w_up=w_up, conv_w=conv_w, conv_b=conv_b, w_down=w_down, g_ffn_post=g_ffn_post, loss_target=loss_target, m_g_mix_pre=m_g_mix_pre, m_w_in=m_w_in, m_g_vnorm=m_g_vnorm, m_w_s=m_w_s, m_b_s=m_b_s, m_g_mem=m_g_mem, m_w_mem_kv=m_w_mem_kv, m_w_gate=m_w_gate, m_b_gate=m_b_gate, m_w_br_sb=m_w_br_sb, m_w_br_gm=m_w_br_gm, m_w_br_xa=m_w_br_xa, m_w_out=m_w_out, m_g_mix_post=m_g_mix_post, m_g_ffn_pre=m_g_ffn_pre, m_w_up=m_w_up, m_conv_w=m_conv_w, m_conv_b=m_conv_b, m_w_down=m_w_down, m_g_ffn_post=m_g_ffn_post, v_g_mix_pre=v_g_mix_pre, v_w_in=v_w_in, v_g_vnorm=v_g_vnorm, v_w_s=v_w_s, v_b_s=v_b_s, v_g_mem=v_g_mem, v_w_mem_kv=v_w_mem_kv, v_w_gate=v_w_gate, v_b_gate=v_b_gate, v_w_br_sb=v_w_br_sb, v_w_br_gm=v_w_br_gm, v_w_br_xa=v_w_br_xa, v_w_out=v_w_out, v_g_mix_post=v_g_mix_post, v_g_ffn_pre=v_g_ffn_pre, v_w_up=v_w_up, v_conv_w=v_conv_w, v_conv_b=v_conv_b, v_w_down=v_w_down, v_g_ffn_post=v_g_ffn_post)
    weights = {n: given[n] for n in TWIN_WEIGHTS}
    shared = {n: given[n] for n in SHARED_INPUTS}
    per_example = {n: given[n] for n in ['x', 'mem']}
    grad_fn = _jax.value_and_grad(_loss, argnums=(0, 1))

    def one_microbatch(ex, loss_target):
        ex = dict(ex)
        diff = ex.pop(TWIN_DIFF_INPUT)
        return grad_fn(weights, diff, {**shared, **ex}, loss_target)

    if N_MICROBATCH == 1:
        loss, (grad_w, grad_x) = one_microbatch(per_example, given["loss_target"])
    else:
        def body(carry, xs):
            loss_sum, grad_sum = carry
            l_k, (gw_k, gx_k) = one_microbatch(xs[0], xs[1])
            with _jax.named_scope("update"):
                return (loss_sum + l_k, _jax.tree.map(_jnp.add, grad_sum, gw_k)), gx_k

        init = (_jnp.zeros((), _jnp.float32), _jax.tree.map(_jnp.zeros_like, weights))
        (loss, grad_w), grad_x = _jax.lax.scan(body, init, (per_example, given["loss_target"]))
    with _jax.named_scope("update"):
        delta_w, new_m, new_v = {}, {}, {}
        for n in TWIN_WEIGHTS:
            delta_w[n], new_m[n], new_v[n] = _adamw(weights[n], grad_w[n], given["m_" + n], given["v_" + n])
    return (loss, grad_x, *[grad_w[n] for n in TWIN_WEIGHTS], *[delta_w[n] for n in TWIN_WEIGHTS],
            *[new_m[n] for n in TWIN_WEIGHTS], *[new_v[n] for n in TWIN_WEIGHTS])
```

```python
import functools

import jax
import jax.numpy as jnp
from jax import lax
from jax.experimental import pallas as pl
from jax.experimental.pallas import tpu as pltpu

F32 = jnp.float32
BF16 = jnp.bfloat16
N_DEV = 8
N_CHIP = 4
EPS = 1e-6
CHUNK = 64
BLK = 128
XA_HEADS = 4
ADAM_LR = 0.001
ADAM_B1 = 0.9
ADAM_B2 = 0.999
ADAM_EPS = 1e-08
ADAM_WD = 0.01
ADAM_STEP = 10
VMEM_LIMIT_V7X = 56 * 1024 * 1024
MESH = pl.DeviceIdType.MESH
ANY = pl.BlockSpec(memory_space=pl.ANY)


def _params(*sem):
    return pltpu.CompilerParams(dimension_semantics=sem, vmem_limit_bytes=VMEM_LIMIT_V7X)


def _tile(n, pref, mult=8):
    best = None
    for t in range(mult, min(n, pref) + 1, mult):
        if n % t == 0:
            best = t
    return best if best is not None else n


def _bf(x):
    return x if x.dtype == BF16 else x.astype(BF16)


def _dot(a, b, ca, cb):
    return lax.dot_general(_bf(a), _bf(b), (((ca,), (cb,)), ((), ())), preferred_element_type=F32)


def _rmsnorm(x, g):
    return (x * lax.rsqrt(jnp.mean(x * x, axis=-1, keepdims=True) + EPS)) * g


def _position():
    x, y, c = lax.axis_index("x"), lax.axis_index("y"), lax.axis_index("c")
    return x, y, c


def _all_gather(name, shards):
    n = len(shards)

    def body(*refs):
        ins, outs = refs[:n], refs[n:2 * n]
        send_sems, recv_sems, local_sems = refs[2 * n:]
        x, y, c = _position()
        me = 4 * x + 2 * y + c
        sibling = (x, y, 1 - c)
        chips = [(1 - x, y), (x, 1 - y), (1 - x, 1 - y)]

        def copy(a, k, block, to, src=None):
            dst = outs[a].at[block]
            return pltpu.make_async_remote_copy(
                src_ref=dst if src is None else src, dst_ref=dst, send_sem=send_sems.at[a, k],
                recv_sem=recv_sems.at[a, k], device_id=to, device_id_type=MESH)

        local = [pltpu.make_async_copy(ins[a], outs[a].at[me], local_sems.at[a]) for a in range(n)]
        for cp in local:
            cp.start()
        first = []
        for a in range(n):
            first.append(copy(a, 0, me, sibling, src=ins[a]))
            for j, chip in enumerate(chips):
                first.append(copy(a, 1 + j, me, (*chip, c), src=ins[a]))
        for cp in first:
            cp.start()
        passed = []
        for a in range(n):
            for j, (px, py) in enumerate(chips):
                block = 4 * px + 2 * py + c
                copy(a, 1 + j, block, sibling).wait_recv()
                forward = copy(a, 4 + j, block, sibling)
                forward.start()
                passed.append(forward)
        for a in range(n):
            copy(a, 0, 4 * x + 2 * y + (1 - c), sibling).wait_recv()
            for j, (px, py) in enumerate(chips):
                copy(a, 4 + j, 4 * px + 2 * py + (1 - c), sibling).wait_recv()
        for cp in first + passed:
            cp.wait_send()
        for cp in local:
            cp.wait()

    return pl.pallas_call(
        body, name=name,
        out_shape=[jax.ShapeDtypeStruct((N_DEV,) + s.shape, s.dtype) for s in shards],
        in_specs=[ANY] * n, out_specs=[ANY] * n,
        scratch_shapes=[pltpu.SemaphoreType.DMA((n, 7)), pltpu.SemaphoreType.DMA((n, 7)),
                        pltpu.SemaphoreType.DMA((n,))],
    )(*shards)


def _pair_exchange(name, partials):
    n = len(partials)

    def body(*refs):
        ins, outs = refs[:n], refs[n:2 * n]
        send_sems, recv_sems = refs[2 * n:]
        x, y, c = _position()
        copies = []
        for a in range(n):
            for k in range(N_CHIP):
                copies.append(pltpu.make_async_remote_copy(
                    src_ref=ins[a].at[k, 1 - c], dst_ref=outs[a].at[k], send_sem=send_sems.at[a, k],
                    recv_sem=recv_sems.at[a, k], device_id=(x, y, 1 - c), device_id_type=MESH))
        for cp in copies:
            cp.start()
        for cp in copies:
            cp.wait()

    return pl.pallas_call(
        body, name=name,
        out_shape=[jax.ShapeDtypeStruct((N_CHIP,) + p.shape[2:], p.dtype) for p in partials],
        in_specs=[ANY] * n, out_specs=[ANY] * n,
        scratch_shapes=[pltpu.SemaphoreType.DMA((n, N_CHIP)), pltpu.SemaphoreType.DMA((n, N_CHIP))],
    )(*partials)


def _chip_exchange(name, sums):
    n = len(sums)

    def body(*refs):
        ins, outs = refs[:n], refs[n:2 * n]
        send_sems, recv_sems = refs[2 * n:]
        x, y, c = _position()
        chips = [(1 - x, y), (x, 1 - y), (1 - x, 1 - y)]
        copies = []
        for a in range(n):
            for j, (px, py) in enumerate(chips):
                copies.append(pltpu.make_async_remote_copy(
                    src_ref=ins[a].at[2 * px + py], dst_ref=outs[a].at[j], send_sem=send_sems.at[a, j],
                    recv_sem=recv_sems.at[a, j], device_id=(px, py, c), device_id_type=MESH))
        for cp in copies:
            cp.start()
        for cp in copies:
            cp.wait()

    return pl.pallas_call(
        body, name=name,
        out_shape=[jax.ShapeDtypeStruct((3,) + s.shape[1:], s.dtype) for s in sums],
        in_specs=[ANY] * n, out_specs=[ANY] * n,
        scratch_shapes=[pltpu.SemaphoreType.DMA((n, 3)), pltpu.SemaphoreType.DMA((n, 3))],
    )(*sums)


def _pair_sum(name, partial, received, core):
    _, _, rows, cols = partial.shape
    tr = _tile(rows, 512)

    def body(core_ref, p_ref, r_ref, o_ref):
        o_ref[...] = (p_ref[...].astype(F32) + r_ref[...].astype(F32)).astype(o_ref.dtype)

    return pl.pallas_call(
        body, name=name, out_shape=jax.ShapeDtypeStruct((N_CHIP, rows, cols), BF16),
        grid_spec=pltpu.PrefetchScalarGridSpec(
            num_scalar_prefetch=1, grid=(N_CHIP, rows // tr),
            in_specs=[pl.BlockSpec((None, None, tr, cols), lambda k, i, core: (k, core[0], i, 0)),
                      pl.BlockSpec((None, tr, cols), lambda k, i, core: (k, i, 0))],
            out_specs=pl.BlockSpec((None, tr, cols), lambda k, i, core: (k, i, 0))),
        compiler_params=_params("parallel", "parallel"),
    )(core, partial, received)


def _mm(name, a, b, out_shape, out_dtypes, grid, a_spec, b_spec, o_spec, ca, cb, k_steps=1):
    n_out = len(out_dtypes)

    def body(a_ref, b_ref, *rest):
        o_refs = rest[:n_out]
        part = _dot(a_ref[...], b_ref[...], ca, cb)
        if k_steps == 1:
            for o in o_refs:
                o[...] = part.astype(o.dtype)
            return
        acc = rest[n_out]
        k = pl.program_id(len(grid) - 1)

        @pl.when(k == 0)
        def _():
            acc[...] = part

        @pl.when(k > 0)
        def _():
            acc[...] += part

        @pl.when(k == k_steps - 1)
        def _():
            for o in o_refs:
                o[...] = acc[...].astype(o.dtype)

    o_block = tuple(d for d in o_spec.block_shape if d is not None)
    sem = ("parallel",) * (len(grid) - 1) + (("arbitrary",) if k_steps > 1 else ("parallel",))
    out = pl.pallas_call(
        body, name=name, grid=grid,
        out_shape=[jax.ShapeDtypeStruct(out_shape, d) for d in out_dtypes],
        in_specs=[a_spec, b_spec], out_specs=[o_spec] * n_out,
        scratch_shapes=[pltpu.VMEM(o_block, F32)] if k_steps > 1 else [],
        compiler_params=_params(*sem),
    )(a, b)
    return out if n_out > 1 else out[0]


def _mm_nn_cs(name, a, w, out_dtypes=(F32,), tm=512):
    m, k = a.shape
    _, _, ns = w.shape
    tm = _tile(m, tm)
    return _mm(name, a, w, (m, N_DEV * ns), out_dtypes, (m // tm, N_DEV),
               pl.BlockSpec((tm, k), lambda i, j: (i, 0)),
               pl.BlockSpec((None, k, ns), lambda i, j: (j, 0, 0)),
               pl.BlockSpec((tm, ns), lambda i, j: (i, j)), 1, 0)


def _mm_nn(name, a, w, out_dtypes=(F32,), tm=512, tn=512):
    m, k = a.shape
    _, n = w.shape
    tm, tn = _tile(m, tm), _tile(n, tn, 128)
    return _mm(name, a, w, (m, n), out_dtypes, (m // tm, n // tn),
               pl.BlockSpec((tm, k), lambda i, j: (i, 0)),
               pl.BlockSpec((k, tn), lambda i, j: (0, j)),
               pl.BlockSpec((tm, tn), lambda i, j: (i, j)), 1, 0)


def _mm_nt_cs(name, a, w, out_dtypes=(F32,), tm=512):
    m, _ = a.shape
    _, k, ns = w.shape
    tm = _tile(m, tm)
    return _mm(name, a, w, (m, k), out_dtypes, (m // tm, N_DEV),
               pl.BlockSpec((tm, ns), lambda i, j: (i, j)),
               pl.BlockSpec((None, k, ns), lambda i, j: (j, 0, 0)),
               pl.BlockSpec((tm, k), lambda i, j: (i, 0)), 1, 1, k_steps=N_DEV)


def _mm_nt(name, a, w, out_dtypes=(F32,), tm=512, tn=512):
    m, k = a.shape
    n, _ = w.shape
    tm, tn = _tile(m, tm), _tile(n, tn, 128)
    return _mm(name, a, w, (m, n), out_dtypes, (m // tm, n // tn),
               pl.BlockSpec((tm, k), lambda i, j: (i, 0)),
               pl.BlockSpec((tn, k), lambda i, j: (j, 0)),
               pl.BlockSpec((tm, tn), lambda i, j: (i, j)), 1, 1)


def _mm_tn_cs(name, a, g, tk=512):
    s, k = a.shape
    ns = g.shape[1] // N_DEV
    tk = _tile(k, tk, 128)
    return _mm(name, a, g, (N_DEV, k, ns), (BF16,), (k // tk, N_DEV),
               pl.BlockSpec((s, tk), lambda i, j: (0, i)),
               pl.BlockSpec((s, ns), lambda i, j: (0, j)),
               pl.BlockSpec((None, tk, ns), lambda i, j: (j, i, 0)), 0, 0)


def _mm_tn(name, a, g, tk=512, tn=1024):
    s, k = a.shape
    n = g.shape[1]
    tk, tn = _tile(k, tk, 128), _tile(n, tn, 128)
    return _mm(name, a, g, (k, n), (BF16,), (k // tk, n // tn),
               pl.BlockSpec((s, tk), lambda i, j: (0, i)),
               pl.BlockSpec((s, tn), lambda i, j: (0, j)),
               pl.BlockSpec((tk, tn), lambda i, j: (i, j)), 0, 0)


def _rowwise(name, fn, rows, bcast, outs, reds, tm):
    n_rows = rows[0][0].shape[0]
    tm = _tile(n_rows, tm)
    n_in, n_b, n_o, n_r = len(rows), len(bcast), len(outs), len(reds)

    def body(*refs):
        ins = refs[:n_in + n_b]
        o_refs = refs[n_in + n_b:n_in + n_b + n_o]
        r_refs = refs[n_in + n_b + n_o:]
        vals = fn(*[r[...] for r in ins])
        for o, v in zip(o_refs, vals[:n_o]):
            o[...] = v.astype(o.dtype)
        first = pl.program_id(0) == 0
        for r, v in zip(r_refs, vals[n_o:]):
            @pl.when(first)
            def _(r=r, v=v):
                r[...] = v.astype(r.dtype)

            @pl.when(jnp.logical_not(first))
            def _(r=r, v=v):
                r[...] += v.astype(r.dtype)

    def whole(shape):
        zeros = (0,) * len(shape)
        return pl.BlockSpec(shape, lambda i: zeros)

    in_specs = [pl.BlockSpec((tm, w), functools.partial(lambda i, cb: (i, cb), cb=cb)) for _, w, cb in rows]
    in_specs += [whole(b.shape) for b in bcast]
    out_specs = [pl.BlockSpec((tm, w), lambda i: (i, 0)) for w, _ in outs]
    out_specs += [whole(s) for s, _ in reds]
    out_shape = [jax.ShapeDtypeStruct((n_rows, w), d) for w, d in outs]
    out_shape += [jax.ShapeDtypeStruct(s, d) for s, d in reds]
    return pl.pallas_call(
        body, name=name, grid=(n_rows // tm,), out_shape=out_shape, in_specs=in_specs,
        out_specs=out_specs, compiler_params=_params("arbitrary" if n_r else "parallel"),
    )(*[r[0] for r in rows], *bcast)


def _norm_fwd(name, x, g):
    d = x.shape[1]
    return _rowwise(name, lambda xt, gt: (_rmsnorm(xt, gt),), [(x, d, 0)], [g], [(d, BF16)], [], 256)[0]


def _residual_norm_fwd(name, x, y, g_post, g_next):
    d = x.shape[1]

    def fn(xt, yt, gp, gn):
        new = xt + _rmsnorm(yt, gp)
        return new, _rmsnorm(new, gn)

    return _rowwise(name, fn, [(x, d, 0), (y, d, 0)], [g_post, g_next], [(d, F32), (d, BF16)], [], 256)


def _merge(z0, z1, z2, t0, t1, t2, b0, b1, b2):
    return (jax.nn.sigmoid(z0 + b0) * t0 + jax.nn.sigmoid(z1 + b1) * t1 + jax.nn.sigmoid(z2 + b2) * t2)


def _merge_fwd(name, zg, ts, b_gate):
    d = ts[0].shape[1]
    rows = [(zg, d, b) for b in range(3)] + [(t, d, 0) for t in ts]
    bias = [b_gate[:, b * d:(b + 1) * d] for b in range(3)]
    return _rowwise(name, lambda *v: (_merge(*v),), rows, bias, [(d, BF16)], [], 128)[0]


def _merge_bwd(name, dmerged, zg, ts, b_gate):
    d = ts[0].shape[1]
    rows = [(dmerged, d, 0)] + [(zg, d, b) for b in range(3)] + [(t, d, 0) for t in ts]
    bias = [b_gate[:, b * d:(b + 1) * d] for b in range(3)]

    def fn(dm, *v):
        _, vjp = jax.vjp(_merge, *v)
        dz0, dz1, dz2, dt0, dt1, dt2, db0, db1, db2 = vjp(dm)
        return (dt0, dt1, dt2, jnp.concatenate([dz0, dz1, dz2], axis=1),
                jnp.concatenate([db0, db1, db2], axis=1))

    return _rowwise(name, fn, rows, bias, [(d, BF16)] * 3 + [(3 * d, BF16)], [((1, 3 * d), F32)], 128)


def _loss_bwd(name, x, y, g_post, target):
    d = x.shape[1]

    def loss(xt, yt, gp, tt):
        err = xt + _rmsnorm(yt, gp) - tt
        return 0.5 * jnp.sum(jnp.mean(err * err, axis=-1))

    def fn(xt, yt, tt, gp):
        val, (dx, dy, dg) = jax.value_and_grad(loss, argnums=(0, 1, 2))(xt, yt, gp, tt)
        return dx, dy, jnp.full((1, BLK), val, F32), dg

    return _rowwise(name, fn, [(x, d, 0), (y, d, 0), (target, d, 0)], [g_post],
                    [(d, F32), (d, BF16)], [((1, BLK), F32), ((1, d), F32)], 256)


def _mid_bwd(name, dx_out, dh, x_mid, y, g_pre, g_post):
    d = dx_out.shape[1]

    def fn(dxo, dht, xm, yt, gpre, gpost):
        _, vjp_pre = jax.vjp(_rmsnorm, xm, gpre)
        dxm, dgpre = vjp_pre(dht)
        dxm = dxo + dxm
        _, vjp_post = jax.vjp(_rmsnorm, yt, gpost)
        dy, dgpost = vjp_post(dxm)
        return dxm, dy, dgpre, dgpost

    return _rowwise(name, fn, [(dx_out, d, 0), (dh, d, 0), (x_mid, d, 0), (y, d, 0)], [g_pre, g_post],
                    [(d, F32), (d, BF16)], [((1, d), F32), ((1, d), F32)], 256)


def _bottom_bwd(name, dx_mid, dh_a, dh_b, x, g_pre, y_prev=None, g_post_prev=None):
    d = dx_mid.shape[1]
    rows = [(dx_mid, d, 0), (dh_a, d, 0), (dh_b, d, 0), (x, d, 0)]
    if y_prev is None:
        def fn(dxm, da, db, xt, gpre):
            _, vjp_pre = jax.vjp(_rmsnorm, xt, gpre)
            dx, dgpre = vjp_pre(da + db)
            return dxm + dx, dgpre

        return _rowwise(name, fn, rows, [g_pre], [(d, F32)], [((1, d), F32)], 256)

    def fn2(dxm, da, db, xt, yt, gpre, gpost):
        _, vjp_pre = jax.vjp(_rmsnorm, xt, gpre)
        dx, dgpre = vjp_pre(da + db)
        dx = dxm + dx
        _, vjp_post = jax.vjp(_rmsnorm, yt, gpost)
        dy, dgpost = vjp_post(dx)
        return dx, dy, dgpre, dgpost

    return _rowwise(name, fn2, rows + [(y_prev, d, 0)], [g_pre, g_post_prev],
                    [(d, F32), (d, BF16)], [((1, d), F32), ((1, d), F32)], 256)


def _mem_norm_bwd(name, dmn, mem, g_mem):
    d = mem.shape[1]

    def fn(dt, mt, g):
        _, vjp = jax.vjp(_rmsnorm, mt, g)
        return (vjp(dt)[1],)

    return _rowwise(name, fn, [(dmn, d, 0), (mem, d, 0)], [g_mem], [], [((1, d), F32)], 256)[0]


def _split_dot(v, tri):
    hi = v.astype(BF16)
    lo = (v - hi.astype(F32)).astype(BF16)
    return (jnp.dot(hi, tri, preferred_element_type=F32) + jnp.dot(lo, tri, preferred_element_type=F32))


def _sb_block(q, kb, i, jj, run, later):
    row = lax.broadcasted_iota(jnp.int32, (BLK, BLK), 0)
    col = lax.broadcasted_iota(jnp.int32, (BLK, BLK), 1)
    strict = (jj * BLK + col) < (i * BLK + row)
    z = _dot(q, kb, 1, 1) * (BLK ** -0.5)
    soft = jnp.log1p(jnp.exp(-jnp.abs(z)))
    log_beta = jnp.minimum(z, 0.0) - soft
    log_keep = jnp.where(strict, jnp.minimum(-z, 0.0) - soft, 0.0)
    suffix = _split_dot(log_keep, later) + run
    a = jnp.where(strict, jnp.exp(log_beta + suffix), 0.0)
    return z, log_keep, a, strict


def _later_matrix():
    row = lax.broadcasted_iota(jnp.int32, (BLK, BLK), 0)
    col = lax.broadcasted_iota(jnp.int32, (BLK, BLK), 1)
    return (row > col).astype(BF16)


def _sb_fwd(name, proj, heads):
    s = proj.shape[0]
    nq = s // BLK

    def body(q_ref, k_ref, v_ref, o_ref):
        i = pl.program_id(1)
        q = q_ref[...]
        later = _later_matrix()

        def step(t, carry):
            acc, run = carry
            jj = i - t
            off = pl.multiple_of(jj * BLK, BLK)
            kb = k_ref[pl.ds(off, BLK), :]
            vb = v_ref[pl.ds(off, BLK), :]
            _, log_keep, a, _ = _sb_block(q, kb, i, jj, run, later)
            acc = acc + _dot(a, vb, 1, 0)
            return acc, run + jnp.sum(log_keep, axis=1, keepdims=True)

        acc, _ = lax.fori_loop(0, i + 1, step, (jnp.zeros((BLK, BLK), F32), jnp.zeros((BLK, 1), F32)))
        o_ref[...] = acc.astype(o_ref.dtype)

    return pl.pallas_call(
        body, name=name, grid=(heads, nq), out_shape=jax.ShapeDtypeStruct((s, heads * BLK), BF16),
        in_specs=[pl.BlockSpec((BLK, BLK), lambda h, i: (i, h)),
                  pl.BlockSpec((s, BLK), lambda h, i: (0, heads + h)),
                  pl.BlockSpec((s, BLK), lambda h, i: (0, 2 * heads + h))],
        out_specs=pl.BlockSpec((BLK, BLK), lambda h, i: (i, h)),
        compiler_params=_params("parallel", "arbitrary"),
    )(proj, proj, proj)


def _sb_bwd(name, proj, do, heads):
    s = proj.shape[0]
    nq = s // BLK

    def body(q_ref, k_ref, v_ref, do_ref, dq_ref, dk_ref, dv_ref, dk_acc, dv_acc, z_keep, dla_keep):
        i = pl.program_id(1)
        q = q_ref[...]
        dout = do_ref[...]
        later = _later_matrix()
        row = lax.broadcasted_iota(jnp.int32, (BLK, BLK), 0)
        col = lax.broadcasted_iota(jnp.int32, (BLK, BLK), 1)
        earlier = (row < col).astype(BF16)

        @pl.when(i == 0)
        def _():
            dk_acc[...] = jnp.zeros_like(dk_acc)
            dv_acc[...] = jnp.zeros_like(dv_acc)

        def down(t, run):
            jj = i - t
            off = pl.multiple_of(jj * BLK, BLK)
            kb = k_ref[pl.ds(off, BLK), :]
            vb = v_ref[pl.ds(off, BLK), :]
            z, log_keep, a, _ = _sb_block(q, kb, i, jj, run, later)
            dla = _dot(dout, vb, 1, 1) * a
            dv_acc[pl.ds(off, BLK), :] += _dot(a, dout, 0, 0)
            z_keep[jj] = z
            dla_keep[jj] = dla
            return run + jnp.sum(log_keep, axis=1, keepdims=True)

        lax.fori_loop(0, i + 1, down, jnp.zeros((BLK, 1), F32))

        def up(jj, carry):
            dq, before = carry
            off = pl.multiple_of(jj * BLK, BLK)
            kb = k_ref[pl.ds(off, BLK), :]
            z = z_keep[jj]
            dla = dla_keep[jj]
            strict = (jj * BLK + col) < (i * BLK + row)
            dkeep = jnp.where(strict, _split_dot(dla, earlier) + before, 0.0)
            sig = jax.nn.sigmoid(z)
            dz = ((dla * (1.0 - sig) - dkeep * sig) * (BLK ** -0.5)).astype(BF16)
            dk_acc[pl.ds(off, BLK), :] += _dot(dz, q, 0, 0)
            return dq + _dot(dz, kb, 1, 0), before + jnp.sum(dla, axis=1, keepdims=True)

        dq, _ = lax.fori_loop(0, i + 1, up, (jnp.zeros((BLK, BLK), F32), jnp.zeros((BLK, 1), F32)))
        dq_ref[...] = dq.astype(dq_ref.dtype)

        @pl.when(i == nq - 1)
        def _():
            dk_ref[...] = dk_acc[...].astype(dk_ref.dtype)
            dv_ref[...] = dv_acc[...].astype(dv_ref.dtype)

    shape = jax.ShapeDtypeStruct((s, heads * BLK), BF16)
    return pl.pallas_call(
        body, name=name, grid=(heads, nq), out_shape=[shape, shape, shape],
        in_specs=[pl.BlockSpec((BLK, BLK), lambda h, i: (i, h)),
                  pl.BlockSpec((s, BLK), lambda h, i: (0, heads + h)),
                  pl.BlockSpec((s, BLK), lambda h, i: (0, 2 * heads + h)),
                  pl.BlockSpec((BLK, BLK), lambda h, i: (i, h))],
        out_specs=[pl.BlockSpec((BLK, BLK), lambda h, i: (i, h)),
                   pl.BlockSpec((s, BLK), lambda h, i: (0, h)),
                   pl.BlockSpec((s, BLK), lambda h, i: (0, h))],
        scratch_shapes=[pltpu.VMEM((s, BLK), F32), pltpu.VMEM((s, BLK), F32),
                        pltpu.VMEM((nq, BLK, BLK), F32), pltpu.VMEM((nq, BLK, BLK), F32)],
        compiler_params=_params("parallel", "arbitrary"),
    )(proj, proj, proj, do)


def _xa_probs(q, k, hd):
    z = _dot(q, k, 1, 1) * (hd ** -0.5)
    z = z - jnp.max(z, axis=-1, keepdims=True)
    e = jnp.exp(z)
    return e / jnp.sum(e, axis=-1, keepdims=True)


def _xa_fwd(name, proj, mem_kv, width):
    s = proj.shape[0]
    m = mem_kv.shape[0]
    hd = width // XA_HEADS
    tq = _tile(s, 256)
    q_first = 5 * XA_HEADS

    def body(q_ref, k_ref, v_ref, o_ref):
        p = _xa_probs(q_ref[...], k_ref[...], hd)
        o_ref[...] = _dot(p, v_ref[...], 1, 0).astype(o_ref.dtype)

    return pl.pallas_call(
        body, name=name, grid=(XA_HEADS, s // tq), out_shape=jax.ShapeDtypeStruct((s, width), BF16),
        in_specs=[pl.BlockSpec((tq, hd), lambda h, i: (i, q_first + h)),
                  pl.BlockSpec((m, hd), lambda h, i: (0, h)),
                  pl.BlockSpec((m, hd), lambda h, i: (0, XA_HEADS + h))],
        out_specs=pl.BlockSpec((tq, hd), lambda h, i: (i, h)),
        compiler_params=_params("parallel", "parallel"),
    )(proj, mem_kv, mem_kv)


def _xa_bwd(name, proj, mem_kv, do, width):
    s = proj.shape[0]
    m = mem_kv.shape[0]
    hd = width // XA_HEADS
    tq = _tile(s, 256)
    nq = s // tq
    q_first = 5 * XA_HEADS

    def body(q_ref, k_ref, v_ref, do_ref, dq_ref, dk_ref, dv_ref, dk_acc, dv_acc):
        i = pl.program_id(1)

        @pl.when(i == 0)
        def _():
            dk_acc[...] = jnp.zeros_like(dk_acc)
            dv_acc[...] = jnp.zeros_like(dv_acc)

        q, k, v, dout = q_ref[...], k_ref[...], v_ref[...], do_ref[...]
        p = _xa_probs(q, k, hd)
        dp = _dot(dout, v, 1, 1)
        dv_acc[...] += _dot(p, dout, 0, 0)
        dz = ((p * (dp - jnp.sum(dp * p, axis=-1, keepdims=True))) * (hd ** -0.5)).astype(BF16)
        dq_ref[...] = _dot(dz, k, 1, 0).astype(dq_ref.dtype)
        dk_acc[...] += _dot(dz, q, 0, 0)

        @pl.when(i == nq - 1)
        def _():
            dk_ref[...] = dk_acc[...].astype(dk_ref.dtype)
            dv_ref[...] = dv_acc[...].astype(dv_ref.dtype)

    dq, dk, dv = pl.pallas_call(
        body, name=name, grid=(XA_HEADS, nq),
        out_shape=[jax.ShapeDtypeStruct((s, width), BF16), jax.ShapeDtypeStruct((m, width), BF16),
                   jax.ShapeDtypeStruct((m, width), BF16)],
        in_specs=[pl.BlockSpec((tq, hd), lambda h, i: (i, q_first + h)),
                  pl.BlockSpec((m, hd), lambda h, i: (0, h)),
                  pl.BlockSpec((m, hd), lambda h, i: (0, XA_HEADS + h)),
                  pl.BlockSpec((tq, hd), lambda h, i: (i, h))],
        out_specs=[pl.BlockSpec((tq, hd), lambda h, i: (i, h)),
                   pl.BlockSpec((m, hd), lambda h, i: (0, h)),
                   pl.BlockSpec((m, hd), lambda h, i: (0, h))],
        scratch_shapes=[pltpu.VMEM((m, hd), F32), pltpu.VMEM((m, hd), F32)],
        compiler_params=_params("parallel", "arbitrary"),
    )(proj, mem_kv, mem_kv, do)
    return dq, dk, dv


def _gm_pointwise(u_in, v_in, g_vnorm):
    return jax.nn.gelu(u_in), _rmsnorm(jax.nn.gelu(v_in), g_vnorm)


def _gm_mask():
    row = lax.broadcasted_iota(jnp.int32, (BLK, BLK), 0)
    col = lax.broadcasted_iota(jnp.int32, (BLK, BLK), 1)
    return (col // CHUNK) <= (row // CHUNK)


def _gm_fwd(name, proj, g_vnorm, w_s, b_s, width):
    s = proj.shape[0]
    groups = width // BLK

    def body(u_ref, v_ref, g_ref, w_ref, b_ref, o_ref):
        u, vn = _gm_pointwise(u_ref[...], v_ref[...], g_ref[...])
        vn = vn.astype(BF16)
        mask = _gm_mask()
        for g in range(groups):
            cols = slice(g * BLK, (g + 1) * BLK)
            w = jnp.where(mask, w_ref[g], 0.0)
            mixed = _dot(w, vn[:, cols], 1, 0) + b_ref[g]
            o_ref[:, cols] = (u[:, cols] * mixed).astype(o_ref.dtype)

    return pl.pallas_call(
        body, name=name, grid=(s // BLK,), out_shape=jax.ShapeDtypeStruct((s, width), BF16),
        in_specs=[pl.BlockSpec((BLK, width), lambda c: (c, 3)), pl.BlockSpec((BLK, width), lambda c: (c, 4)),
                  pl.BlockSpec((1, width), lambda c: (0, 0)),
                  pl.BlockSpec((groups, BLK, BLK), lambda c: (0, 0, 0)),
                  pl.BlockSpec((groups, BLK, 1), lambda c: (0, 0, 0))],
        out_specs=pl.BlockSpec((BLK, width), lambda c: (c, 0)),
        compiler_params=_params("parallel"),
    )(proj, proj, g_vnorm, w_s, b_s)


def _gm_bwd(name, proj, do, g_vnorm, w_s, b_s, width):
    s = proj.shape[0]
    groups = width // BLK

    def body(u_ref, v_ref, do_ref, g_ref, w_ref, b_ref, du_ref, dv_ref, dw_ref, db_ref, dg_ref, dvn_buf):
        first = pl.program_id(0) == 0
        (u, vn), vjp = jax.vjp(_gm_pointwise, u_ref[...], v_ref[...], g_ref[...])
        vn16 = vn.astype(BF16)
        dout = do_ref[...]
        mask = _gm_mask()
        du_parts = []
        for g in range(groups):
            cols = slice(g * BLK, (g + 1) * BLK)
            w = jnp.where(mask, w_ref[g], 0.0)
            mixed = _dot(w, vn16[:, cols], 1, 0) + b_ref[g]
            du_parts.append(dout[:, cols] * mixed)
            dmixed = dout[:, cols] * u[:, cols]
            dw = jnp.where(mask, _dot(dmixed, vn16[:, cols], 1, 1), 0.0)
            db = jnp.sum(dmixed, axis=1, keepdims=True)
            dvn_buf[:, cols] = _dot(w, dmixed, 0, 0)

            @pl.when(first)
            def _(g=g, dw=dw, db=db):
                dw_ref[g] = dw
                db_ref[g] = db

            @pl.when(jnp.logical_not(first))
            def _(g=g, dw=dw, db=db):
                dw_ref[g] += dw
                db_ref[g] += db

        du_in, dv_in, dg = vjp((jnp.concatenate(du_parts, axis=1), dvn_buf[...]))
        du_ref[...] = du_in.astype(du_ref.dtype)
        dv_ref[...] = dv_in.astype(dv_ref.dtype)

        @pl.when(first)
        def _():
            dg_ref[...] = dg

        @pl.when(jnp.logical_not(first))
        def _():
            dg_ref[...] += dg

    shape = jax.ShapeDtypeStruct((s, width), BF16)
    return pl.pallas_call(
        body, name=name, grid=(s // BLK,),
        out_shape=[shape, shape, jax.ShapeDtypeStruct((groups, BLK, BLK), F32),
                   jax.ShapeDtypeStruct((groups, BLK, 1), F32), jax.ShapeDtypeStruct((1, width), F32)],
        in_specs=[pl.BlockSpec((BLK, width), lambda c: (c, 3)), pl.BlockSpec((BLK, width), lambda c: (c, 4)),
                  pl.BlockSpec((BLK, width), lambda c: (c, 0)),
                  pl.BlockSpec((1, width), lambda c: (0, 0)),
                  pl.BlockSpec((groups, BLK, BLK), lambda c: (0, 0, 0)),
                  pl.BlockSpec((groups, BLK, 1), lambda c: (0, 0, 0))],
        out_specs=[pl.BlockSpec((BLK, width), lambda c: (c, 0)), pl.BlockSpec((BLK, width), lambda c: (c, 0)),
                   pl.BlockSpec((groups, BLK, BLK), lambda c: (0, 0, 0)),
                   pl.BlockSpec((groups, BLK, 1), lambda c: (0, 0, 0)),
                   pl.BlockSpec((1, width), lambda c: (0, 0))],
        scratch_shapes=[pltpu.VMEM((BLK, width), F32)],
        compiler_params=_params("arbitrary"),
    )(proj, proj, do, g_vnorm, w_s, b_s)


def _shift_down(v, k):
    row = lax.broadcasted_iota(jnp.int32, v.shape, 0)
    return jnp.where(row >= k, pltpu.roll(v, k, axis=0), 0.0)


def _shift_up(v, k):
    n = v.shape[0]
    row = lax.broadcasted_iota(jnp.int32, v.shape, 0)
    return jnp.where(row < n - k, pltpu.roll(v, n - k, axis=0), 0.0)


def _conv(gate, w, b):
    return b + w[0:1] * _shift_down(gate, 2) + w[1:2] * _shift_down(gate, 1) + w[2:3] * gate


def _conv_fwd(name, up, conv_w, conv_b):
    s, f2 = up.shape
    f = f2 // 2
    tc = _tile(f, 256, 128)
    nf = f // tc

    def body(g_ref, v_ref, w_ref, b_ref, o_ref):
        o_ref[...] = (jax.nn.gelu(_conv(g_ref[...], w_ref[...], b_ref[...])) * v_ref[...]).astype(o_ref.dtype)

    return pl.pallas_call(
        body, name=name, grid=(nf,), out_shape=jax.ShapeDtypeStruct((s, f), BF16),
        in_specs=[pl.BlockSpec((s, tc), lambda j: (0, j)), pl.BlockSpec((s, tc), lambda j: (0, nf + j)),
                  pl.BlockSpec((3, tc), lambda j: (0, j)), pl.BlockSpec((1, tc), lambda j: (0, j))],
        out_specs=pl.BlockSpec((s, tc), lambda j: (0, j)),
        compiler_params=_params("parallel"),
    )(up, up, conv_w, conv_b)


def _conv_bwd(name, up, da, conv_w, conv_b):
    s, f2 = up.shape
    f = f2 // 2
    tc = _tile(f, 256, 128)
    nf = f // tc

    def body(g_ref, v_ref, da_ref, w_ref, b_ref, dg_ref, dv_ref, dw_ref, db_ref):
        gate, val, dact, w = g_ref[...], v_ref[...], da_ref[...], w_ref[...]
        act, vjp = jax.vjp(jax.nn.gelu, _conv(gate, w, b_ref[...]))
        dv_ref[...] = (dact * act).astype(dv_ref.dtype)
        dconv = vjp(dact * val)[0]
        dg_ref[...] = (w[2:3] * dconv + w[1:2] * _shift_up(dconv, 1) + w[0:1] * _shift_up(dconv, 2)
                       ).astype(dg_ref.dtype)
        dw_ref[0:1, :] = jnp.sum(dconv * _shift_down(gate, 2), axis=0, keepdims=True)
        dw_ref[1:2, :] = jnp.sum(dconv * _shift_down(gate, 1), axis=0, keepdims=True)
        dw_ref[2:3, :] = jnp.sum(dconv * gate, axis=0, keepdims=True)
        db_ref[...] = jnp.sum(dconv, axis=0, keepdims=True)

    shape = jax.ShapeDtypeStruct((s, f), BF16)
    return pl.pallas_call(
        body, name=name, grid=(nf,),
        out_shape=[shape, shape, jax.ShapeDtypeStruct((3, f), F32), jax.ShapeDtypeStruct((1, f), F32)],
        in_specs=[pl.BlockSpec((s, tc), lambda j: (0, j)), pl.BlockSpec((s, tc), lambda j: (0, nf + j)),
                  pl.BlockSpec((s, tc), lambda j: (0, j)),
                  pl.BlockSpec((3, tc), lambda j: (0, j)), pl.BlockSpec((1, tc), lambda j: (0, j))],
        out_specs=[pl.BlockSpec((s, tc), lambda j: (0, j)), pl.BlockSpec((s, tc), lambda j: (0, j)),
                   pl.BlockSpec((3, tc), lambda j: (0, j)), pl.BlockSpec((1, tc), lambda j: (0, j))],
        compiler_params=_params("parallel"),
    )(up, up, da, conv_w, conv_b)


def _adamw(w, g, m, v):
    m = ADAM_B1 * m + (1.0 - ADAM_B1) * g
    v = ADAM_B2 * v + (1.0 - ADAM_B2) * jnp.square(g)
    m_hat = m / (1.0 - ADAM_B1 ** ADAM_STEP)
    v_hat = v / (1.0 - ADAM_B2 ** ADAM_STEP)
    delta = -ADAM_LR * (m_hat / (jnp.sqrt(v_hat) + ADAM_EPS) + ADAM_WD * w)
    return delta, m, v


def _update_shard(name, layer, w, m, v, own, received, chip, previous=None):
    _, rows, cols = w.shape
    tr = _tile(rows, 256)

    def body(chip_ref, w_ref, m_ref, v_ref, own_ref, rec_ref, *rest):
        g_ref, d_ref, nm_ref, nv_ref = rest[-4:]
        g = (own_ref[...].astype(F32) + rec_ref[0].astype(F32) + rec_ref[1].astype(F32)
             + rec_ref[2].astype(F32))
        delta, new_m, new_v = _adamw(w_ref[...], g, m_ref[...], v_ref[...])
        g_ref[...] = g
        d_ref[...] = delta
        nm_ref[...] = new_m
        nv_ref[...] = new_v

    layer_spec = pl.BlockSpec((None, tr, cols), lambda i, chip_ref: (layer, i, 0))
    in_specs = [layer_spec] * 3 + [pl.BlockSpec((None, tr, cols), lambda i, chip_ref: (chip_ref[0], i, 0)),
                                   pl.BlockSpec((3, tr, cols), lambda i, chip_ref: (0, i, 0))]
    args = [chip, w, m, v, own, received]
    aliases = {}
    if previous is not None:
        in_specs += [ANY] * 4
        aliases = {len(args) + k: k for k in range(4)}
        args += list(previous)
    return pl.pallas_call(
        body, name=name, out_shape=[jax.ShapeDtypeStruct(w.shape, F32)] * 4,
        grid_spec=pltpu.PrefetchScalarGridSpec(
            num_scalar_prefetch=1, grid=(rows // tr,), in_specs=in_specs, out_specs=[layer_spec] * 4),
        input_output_aliases=aliases, compiler_params=_params("parallel"),
    )(*args)


def _sum_devices(name, gathered):
    _, rows, cols = gathered.shape

    def body(g_ref, o_ref):
        total = g_ref[0]
        for d in range(1, N_DEV):
            total = total + g_ref[d]
        o_ref[...] = total

    return pl.pallas_call(
        body, name=name, out_shape=jax.ShapeDtypeStruct((rows, cols), F32),
        in_specs=[pl.BlockSpec((N_DEV, rows, cols), lambda: (0, 0, 0))],
        out_specs=pl.BlockSpec((rows, cols), lambda: (0, 0)),
        compiler_params=pltpu.CompilerParams(vmem_limit_bytes=VMEM_LIMIT_V7X),
    )(gathered)


def _update_small(name, w, g, m, v):
    def body(w_ref, g_ref, m_ref, v_ref, d_ref, nm_ref, nv_ref):
        delta, new_m, new_v = _adamw(w_ref[...], g_ref[...], m_ref[...], v_ref[...])
        d_ref[...] = delta
        nm_ref[...] = new_m
        nv_ref[...] = new_v

    spec = pl.BlockSpec(w.shape, lambda: (0, 0))
    return pl.pallas_call(
        body, name=name, out_shape=[jax.ShapeDtypeStruct(w.shape, F32)] * 3,
        in_specs=[spec] * 4, out_specs=[spec] * 3,
        compiler_params=pltpu.CompilerParams(vmem_limit_bytes=VMEM_LIMIT_V7X),
    )(w, g, m, v)


def _pack(arrays):
    flat = jnp.concatenate([a.reshape(-1) for a in arrays])
    pad = (-flat.shape[0]) % (8 * BLK)
    return jnp.pad(flat, (0, pad)).reshape(-1, BLK)


def _unpack(packed, shapes):
    flat = packed.reshape(-1)
    out, at = [], 0
    for shape in shapes:
        size = 1
        for d in shape:
            size *= d
        out.append(flat[at:at + size].reshape(shape))
        at += size
    return out


SHARDED = ("w_in", "w_mem_kv", "w_gate", "w_br_sb", "w_br_gm", "w_br_xa", "w_out", "w_up", "w_down")
SMALL = ("g_mix_pre", "g_vnorm", "w_s", "b_s", "g_mem", "b_gate", "g_mix_post", "g_ffn_pre", "conv_w",
         "conv_b", "g_ffn_post")
WEIGHTS = ("g_mix_pre", "w_in", "g_vnorm", "w_s", "b_s", "g_mem", "w_mem_kv", "w_gate", "b_gate", "w_br_sb",
           "w_br_gm", "w_br_xa", "w_out", "g_mix_post", "g_ffn_pre", "w_up", "conv_w", "conv_b", "w_down",
           "g_ffn_post")


def _cast_shard(name, w):
    rows, cols = w.shape
    return _rowwise(name, lambda t: (t,), [(w, cols, 0)], [], [(cols, BF16)], [], 512)[0]


def _forward_layer(l, x, h, mem, p, gathered, last):
    f32_bf16 = (F32, BF16)
    width = x.shape[1] // 2
    heads = width // BLK
    row = lambda name: p[name][l][None, :]
    w = gathered[l]
    keep = {"x": x, "h": h}
    proj, proj16 = _mm_nn_cs(f"proj{l}", h, w["w_in"], f32_bf16)
    zg = _mm_nn_cs(f"gates{l}", h, w["w_gate"])
    mn = _norm_fwd(f"mem_norm{l}", mem, row("g_mem"))
    mem_kv = _mm_nn(f"mem_kv{l}", mn, w["w_mem_kv"], (BF16,))
    o_sb = _sb_fwd(f"sb_fwd{l}", proj16, heads)
    b_s = p["b_s"][l][:, :, None]
    o_gm = _gm_fwd(f"gm_fwd{l}", proj, row("g_vnorm"), p["w_s"][l], b_s, width)
    o_xa = _xa_fwd(f"xa_fwd{l}", proj16, mem_kv, width)
    ts = [_mm_nn_cs(f"branch_{n}{l}", o, w[f"w_br_{n}"]) for n, o in (("sb", o_sb), ("gm", o_gm), ("xa", o_xa))]
    merged = _merge_fwd(f"merge{l}", zg, ts, row("b_gate"))
    y1 = _mm_nn(f"out{l}", merged, w["w_out"])
    x1, h2 = _residual_norm_fwd(f"mix_post{l}", x, y1, row("g_mix_post"), row("g_ffn_pre"))
    up = _mm_nn_cs(f"up{l}", h2, w["w_up"])
    act = _conv_fwd(f"conv_fwd{l}", up, p["conv_w_full"][l], row("conv_b"))
    y2 = _mm_nn(f"down{l}", act, w["w_down"])
    keep.update(proj=proj, proj16=proj16, zg=zg, mn=mn, mem_kv=mem_kv, o_sb=o_sb, o_gm=o_gm, o_xa=o_xa, ts=ts,
                merged=merged, y1=y1, x1=x1, h2=h2, up=up, act=act, y2=y2, b_s=b_s)
    return keep


def kernel(x, mem, g_mix_pre, w_in, g_vnorm, w_s, b_s, g_mem, w_mem_kv, w_gate, b_gate, w_br_sb, w_br_gm, w_br_xa, w_out, g_mix_post, g_ffn_pre, w_up, conv_w, conv_b, w_down, g_ffn_post, loss_target, m_g_mix_pre, m_w_in, m_g_vnorm, m_w_s, m_b_s, m_g_mem, m_w_mem_kv, m_w_gate, m_b_gate, m_w_br_sb, m_w_br_gm, m_w_br_xa, m_w_out, m_g_mix_post, m_g_ffn_pre, m_w_up, m_conv_w, m_conv_b, m_w_down, m_g_ffn_post, v_g_mix_pre, v_w_in, v_g_vnorm, v_w_s, v_b_s, v_g_mem, v_w_mem_kv, v_w_gate, v_b_gate, v_w_br_sb, v_w_br_gm, v_w_br_xa, v_w_out, v_g_mix_post, v_g_ffn_pre, v_w_up, v_conv_w, v_conv_b, v_w_down, v_g_ffn_post):
    p = dict(g_mix_pre=g_mix_pre, w_in=w_in, g_vnorm=g_vnorm, w_s=w_s, b_s=b_s, g_mem=g_mem, w_mem_kv=w_mem_kv,
             w_gate=w_gate, b_gate=b_gate, w_br_sb=w_br_sb, w_br_gm=w_br_gm, w_br_xa=w_br_xa, w_out=w_out,
             g_mix_post=g_mix_post, g_ffn_pre=g_ffn_pre, w_up=w_up, conv_w=conv_w, conv_b=conv_b, w_down=w_down,
             g_ffn_post=g_ffn_post)
    mom = dict(g_mix_pre=m_g_mix_pre, w_in=m_w_in, g_vnorm=m_g_vnorm, w_s=m_w_s, b_s=m_b_s, g_mem=m_g_mem,
               w_mem_kv=m_w_mem_kv, w_gate=m_w_gate, b_gate=m_b_gate, w_br_sb=m_w_br_sb, w_br_gm=m_w_br_gm,
               w_br_xa=m_w_br_xa, w_out=m_w_out, g_mix_post=m_g_mix_post, g_ffn_pre=m_g_ffn_pre, w_up=m_w_up,
               conv_w=m_conv_w, conv_b=m_conv_b, w_down=m_w_down, g_ffn_post=m_g_ffn_post)
    var = dict(g_mix_pre=v_g_mix_pre, w_in=v_w_in, g_vnorm=v_g_vnorm, w_s=v_w_s, b_s=v_b_s, g_mem=v_g_mem,
               w_mem_kv=v_w_mem_kv, w_gate=v_w_gate, b_gate=v_b_gate, w_br_sb=v_w_br_sb, w_br_gm=v_w_br_gm,
               w_br_xa=v_w_br_xa, w_out=v_w_out, g_mix_post=v_g_mix_post, g_ffn_pre=v_g_ffn_pre, w_up=v_w_up,
               conv_w=v_conv_w, conv_b=v_conv_b, w_down=v_w_down, g_ffn_post=v_g_ffn_post)
    depth = w_in.shape[0]
    x = x[0]
    mem = mem[0]
    target = loss_target[0]
    s, d = x.shape
    width = d // 2
    heads = width // BLK
    cx, cy, cc = lax.axis_index("x"), lax.axis_index("y"), lax.axis_index("c")
    core = cc.astype(jnp.int32).reshape(1)
    chip = (2 * cx + cy).astype(jnp.int32).reshape(1)
    me = 4 * cx + 2 * cy + cc
    row_sharded = ("w_mem_kv", "w_out", "w_down")

    gathered = []
    for l in range(depth):
        shards = [_cast_shard(f"cast_{n}{l}", p[n][l]) for n in SHARDED]
        full = dict(zip(SHARDED, _all_gather(f"gather_weights{l}", shards)))
        for n in row_sharded:
            full[n] = full[n].reshape(-1, full[n].shape[-1])
        gathered.append(full)
    conv_all = _all_gather("gather_conv_w", [conv_w])[0]
    p["conv_w_full"] = jnp.transpose(conv_all, (1, 2, 0, 3)).reshape(depth, 3, -1)

    kept = []
    h = _norm_fwd("pre_norm0", x, g_mix_pre[0][None, :])
    for l in range(depth):
        keep = _forward_layer(l, x, h, mem, p, gathered, l == depth - 1)
        kept.append(keep)
        if l + 1 < depth:
            x, h = _residual_norm_fwd(f"ffn_post{l}", keep["x1"], keep["y2"], g_ffn_post[l][None, :],
                                      g_mix_pre[l + 1][None, :])

    top = kept[-1]
    dx, dy2, loss_part, dg_ffn_post = _loss_bwd("loss", top["x1"], top["y2"], g_ffn_post[depth - 1][None, :], target)
    loss = lax.psum(loss_part[0, 0], ("x", "y", "c"))
    small_grads = [dict() for _ in range(depth)]
    partials = [dict() for _ in range(depth)]
    small_grads[depth - 1]["g_ffn_post"] = dg_ffn_post
    for l in reversed(range(depth)):
        k = kept[l]
        w = gathered[l]
        sg = small_grads[l]
        row = lambda name: p[name][l][None, :]
        partials[l]["w_down"] = _mm_tn(f"d_w_down{l}", k["act"], dy2)
        dact = _mm_nt(f"d_act{l}", dy2, w["w_down"])
        dgate, dval, sg["conv_w"], sg["conv_b"] = _conv_bwd(f"conv_bwd{l}", k["up"], dact, p["conv_w_full"][l],
                                                             row("conv_b"))
        dup = jnp.concatenate([dgate, dval], axis=1)
        partials[l]["w_up"] = _mm_tn_cs(f"d_w_up{l}", k["h2"], dup)
        dh2 = _mm_nt_cs(f"d_h2{l}", dup, w["w_up"])
        dx1, dy1, sg["g_ffn_pre"], sg["g_mix_post"] = _mid_bwd(f"mid_bwd{l}", dx, dh2, k["x1"], k["y1"],
                                                                row("g_ffn_pre"), row("g_mix_post"))
        partials[l]["w_out"] = _mm_tn(f"d_w_out{l}", k["merged"], dy1)
        dmerged = _mm_nt(f"d_merged{l}", dy1, w["w_out"])
        dt_sb, dt_gm, dt_xa, dzg, sg["b_gate"] = _merge_bwd(f"merge_bwd{l}", dmerged, k["zg"], k["ts"], row("b_gate"))
        do = {}
        for n, dt in (("sb", dt_sb), ("gm", dt_gm), ("xa", dt_xa)):
            partials[l][f"w_br_{n}"] = _mm_tn_cs(f"d_w_br_{n}{l}", k[f"o_{n}"], dt)
            do[n] = _mm_nt_cs(f"d_o_{n}{l}", dt, w[f"w_br_{n}"], (F32,) if n == "gm" else (BF16,))
        partials[l]["w_gate"] = _mm_tn_cs(f"d_w_gate{l}", k["h"], dzg)
        dh_gate = _mm_nt_cs(f"d_h_gate{l}", dzg, w["w_gate"])
        dq_xa, dk_mem, dv_mem = _xa_bwd(f"xa_bwd{l}", k["proj16"], k["mem_kv"], do["xa"], width)
        dmem_kv = jnp.concatenate([dk_mem, dv_mem], axis=1)
        partials[l]["w_mem_kv"] = _mm_tn(f"d_w_mem_kv{l}", k["mn"], dmem_kv)
        dmn = _mm_nt(f"d_mn{l}", dmem_kv, w["w_mem_kv"])
        sg["g_mem"] = _mem_norm_bwd(f"mem_norm_bwd{l}", dmn, mem, row("g_mem"))
        du, dvg, sg["w_s"], db_s, sg["g_vnorm"] = _gm_bwd(f"gm_bwd{l}", k["proj"], do["gm"], row("g_vnorm"),
                                                          p["w_s"][l], k["b_s"], width)
        sg["b_s"] = db_s[:, :, 0]
        dq, dk, dv = _sb_bwd(f"sb_bwd{l}", k["proj16"], do["sb"], heads)
        dproj = jnp.concatenate([dq, dk, dv, du, dvg, dq_xa], axis=1)
        partials[l]["w_in"] = _mm_tn_cs(f"d_w_in{l}", k["h"], dproj)
        dh_in = _mm_nt_cs(f"d_h_in{l}", dproj, w["w_in"])
        if l > 0:
            below = kept[l - 1]
            dx, dy2, sg["g_mix_pre"], small_grads[l - 1]["g_ffn_post"] = _bottom_bwd(
                f"bottom_bwd{l}", dx1, dh_in, dh_gate, k["x"], row("g_mix_pre"), below["y2"],
                p["g_ffn_post"][l - 1][None, :])
        else:
            dx, sg["g_mix_pre"] = _bottom_bwd(f"bottom_bwd{l}", dx1, dh_in, dh_gate, k["x"], row("g_mix_pre"))

    outs = {}
    for n in SHARDED:
        outs[n] = None
    for l in reversed(range(depth)):
        blocks = []
        for n in SHARDED:
            part = partials[l][n]
            if n in row_sharded:
                part = part.reshape(N_DEV, -1, part.shape[-1])
            blocks.append(part.reshape((N_CHIP, 2) + part.shape[1:]))
        received = _pair_exchange(f"pair_exchange{l}", blocks)
        sums = [_pair_sum(f"pair_sum_{n}{l}", b, r, core) for n, b, r in zip(SHARDED, blocks, received)]
        landed = _chip_exchange(f"chip_exchange{l}", sums)
        for n, own, rec in zip(SHARDED, sums, landed):
            outs[n] = _update_shard(f"update_{n}{l}", l, p[n], mom[n], var[n], own, rec, chip, outs[n])

    small_shapes = [small_grads[l][n].shape for l in range(depth) for n in SMALL]
    packed = _pack([small_grads[l][n] for l in range(depth) for n in SMALL])
    total = _sum_devices("sum_small", _all_gather("gather_small", [packed])[0])
    flat = _unpack(total, small_shapes)
    grads = {}
    for i, n in enumerate(SMALL):
        g = jnp.stack([flat[l * len(SMALL) + i] for l in range(depth)]).reshape((depth,) + p[n].shape[1:]
                                                                                 if n != "conv_w" else (depth, 3, -1))
        if n == "conv_w":
            per = conv_w.shape[2]
            g = lax.dynamic_slice_in_dim(g, me * per, per, axis=2)
        grads[n] = g
    delta, new_m, new_v = _update_small(
        "update_small", _pack([p[n] for n in SMALL]), _pack([grads[n] for n in SMALL]),
        _pack([mom[n] for n in SMALL]), _pack([var[n] for n in SMALL]))
    shapes = [p[n].shape for n in SMALL]
    for n, dl, nm, nv in zip(SMALL, _unpack(delta, shapes), _unpack(new_m, shapes), _unpack(new_v, shapes)):
        outs[n] = (grads[n], dl, nm, nv)

    result = [loss, dx[None]]
    for k in range(4):
        result += [outs[n][k] for n in WEIGHTS]
    return tuple(result)
```

```python
import functools

import jax
import jax.numpy as jnp
from jax import lax
from jax.experimental import pallas as pl
from jax.experimental.pallas import tpu as pltpu

F32 = jnp.float32
BF16 = jnp.bfloat16
N_DEV = 8
N_CHIP = 4
EPS = 1e-6
CHUNK = 64
BLK = 128
XA_HEADS = 4
SB_HEADS_PER_STEP = 4
ADAM_LR = 0.001
ADAM_B1 = 0.9
ADAM_B2 = 0.999
ADAM_EPS = 1e-08
ADAM_WD = 0.01
ADAM_STEP = 10
VMEM_LIMIT_V7X = 56 * 1024 * 1024
MESH = pl.DeviceIdType.MESH
ANY = pl.BlockSpec(memory_space=pl.ANY)


def _params(*sem):
    return pltpu.CompilerParams(dimension_semantics=sem, vmem_limit_bytes=VMEM_LIMIT_V7X)


def _tile(n, pref, mult=8):
    best = None
    for t in range(mult, min(n, pref) + 1, mult):
        if n % t == 0:
            best = t
    return best if best is not None else n


def _bf(x):
    return x if x.dtype == BF16 else x.astype(BF16)


def _dot(a, b, ca, cb):
    return lax.dot_general(_bf(a), _bf(b), (((ca,), (cb,)), ((), ())), preferred_element_type=F32)


def _rmsnorm(x, g):
    return (x * lax.rsqrt(jnp.mean(x * x, axis=-1, keepdims=True) + EPS)) * g


def _position():
    x, y, c = lax.axis_index("x"), lax.axis_index("y"), lax.axis_index("c")
    return x, y, c


def _all_gather(name, shards):
    n = len(shards)

    def body(*refs):
        ins, outs = refs[:n], refs[n:2 * n]
        send_sems, recv_sems, local_sems = refs[2 * n:]
        x, y, c = _position()
        me = 4 * x + 2 * y + c
        sibling = (x, y, 1 - c)
        chips = [(1 - x, y), (x, 1 - y), (1 - x, 1 - y)]

        def copy(a, k, block, to, src=None):
            dst = outs[a].at[block]
            return pltpu.make_async_remote_copy(
                src_ref=dst if src is None else src, dst_ref=dst, send_sem=send_sems.at[a, k],
                recv_sem=recv_sems.at[a, k], device_id=to, device_id_type=MESH)

        local = [pltpu.make_async_copy(ins[a], outs[a].at[me], local_sems.at[a]) for a in range(n)]
        for cp in local:
            cp.start()
        first = []
        for a in range(n):
            first.append(copy(a, 0, me, sibling, src=ins[a]))
            for j, chip in enumerate(chips):
                first.append(copy(a, 1 + j, me, (*chip, c), src=ins[a]))
        for cp in first:
            cp.start()
        passed = []
        for a in range(n):
            for j, (px, py) in enumerate(chips):
                block = 4 * px + 2 * py + c
                copy(a, 1 + j, block, sibling).wait_recv()
                forward = copy(a, 4 + j, block, sibling)
                forward.start()
                passed.append(forward)
        for a in range(n):
            copy(a, 0, 4 * x + 2 * y + (1 - c), sibling).wait_recv()
            for j, (px, py) in enumerate(chips):
                copy(a, 4 + j, 4 * px + 2 * py + (1 - c), sibling).wait_recv()
        for cp in first + passed:
            cp.wait_send()
        for cp in local:
            cp.wait()

    return pl.pallas_call(
        body, name=name,
        out_shape=[jax.ShapeDtypeStruct((N_DEV,) + s.shape, s.dtype) for s in shards],
        in_specs=[ANY] * n, out_specs=[ANY] * n,
        scratch_shapes=[pltpu.SemaphoreType.DMA((n, 7)), pltpu.SemaphoreType.DMA((n, 7)),
                        pltpu.SemaphoreType.DMA((n,))],
    )(*shards)


HBM = pl.BlockSpec(memory_space=pltpu.HBM)
SEM = pl.BlockSpec(memory_space=pltpu.SEMAPHORE)
DATAFLOW = pltpu.SideEffectType.DATAFLOW_SIDE_EFFECTING


def _remote(src, dst, send, recv, k, peer):
    return pltpu.make_async_remote_copy(src_ref=src, dst_ref=dst, send_sem=send.at[k], recv_sem=recv.at[k],
                                        device_id=peer, device_id_type=MESH)


def _gather_first_copies(n):
    def build(refs, send, recv):
        x, y, c = _position()
        me = 4 * x + 2 * y + c
        peers = [(x, y, 1 - c), (1 - x, y, c), (x, 1 - y, c), (1 - x, 1 - y, c)]
        return [_remote(refs[a].at[me], refs[a].at[me], send, recv, 4 * a + k, peer)
                for a in range(n) for k, peer in enumerate(peers)]
    return build, 4 * n


def _gather_relay_copies(n):
    def build(refs, send, recv):
        x, y, c = _position()
        blocks = [4 * px + 2 * py + c for px, py in ((1 - x, y), (x, 1 - y), (1 - x, 1 - y))]
        return [_remote(refs[a].at[blk], refs[a].at[blk], send, recv, 3 * a + j, (x, y, 1 - c))
                for a in range(n) for j, blk in enumerate(blocks)]
    return build, 3 * n


def _pair_copies(n):
    def build(refs, send, recv):
        x, y, c = _position()
        return [_remote(refs[a].at[k, 1 - c], refs[n + a].at[k], send, recv, N_CHIP * a + k, (x, y, 1 - c))
                for a in range(n) for k in range(N_CHIP)]
    return build, N_CHIP * n


def _chip_copies(n):
    def build(refs, send, recv):
        x, y, c = _position()
        chips = ((1 - x, y), (x, 1 - y), (1 - x, 1 - y))
        return [_remote(refs[a].at[2 * px + py], refs[n + a].at[j], send, recv, 3 * a + j, (px, py, c))
                for a in range(n) for j, (px, py) in enumerate(chips)]
    return build, 3 * n


def _split_start(name, buffers, copies, after=None):
    build, n_copies = copies
    nb = len(buffers)
    extra = [] if after is None else [after]

    def body(*refs):
        at = nb + len(extra)
        for cp in build(refs[:nb], refs[at], refs[at + 1]):
            cp.start()
        token = refs[at + 2 + nb]
        token[...] = jnp.zeros_like(token)

    outs = pl.pallas_call(
        body, name=name,
        out_shape=[pltpu.SemaphoreType.DMA((n_copies,)), pltpu.SemaphoreType.DMA((n_copies,))]
        + [pltpu.HBM(b.shape, b.dtype) for b in buffers] + [jax.ShapeDtypeStruct((8, BLK), F32)],
        in_specs=[HBM] * nb + [ANY] * len(extra),
        out_specs=[SEM, SEM] + [HBM] * nb + [pl.BlockSpec(memory_space=pltpu.VMEM)],
        input_output_aliases={i: 2 + i for i in range(nb)},
        compiler_params=pltpu.CompilerParams(has_side_effects=DATAFLOW),
    )(*[pltpu.with_memory_space_constraint(b, pltpu.HBM) for b in buffers], *extra)
    return (outs[0], outs[1]), list(outs[2:2 + nb]), outs[-1]


def _split_wait(name, buffers, sems, copies, after, then=None):
    build, _ = copies
    nb = len(buffers)
    n_next = 0 if then is None else then[1]
    first_out = nb + 3

    def body(*refs):
        for cp in build(refs[:nb], refs[nb], refs[nb + 1]):
            cp.wait()
        if then is not None:
            for cp in then[0](refs[:nb], refs[first_out], refs[first_out + 1]):
                cp.start()

    sem_shapes = [] if then is None else [pltpu.SemaphoreType.DMA((n_next,))] * 2
    outs = pl.pallas_call(
        body, name=name,
        out_shape=sem_shapes + [pltpu.HBM(b.shape, b.dtype) for b in buffers],
        in_specs=[HBM] * nb + [SEM, SEM, ANY],
        out_specs=[SEM] * len(sem_shapes) + [HBM] * nb,
        input_output_aliases={i: len(sem_shapes) + i for i in range(nb)},
        compiler_params=pltpu.CompilerParams(has_side_effects=DATAFLOW),
    )(*buffers, sems[0], sems[1], after)
    if then is None:
        return list(outs)
    return (outs[0], outs[1]), list(outs[2:])


def _pair_sum(name, partial, received, core):
    _, _, rows, cols = partial.shape
    tr = _tile(rows, 512)

    def body(core_ref, p_ref, r_ref, o_ref):
        o_ref[...] = (p_ref[...].astype(F32) + r_ref[...].astype(F32)).astype(o_ref.dtype)

    return pl.pallas_call(
        body, name=name, out_shape=jax.ShapeDtypeStruct((N_CHIP, rows, cols), BF16),
        grid_spec=pltpu.PrefetchScalarGridSpec(
            num_scalar_prefetch=1, grid=(N_CHIP, rows // tr),
            in_specs=[pl.BlockSpec((None, None, tr, cols), lambda k, i, core: (k, core[0], i, 0)),
                      pl.BlockSpec((None, tr, cols), lambda k, i, core: (k, i, 0))],
            out_specs=pl.BlockSpec((None, tr, cols), lambda k, i, core: (k, i, 0))),
        compiler_params=_params("parallel", "parallel"),
    )(core, partial, received)


def _mm(name, a, b, out_shape, out_dtypes, grid, a_spec, b_spec, o_spec, ca, cb, k_steps=1, after=None):
    n_out = len(out_dtypes)
    extra = [] if after is None else [after]

    def body(a_ref, b_ref, *rest):
        rest = rest[len(extra):]
        o_refs = rest[:n_out]
        part = _dot(a_ref[...], b_ref[...], ca, cb)
        if k_steps == 1:
            for o in o_refs:
                o[...] = part.astype(o.dtype)
            return
        acc = rest[n_out]
        k = pl.program_id(len(grid) - 1)

        @pl.when(k == 0)
        def _():
            acc[...] = part

        @pl.when(k > 0)
        def _():
            acc[...] += part

        @pl.when(k == k_steps - 1)
        def _():
            for o in o_refs:
                o[...] = acc[...].astype(o.dtype)

    o_block = tuple(d for d in o_spec.block_shape if d is not None)
    sem = ("parallel",) * (len(grid) - 1) + (("arbitrary",) if k_steps > 1 else ("parallel",))
    out = pl.pallas_call(
        body, name=name, grid=grid,
        out_shape=[jax.ShapeDtypeStruct(out_shape, d) for d in out_dtypes],
        in_specs=[a_spec, b_spec] + [ANY] * len(extra), out_specs=[o_spec] * n_out,
        scratch_shapes=[pltpu.VMEM(o_block, F32)] if k_steps > 1 else [],
        compiler_params=_params(*sem),
    )(a, b, *extra)
    return out if n_out > 1 else out[0]


def _mm_nn_cs(name, a, w, out_dtypes=(F32,), tm=512, **kw):
    m, k = a.shape
    _, _, ns = w.shape
    tm = _tile(m, tm)
    return _mm(name, a, w, (m, N_DEV * ns), out_dtypes, (m // tm, N_DEV),
               pl.BlockSpec((tm, k), lambda i, j: (i, 0)),
               pl.BlockSpec((None, k, ns), lambda i, j: (j, 0, 0)),
               pl.BlockSpec((tm, ns), lambda i, j: (i, j)), 1, 0, **kw)


def _mm_nn(name, a, w, out_dtypes=(F32,), tm=512, tn=512, **kw):
    m, k = a.shape
    _, n = w.shape
    tm, tn = _tile(m, tm), _tile(n, tn, 128)
    return _mm(name, a, w, (m, n), out_dtypes, (m // tm, n // tn),
               pl.BlockSpec((tm, k), lambda i, j: (i, 0)),
               pl.BlockSpec((k, tn), lambda i, j: (0, j)),
               pl.BlockSpec((tm, tn), lambda i, j: (i, j)), 1, 0, **kw)


def _mm_nt_cs(name, a, w, out_dtypes=(F32,), tm=512, **kw):
    m, _ = a.shape
    _, k, ns = w.shape
    tm = _tile(m, tm)
    return _mm(name, a, w, (m, k), out_dtypes, (m // tm, N_DEV),
               pl.BlockSpec((tm, ns), lambda i, j: (i, j)),
               pl.BlockSpec((None, k, ns), lambda i, j: (j, 0, 0)),
               pl.BlockSpec((tm, k), lambda i, j: (i, 0)), 1, 1, k_steps=N_DEV, **kw)


def _mm_nt(name, a, w, out_dtypes=(F32,), tm=512, tn=512, **kw):
    m, k = a.shape
    n, _ = w.shape
    tm, tn = _tile(m, tm), _tile(n, tn, 128)
    return _mm(name, a, w, (m, n), out_dtypes, (m // tm, n // tn),
               pl.BlockSpec((tm, k), lambda i, j: (i, 0)),
               pl.BlockSpec((tn, k), lambda i, j: (j, 0)),
               pl.BlockSpec((tm, tn), lambda i, j: (i, j)), 1, 1, **kw)


def _mm_tn_cs(name, a, g, tk=512, **kw):
    s, k = a.shape
    ns = g.shape[1] // N_DEV
    tk = _tile(k, tk, 128)
    return _mm(name, a, g, (N_DEV, k, ns), (BF16,), (k // tk, N_DEV),
               pl.BlockSpec((s, tk), lambda i, j: (0, i)),
               pl.BlockSpec((s, ns), lambda i, j: (0, j)),
               pl.BlockSpec((None, tk, ns), lambda i, j: (j, i, 0)), 0, 0, **kw)


def _mm_tn(name, a, g, tk=512, tn=1024, **kw):
    s, k = a.shape
    n = g.shape[1]
    tk, tn = _tile(k, tk, 128), _tile(n, tn, 128)
    return _mm(name, a, g, (k, n), (BF16,), (k // tk, n // tn),
               pl.BlockSpec((s, tk), lambda i, j: (0, i)),
               pl.BlockSpec((s, tn), lambda i, j: (0, j)),
               pl.BlockSpec((tk, tn), lambda i, j: (i, j)), 0, 0, **kw)


def _rowwise(name, fn, rows, bcast, outs, reds, tm, after=None):
    n_rows = rows[0][0].shape[0]
    tm = _tile(n_rows, tm)
    n_in, n_b, n_o, n_r = len(rows), len(bcast), len(outs), len(reds)
    extra = [] if after is None else [after]

    def body(*refs):
        ins = refs[:n_in + n_b]
        refs = refs[len(extra):]
        o_refs = refs[n_in + n_b:n_in + n_b + n_o]
        r_refs = refs[n_in + n_b + n_o:]
        vals = fn(*[r[...] for r in ins])
        for o, v in zip(o_refs, vals[:n_o]):
            o[...] = v.astype(o.dtype)
        first = pl.program_id(0) == 0
        for r, v in zip(r_refs, vals[n_o:]):
            @pl.when(first)
            def _(r=r, v=v):
                r[...] = v.astype(r.dtype)

            @pl.when(jnp.logical_not(first))
            def _(r=r, v=v):
                r[...] += v.astype(r.dtype)

    def whole(shape):
        zeros = (0,) * len(shape)
        return pl.BlockSpec(shape, lambda i: zeros)

    in_specs = [pl.BlockSpec((tm, w), functools.partial(lambda i, cb: (i, cb), cb=cb)) for _, w, cb in rows]
    in_specs += [whole(b.shape) for b in bcast] + [ANY] * len(extra)
    out_specs = [pl.BlockSpec((tm, w), lambda i: (i, 0)) for w, _ in outs]
    out_specs += [whole(s) for s, _ in reds]
    out_shape = [jax.ShapeDtypeStruct((n_rows, w), d) for w, d in outs]
    out_shape += [jax.ShapeDtypeStruct(s, d) for s, d in reds]
    return pl.pallas_call(
        body, name=name, grid=(n_rows // tm,), out_shape=out_shape, in_specs=in_specs,
        out_specs=out_specs, compiler_params=_params("arbitrary" if n_r else "parallel"),
    )(*[r[0] for r in rows], *bcast, *extra)


def _norm_fwd(name, x, g, after=None):
    d = x.shape[1]
    return _rowwise(name, lambda xt, gt: (_rmsnorm(xt, gt),), [(x, d, 0)], [g], [(d, BF16)], [], 256,
                    after=after)[0]


def _residual_norm_fwd(name, x, y, g_post, g_next):
    d = x.shape[1]

    def fn(xt, yt, gp, gn):
        new = xt + _rmsnorm(yt, gp)
        return new, _rmsnorm(new, gn)

    return _rowwise(name, fn, [(x, d, 0), (y, d, 0)], [g_post, g_next], [(d, F32), (d, BF16)], [], 256)


def _merge(z0, z1, z2, t0, t1, t2, b0, b1, b2):
    return (jax.nn.sigmoid(z0 + b0) * t0 + jax.nn.sigmoid(z1 + b1) * t1 + jax.nn.sigmoid(z2 + b2) * t2)


def _merge_fwd(name, zg, ts, b_gate):
    d = ts[0].shape[1]
    rows = [(zg, d, b) for b in range(3)] + [(t, d, 0) for t in ts]
    bias = [b_gate[:, b * d:(b + 1) * d] for b in range(3)]
    return _rowwise(name, lambda *v: (_merge(*v),), rows, bias, [(d, BF16)], [], 128)[0]


def _merge_bwd(name, dmerged, zg, ts, b_gate):
    d = ts[0].shape[1]
    rows = [(dmerged, d, 0)] + [(zg, d, b) for b in range(3)] + [(t, d, 0) for t in ts]
    bias = [b_gate[:, b * d:(b + 1) * d] for b in range(3)]

    def fn(dm, *v):
        _, vjp = jax.vjp(_merge, *v)
        dz0, dz1, dz2, dt0, dt1, dt2, db0, db1, db2 = vjp(dm)
        return (dt0, dt1, dt2, jnp.concatenate([dz0, dz1, dz2], axis=1),
                jnp.concatenate([db0, db1, db2], axis=1))

    return _rowwise(name, fn, rows, bias, [(d, BF16)] * 3 + [(3 * d, BF16)], [((1, 3 * d), F32)], 128)


def _loss_bwd(name, x, y, g_post, target):
    d = x.shape[1]

    def loss(xt, yt, gp, tt):
        err = xt + _rmsnorm(yt, gp) - tt
        return 0.5 * jnp.sum(jnp.mean(err * err, axis=-1))

    def fn(xt, yt, tt, gp):
        val, (dx, dy, dg) = jax.value_and_grad(loss, argnums=(0, 1, 2))(xt, yt, gp, tt)
        return dx, dy, jnp.full((1, BLK), val, F32), dg

    return _rowwise(name, fn, [(x, d, 0), (y, d, 0), (target, d, 0)], [g_post],
                    [(d, F32), (d, BF16)], [((1, BLK), F32), ((1, d), F32)], 256)


def _mid_bwd(name, dx_out, dh, x_mid, y, g_pre, g_post):
    d = dx_out.shape[1]

    def fn(dxo, dht, xm, yt, gpre, gpost):
        _, vjp_pre = jax.vjp(_rmsnorm, xm, gpre)
        dxm, dgpre = vjp_pre(dht)
        dxm = dxo + dxm
        _, vjp_post = jax.vjp(_rmsnorm, yt, gpost)
        dy, dgpost = vjp_post(dxm)
        return dxm, dy, dgpre, dgpost

    return _rowwise(name, fn, [(dx_out, d, 0), (dh, d, 0), (x_mid, d, 0), (y, d, 0)], [g_pre, g_post],
                    [(d, F32), (d, BF16)], [((1, d), F32), ((1, d), F32)], 256)


def _bottom_bwd(name, dx_mid, dh_a, dh_b, x, g_pre, y_prev=None, g_post_prev=None):
    d = dx_mid.shape[1]
    rows = [(dx_mid, d, 0), (dh_a, d, 0), (dh_b, d, 0), (x, d, 0)]
    if y_prev is None:
        def fn(dxm, da, db, xt, gpre):
            _, vjp_pre = jax.vjp(_rmsnorm, xt, gpre)
            dx, dgpre = vjp_pre(da + db)
            return dxm + dx, dgpre

        return _rowwise(name, fn, rows, [g_pre], [(d, F32)], [((1, d), F32)], 256)

    def fn2(dxm, da, db, xt, yt, gpre, gpost):
        _, vjp_pre = jax.vjp(_rmsnorm, xt, gpre)
        dx, dgpre = vjp_pre(da + db)
        dx = dxm + dx
        _, vjp_post = jax.vjp(_rmsnorm, yt, gpost)
        dy, dgpost = vjp_post(dx)
        return dx, dy, dgpre, dgpost

    return _rowwise(name, fn2, rows + [(y_prev, d, 0)], [g_pre, g_post_prev],
                    [(d, F32), (d, BF16)], [((1, d), F32), ((1, d), F32)], 256)


def _mem_norm_bwd(name, dmn, mem, g_mem):
    d = mem.shape[1]

    def fn(dt, mt, g):
        _, vjp = jax.vjp(_rmsnorm, mt, g)
        return (vjp(dt)[1],)

    return _rowwise(name, fn, [(dmn, d, 0), (mem, d, 0)], [g_mem], [], [((1, d), F32)], 256)[0]


def _split_dot(v, tri):
    hi = v.astype(BF16)
    lo = (v - hi.astype(F32)).astype(BF16)
    return (jnp.dot(hi, tri, preferred_element_type=F32) + jnp.dot(lo, tri, preferred_element_type=F32))


def _sb_scores(q, kb, strict=None):
    z = _dot(q, kb, 1, 1) * (BLK ** -0.5)
    soft = jnp.log1p(jnp.exp(-jnp.abs(z)))
    log_beta = jnp.minimum(z, 0.0) - soft
    log_keep = jnp.minimum(-z, 0.0) - soft
    if strict is not None:
        log_keep = jnp.where(strict, log_keep, 0.0)
    return z, log_beta, log_keep


def _block_iotas():
    return (lax.broadcasted_iota(jnp.int32, (BLK, BLK), 0), lax.broadcasted_iota(jnp.int32, (BLK, BLK), 1))


def _sb_fwd(name, proj, heads):
    s = proj.shape[0]
    nq = s // BLK
    hb = _tile(heads, SB_HEADS_PER_STEP, 1)
    groups = heads // hb
    wide = hb * BLK

    def body(q_ref, k_ref, v_ref, o_ref, tot_ref, acc_ref):
        i = pl.program_id(1)
        row, col = _block_iotas()
        diag = col < row
        later = (row > col).astype(BF16)
        head_cols = [slice(hh * BLK, (hh + 1) * BLK) for hh in range(hb)]

        def tiles(off, runs, strict):
            scores = [_sb_scores(q_ref[:, c], k_ref[pl.ds(off, BLK), c], strict) for c in head_cols]
            suffixes = [_split_dot(sc[2], later) for sc in scores]
            new = []
            for hh, c in enumerate(head_cols):
                a = jnp.exp(scores[hh][1] + suffixes[hh] + runs[hh])
                if strict is not None:
                    a = jnp.where(strict, a, 0.0)
                acc_ref[:, c] += _dot(a, v_ref[pl.ds(off, BLK), c], 1, 0)
                new.append(runs[hh] + jnp.sum(scores[hh][2], axis=1, keepdims=True))
            return tuple(new)

        acc_ref[...] = jnp.zeros_like(acc_ref)
        runs = tiles(pl.multiple_of(i * BLK, BLK), (jnp.zeros((BLK, 1), F32),) * hb, diag)
        runs = lax.fori_loop(0, i, lambda t, r: tiles(pl.multiple_of((i - 1 - t) * BLK, BLK), r, None), runs)
        o_ref[...] = acc_ref[...].astype(o_ref.dtype)
        for hh in range(hb):
            tot_ref[:, hh:hh + 1] = runs[hh]

    return pl.pallas_call(
        body, name=name, grid=(groups, nq),
        out_shape=[jax.ShapeDtypeStruct((s, heads * BLK), BF16), jax.ShapeDtypeStruct((groups, s, hb), F32)],
        in_specs=[pl.BlockSpec((BLK, wide), lambda g, i: (i, g)),
                  pl.BlockSpec((s, wide), lambda g, i: (0, groups + g)),
                  pl.BlockSpec((s, wide), lambda g, i: (0, 2 * groups + g))],
        out_specs=[pl.BlockSpec((BLK, wide), lambda g, i: (i, g)),
                   pl.BlockSpec((None, BLK, hb), lambda g, i: (g, i, 0))],
        scratch_shapes=[pltpu.VMEM((BLK, wide), F32)],
        compiler_params=_params("parallel", "arbitrary"),
    )(proj, proj, proj)


def _sb_bwd(name, proj, do, tot, heads):
    s = proj.shape[0]
    nq = s // BLK
    groups, _, hb = tot.shape
    wide = hb * BLK

    def body(q_ref, k_ref, v_ref, do_ref, tot_ref, dq_ref, dk_ref, dv_ref, dq_acc, dk_acc, dv_acc):
        i = pl.program_id(1)
        row, col = _block_iotas()
        diag = col < row
        upto = (row <= col).astype(BF16)
        earlier = (row < col).astype(BF16)

        @pl.when(i == 0)
        def _():
            dk_acc[...] = jnp.zeros_like(dk_acc)
            dv_acc[...] = jnp.zeros_like(dv_acc)

        dq_acc[...] = jnp.zeros_like(dq_acc)

        head_cols = [slice(hh * BLK, (hh + 1) * BLK) for hh in range(hb)]

        def tiles(off, carry, strict):
            rows = pl.ds(off, BLK)
            scores = [_sb_scores(q_ref[:, c], k_ref[rows, c], strict) for c in head_cols]
            das = [_dot(do_ref[:, c], v_ref[rows, c], 1, 1) for c in head_cols]
            prefixes = [_split_dot(sc[2], upto) for sc in scores]
            dlas = []
            for hh, c in enumerate(head_cols):
                suffix = tot_ref[:, hh:hh + 1] - (prefixes[hh] + carry[2 * hh])
                a = jnp.exp(scores[hh][1] + suffix)
                if strict is not None:
                    a = jnp.where(strict, a, 0.0)
                dlas.append(das[hh] * a)
                dv_acc[rows, c] += _dot(a, do_ref[:, c], 0, 0)
            dkeeps = [_split_dot(dla, earlier) for dla in dlas]
            new = ()
            for hh, c in enumerate(head_cols):
                dkeep = dkeeps[hh] + carry[2 * hh + 1]
                if strict is not None:
                    dkeep = jnp.where(strict, dkeep, 0.0)
                sig = jax.nn.sigmoid(scores[hh][0])
                dz = ((dlas[hh] * (1.0 - sig) - dkeep * sig) * (BLK ** -0.5)).astype(BF16)
                dk_acc[rows, c] += _dot(dz, q_ref[:, c], 0, 0)
                dq_acc[:, c] += _dot(dz, k_ref[rows, c], 1, 0)
                new += (carry[2 * hh] + jnp.sum(scores[hh][2], axis=1, keepdims=True),
                        carry[2 * hh + 1] + jnp.sum(dlas[hh], axis=1, keepdims=True))
            return new

        carry = lax.fori_loop(0, i, lambda jj, cr: tiles(pl.multiple_of(jj * BLK, BLK), cr, None),
                              (jnp.zeros((BLK, 1), F32),) * (2 * hb))
        tiles(pl.multiple_of(i * BLK, BLK), carry, diag)
        dq_ref[...] = dq_acc[...].astype(dq_ref.dtype)

        @pl.when(i == nq - 1)
        def _():
            dk_ref[...] = dk_acc[...].astype(dk_ref.dtype)
            dv_ref[...] = dv_acc[...].astype(dv_ref.dtype)

    shape = jax.ShapeDtypeStruct((s, heads * BLK), BF16)
    return pl.pallas_call(
        body, name=name, grid=(groups, nq), out_shape=[shape, shape, shape],
        in_specs=[pl.BlockSpec((BLK, wide), lambda g, i: (i, g)),
                  pl.BlockSpec((s, wide), lambda g, i: (0, groups + g)),
                  pl.BlockSpec((s, wide), lambda g, i: (0, 2 * groups + g)),
                  pl.BlockSpec((BLK, wide), lambda g, i: (i, g)),
                  pl.BlockSpec((None, BLK, hb), lambda g, i: (g, i, 0))],
        out_specs=[pl.BlockSpec((BLK, wide), lambda g, i: (i, g)),
                   pl.BlockSpec((s, wide), lambda g, i: (0, g)),
                   pl.BlockSpec((s, wide), lambda g, i: (0, g))],
        scratch_shapes=[pltpu.VMEM((BLK, wide), F32), pltpu.VMEM((s, wide), F32), pltpu.VMEM((s, wide), F32)],
        compiler_params=_params("parallel", "arbitrary"),
    )(proj, proj, proj, do, tot)


def _xa_probs(q, k, hd):
    z = _dot(q, k, 1, 1) * (hd ** -0.5)
    z = z - jnp.max(z, axis=-1, keepdims=True)
    e = jnp.exp(z)
    return e / jnp.sum(e, axis=-1, keepdims=True)


def _xa_fwd(name, proj, mem_kv, width):
    s = proj.shape[0]
    m = mem_kv.shape[0]
    hd = width // XA_HEADS
    tq = _tile(s, 256)
    q_first = 5 * XA_HEADS

    def body(q_ref, k_ref, v_ref, o_ref):
        p = _xa_probs(q_ref[...], k_ref[...], hd)
        o_ref[...] = _dot(p, v_ref[...], 1, 0).astype(o_ref.dtype)

    return pl.pallas_call(
        body, name=name, grid=(XA_HEADS, s // tq), out_shape=jax.ShapeDtypeStruct((s, width), BF16),
        in_specs=[pl.BlockSpec((tq, hd), lambda h, i: (i, q_first + h)),
                  pl.BlockSpec((m, hd), lambda h, i: (0, h)),
                  pl.BlockSpec((m, hd), lambda h, i: (0, XA_HEADS + h))],
        out_specs=pl.BlockSpec((tq, hd), lambda h, i: (i, h)),
        compiler_params=_params("parallel", "parallel"),
    )(proj, mem_kv, mem_kv)


def _xa_bwd(name, proj, mem_kv, do, width):
    s = proj.shape[0]
    m = mem_kv.shape[0]
    hd = width // XA_HEADS
    tq = _tile(s, 256)
    nq = s // tq
    q_first = 5 * XA_HEADS

    def body(q_ref, k_ref, v_ref, do_ref, dq_ref, dk_ref, dv_ref, dk_acc, dv_acc):
        i = pl.program_id(1)

        @pl.when(i == 0)
        def _():
            dk_acc[...] = jnp.zeros_like(dk_acc)
            dv_acc[...] = jnp.zeros_like(dv_acc)

        q, k, v, dout = q_ref[...], k_ref[...], v_ref[...], do_ref[...]
        p = _xa_probs(q, k, hd)
        dp = _dot(dout, v, 1, 1)
        dv_acc[...] += _dot(p, dout, 0, 0)
        dz = ((p * (dp - jnp.sum(dp * p, axis=-1, keepdims=True))) * (hd ** -0.5)).astype(BF16)
        dq_ref[...] = _dot(dz, k, 1, 0).astype(dq_ref.dtype)
        dk_acc[...] += _dot(dz, q, 0, 0)

        @pl.when(i == nq - 1)
        def _():
            dk_ref[...] = dk_acc[...].astype(dk_ref.dtype)
            dv_ref[...] = dv_acc[...].astype(dv_ref.dtype)

    dq, dk, dv = pl.pallas_call(
        body, name=name, grid=(XA_HEADS, nq),
        out_shape=[jax.ShapeDtypeStruct((s, width), BF16), jax.ShapeDtypeStruct((m, width), BF16),
                   jax.ShapeDtypeStruct((m, width), BF16)],
        in_specs=[pl.BlockSpec((tq, hd), lambda h, i: (i, q_first + h)),
                  pl.BlockSpec((m, hd), lambda h, i: (0, h)),
                  pl.BlockSpec((m, hd), lambda h, i: (0, XA_HEADS + h)),
                  pl.BlockSpec((tq, hd), lambda h, i: (i, h))],
        out_specs=[pl.BlockSpec((tq, hd), lambda h, i: (i, h)),
                   pl.BlockSpec((m, hd), lambda h, i: (0, h)),
                   pl.BlockSpec((m, hd), lambda h, i: (0, h))],
        scratch_shapes=[pltpu.VMEM((m, hd), F32), pltpu.VMEM((m, hd), F32)],
        compiler_params=_params("parallel", "arbitrary"),
    )(proj, mem_kv, mem_kv, do)
    return dq, dk, dv


def _gm_pointwise(u_in, v_in, g_vnorm):
    return jax.nn.gelu(u_in), _rmsnorm(jax.nn.gelu(v_in), g_vnorm)


def _gm_mask():
    row = lax.broadcasted_iota(jnp.int32, (BLK, BLK), 0)
    col = lax.broadcasted_iota(jnp.int32, (BLK, BLK), 1)
    return (col // CHUNK) <= (row // CHUNK)


def _gm_fwd(name, proj, g_vnorm, w_s, b_s, width):
    s = proj.shape[0]
    groups = width // BLK

    def body(u_ref, v_ref, g_ref, w_ref, b_ref, o_ref):
        u, vn = _gm_pointwise(u_ref[...], v_ref[...], g_ref[...])
        vn = vn.astype(BF16)
        mask = _gm_mask()
        for g in range(groups):
            cols = slice(g * BLK, (g + 1) * BLK)
            w = jnp.where(mask, w_ref[g], 0.0)
            mixed = _dot(w, vn[:, cols], 1, 0) + b_ref[g]
            o_ref[:, cols] = (u[:, cols] * mixed).astype(o_ref.dtype)

    return pl.pallas_call(
        body, name=name, grid=(s // BLK,), out_shape=jax.ShapeDtypeStruct((s, width), BF16),
        in_specs=[pl.BlockSpec((BLK, width), lambda c: (c, 3)), pl.BlockSpec((BLK, width), lambda c: (c, 4)),
                  pl.BlockSpec((1, width), lambda c: (0, 0)),
                  pl.BlockSpec((groups, BLK, BLK), lambda c: (0, 0, 0)),
                  pl.BlockSpec((groups, BLK, 1), lambda c: (0, 0, 0))],
        out_specs=pl.BlockSpec((BLK, width), lambda c: (c, 0)),
        compiler_params=_params("parallel"),
    )(proj, proj, g_vnorm, w_s, b_s)


def _gm_bwd(name, proj, do, g_vnorm, w_s, b_s, width):
    s = proj.shape[0]
    groups = width // BLK

    def body(u_ref, v_ref, do_ref, g_ref, w_ref, b_ref, du_ref, dv_ref, dw_ref, db_ref, dg_ref, dvn_buf):
        first = pl.program_id(0) == 0
        (u, vn), vjp = jax.vjp(_gm_pointwise, u_ref[...], v_ref[...], g_ref[...])
        vn16 = vn.astype(BF16)
        dout = do_ref[...]
        mask = _gm_mask()
        du_parts = []
        for g in range(groups):
            cols = slice(g * BLK, (g + 1) * BLK)
            w = jnp.where(mask, w_ref[g], 0.0)
            mixed = _dot(w, vn16[:, cols], 1, 0) + b_ref[g]
            du_parts.append(dout[:, cols] * mixed)
            dmixed = dout[:, cols] * u[:, cols]
            dw = jnp.where(mask, _dot(dmixed, vn16[:, cols], 1, 1), 0.0)
            db = jnp.sum(dmixed, axis=1, keepdims=True)
            dvn_buf[:, cols] = _dot(w, dmixed, 0, 0)

            @pl.when(first)
            def _(g=g, dw=dw, db=db):
                dw_ref[g] = dw
                db_ref[g] = db

            @pl.when(jnp.logical_not(first))
            def _(g=g, dw=dw, db=db):
                dw_ref[g] += dw
                db_ref[g] += db

        du_in, dv_in, dg = vjp((jnp.concatenate(du_parts, axis=1), dvn_buf[...]))
        du_ref[...] = du_in.astype(du_ref.dtype)
        dv_ref[...] = dv_in.astype(dv_ref.dtype)

        @pl.when(first)
        def _():
            dg_ref[...] = dg

        @pl.when(jnp.logical_not(first))
        def _():
            dg_ref[...] += dg

    shape = jax.ShapeDtypeStruct((s, width), BF16)
    return pl.pallas_call(
        body, name=name, grid=(s // BLK,),
        out_shape=[shape, shape, jax.ShapeDtypeStruct((groups, BLK, BLK), F32),
                   jax.ShapeDtypeStruct((groups, BLK, 1), F32), jax.ShapeDtypeStruct((1, width), F32)],
        in_specs=[pl.BlockSpec((BLK, width), lambda c: (c, 3)), pl.BlockSpec((BLK, width), lambda c: (c, 4)),
                  pl.BlockSpec((BLK, width), lambda c: (c, 0)),
                  pl.BlockSpec((1, width), lambda c: (0, 0)),
                  pl.BlockSpec((groups, BLK, BLK), lambda c: (0, 0, 0)),
                  pl.BlockSpec((groups, BLK, 1), lambda c: (0, 0, 0))],
        out_specs=[pl.BlockSpec((BLK, width), lambda c: (c, 0)), pl.BlockSpec((BLK, width), lambda c: (c, 0)),
                   pl.BlockSpec((groups, BLK, BLK), lambda c: (0, 0, 0)),
                   pl.BlockSpec((groups, BLK, 1), lambda c: (0, 0, 0)),
                   pl.BlockSpec((1, width), lambda c: (0, 0))],
        scratch_shapes=[pltpu.VMEM((BLK, width), F32)],
        compiler_params=_params("arbitrary"),
    )(proj, proj, do, g_vnorm, w_s, b_s)


def _shift_down(v, k):
    row = lax.broadcasted_iota(jnp.int32, v.shape, 0)
    return jnp.where(row >= k, pltpu.roll(v, k, axis=0), 0.0)


def _shift_up(v, k):
    n = v.shape[0]
    row = lax.broadcasted_iota(jnp.int32, v.shape, 0)
    return jnp.where(row < n - k, pltpu.roll(v, n - k, axis=0), 0.0)


def _conv(gate, w, b):
    return b + w[0:1] * _shift_down(gate, 2) + w[1:2] * _shift_down(gate, 1) + w[2:3] * gate


def _conv_fwd(name, up, conv_w, conv_b):
    s, f2 = up.shape
    f = f2 // 2
    tc = _tile(f, 256, 128)
    nf = f // tc

    def body(g_ref, v_ref, w_ref, b_ref, o_ref):
        o_ref[...] = (jax.nn.gelu(_conv(g_ref[...], w_ref[...], b_ref[...])) * v_ref[...]).astype(o_ref.dtype)

    return pl.pallas_call(
        body, name=name, grid=(nf,), out_shape=jax.ShapeDtypeStruct((s, f), BF16),
        in_specs=[pl.BlockSpec((s, tc), lambda j: (0, j)), pl.BlockSpec((s, tc), lambda j: (0, nf + j)),
                  pl.BlockSpec((3, tc), lambda j: (0, j)), pl.BlockSpec((1, tc), lambda j: (0, j))],
        out_specs=pl.BlockSpec((s, tc), lambda j: (0, j)),
        compiler_params=_params("parallel"),
    )(up, up, conv_w, conv_b)


def _conv_bwd(name, up, da, conv_w, conv_b):
    s, f2 = up.shape
    f = f2 // 2
    tc = _tile(f, 256, 128)
    nf = f // tc

    def body(g_ref, v_ref, da_ref, w_ref, b_ref, dg_ref, dv_ref, dw_ref, db_ref):
        gate, val, dact, w = g_ref[...], v_ref[...], da_ref[...], w_ref[...]
        act, vjp = jax.vjp(jax.nn.gelu, _conv(gate, w, b_ref[...]))
        dv_ref[...] = (dact * act).astype(dv_ref.dtype)
        dconv = vjp(dact * val)[0]
        dg_ref[...] = (w[2:3] * dconv + w[1:2] * _shift_up(dconv, 1) + w[0:1] * _shift_up(dconv, 2)
                       ).astype(dg_ref.dtype)
        dw_ref[0:1, :] = jnp.sum(dconv * _shift_down(gate, 2), axis=0, keepdims=True)
        dw_ref[1:2, :] = jnp.sum(dconv * _shift_down(gate, 1), axis=0, keepdims=True)
        dw_ref[2:3, :] = jnp.sum(dconv * gate, axis=0, keepdims=True)
        db_ref[...] = jnp.sum(dconv, axis=0, keepdims=True)

    shape = jax.ShapeDtypeStruct((s, f), BF16)
    return pl.pallas_call(
        body, name=name, grid=(nf,),
        out_shape=[shape, shape, jax.ShapeDtypeStruct((3, f), F32), jax.ShapeDtypeStruct((1, f), F32)],
        in_specs=[pl.BlockSpec((s, tc), lambda j: (0, j)), pl.BlockSpec((s, tc), lambda j: (0, nf + j)),
                  pl.BlockSpec((s, tc), lambda j: (0, j)),
                  pl.BlockSpec((3, tc), lambda j: (0, j)), pl.BlockSpec((1, tc), lambda j: (0, j))],
        out_specs=[pl.BlockSpec((s, tc), lambda j: (0, j)), pl.BlockSpec((s, tc), lambda j: (0, j)),
                   pl.BlockSpec((3, tc), lambda j: (0, j)), pl.BlockSpec((1, tc), lambda j: (0, j))],
        compiler_params=_params("parallel"),
    )(up, up, da, conv_w, conv_b)


def _adamw(w, g, m, v):
    m = ADAM_B1 * m + (1.0 - ADAM_B1) * g
    v = ADAM_B2 * v + (1.0 - ADAM_B2) * jnp.square(g)
    m_hat = m / (1.0 - ADAM_B1 ** ADAM_STEP)
    v_hat = v / (1.0 - ADAM_B2 ** ADAM_STEP)
    delta = -ADAM_LR * (m_hat / (jnp.sqrt(v_hat) + ADAM_EPS) + ADAM_WD * w)
    return delta, m, v


def _update_shard(name, layer, w, m, v, own, received, chip, previous=None):
    _, rows, cols = w.shape
    tr = _tile(rows, 256)

    def body(chip_ref, w_ref, m_ref, v_ref, own_ref, rec_ref, *rest):
        g_ref, d_ref, nm_ref, nv_ref = rest[-4:]
        g = (own_ref[...].astype(F32) + rec_ref[0].astype(F32) + rec_ref[1].astype(F32)
             + rec_ref[2].astype(F32))
        delta, new_m, new_v = _adamw(w_ref[...], g, m_ref[...], v_ref[...])
        g_ref[...] = g
        d_ref[...] = delta
        nm_ref[...] = new_m
        nv_ref[...] = new_v

    layer_spec = pl.BlockSpec((None, tr, cols), lambda i, chip_ref: (layer, i, 0))
    in_specs = [layer_spec] * 3 + [pl.BlockSpec((None, tr, cols), lambda i, chip_ref: (chip_ref[0], i, 0)),
                                   pl.BlockSpec((3, tr, cols), lambda i, chip_ref: (0, i, 0))]
    args = [chip, w, m, v, own, received]
    aliases = {}
    if previous is not None:
        in_specs += [ANY] * 4
        aliases = {len(args) + k: k for k in range(4)}
        args += list(previous)
    return pl.pallas_call(
        body, name=name, out_shape=[jax.ShapeDtypeStruct(w.shape, F32)] * 4,
        grid_spec=pltpu.PrefetchScalarGridSpec(
            num_scalar_prefetch=1, grid=(rows // tr,), in_specs=in_specs, out_specs=[layer_spec] * 4),
        input_output_aliases=aliases, compiler_params=_params("parallel"),
    )(*args)


def _sum_devices(name, gathered):
    _, rows, cols = gathered.shape

    def body(g_ref, o_ref):
        total = g_ref[0]
        for d in range(1, N_DEV):
            total = total + g_ref[d]
        o_ref[...] = total

    return pl.pallas_call(
        body, name=name, out_shape=jax.ShapeDtypeStruct((rows, cols), F32),
        in_specs=[pl.BlockSpec((N_DEV, rows, cols), lambda: (0, 0, 0))],
        out_specs=pl.BlockSpec((rows, cols), lambda: (0, 0)),
        compiler_params=pltpu.CompilerParams(vmem_limit_bytes=VMEM_LIMIT_V7X),
    )(gathered)


def _update_small(name, w, g, m, v):
    def body(w_ref, g_ref, m_ref, v_ref, d_ref, nm_ref, nv_ref):
        delta, new_m, new_v = _adamw(w_ref[...], g_ref[...], m_ref[...], v_ref[...])
        d_ref[...] = delta
        nm_ref[...] = new_m
        nv_ref[...] = new_v

    spec = pl.BlockSpec(w.shape, lambda: (0, 0))
    return pl.pallas_call(
        body, name=name, out_shape=[jax.ShapeDtypeStruct(w.shape, F32)] * 3,
        in_specs=[spec] * 4, out_specs=[spec] * 3,
        compiler_params=pltpu.CompilerParams(vmem_limit_bytes=VMEM_LIMIT_V7X),
    )(w, g, m, v)


def _pack(arrays):
    flat = jnp.concatenate([a.reshape(-1) for a in arrays])
    pad = (-flat.shape[0]) % (8 * BLK)
    return jnp.pad(flat, (0, pad)).reshape(-1, BLK)


def _unpack(packed, shapes):
    flat = packed.reshape(-1)
    out, at = [], 0
    for shape in shapes:
        size = 1
        for d in shape:
            size *= d
        out.append(flat[at:at + size].reshape(shape))
        at += size
    return out


SHARDED = ("w_in", "w_mem_kv", "w_gate", "w_br_sb", "w_br_gm", "w_br_xa", "w_out", "w_up", "w_down")
SMALL = ("g_mix_pre", "g_vnorm", "w_s", "b_s", "g_mem", "b_gate", "g_mix_post", "g_ffn_pre", "conv_w",
         "conv_b", "g_ffn_post")
WEIGHTS = ("g_mix_pre", "w_in", "g_vnorm", "w_s", "b_s", "g_mem", "w_mem_kv", "w_gate", "b_gate", "w_br_sb",
           "w_br_gm", "w_br_xa", "w_out", "g_mix_post", "g_ffn_pre", "w_up", "conv_w", "conv_b", "w_down",
           "g_ffn_post")


ROW_SHARDED = ("w_mem_kv", "w_out", "w_down")
GATHER_GROUPS = (("w_in", "w_gate", "w_mem_kv"), ("w_br_sb", "w_br_gm", "w_br_xa", "w_out"), ("w_up", "w_down"))
REDUCE_GROUPS = (("w_down", "w_up"), ("w_out", "w_br_sb", "w_br_gm", "w_br_xa", "w_gate"), ("w_mem_kv", "w_in"))


def _cast_into_place(name, w, me):
    rows, cols = w.shape
    tr = _tile(rows, 512)

    def body(me_ref, w_ref, o_ref):
        o_ref[...] = w_ref[...].astype(o_ref.dtype)

    return pl.pallas_call(
        body, name=name, out_shape=jax.ShapeDtypeStruct((N_DEV, rows, cols), BF16),
        grid_spec=pltpu.PrefetchScalarGridSpec(
            num_scalar_prefetch=1, grid=(rows // tr,),
            in_specs=[pl.BlockSpec((tr, cols), lambda i, me_ref: (i, 0))],
            out_specs=pl.BlockSpec((None, tr, cols), lambda i, me_ref: (me_ref[0], i, 0))),
        compiler_params=_params("parallel"),
    )(me, w)


class _Gather:
    def __init__(self, tag, names, places, after):
        self.tag, self.names, self.n = tag, names, len(places)
        self.sems, self.places, self.token = _split_start(
            f"gather_start_{tag}", places, _gather_first_copies(self.n), after)

    def relay(self, after):
        self.sems, self.places = _split_wait(
            f"gather_relay_{self.tag}", self.places, self.sems, _gather_first_copies(self.n), after,
            then=_gather_relay_copies(self.n))

    def finish(self, after):
        places = _split_wait(f"gather_finish_{self.tag}", self.places, self.sems,
                             _gather_relay_copies(self.n), after)
        full = dict(zip(self.names, places))
        for name in self.names:
            if name in ROW_SHARDED:
                full[name] = full[name].reshape(-1, full[name].shape[-1])
        return full


class _Reduce:
    def __init__(self, tag, names, partials):
        self.tag, self.names, self.n = tag, names, len(partials)
        blocks = [p.reshape((N_CHIP, 2, -1, p.shape[-1])) for p in partials]
        lands = [lax.empty((N_CHIP,) + b.shape[2:], b.dtype) for b in blocks]
        self.sems, self.buffers, self.token = _split_start(
            f"pair_start_{tag}", blocks + lands, _pair_copies(self.n))

    def middle(self, after, core):
        n = self.n
        got = _split_wait(f"pair_finish_{self.tag}", self.buffers, self.sems, _pair_copies(n), after)
        sums = [_pair_sum(f"pair_sum_{name}{self.tag}", got[a], got[n + a], core)
                for a, name in enumerate(self.names)]
        lands = [lax.empty((3,) + s.shape[1:], s.dtype) for s in sums]
        self.sems, self.buffers, self.token = _split_start(
            f"chip_start_{self.tag}", sums + lands, _chip_copies(n))

    def finish(self, after):
        n = self.n
        got = _split_wait(f"chip_finish_{self.tag}", self.buffers, self.sems, _chip_copies(n), after)
        return [(name, got[a], got[n + a]) for a, name in enumerate(self.names)]


def kernel(x, mem, g_mix_pre, w_in, g_vnorm, w_s, b_s, g_mem, w_mem_kv, w_gate, b_gate, w_br_sb, w_br_gm, w_br_xa, w_out, g_mix_post, g_ffn_pre, w_up, conv_w, conv_b, w_down, g_ffn_post, loss_target, m_g_mix_pre, m_w_in, m_g_vnorm, m_w_s, m_b_s, m_g_mem, m_w_mem_kv, m_w_gate, m_b_gate, m_w_br_sb, m_w_br_gm, m_w_br_xa, m_w_out, m_g_mix_post, m_g_ffn_pre, m_w_up, m_conv_w, m_conv_b, m_w_down, m_g_ffn_post, v_g_mix_pre, v_w_in, v_g_vnorm, v_w_s, v_b_s, v_g_mem, v_w_mem_kv, v_w_gate, v_b_gate, v_w_br_sb, v_w_br_gm, v_w_br_xa, v_w_out, v_g_mix_post, v_g_ffn_pre, v_w_up, v_conv_w, v_conv_b, v_w_down, v_g_ffn_post):
    p = dict(g_mix_pre=g_mix_pre, w_in=w_in, g_vnorm=g_vnorm, w_s=w_s, b_s=b_s, g_mem=g_mem, w_mem_kv=w_mem_kv,
             w_gate=w_gate, b_gate=b_gate, w_br_sb=w_br_sb, w_br_gm=w_br_gm, w_br_xa=w_br_xa, w_out=w_out,
             g_mix_post=g_mix_post, g_ffn_pre=g_ffn_pre, w_up=w_up, conv_w=conv_w, conv_b=conv_b, w_down=w_down,
             g_ffn_post=g_ffn_post)
    mom = dict(g_mix_pre=m_g_mix_pre, w_in=m_w_in, g_vnorm=m_g_vnorm, w_s=m_w_s, b_s=m_b_s, g_mem=m_g_mem,
               w_mem_kv=m_w_mem_kv, w_gate=m_w_gate, b_gate=m_b_gate, w_br_sb=m_w_br_sb, w_br_gm=m_w_br_gm,
               w_br_xa=m_w_br_xa, w_out=m_w_out, g_mix_post=m_g_mix_post, g_ffn_pre=m_g_ffn_pre, w_up=m_w_up,
               conv_w=m_conv_w, conv_b=m_conv_b, w_down=m_w_down, g_ffn_post=m_g_ffn_post)
    var = dict(g_mix_pre=v_g_mix_pre, w_in=v_w_in, g_vnorm=v_g_vnorm, w_s=v_w_s, b_s=v_b_s, g_mem=v_g_mem,
               w_mem_kv=v_w_mem_kv, w_gate=v_w_gate, b_gate=v_b_gate, w_br_sb=v_w_br_sb, w_br_gm=v_w_br_gm,
               w_br_xa=v_w_br_xa, w_out=v_w_out, g_mix_post=v_g_mix_post, g_ffn_pre=v_g_ffn_pre, w_up=v_w_up,
               conv_w=v_conv_w, conv_b=v_conv_b, w_down=v_w_down, g_ffn_post=v_g_ffn_post)
    depth = w_in.shape[0]
    x = x[0]
    mem = mem[0]
    target = loss_target[0]
    s, d = x.shape
    width = d // 2
    heads = width // BLK
    cx, cy, cc = lax.axis_index("x"), lax.axis_index("y"), lax.axis_index("c")
    core = cc.astype(jnp.int32).reshape(1)
    chip = (2 * cx + cy).astype(jnp.int32).reshape(1)
    me = 4 * cx + 2 * cy + cc
    me_index = me.astype(jnp.int32).reshape(1)

    conv_all = _all_gather("gather_conv_w", [conv_w])[0]
    conv_w_full = jnp.transpose(conv_all, (1, 2, 0, 3)).reshape(depth, 3, -1)
    gathers = {}
    token = conv_all
    for l in range(depth):
        for gi, names in enumerate(GATHER_GROUPS):
            places = [_cast_into_place(f"cast_{n}{l}", p[n][l], me_index) for n in names]
            gathers[l, gi] = _Gather(f"{l}{gi}", names, places, token)
            token = gathers[l, gi].token

    kept, gathered = [], []
    h = _norm_fwd("pre_norm0", x, g_mix_pre[0][None, :], after=token)
    gathers[0, 0].relay(h)
    for l in range(depth):
        row = lambda name: p[name][l][None, :]
        first, second, third = (gathers[l, gi] for gi in range(3))
        w = first.finish(h)
        proj, proj16 = _mm_nn_cs(f"proj{l}", h, w["w_in"], (F32, BF16))
        zg = _mm_nn_cs(f"gates{l}", h, w["w_gate"])
        mn = _norm_fwd(f"mem_norm{l}", mem, row("g_mem"))
        mem_kv = _mm_nn(f"mem_kv{l}", mn, w["w_mem_kv"], (BF16,))
        second.relay(mem_kv)
        o_sb, sb_tot = _sb_fwd(f"sb_fwd{l}", proj16, heads)
        b_s3 = b_s[l][:, :, None]
        o_gm = _gm_fwd(f"gm_fwd{l}", proj, row("g_vnorm"), w_s[l], b_s3, width)
        o_xa = _xa_fwd(f"xa_fwd{l}", proj16, mem_kv, width)
        w.update(second.finish(o_xa))
        ts = [_mm_nn_cs(f"branch_{n}{l}", o, w[f"w_br_{n}"]) for n, o in (("sb", o_sb), ("gm", o_gm), ("xa", o_xa))]
        merged = _merge_fwd(f"merge{l}", zg, ts, row("b_gate"))
        third.relay(merged)
        y1 = _mm_nn(f"out{l}", merged, w["w_out"])
        x1, h2 = _residual_norm_fwd(f"mix_post{l}", x, y1, row("g_mix_post"), row("g_ffn_pre"))
        w.update(third.finish(h2))
        up = _mm_nn_cs(f"up{l}", h2, w["w_up"])
        if l + 1 < depth:
            gathers[l + 1, 0].relay(up)
        act = _conv_fwd(f"conv_fwd{l}", up, conv_w_full[l], row("conv_b"))
        y2 = _mm_nn(f"down{l}", act, w["w_down"])
        kept.append(dict(x=x, h=h, proj=proj, proj16=proj16, zg=zg, mn=mn, mem_kv=mem_kv, o_sb=o_sb, o_gm=o_gm,
                         o_xa=o_xa, ts=ts, merged=merged, y1=y1, x1=x1, h2=h2, up=up, act=act, y2=y2, b_s=b_s3,
                         sb_tot=sb_tot))
        gathered.append(w)
        if l + 1 < depth:
            x, h = _residual_norm_fwd(f"ffn_post{l}", x1, y2, g_ffn_post[l][None, :], g_mix_pre[l + 1][None, :])

    top = kept[-1]
    dx, dy2, loss_part, dg_ffn_post = _loss_bwd("loss", top["x1"], top["y2"], g_ffn_post[depth - 1][None, :], target)
    loss = lax.psum(loss_part[0, 0], ("x", "y", "c"))
    small_grads = [dict() for _ in range(depth)]
    small_grads[depth - 1]["g_ffn_post"] = dg_ffn_post
    outs = {n: None for n in SHARDED}
    in_flight = []

    def exchange_step(new, after):
        started = [e for e in in_flight if e[1] == 1]
        summed = [e for e in in_flight if e[1] == 2]
        if new is not None:
            in_flight.append([new, 1])
            after = new.token
        for entry in started:
            entry[0].middle(after, core)
            after = entry[0].token
            entry[1] = 2
        for entry in summed:
            layer = int(entry[0].tag[0])
            for name, own, received in entry[0].finish(after):
                outs[name] = _update_shard(f"update_{name}{layer}", layer, p[name], mom[name], var[name],
                                           own, received, chip, outs[name])
            in_flight.remove(entry)
        return after

    for l in reversed(range(depth)):
        k = kept[l]
        w = gathered[l]
        sg = small_grads[l]
        row = lambda name: p[name][l][None, :]
        partials = {}
        group = lambda gi: _Reduce(f"{l}{gi}", REDUCE_GROUPS[gi], [partials[n] for n in REDUCE_GROUPS[gi]])
        partials["w_down"] = _mm_tn(f"d_w_down{l}", k["act"], dy2)
        dact = _mm_nt(f"d_act{l}", dy2, w["w_down"])
        dgate, dval, sg["conv_w"], sg["conv_b"] = _conv_bwd(f"conv_bwd{l}", k["up"], dact, conv_w_full[l],
                                                             row("conv_b"))
        dup = jnp.concatenate([dgate, dval], axis=1)
        partials["w_up"] = _mm_tn_cs(f"d_w_up{l}", k["h2"], dup)
        behind = exchange_step(group(0), partials["w_up"])
        dh2 = _mm_nt_cs(f"d_h2{l}", dup, w["w_up"], after=behind)
        dx1, dy1, sg["g_ffn_pre"], sg["g_mix_post"] = _mid_bwd(f"mid_bwd{l}", dx, dh2, k["x1"], k["y1"],
                                                                row("g_ffn_pre"), row("g_mix_post"))
        partials["w_out"] = _mm_tn(f"d_w_out{l}", k["merged"], dy1)
        dmerged = _mm_nt(f"d_merged{l}", dy1, w["w_out"])
        dt_sb, dt_gm, dt_xa, dzg, sg["b_gate"] = _merge_bwd(f"merge_bwd{l}", dmerged, k["zg"], k["ts"], row("b_gate"))
        do = {}
        for n, dt in (("sb", dt_sb), ("gm", dt_gm), ("xa", dt_xa)):
            partials[f"w_br_{n}"] = _mm_tn_cs(f"d_w_br_{n}{l}", k[f"o_{n}"], dt)
            do[n] = _mm_nt_cs(f"d_o_{n}{l}", dt, w[f"w_br_{n}"], (F32,) if n == "gm" else (BF16,))
        partials["w_gate"] = _mm_tn_cs(f"d_w_gate{l}", k["h"], dzg)
        behind = exchange_step(group(1), partials["w_gate"])
        dh_gate = _mm_nt_cs(f"d_h_gate{l}", dzg, w["w_gate"], after=behind)
        dq_xa, dk_mem, dv_mem = _xa_bwd(f"xa_bwd{l}", k["proj16"], k["mem_kv"], do["xa"], width)
        dmem_kv = jnp.concatenate([dk_mem, dv_mem], axis=1)
        partials["w_mem_kv"] = _mm_tn(f"d_w_mem_kv{l}", k["mn"], dmem_kv)
        dmn = _mm_nt(f"d_mn{l}", dmem_kv, w["w_mem_kv"])
        sg["g_mem"] = _mem_norm_bwd(f"mem_norm_bwd{l}", dmn, mem, row("g_mem"))
        du, dvg, sg["w_s"], db_s, sg["g_vnorm"] = _gm_bwd(f"gm_bwd{l}", k["proj"], do["gm"], row("g_vnorm"),
                                                          p["w_s"][l], k["b_s"], width)
        sg["b_s"] = db_s[:, :, 0]
        dq, dk, dv = _sb_bwd(f"sb_bwd{l}", k["proj16"], do["sb"], k["sb_tot"], heads)
        dproj = jnp.concatenate([dq, dk, dv, du, dvg, dq_xa], axis=1)
        partials["w_in"] = _mm_tn_cs(f"d_w_in{l}", k["h"], dproj)
        behind = exchange_step(group(2), partials["w_in"])
        dh_in = _mm_nt_cs(f"d_h_in{l}", dproj, w["w_in"], after=behind)
        if l > 0:
            below = kept[l - 1]
            dx, dy2, sg["g_mix_pre"], small_grads[l - 1]["g_ffn_post"] = _bottom_bwd(
                f"bottom_bwd{l}", dx1, dh_in, dh_gate, k["x"], row("g_mix_pre"), below["y2"],
                p["g_ffn_post"][l - 1][None, :])
        else:
            dx, sg["g_mix_pre"] = _bottom_bwd(f"bottom_bwd{l}", dx1, dh_in, dh_gate, k["x"], row("g_mix_pre"))

    behind = dx
    while in_flight:
        behind = exchange_step(None, behind)

    small_shapes = [small_grads[l][n].shape for l in range(depth) for n in SMALL]
    packed = _pack([small_grads[l][n] for l in range(depth) for n in SMALL])
    total = _sum_devices("sum_small", _all_gather("gather_small", [packed])[0])
    flat = _unpack(total, small_shapes)
    grads = {}
    for i, n in enumerate(SMALL):
        g = jnp.stack([flat[l * len(SMALL) + i] for l in range(depth)]).reshape((depth,) + p[n].shape[1:]
                                                                                 if n != "conv_w" else (depth, 3, -1))
        if n == "conv_w":
            per = conv_w.shape[2]
            g = lax.dynamic_slice_in_dim(g, me * per, per, axis=2)
        grads[n] = g
    delta, new_m, new_v = _update_small(
        "update_small", _pack([p[n] for n in SMALL]), _pack([grads[n] for n in SMALL]),
        _pack([mom[n] for n in SMALL]), _pack([var[n] for n in SMALL]))
    shapes = [p[n].shape for n in SMALL]
    for n, dl, nm, nv in zip(SMALL, _unpack(delta, shapes), _unpack(new_m, shapes), _unpack(new_v, shapes)):
        outs[n] = (grads[n], dl, nm, nv)

    result = [loss, dx[None]]
    for k in range(4):
        result += [outs[n][k] for n in WEIGHTS]
    return tuple(result)
```

```python
import functools

import jax
import jax.numpy as jnp
from jax import lax
from jax.experimental import pallas as pl
from jax.experimental.pallas import tpu as pltpu

F32 = jnp.float32
BF16 = jnp.bfloat16
N_DEV = 8
N_CHIP = 4
EPS = 1e-6
CHUNK = 64
BLK = 128
XA_HEADS = 4
SB_HEADS_PER_STEP = 8
ADAM_LR = 0.001
ADAM_B1 = 0.9
ADAM_B2 = 0.999
ADAM_EPS = 1e-08
ADAM_WD = 0.01
ADAM_STEP = 10
VMEM_LIMIT_V7X = 56 * 1024 * 1024
STREAM_BLOCK_BYTES = 4 * 1024 * 1024
MESH = pl.DeviceIdType.MESH
ANY = pl.BlockSpec(memory_space=pl.ANY)


def _params(*sem):
    return pltpu.CompilerParams(dimension_semantics=sem, vmem_limit_bytes=VMEM_LIMIT_V7X)


def _tile(n, pref, mult=8):
    best = None
    for t in range(mult, min(n, pref) + 1, mult):
        if n % t == 0:
            best = t
    return best if best is not None else n


def _bf(x):
    return x if x.dtype == BF16 else x.astype(BF16)


def _dot(a, b, ca, cb):
    return lax.dot_general(_bf(a), _bf(b), (((ca,), (cb,)), ((), ())), preferred_element_type=F32)


def _rmsnorm(x, g):
    return (x * lax.rsqrt(jnp.mean(x * x, axis=-1, keepdims=True) + EPS)) * g


def _position():
    x, y, c = lax.axis_index("x"), lax.axis_index("y"), lax.axis_index("c")
    return x, y, c


def _all_gather(name, shards):
    n = len(shards)

    def body(*refs):
        ins, outs = refs[:n], refs[n:2 * n]
        send_sems, recv_sems, local_sems = refs[2 * n:]
        x, y, c = _position()
        me = 4 * x + 2 * y + c
        sibling = (x, y, 1 - c)
        chips = [(1 - x, y), (x, 1 - y), (1 - x, 1 - y)]

        def copy(a, k, block, to, src=None):
            dst = outs[a].at[block]
            return pltpu.make_async_remote_copy(
                src_ref=dst if src is None else src, dst_ref=dst, send_sem=send_sems.at[a, k],
                recv_sem=recv_sems.at[a, k], device_id=to, device_id_type=MESH)

        local = [pltpu.make_async_copy(ins[a], outs[a].at[me], local_sems.at[a]) for a in range(n)]
        for cp in local:
            cp.start()
        first = []
        for a in range(n):
            first.append(copy(a, 0, me, sibling, src=ins[a]))
            for j, chip in enumerate(chips):
                first.append(copy(a, 1 + j, me, (*chip, c), src=ins[a]))
        for cp in first:
            cp.start()
        passed = []
        for a in range(n):
            for j, (px, py) in enumerate(chips):
                block = 4 * px + 2 * py + c
                copy(a, 1 + j, block, sibling).wait_recv()
                forward = copy(a, 4 + j, block, sibling)
                forward.start()
                passed.append(forward)
        for a in range(n):
            copy(a, 0, 4 * x + 2 * y + (1 - c), sibling).wait_recv()
            for j, (px, py) in enumerate(chips):
                copy(a, 4 + j, 4 * px + 2 * py + (1 - c), sibling).wait_recv()
        for cp in first + passed:
            cp.wait_send()
        for cp in local:
            cp.wait()

    return pl.pallas_call(
        body, name=name,
        out_shape=[jax.ShapeDtypeStruct((N_DEV,) + s.shape, s.dtype) for s in shards],
        in_specs=[ANY] * n, out_specs=[ANY] * n,
        scratch_shapes=[pltpu.SemaphoreType.DMA((n, 7)), pltpu.SemaphoreType.DMA((n, 7)),
                        pltpu.SemaphoreType.DMA((n,))],
    )(*shards)


HBM = pl.BlockSpec(memory_space=pltpu.HBM)
SEM = pl.BlockSpec(memory_space=pltpu.SEMAPHORE)
DATAFLOW = pltpu.SideEffectType.DATAFLOW_SIDE_EFFECTING


def _remote(src, dst, send, recv, k, peer):
    return pltpu.make_async_remote_copy(src_ref=src, dst_ref=dst, send_sem=send.at[k], recv_sem=recv.at[k],
                                        device_id=peer, device_id_type=MESH)


def _gather_first_copies(n):
    def build(refs, send, recv):
        x, y, c = _position()
        me = 4 * x + 2 * y + c
        peers = [(x, y, 1 - c), (1 - x, y, c), (x, 1 - y, c), (1 - x, 1 - y, c)]
        return [_remote(refs[a].at[me], refs[a].at[me], send, recv, 4 * a + k, peer)
                for a in range(n) for k, peer in enumerate(peers)]
    return build, 4 * n


def _gather_relay_copies(n):
    def build(refs, send, recv):
        x, y, c = _position()
        blocks = [4 * px + 2 * py + c for px, py in ((1 - x, y), (x, 1 - y), (1 - x, 1 - y))]
        return [_remote(refs[a].at[blk], refs[a].at[blk], send, recv, 3 * a + j, (x, y, 1 - c))
                for a in range(n) for j, blk in enumerate(blocks)]
    return build, 3 * n


def _pair_copies(n):
    def build(refs, send, recv):
        x, y, c = _position()
        return [_remote(refs[a].at[k, 1 - c], refs[n + a].at[k], send, recv, N_CHIP * a + k, (x, y, 1 - c))
                for a in range(n) for k in range(N_CHIP)]
    return build, N_CHIP * n


def _chip_copies(n):
    def build(refs, send, recv):
        x, y, c = _position()
        chips = ((1 - x, y), (x, 1 - y), (1 - x, 1 - y))
        return [_remote(refs[a].at[2 * px + py], refs[n + a].at[j], send, recv, 3 * a + j, (px, py, c))
                for a in range(n) for j, (px, py) in enumerate(chips)]
    return build, 3 * n


def _split_start(name, buffers, copies, after=None):
    build, n_copies = copies
    nb = len(buffers)
    extra = [] if after is None else [after]

    def body(*refs):
        at = nb + len(extra)
        for cp in build(refs[:nb], refs[at], refs[at + 1]):
            cp.start()
        token = refs[at + 2 + nb]
        token[...] = jnp.zeros_like(token)

    outs = pl.pallas_call(
        body, name=name,
        out_shape=[pltpu.SemaphoreType.DMA((n_copies,)), pltpu.SemaphoreType.DMA((n_copies,))]
        + [pltpu.HBM(b.shape, b.dtype) for b in buffers] + [jax.ShapeDtypeStruct((8, BLK), F32)],
        in_specs=[HBM] * nb + [ANY] * len(extra),
        out_specs=[SEM, SEM] + [HBM] * nb + [pl.BlockSpec(memory_space=pltpu.VMEM)],
        input_output_aliases={i: 2 + i for i in range(nb)},
        compiler_params=pltpu.CompilerParams(has_side_effects=DATAFLOW),
    )(*[pltpu.with_memory_space_constraint(b, pltpu.HBM) for b in buffers], *extra)
    return (outs[0], outs[1]), list(outs[2:2 + nb]), outs[-1]


def _split_wait(name, buffers, sems, copies, after, then=None):
    build, _ = copies
    nb = len(buffers)
    n_next = 0 if then is None else then[1]
    first_out = nb + 3

    def body(*refs):
        for cp in build(refs[:nb], refs[nb], refs[nb + 1]):
            cp.wait()
        if then is not None:
            for cp in then[0](refs[:nb], refs[first_out], refs[first_out + 1]):
                cp.start()

    sem_shapes = [] if then is None else [pltpu.SemaphoreType.DMA((n_next,))] * 2
    outs = pl.pallas_call(
        body, name=name,
        out_shape=sem_shapes + [pltpu.HBM(b.shape, b.dtype) for b in buffers],
        in_specs=[HBM] * nb + [SEM, SEM, ANY],
        out_specs=[SEM] * len(sem_shapes) + [HBM] * nb,
        input_output_aliases={i: len(sem_shapes) + i for i in range(nb)},
        compiler_params=pltpu.CompilerParams(has_side_effects=DATAFLOW),
    )(*buffers, sems[0], sems[1], after)
    if then is None:
        return list(outs)
    return (outs[0], outs[1]), list(outs[2:])


def _pair_sum(name, partial, received, core):
    _, _, rows, cols = partial.shape
    tr = _tile(rows, max(8, STREAM_BLOCK_BYTES // (2 * cols)))

    def body(core_ref, p_ref, r_ref, o_ref):
        o_ref[...] = (p_ref[...].astype(F32) + r_ref[...].astype(F32)).astype(o_ref.dtype)

    return pl.pallas_call(
        body, name=name, out_shape=jax.ShapeDtypeStruct((N_CHIP, rows, cols), BF16),
        grid_spec=pltpu.PrefetchScalarGridSpec(
            num_scalar_prefetch=1, grid=(N_CHIP, rows // tr),
            in_specs=[pl.BlockSpec((None, None, tr, cols), lambda k, i, core: (k, core[0], i, 0)),
                      pl.BlockSpec((None, tr, cols), lambda k, i, core: (k, i, 0))],
            out_specs=pl.BlockSpec((None, tr, cols), lambda k, i, core: (k, i, 0))),
        compiler_params=_params("parallel", "parallel"),
    )(core, partial, received)


def _mm(name, a, b, out_shape, out_dtypes, grid, a_spec, b_spec, o_spec, ca, cb, k_steps=1, after=None):
    n_out = len(out_dtypes)
    extra = [] if after is None else [after]

    def body(a_ref, b_ref, *rest):
        rest = rest[len(extra):]
        o_refs = rest[:n_out]
        part = _dot(a_ref[...], b_ref[...], ca, cb)
        if k_steps == 1:
            for o in o_refs:
                o[...] = part.astype(o.dtype)
            return
        acc = rest[n_out]
        k = pl.program_id(len(grid) - 1)

        @pl.when(k == 0)
        def _():
            acc[...] = part

        @pl.when(k > 0)
        def _():
            acc[...] += part

        @pl.when(k == k_steps - 1)
        def _():
            for o in o_refs:
                o[...] = acc[...].astype(o.dtype)

    o_block = tuple(d for d in o_spec.block_shape if d is not None)
    sem = ("parallel",) * (len(grid) - 1) + (("arbitrary",) if k_steps > 1 else ("parallel",))
    out = pl.pallas_call(
        body, name=name, grid=grid,
        out_shape=[jax.ShapeDtypeStruct(out_shape, d) for d in out_dtypes],
        in_specs=[a_spec, b_spec] + [ANY] * len(extra), out_specs=[o_spec] * n_out,
        scratch_shapes=[pltpu.VMEM(o_block, F32)] if k_steps > 1 else [],
        compiler_params=_params(*sem),
    )(a, b, *extra)
    return out if n_out > 1 else out[0]


def _mm_nn_cs(name, a, w, out_dtypes=(F32,), tm=512, **kw):
    m, k = a.shape
    _, _, ns = w.shape
    tm = _tile(m, tm)
    return _mm(name, a, w, (m, N_DEV * ns), out_dtypes, (m // tm, N_DEV),
               pl.BlockSpec((tm, k), lambda i, j: (i, 0)),
               pl.BlockSpec((None, k, ns), lambda i, j: (j, 0, 0)),
               pl.BlockSpec((tm, ns), lambda i, j: (i, j)), 1, 0, **kw)


def _mm_nn(name, a, w, out_dtypes=(F32,), tm=512, tn=512, **kw):
    m, k = a.shape
    _, n = w.shape
    tm, tn = _tile(m, tm), _tile(n, tn, 128)
    return _mm(name, a, w, (m, n), out_dtypes, (m // tm, n // tn),
               pl.BlockSpec((tm, k), lambda i, j: (i, 0)),
               pl.BlockSpec((k, tn), lambda i, j: (0, j)),
               pl.BlockSpec((tm, tn), lambda i, j: (i, j)), 1, 0, **kw)


def _mm_nt_cs(name, a, w, out_dtypes=(F32,), tm=512, **kw):
    m, _ = a.shape
    _, k, ns = w.shape
    tm = _tile(m, tm)
    return _mm(name, a, w, (m, k), out_dtypes, (m // tm, N_DEV),
               pl.BlockSpec((tm, ns), lambda i, j: (i, j)),
               pl.BlockSpec((None, k, ns), lambda i, j: (j, 0, 0)),
               pl.BlockSpec((tm, k), lambda i, j: (i, 0)), 1, 1, k_steps=N_DEV, **kw)


def _mm_nt_cs_whole(name, a, w, out_dtype, tm=512):
    m, _ = a.shape
    _, k, ns = w.shape
    tm = _tile(m, tm)

    def body(a_ref, w_ref, o_ref):
        acc = _dot(a_ref[:, 0:ns], w_ref[0], 1, 1)
        for j in range(1, N_DEV):
            acc = acc + _dot(a_ref[:, j * ns:(j + 1) * ns], w_ref[j], 1, 1)
        o_ref[...] = acc.astype(o_ref.dtype)

    return pl.pallas_call(
        body, name=name, grid=(m // tm,), out_shape=jax.ShapeDtypeStruct((m, k), out_dtype),
        in_specs=[pl.BlockSpec((tm, N_DEV * ns), lambda i: (i, 0)),
                  pl.BlockSpec((N_DEV, k, ns), lambda i: (0, 0, 0))],
        out_specs=pl.BlockSpec((tm, k), lambda i: (i, 0)),
        compiler_params=_params("parallel"),
    )(a, w)


def _mm_nt(name, a, w, out_dtypes=(F32,), tm=512, tn=512, **kw):
    m, k = a.shape
    n, _ = w.shape
    tm, tn = _tile(m, tm), _tile(n, tn, 128)
    return _mm(name, a, w, (m, n), out_dtypes, (m // tm, n // tn),
               pl.BlockSpec((tm, k), lambda i, j: (i, 0)),
               pl.BlockSpec((tn, k), lambda i, j: (j, 0)),
               pl.BlockSpec((tm, tn), lambda i, j: (i, j)), 1, 1, **kw)


def _mm_tn_cs(name, a, g, tk=512, **kw):
    s, k = a.shape
    ns = g.shape[1] // N_DEV
    tk = _tile(k, tk, 128)
    return _mm(name, a, g, (N_DEV, k, ns), (BF16,), (k // tk, N_DEV),
               pl.BlockSpec((s, tk), lambda i, j: (0, i)),
               pl.BlockSpec((s, ns), lambda i, j: (0, j)),
               pl.BlockSpec((None, tk, ns), lambda i, j: (j, i, 0)), 0, 0, **kw)


def _mm_tn(name, a, g, tk=512, tn=1024, **kw):
    s, k = a.shape
    n = g.shape[1]
    tk, tn = _tile(k, tk, 128), _tile(n, tn, 128)
    return _mm(name, a, g, (k, n), (BF16,), (k // tk, n // tn),
               pl.BlockSpec((s, tk), lambda i, j: (0, i)),
               pl.BlockSpec((s, tn), lambda i, j: (0, j)),
               pl.BlockSpec((tk, tn), lambda i, j: (i, j)), 0, 0, **kw)


def _rowwise(name, fn, rows, bcast, outs, reds, tm, after=None):
    n_rows = rows[0][0].shape[0]
    tm = _tile(n_rows, tm)
    n_in, n_b, n_o, n_r = len(rows), len(bcast), len(outs), len(reds)
    extra = [] if after is None else [after]

    def body(*refs):
        ins = refs[:n_in + n_b]
        refs = refs[len(extra):]
        o_refs = refs[n_in + n_b:n_in + n_b + n_o]
        r_refs = refs[n_in + n_b + n_o:]
        vals = fn(*[r[...] for r in ins])
        for o, v in zip(o_refs, vals[:n_o]):
            o[...] = v.astype(o.dtype)
        first = pl.program_id(0) == 0
        for r, v in zip(r_refs, vals[n_o:]):
            @pl.when(first)
            def _(r=r, v=v):
                r[...] = v.astype(r.dtype)

            @pl.when(jnp.logical_not(first))
            def _(r=r, v=v):
                r[...] += v.astype(r.dtype)

    def whole(shape):
        zeros = (0,) * len(shape)
        return pl.BlockSpec(shape, lambda i: zeros)

    in_specs = [pl.BlockSpec((tm, w), functools.partial(lambda i, cb: (i, cb), cb=cb)) for _, w, cb in rows]
    in_specs += [whole(b.shape) for b in bcast] + [ANY] * len(extra)
    out_specs = [pl.BlockSpec((tm, w), lambda i: (i, 0)) for w, _ in outs]
    out_specs += [whole(s) for s, _ in reds]
    out_shape = [jax.ShapeDtypeStruct((n_rows, w), d) for w, d in outs]
    out_shape += [jax.ShapeDtypeStruct(s, d) for s, d in reds]
    return pl.pallas_call(
        body, name=name, grid=(n_rows // tm,), out_shape=out_shape, in_specs=in_specs,
        out_specs=out_specs, compiler_params=_params("arbitrary" if n_r else "parallel"),
    )(*[r[0] for r in rows], *bcast, *extra)


def _norm_fwd(name, x, g, after=None):
    d = x.shape[1]
    return _rowwise(name, lambda xt, gt: (_rmsnorm(xt, gt),), [(x, d, 0)], [g], [(d, BF16)], [], 256,
                    after=after)[0]


def _residual_norm_fwd(name, x, y, g_post, g_next):
    d = x.shape[1]

    def fn(xt, yt, gp, gn):
        new = xt + _rmsnorm(yt, gp)
        return new, _rmsnorm(new, gn)

    return _rowwise(name, fn, [(x, d, 0), (y, d, 0)], [g_post, g_next], [(d, F32), (d, BF16)], [], 256)


def _merge(z0, z1, z2, t0, t1, t2, b0, b1, b2):
    return (jax.nn.sigmoid(z0 + b0) * t0 + jax.nn.sigmoid(z1 + b1) * t1 + jax.nn.sigmoid(z2 + b2) * t2)


def _merge_fwd(name, zg, ts, b_gate):
    d = ts[0].shape[1]
    rows = [(zg, d, b) for b in range(3)] + [(t, d, 0) for t in ts]
    bias = [b_gate[:, b * d:(b + 1) * d] for b in range(3)]
    return _rowwise(name, lambda *v: (_merge(*v),), rows, bias, [(d, BF16)], [], 128)[0]


def _merge_bwd(name, dmerged, zg, ts, b_gate):
    d = ts[0].shape[1]
    rows = [(dmerged, d, 0)] + [(zg, d, b) for b in range(3)] + [(t, d, 0) for t in ts]
    bias = [b_gate[:, b * d:(b + 1) * d] for b in range(3)]

    def fn(dm, *v):
        _, vjp = jax.vjp(_merge, *v)
        dz0, dz1, dz2, dt0, dt1, dt2, db0, db1, db2 = vjp(dm)
        return (dt0, dt1, dt2, jnp.concatenate([dz0, dz1, dz2], axis=1),
                jnp.concatenate([db0, db1, db2], axis=1))

    return _rowwise(name, fn, rows, bias, [(d, BF16)] * 3 + [(3 * d, BF16)], [((1, 3 * d), F32)], 128)


def _loss_bwd(name, x, y, g_post, target):
    d = x.shape[1]

    def loss(xt, yt, gp, tt):
        err = xt + _rmsnorm(yt, gp) - tt
        return 0.5 * jnp.sum(jnp.mean(err * err, axis=-1))

    def fn(xt, yt, tt, gp):
        val, (dx, dy, dg) = jax.value_and_grad(loss, argnums=(0, 1, 2))(xt, yt, gp, tt)
        return dx, dy, jnp.full((1, BLK), val, F32), dg

    return _rowwise(name, fn, [(x, d, 0), (y, d, 0), (target, d, 0)], [g_post],
                    [(d, F32), (d, BF16)], [((1, BLK), F32), ((1, d), F32)], 256)


def _mid_bwd(name, dx_out, dh, x_mid, y, g_pre, g_post):
    d = dx_out.shape[1]

    def fn(dxo, dht, xm, yt, gpre, gpost):
        _, vjp_pre = jax.vjp(_rmsnorm, xm, gpre)
        dxm, dgpre = vjp_pre(dht)
        dxm = dxo + dxm
        _, vjp_post = jax.vjp(_rmsnorm, yt, gpost)
        dy, dgpost = vjp_post(dxm)
        return dxm, dy, dgpre, dgpost

    return _rowwise(name, fn, [(dx_out, d, 0), (dh, d, 0), (x_mid, d, 0), (y, d, 0)], [g_pre, g_post],
                    [(d, F32), (d, BF16)], [((1, d), F32), ((1, d), F32)], 256)


def _bottom_bwd(name, dx_mid, dh_a, dh_b, x, g_pre, y_prev=None, g_post_prev=None):
    d = dx_mid.shape[1]
    rows = [(dx_mid, d, 0), (dh_a, d, 0), (dh_b, d, 0), (x, d, 0)]
    if y_prev is None:
        def fn(dxm, da, db, xt, gpre):
            _, vjp_pre = jax.vjp(_rmsnorm, xt, gpre)
            dx, dgpre = vjp_pre(da + db)
            return dxm + dx, dgpre

        return _rowwise(name, fn, rows, [g_pre], [(d, F32)], [((1, d), F32)], 256)

    def fn2(dxm, da, db, xt, yt, gpre, gpost):
        _, vjp_pre = jax.vjp(_rmsnorm, xt, gpre)
        dx, dgpre = vjp_pre(da + db)
        dx = dxm + dx
        _, vjp_post = jax.vjp(_rmsnorm, yt, gpost)
        dy, dgpost = vjp_post(dx)
        return dx, dy, dgpre, dgpost

    return _rowwise(name, fn2, rows + [(y_prev, d, 0)], [g_pre, g_post_prev],
                    [(d, F32), (d, BF16)], [((1, d), F32), ((1, d), F32)], 256)


def _mem_norm_bwd(name, dmn, mem, g_mem):
    d = mem.shape[1]

    def fn(dt, mt, g):
        _, vjp = jax.vjp(_rmsnorm, mt, g)
        return (vjp(dt)[1],)

    return _rowwise(name, fn, [(dmn, d, 0), (mem, d, 0)], [g_mem], [], [((1, d), F32)], 256)[0]


def _split_dot(v, tri):
    hi = v.astype(BF16)
    lo = (v - hi.astype(F32)).astype(BF16)
    return (jnp.dot(hi, tri, preferred_element_type=F32) + jnp.dot(lo, tri, preferred_element_type=F32))


def _sb_scores(q, kb, strict=None):
    z = _dot(q, kb, 1, 1) * (BLK ** -0.5)
    soft = jnp.log1p(jnp.exp(-jnp.abs(z)))
    log_beta = jnp.minimum(z, 0.0) - soft
    log_keep = jnp.minimum(-z, 0.0) - soft
    if strict is not None:
        log_keep = jnp.where(strict, log_keep, 0.0)
    return z, log_beta, log_keep


def _block_iotas():
    return (lax.broadcasted_iota(jnp.int32, (BLK, BLK), 0), lax.broadcasted_iota(jnp.int32, (BLK, BLK), 1))


def _sb_fwd(name, proj, heads):
    s = proj.shape[0]
    nq = s // BLK
    hb = _tile(heads, SB_HEADS_PER_STEP, 1)
    groups = heads // hb
    wide = hb * BLK

    def body(q_ref, k_ref, v_ref, o_ref, tot_ref, acc_ref):
        i = pl.program_id(1)
        row, col = _block_iotas()
        diag = col < row
        later = (row > col).astype(BF16)
        head_cols = [slice(hh * BLK, (hh + 1) * BLK) for hh in range(hb)]

        def tiles(off, runs, strict):
            scores = [_sb_scores(q_ref[:, c], k_ref[pl.ds(off, BLK), c], strict) for c in head_cols]
            suffixes = [_split_dot(sc[2], later) for sc in scores]
            new = []
            for hh, c in enumerate(head_cols):
                a = jnp.exp(scores[hh][1] + suffixes[hh] + runs[hh])
                if strict is not None:
                    a = jnp.where(strict, a, 0.0)
                acc_ref[:, c] += _dot(a, v_ref[pl.ds(off, BLK), c], 1, 0)
                new.append(runs[hh] + jnp.sum(scores[hh][2], axis=1, keepdims=True))
            return tuple(new)

        acc_ref[...] = jnp.zeros_like(acc_ref)
        runs = tiles(pl.multiple_of(i * BLK, BLK), (jnp.zeros((BLK, 1), F32),) * hb, diag)
        runs = lax.fori_loop(0, i, lambda t, r: tiles(pl.multiple_of((i - 1 - t) * BLK, BLK), r, None), runs)
        o_ref[...] = acc_ref[...].astype(o_ref.dtype)
        for hh in range(hb):
            tot_ref[:, hh:hh + 1] = runs[hh]

    return pl.pallas_call(
        body, name=name, grid=(groups, nq),
        out_shape=[jax.ShapeDtypeStruct((s, heads * BLK), BF16), jax.ShapeDtypeStruct((groups, s, hb), F32)],
        in_specs=[pl.BlockSpec((BLK, wide), lambda g, i: (i, g)),
                  pl.BlockSpec((s, wide), lambda g, i: (0, groups + g)),
                  pl.BlockSpec((s, wide), lambda g, i: (0, 2 * groups + g))],
        out_specs=[pl.BlockSpec((BLK, wide), lambda g, i: (i, g)),
                   pl.BlockSpec((None, BLK, hb), lambda g, i: (g, i, 0))],
        scratch_shapes=[pltpu.VMEM((BLK, wide), F32)],
        compiler_params=_params("parallel", "arbitrary"),
    )(proj, proj, proj)


def _sb_bwd(name, proj, do, tot, heads):
    s = proj.shape[0]
    nq = s // BLK
    groups, _, hb = tot.shape
    wide = hb * BLK

    def body(q_ref, k_ref, v_ref, do_ref, tot_ref, dq_ref, dk_ref, dv_ref, dq_acc, dk_acc, dv_acc):
        i = pl.program_id(1)
        row, col = _block_iotas()
        diag = col < row
        upto = (row <= col).astype(BF16)
        earlier = (row < col).astype(BF16)

        @pl.when(i == 0)
        def _():
            dk_acc[...] = jnp.zeros_like(dk_acc)
            dv_acc[...] = jnp.zeros_like(dv_acc)

        dq_acc[...] = jnp.zeros_like(dq_acc)

        head_cols = [slice(hh * BLK, (hh + 1) * BLK) for hh in range(hb)]

        def tiles(off, carry, strict):
            rows = pl.ds(off, BLK)
            scores = [_sb_scores(q_ref[:, c], k_ref[rows, c], strict) for c in head_cols]
            das = [_dot(do_ref[:, c], v_ref[rows, c], 1, 1) for c in head_cols]
            prefixes = [_split_dot(sc[2], upto) for sc in scores]
            dlas = []
            for hh, c in enumerate(head_cols):
                suffix = tot_ref[:, hh:hh + 1] - (prefixes[hh] + carry[2 * hh])
                a = jnp.exp(scores[hh][1] + suffix)
                if strict is not None:
                    a = jnp.where(strict, a, 0.0)
                dlas.append(das[hh] * a)
                dv_acc[rows, c] += _dot(a, do_ref[:, c], 0, 0)
            dkeeps = [_split_dot(dla, earlier) for dla in dlas]
            new = ()
            for hh, c in enumerate(head_cols):
                dkeep = dkeeps[hh] + carry[2 * hh + 1]
                if strict is not None:
                    dkeep = jnp.where(strict, dkeep, 0.0)
                sig = jax.nn.sigmoid(scores[hh][0])
                dz = ((dlas[hh] * (1.0 - sig) - dkeep * sig) * (BLK ** -0.5)).astype(BF16)
                dk_acc[rows, c] += _dot(dz, q_ref[:, c], 0, 0)
                dq_acc[:, c] += _dot(dz, k_ref[rows, c], 1, 0)
                new += (carry[2 * hh] + jnp.sum(scores[hh][2], axis=1, keepdims=True),
                        carry[2 * hh + 1] + jnp.sum(dlas[hh], axis=1, keepdims=True))
            return new

        carry = lax.fori_loop(0, i, lambda jj, cr: tiles(pl.multiple_of(jj * BLK, BLK), cr, None),
                              (jnp.zeros((BLK, 1), F32),) * (2 * hb))
        tiles(pl.multiple_of(i * BLK, BLK), carry, diag)
        dq_ref[...] = dq_acc[...].astype(dq_ref.dtype)

        @pl.when(i == nq - 1)
        def _():
            dk_ref[...] = dk_acc[...].astype(dk_ref.dtype)
            dv_ref[...] = dv_acc[...].astype(dv_ref.dtype)

    shape = jax.ShapeDtypeStruct((s, heads * BLK), BF16)
    return pl.pallas_call(
        body, name=name, grid=(groups, nq), out_shape=[shape, shape, shape],
        in_specs=[pl.BlockSpec((BLK, wide), lambda g, i: (i, g)),
                  pl.BlockSpec((s, wide), lambda g, i: (0, groups + g)),
                  pl.BlockSpec((s, wide), lambda g, i: (0, 2 * groups + g)),
                  pl.BlockSpec((BLK, wide), lambda g, i: (i, g)),
                  pl.BlockSpec((None, BLK, hb), lambda g, i: (g, i, 0))],
        out_specs=[pl.BlockSpec((BLK, wide), lambda g, i: (i, g)),
                   pl.BlockSpec((s, wide), lambda g, i: (0, g)),
                   pl.BlockSpec((s, wide), lambda g, i: (0, g))],
        scratch_shapes=[pltpu.VMEM((BLK, wide), F32), pltpu.VMEM((s, wide), F32), pltpu.VMEM((s, wide), F32)],
        compiler_params=_params("parallel", "arbitrary"),
    )(proj, proj, proj, do, tot)


def _xa_probs(q, k, hd):
    z = _dot(q, k, 1, 1) * (hd ** -0.5)
    z = z - jnp.max(z, axis=-1, keepdims=True)
    e = jnp.exp(z)
    return e / jnp.sum(e, axis=-1, keepdims=True)


def _xa_fwd(name, proj, mem_kv, width):
    s = proj.shape[0]
    m = mem_kv.shape[0]
    hd = width // XA_HEADS
    tq = _tile(s, 256)
    q_first = 5 * XA_HEADS

    def body(q_ref, k_ref, v_ref, o_ref):
        p = _xa_probs(q_ref[...], k_ref[...], hd)
        o_ref[...] = _dot(p, v_ref[...], 1, 0).astype(o_ref.dtype)

    return pl.pallas_call(
        body, name=name, grid=(XA_HEADS, s // tq), out_shape=jax.ShapeDtypeStruct((s, width), BF16),
        in_specs=[pl.BlockSpec((tq, hd), lambda h, i: (i, q_first + h)),
                  pl.BlockSpec((m, hd), lambda h, i: (0, h)),
                  pl.BlockSpec((m, hd), lambda h, i: (0, XA_HEADS + h))],
        out_specs=pl.BlockSpec((tq, hd), lambda h, i: (i, h)),
        compiler_params=_params("parallel", "parallel"),
    )(proj, mem_kv, mem_kv)


def _xa_bwd(name, proj, mem_kv, do, width):
    s = proj.shape[0]
    m = mem_kv.shape[0]
    hd = width // XA_HEADS
    tq = _tile(s, 256)
    nq = s // tq
    q_first = 5 * XA_HEADS

    def body(q_ref, k_ref, v_ref, do_ref, dq_ref, dk_ref, dv_ref, dk_acc, dv_acc):
        i = pl.program_id(1)

        @pl.when(i == 0)
        def _():
            dk_acc[...] = jnp.zeros_like(dk_acc)
            dv_acc[...] = jnp.zeros_like(dv_acc)

        q, k, v, dout = q_ref[...], k_ref[...], v_ref[...], do_ref[...]
        p = _xa_probs(q, k, hd)
        dp = _dot(dout, v, 1, 1)
        dv_acc[...] += _dot(p, dout, 0, 0)
        dz = ((p * (dp - jnp.sum(dp * p, axis=-1, keepdims=True))) * (hd ** -0.5)).astype(BF16)
        dq_ref[...] = _dot(dz, k, 1, 0).astype(dq_ref.dtype)
        dk_acc[...] += _dot(dz, q, 0, 0)

        @pl.when(i == nq - 1)
        def _():
            dk_ref[...] = dk_acc[...].astype(dk_ref.dtype)
            dv_ref[...] = dv_acc[...].astype(dv_ref.dtype)

    dq, dk, dv = pl.pallas_call(
        body, name=name, grid=(XA_HEADS, nq),
        out_shape=[jax.ShapeDtypeStruct((s, width), BF16), jax.ShapeDtypeStruct((m, width), BF16),
                   jax.ShapeDtypeStruct((m, width), BF16)],
        in_specs=[pl.BlockSpec((tq, hd), lambda h, i: (i, q_first + h)),
                  pl.BlockSpec((m, hd), lambda h, i: (0, h)),
                  pl.BlockSpec((m, hd), lambda h, i: (0, XA_HEADS + h)),
                  pl.BlockSpec((tq, hd), lambda h, i: (i, h))],
        out_specs=[pl.BlockSpec((tq, hd), lambda h, i: (i, h)),
                   pl.BlockSpec((m, hd), lambda h, i: (0, h)),
                   pl.BlockSpec((m, hd), lambda h, i: (0, h))],
        scratch_shapes=[pltpu.VMEM((m, hd), F32), pltpu.VMEM((m, hd), F32)],
        compiler_params=_params("parallel", "arbitrary"),
    )(proj, mem_kv, mem_kv, do)
    return dq, dk, dv


def _gm_pointwise(u_in, v_in, g_vnorm):
    return jax.nn.gelu(u_in), _rmsnorm(jax.nn.gelu(v_in), g_vnorm)


def _gm_mask():
    row = lax.broadcasted_iota(jnp.int32, (BLK, BLK), 0)
    col = lax.broadcasted_iota(jnp.int32, (BLK, BLK), 1)
    return (col // CHUNK) <= (row // CHUNK)


def _gm_fwd(name, proj, g_vnorm, w_s, b_s, width):
    s = proj.shape[0]
    groups = width // BLK

    def body(u_ref, v_ref, g_ref, w_ref, b_ref, o_ref):
        u, vn = _gm_pointwise(u_ref[...], v_ref[...], g_ref[...])
        vn = vn.astype(BF16)
        mask = _gm_mask()
        for g in range(groups):
            cols = slice(g * BLK, (g + 1) * BLK)
            w = jnp.where(mask, w_ref[g], 0.0)
            mixed = _dot(w, vn[:, cols], 1, 0) + b_ref[g]
            o_ref[:, cols] = (u[:, cols] * mixed).astype(o_ref.dtype)

    return pl.pallas_call(
        body, name=name, grid=(s // BLK,), out_shape=jax.ShapeDtypeStruct((s, width), BF16),
        in_specs=[pl.BlockSpec((BLK, width), lambda c: (c, 3)), pl.BlockSpec((BLK, width), lambda c: (c, 4)),
                  pl.BlockSpec((1, width), lambda c: (0, 0)),
                  pl.BlockSpec((groups, BLK, BLK), lambda c: (0, 0, 0)),
                  pl.BlockSpec((groups, BLK, 1), lambda c: (0, 0, 0))],
        out_specs=pl.BlockSpec((BLK, width), lambda c: (c, 0)),
        compiler_params=_params("parallel"),
    )(proj, proj, g_vnorm, w_s, b_s)


def _gm_bwd(name, proj, do, g_vnorm, w_s, b_s, width):
    s = proj.shape[0]
    groups = width // BLK

    def body(u_ref, v_ref, do_ref, g_ref, w_ref, b_ref, du_ref, dv_ref, dw_ref, db_ref, dg_ref, dvn_buf):
        first = pl.program_id(0) == 0
        (u, vn), vjp = jax.vjp(_gm_pointwise, u_ref[...], v_ref[...], g_ref[...])
        vn16 = vn.astype(BF16)
        dout = do_ref[...]
        mask = _gm_mask()
        du_parts = []
        for g in range(groups):
            cols = slice(g * BLK, (g + 1) * BLK)
            w = jnp.where(mask, w_ref[g], 0.0)
            mixed = _dot(w, vn16[:, cols], 1, 0) + b_ref[g]
            du_parts.append(dout[:, cols] * mixed)
            dmixed = dout[:, cols] * u[:, cols]
            dw = jnp.where(mask, _dot(dmixed, vn16[:, cols], 1, 1), 0.0)
            db = jnp.sum(dmixed, axis=1, keepdims=True)
            dvn_buf[:, cols] = _dot(w, dmixed, 0, 0)

            @pl.when(first)
            def _(g=g, dw=dw, db=db):
                dw_ref[g] = dw
                db_ref[g] = db

            @pl.when(jnp.logical_not(first))
            def _(g=g, dw=dw, db=db):
                dw_ref[g] += dw
                db_ref[g] += db

        du_in, dv_in, dg = vjp((jnp.concatenate(du_parts, axis=1), dvn_buf[...]))
        du_ref[...] = du_in.astype(du_ref.dtype)
        dv_ref[...] = dv_in.astype(dv_ref.dtype)

        @pl.when(first)
        def _():
            dg_ref[...] = dg

        @pl.when(jnp.logical_not(first))
        def _():
            dg_ref[...] += dg

    shape = jax.ShapeDtypeStruct((s, width), BF16)
    return pl.pallas_call(
        body, name=name, grid=(s // BLK,),
        out_shape=[shape, shape, jax.ShapeDtypeStruct((groups, BLK, BLK), F32),
                   jax.ShapeDtypeStruct((groups, BLK, 1), F32), jax.ShapeDtypeStruct((1, width), F32)],
        in_specs=[pl.BlockSpec((BLK, width), lambda c: (c, 3)), pl.BlockSpec((BLK, width), lambda c: (c, 4)),
                  pl.BlockSpec((BLK, width), lambda c: (c, 0)),
                  pl.BlockSpec((1, width), lambda c: (0, 0)),
                  pl.BlockSpec((groups, BLK, BLK), lambda c: (0, 0, 0)),
                  pl.BlockSpec((groups, BLK, 1), lambda c: (0, 0, 0))],
        out_specs=[pl.BlockSpec((BLK, width), lambda c: (c, 0)), pl.BlockSpec((BLK, width), lambda c: (c, 0)),
                   pl.BlockSpec((groups, BLK, BLK), lambda c: (0, 0, 0)),
                   pl.BlockSpec((groups, BLK, 1), lambda c: (0, 0, 0)),
                   pl.BlockSpec((1, width), lambda c: (0, 0))],
        scratch_shapes=[pltpu.VMEM((BLK, width), F32)],
        compiler_params=_params("arbitrary"),
    )(proj, proj, do, g_vnorm, w_s, b_s)


def _shift_down(v, k):
    row = lax.broadcasted_iota(jnp.int32, v.shape, 0)
    return jnp.where(row >= k, pltpu.roll(v, k, axis=0), 0.0)


def _shift_up(v, k):
    n = v.shape[0]
    row = lax.broadcasted_iota(jnp.int32, v.shape, 0)
    return jnp.where(row < n - k, pltpu.roll(v, n - k, axis=0), 0.0)


def _conv(gate, w, b):
    return b + w[0:1] * _shift_down(gate, 2) + w[1:2] * _shift_down(gate, 1) + w[2:3] * gate


def _conv_fwd(name, up, conv_w, conv_b):
    s, f2 = up.shape
    f = f2 // 2
    tc = _tile(f, 256, 128)
    nf = f // tc

    def body(g_ref, v_ref, w_ref, b_ref, o_ref):
        o_ref[...] = (jax.nn.gelu(_conv(g_ref[...], w_ref[...], b_ref[...])) * v_ref[...]).astype(o_ref.dtype)

    return pl.pallas_call(
        body, name=name, grid=(nf,), out_shape=jax.ShapeDtypeStruct((s, f), BF16),
        in_specs=[pl.BlockSpec((s, tc), lambda j: (0, j)), pl.BlockSpec((s, tc), lambda j: (0, nf + j)),
                  pl.BlockSpec((3, tc), lambda j: (0, j)), pl.BlockSpec((1, tc), lambda j: (0, j))],
        out_specs=pl.BlockSpec((s, tc), lambda j: (0, j)),
        compiler_params=_params("parallel"),
    )(up, up, conv_w, conv_b)


def _conv_bwd(name, up, da, conv_w, conv_b):
    s, f2 = up.shape
    f = f2 // 2
    tc = _tile(f, 256, 128)
    nf = f // tc

    def body(g_ref, v_ref, da_ref, w_ref, b_ref, dg_ref, dv_ref, dw_ref, db_ref):
        gate, val, dact, w = g_ref[...], v_ref[...], da_ref[...], w_ref[...]
        act, vjp = jax.vjp(jax.nn.gelu, _conv(gate, w, b_ref[...]))
        dv_ref[...] = (dact * act).astype(dv_ref.dtype)
        dconv = vjp(dact * val)[0]
        dg_ref[...] = (w[2:3] * dconv + w[1:2] * _shift_up(dconv, 1) + w[0:1] * _shift_up(dconv, 2)
                       ).astype(dg_ref.dtype)
        dw_ref[0:1, :] = jnp.sum(dconv * _shift_down(gate, 2), axis=0, keepdims=True)
        dw_ref[1:2, :] = jnp.sum(dconv * _shift_down(gate, 1), axis=0, keepdims=True)
        dw_ref[2:3, :] = jnp.sum(dconv * gate, axis=0, keepdims=True)
        db_ref[...] = jnp.sum(dconv, axis=0, keepdims=True)

    shape = jax.ShapeDtypeStruct((s, f), BF16)
    return pl.pallas_call(
        body, name=name, grid=(nf,),
        out_shape=[shape, shape, jax.ShapeDtypeStruct((3, f), F32), jax.ShapeDtypeStruct((1, f), F32)],
        in_specs=[pl.BlockSpec((s, tc), lambda j: (0, j)), pl.BlockSpec((s, tc), lambda j: (0, nf + j)),
                  pl.BlockSpec((s, tc), lambda j: (0, j)),
                  pl.BlockSpec((3, tc), lambda j: (0, j)), pl.BlockSpec((1, tc), lambda j: (0, j))],
        out_specs=[pl.BlockSpec((s, tc), lambda j: (0, j)), pl.BlockSpec((s, tc), lambda j: (0, j)),
                   pl.BlockSpec((3, tc), lambda j: (0, j)), pl.BlockSpec((1, tc), lambda j: (0, j))],
        compiler_params=_params("parallel"),
    )(up, up, da, conv_w, conv_b)


def _adamw(w, g, m, v):
    m = ADAM_B1 * m + (1.0 - ADAM_B1) * g
    v = ADAM_B2 * v + (1.0 - ADAM_B2) * jnp.square(g)
    m_hat = m / (1.0 - ADAM_B1 ** ADAM_STEP)
    v_hat = v / (1.0 - ADAM_B2 ** ADAM_STEP)
    delta = -ADAM_LR * (m_hat / (jnp.sqrt(v_hat) + ADAM_EPS) + ADAM_WD * w)
    return delta, m, v


def _update_shard(name, layer, w, m, v, own, received, chip, previous=None):
    _, rows, cols = w.shape
    tr = _tile(rows, 256)

    def body(chip_ref, w_ref, m_ref, v_ref, own_ref, rec_ref, *rest):
        g_ref, d_ref, nm_ref, nv_ref = rest[-4:]
        g = (own_ref[...].astype(F32) + rec_ref[0].astype(F32) + rec_ref[1].astype(F32)
             + rec_ref[2].astype(F32))
        delta, new_m, new_v = _adamw(w_ref[...], g, m_ref[...], v_ref[...])
        g_ref[...] = g
        d_ref[...] = delta
        nm_ref[...] = new_m
        nv_ref[...] = new_v

    layer_spec = pl.BlockSpec((None, tr, cols), lambda i, chip_ref: (layer, i, 0))
    in_specs = [layer_spec] * 3 + [pl.BlockSpec((None, tr, cols), lambda i, chip_ref: (chip_ref[0], i, 0)),
                                   pl.BlockSpec((3, tr, cols), lambda i, chip_ref: (0, i, 0))]
    args = [chip, w, m, v, own, received]
    aliases = {}
    if previous is not None:
        in_specs += [ANY] * 4
        aliases = {len(args) + k: k for k in range(4)}
        args += list(previous)
    return pl.pallas_call(
        body, name=name, out_shape=[jax.ShapeDtypeStruct(w.shape, F32)] * 4,
        grid_spec=pltpu.PrefetchScalarGridSpec(
            num_scalar_prefetch=1, grid=(rows // tr,), in_specs=in_specs, out_specs=[layer_spec] * 4),
        input_output_aliases=aliases, compiler_params=_params("parallel"),
    )(*args)


def _sum_devices(name, gathered):
    _, rows, cols = gathered.shape

    def body(g_ref, o_ref):
        total = g_ref[0]
        for d in range(1, N_DEV):
            total = total + g_ref[d]
        o_ref[...] = total

    return pl.pallas_call(
        body, name=name, out_shape=jax.ShapeDtypeStruct((rows, cols), F32),
        in_specs=[pl.BlockSpec((N_DEV, rows, cols), lambda: (0, 0, 0))],
        out_specs=pl.BlockSpec((rows, cols), lambda: (0, 0)),
        compiler_params=pltpu.CompilerParams(vmem_limit_bytes=VMEM_LIMIT_V7X),
    )(gathered)


def _update_small(name, w, g, m, v):
    def body(w_ref, g_ref, m_ref, v_ref, d_ref, nm_ref, nv_ref):
        delta, new_m, new_v = _adamw(w_ref[...], g_ref[...], m_ref[...], v_ref[...])
        d_ref[...] = delta
        nm_ref[...] = new_m
        nv_ref[...] = new_v

    spec = pl.BlockSpec(w.shape, lambda: (0, 0))
    return pl.pallas_call(
        body, name=name, out_shape=[jax.ShapeDtypeStruct(w.shape, F32)] * 3,
        in_specs=[spec] * 4, out_specs=[spec] * 3,
        compiler_params=pltpu.CompilerParams(vmem_limit_bytes=VMEM_LIMIT_V7X),
    )(w, g, m, v)


def _pack(arrays):
    flat = jnp.concatenate([a.reshape(-1) for a in arrays])
    pad = (-flat.shape[0]) % (8 * BLK)
    return jnp.pad(flat, (0, pad)).reshape(-1, BLK)


def _unpack(packed, shapes):
    flat = packed.reshape(-1)
    out, at = [], 0
    for shape in shapes:
        size = 1
        for d in shape:
            size *= d
        out.append(flat[at:at + size].reshape(shape))
        at += size
    return out


SHARDED = ("w_in", "w_mem_kv", "w_gate", "w_br_sb", "w_br_gm", "w_br_xa", "w_out", "w_up", "w_down")
SMALL = ("g_mix_pre", "g_vnorm", "w_s", "b_s", "g_mem", "b_gate", "g_mix_post", "g_ffn_pre", "conv_w",
         "conv_b", "g_ffn_post")
WEIGHTS = ("g_mix_pre", "w_in", "g_vnorm", "w_s", "b_s", "g_mem", "w_mem_kv", "w_gate", "b_gate", "w_br_sb",
           "w_br_gm", "w_br_xa", "w_out", "g_mix_post", "g_ffn_pre", "w_up", "conv_w", "conv_b", "w_down",
           "g_ffn_post")


ROW_SHARDED = ("w_mem_kv", "w_out", "w_down")
GATHER_GROUPS = (("w_in",), ("w_mem_kv", "w_gate"), ("w_br_sb", "w_br_gm", "w_br_xa", "w_out"), ("w_up", "w_down"))
REDUCE_GROUPS = (("w_down", "w_up"), ("w_out", "w_br_sb", "w_br_gm", "w_br_xa", "w_mem_kv", "w_gate"), ("w_in",))


def _cast_into_place(name, w, layer, me):
    _, rows, cols = w.shape
    tr = _tile(rows, max(8, STREAM_BLOCK_BYTES // (4 * cols)))

    def body(me_ref, w_ref, o_ref):
        o_ref[...] = w_ref[...].astype(o_ref.dtype)

    return pl.pallas_call(
        body, name=name, out_shape=jax.ShapeDtypeStruct((N_DEV, rows, cols), BF16),
        grid_spec=pltpu.PrefetchScalarGridSpec(
            num_scalar_prefetch=1, grid=(rows // tr,),
            in_specs=[pl.BlockSpec((None, tr, cols), lambda i, me_ref: (layer, i, 0))],
            out_specs=pl.BlockSpec((None, tr, cols), lambda i, me_ref: (me_ref[0], i, 0))),
        compiler_params=_params("parallel"),
    )(me, w)


class _Gather:
    def __init__(self, tag, names, places, after):
        self.tag, self.names, self.n = tag, names, len(places)
        self.sems, self.places, self.token = _split_start(
            f"gather_start_{tag}", places, _gather_first_copies(self.n), after)

    def relay(self, after):
        self.sems, self.places = _split_wait(
            f"gather_relay_{self.tag}", self.places, self.sems, _gather_first_copies(self.n), after,
            then=_gather_relay_copies(self.n))

    def finish(self, after):
        places = _split_wait(f"gather_finish_{self.tag}", self.places, self.sems,
                             _gather_relay_copies(self.n), after)
        full = dict(zip(self.names, places))
        for name in self.names:
            if name in ROW_SHARDED:
                full[name] = full[name].reshape(-1, full[name].shape[-1])
        return full


class _Reduce:
    def __init__(self, tag, names, partials):
        self.tag, self.names, self.n = tag, names, len(partials)
        blocks = [p.reshape((N_CHIP, 2, -1, p.shape[-1])) for p in partials]
        lands = [lax.empty((N_CHIP,) + b.shape[2:], b.dtype) for b in blocks]
        self.sems, self.buffers, self.token = _split_start(
            f"pair_start_{tag}", blocks + lands, _pair_copies(self.n))

    def middle(self, after, core):
        n = self.n
        got = _split_wait(f"pair_finish_{self.tag}", self.buffers, self.sems, _pair_copies(n), after)
        sums = [_pair_sum(f"pair_sum_{name}{self.tag}", got[a], got[n + a], core)
                for a, name in enumerate(self.names)]
        lands = [lax.empty((3,) + s.shape[1:], s.dtype) for s in sums]
        self.sems, self.buffers, self.token = _split_start(
            f"chip_start_{self.tag}", sums + lands, _chip_copies(n))

    def finish(self, after):
        n = self.n
        got = _split_wait(f"chip_finish_{self.tag}", self.buffers, self.sems, _chip_copies(n), after)
        return [(name, got[a], got[n + a]) for a, name in enumerate(self.names)]


def kernel(x, mem, g_mix_pre, w_in, g_vnorm, w_s, b_s, g_mem, w_mem_kv, w_gate, b_gate, w_br_sb, w_br_gm, w_br_xa, w_out, g_mix_post, g_ffn_pre, w_up, conv_w, conv_b, w_down, g_ffn_post, loss_target, m_g_mix_pre, m_w_in, m_g_vnorm, m_w_s, m_b_s, m_g_mem, m_w_mem_kv, m_w_gate, m_b_gate, m_w_br_sb, m_w_br_gm, m_w_br_xa, m_w_out, m_g_mix_post, m_g_ffn_pre, m_w_up, m_conv_w, m_conv_b, m_w_down, m_g_ffn_post, v_g_mix_pre, v_w_in, v_g_vnorm, v_w_s, v_b_s, v_g_mem, v_w_mem_kv, v_w_gate, v_b_gate, v_w_br_sb, v_w_br_gm, v_w_br_xa, v_w_out, v_g_mix_post, v_g_ffn_pre, v_w_up, v_conv_w, v_conv_b, v_w_down, v_g_ffn_post):
    p = dict(g_mix_pre=g_mix_pre, w_in=w_in, g_vnorm=g_vnorm, w_s=w_s, b_s=b_s, g_mem=g_mem, w_mem_kv=w_mem_kv,
             w_gate=w_gate, b_gate=b_gate, w_br_sb=w_br_sb, w_br_gm=w_br_gm, w_br_xa=w_br_xa, w_out=w_out,
             g_mix_post=g_mix_post, g_ffn_pre=g_ffn_pre, w_up=w_up, conv_w=conv_w, conv_b=conv_b, w_down=w_down,
             g_ffn_post=g_ffn_post)
    mom = dict(g_mix_pre=m_g_mix_pre, w_in=m_w_in, g_vnorm=m_g_vnorm, w_s=m_w_s, b_s=m_b_s, g_mem=m_g_mem,
               w_mem_kv=m_w_mem_kv, w_gate=m_w_gate, b_gate=m_b_gate, w_br_sb=m_w_br_sb, w_br_gm=m_w_br_gm,
               w_br_xa=m_w_br_xa, w_out=m_w_out, g_mix_post=m_g_mix_post, g_ffn_pre=m_g_ffn_pre, w_up=m_w_up,
               conv_w=m_conv_w, conv_b=m_conv_b, w_down=m_w_down, g_ffn_post=m_g_ffn_post)
    var = dict(g_mix_pre=v_g_mix_pre, w_in=v_w_in, g_vnorm=v_g_vnorm, w_s=v_w_s, b_s=v_b_s, g_mem=v_g_mem,
               w_mem_kv=v_w_mem_kv, w_gate=v_w_gate, b_gate=v_b_gate, w_br_sb=v_w_br_sb, w_br_gm=v_w_br_gm,
               w_br_xa=v_w_br_xa, w_out=v_w_out, g_mix_post=v_g_mix_post, g_ffn_pre=v_g_ffn_pre, w_up=v_w_up,
               conv_w=v_conv_w, conv_b=v_conv_b, w_down=v_w_down, g_ffn_post=v_g_ffn_post)
    depth = w_in.shape[0]
    x = x[0]
    mem = mem[0]
    target = loss_target[0]
    s, d = x.shape
    width = d // 2
    heads = width // BLK
    cx, cy, cc = lax.axis_index("x"), lax.axis_index("y"), lax.axis_index("c")
    core = cc.astype(jnp.int32).reshape(1)
    chip = (2 * cx + cy).astype(jnp.int32).reshape(1)
    me = 4 * cx + 2 * cy + cc
    me_index = me.astype(jnp.int32).reshape(1)

    conv_all = _all_gather("gather_conv_w", [conv_w])[0]
    conv_w_full = jnp.transpose(conv_all, (1, 2, 0, 3)).reshape(depth, 3, -1)
    gathers = {}
    token = conv_all
    for l in range(depth):
        for gi, names in enumerate(GATHER_GROUPS):
            places = [_cast_into_place(f"cast_{n}{l}", p[n], l, me_index) for n in names]
            gathers[l, gi] = _Gather(f"{l}{gi}", names, places, token)
            token = gathers[l, gi].token

    kept, gathered = [], []
    h = _norm_fwd("pre_norm0", x, g_mix_pre[0][None, :], after=token)
    gathers[0, 0].relay(h)
    for l in range(depth):
        row = lambda name: p[name][l][None, :]
        first, second, third, fourth = (gathers[l, gi] for gi in range(4))
        w = first.finish(h)
        proj, proj16 = _mm_nn_cs(f"proj{l}", h, w["w_in"], (F32, BF16), tm=1024)
        o_sb, sb_tot = _sb_fwd(f"sb_fwd{l}", proj16, heads)
        b_s3 = b_s[l][:, :, None]
        o_gm = _gm_fwd(f"gm_fwd{l}", proj, row("g_vnorm"), w_s[l], b_s3, width)
        second.relay(o_gm)
        mn = _norm_fwd(f"mem_norm{l}", mem, row("g_mem"))
        w.update(second.finish(mn))
        third.relay(mn)
        mem_kv = _mm_nn(f"mem_kv{l}", mn, w["w_mem_kv"], (BF16,))
        o_xa = _xa_fwd(f"xa_fwd{l}", proj16, mem_kv, width)
        zg = _mm_nn_cs(f"gates{l}", h, w["w_gate"], tm=1024)
        w.update(third.finish(zg))
        ts = [_mm_nn_cs(f"branch_{n}{l}", o, w[f"w_br_{n}"], tm=2048)
              for n, o in (("sb", o_sb), ("gm", o_gm), ("xa", o_xa))]
        merged = _merge_fwd(f"merge{l}", zg, ts, row("b_gate"))
        fourth.relay(merged)
        y1 = _mm_nn(f"out{l}", merged, w["w_out"], tm=1024, tn=1024)
        x1, h2 = _residual_norm_fwd(f"mix_post{l}", x, y1, row("g_mix_post"), row("g_ffn_pre"))
        w.update(fourth.finish(h2))
        up = _mm_nn_cs(f"up{l}", h2, w["w_up"])
        if l + 1 < depth:
            gathers[l + 1, 0].relay(up)
        act = _conv_fwd(f"conv_fwd{l}", up, conv_w_full[l], row("conv_b"))
        y2 = _mm_nn(f"down{l}", act, w["w_down"])
        kept.append(dict(x=x, h=h, proj=proj, proj16=proj16, zg=zg, mn=mn, mem_kv=mem_kv, o_sb=o_sb, o_gm=o_gm,
                         o_xa=o_xa, ts=ts, merged=merged, y1=y1, x1=x1, h2=h2, up=up, act=act, y2=y2, b_s=b_s3,
                         sb_tot=sb_tot))
        gathered.append(w)
        if l + 1 < depth:
            x, h = _residual_norm_fwd(f"ffn_post{l}", x1, y2, g_ffn_post[l][None, :], g_mix_pre[l + 1][None, :])

    top = kept[-1]
    dx, dy2, loss_part, dg_ffn_post = _loss_bwd("loss", top["x1"], top["y2"], g_ffn_post[depth - 1][None, :], target)
    loss = lax.psum(loss_part[0, 0], ("x", "y", "c"))
    small_grads = [dict() for _ in range(depth)]
    small_grads[depth - 1]["g_ffn_post"] = dg_ffn_post
    outs = {n: None for n in SHARDED}
    in_flight = []

    def reduce_finish(entry, after):
        layer = int(entry[0].tag[0])
        for name, own, received in entry[0].finish(after):
            outs[name] = _update_shard(f"update_{name}{layer}", layer, p[name], mom[name], var[name],
                                       own, received, chip, outs[name])
        in_flight.remove(entry)

    def reduce_start(new):
        for entry in in_flight:
            entry[2] += entry[1] == 2
        in_flight.append([new, 1, 0])
        for entry in [e for e in in_flight if e[2] >= 2]:
            reduce_finish(entry, new.token)
        return new.token

    def reduce_middle(after):
        for entry in [e for e in in_flight if e[1] == 1]:
            entry[0].middle(after, core)
            after = entry[0].token
            entry[1] = 2
        return after

    behind = None

    for l in reversed(range(depth)):
        k = kept[l]
        w = gathered[l]
        sg = small_grads[l]
        row = lambda name: p[name][l][None, :]
        partials = {}
        group = lambda gi: _Reduce(f"{l}{gi}", REDUCE_GROUPS[gi], [partials[n] for n in REDUCE_GROUPS[gi]])
        partials["w_down"] = _mm_tn(f"d_w_down{l}", k["act"], dy2, after=behind)
        dact = _mm_nt(f"d_act{l}", dy2, w["w_down"], tm=2048)
        dgate, dval, sg["conv_w"], sg["conv_b"] = _conv_bwd(f"conv_bwd{l}", k["up"], dact, conv_w_full[l],
                                                             row("conv_b"))
        dup = jnp.concatenate([dgate, dval], axis=1)
        partials["w_up"] = _mm_tn_cs(f"d_w_up{l}", k["h2"], dup)
        behind = reduce_start(group(0))
        dh2 = _mm_nt_cs(f"d_h2{l}", dup, w["w_up"], after=behind)
        dx1, dy1, sg["g_ffn_pre"], sg["g_mix_post"] = _mid_bwd(f"mid_bwd{l}", dx, dh2, k["x1"], k["y1"],
                                                                row("g_ffn_pre"), row("g_mix_post"))
        behind = reduce_middle(dy1)
        partials["w_out"] = _mm_tn(f"d_w_out{l}", k["merged"], dy1, tk=1024, after=behind)
        dmerged = _mm_nt(f"d_merged{l}", dy1, w["w_out"], tm=1024, tn=1024)
        dt_sb, dt_gm, dt_xa, dzg, sg["b_gate"] = _merge_bwd(f"merge_bwd{l}", dmerged, k["zg"], k["ts"], row("b_gate"))
        do = {}
        for n, dt in (("sb", dt_sb), ("gm", dt_gm), ("xa", dt_xa)):
            partials[f"w_br_{n}"] = _mm_tn_cs(f"d_w_br_{n}{l}", k[f"o_{n}"], dt, tk=1024)
            do[n] = _mm_nt_cs_whole(f"d_o_{n}{l}", dt, w[f"w_br_{n}"], F32 if n == "gm" else BF16)
        dq_xa, dk_mem, dv_mem = _xa_bwd(f"xa_bwd{l}", k["proj16"], k["mem_kv"], do["xa"], width)
        dmem_kv = jnp.concatenate([dk_mem, dv_mem], axis=1)
        partials["w_mem_kv"] = _mm_tn(f"d_w_mem_kv{l}", k["mn"], dmem_kv)
        partials["w_gate"] = _mm_tn_cs(f"d_w_gate{l}", k["h"], dzg, tk=1024)
        behind = reduce_start(group(1))
        dh_gate = _mm_nt_cs(f"d_h_gate{l}", dzg, w["w_gate"], tm=1024, after=behind)
        behind = reduce_middle(dh_gate)
        dmn = _mm_nt(f"d_mn{l}", dmem_kv, w["w_mem_kv"], after=behind)
        sg["g_mem"] = _mem_norm_bwd(f"mem_norm_bwd{l}", dmn, mem, row("g_mem"))
        du, dvg, sg["w_s"], db_s, sg["g_vnorm"] = _gm_bwd(f"gm_bwd{l}", k["proj"], do["gm"], row("g_vnorm"),
                                                          p["w_s"][l], k["b_s"], width)
        sg["b_s"] = db_s[:, :, 0]
        dq, dk, dv = _sb_bwd(f"sb_bwd{l}", k["proj16"], do["sb"], k["sb_tot"], heads)
        dproj = jnp.concatenate([dq, dk, dv, du, dvg, dq_xa], axis=1)
        partials["w_in"] = _mm_tn_cs(f"d_w_in{l}", k["h"], dproj, tk=1024)
        behind = reduce_start(group(2))
        dh_in = _mm_nt_cs(f"d_h_in{l}", dproj, w["w_in"], tm=1024, after=behind)
        if l > 0:
            below = kept[l - 1]
            dx, dy2, sg["g_mix_pre"], small_grads[l - 1]["g_ffn_post"] = _bottom_bwd(
                f"bottom_bwd{l}", dx1, dh_in, dh_gate, k["x"], row("g_mix_pre"), below["y2"],
                p["g_ffn_post"][l - 1][None, :])
        else:
            dx, sg["g_mix_pre"] = _bottom_bwd(f"bottom_bwd{l}", dx1, dh_in, dh_gate, k["x"], row("g_mix_pre"))
        behind = reduce_middle(dx)

    for entry in list(in_flight):
        reduce_finish(entry, behind)

    small_shapes = [small_grads[l][n].shape for l in range(depth) for n in SMALL]
    packed = _pack([small_grads[l][n] for l in range(depth) for n in SMALL])
    total = _sum_devices("sum_small", _all_gather("gather_small", [packed])[0])
    flat = _unpack(total, small_shapes)
    grads = {}
    for i, n in enumerate(SMALL):
        g = jnp.stack([flat[l * len(SMALL) + i] for l in range(depth)]).reshape((depth,) + p[n].shape[1:]
                                                                                 if n != "conv_w" else (depth, 3, -1))
        if n == "conv_w":
            per = conv_w.shape[2]
            g = lax.dynamic_slice_in_dim(g, me * per, per, axis=2)
        grads[n] = g
    delta, new_m, new_v = _update_small(
        "update_small", _pack([p[n] for n in SMALL]), _pack([grads[n] for n in SMALL]),
        _pack([mom[n] for n in SMALL]), _pack([var[n] for n in SMALL]))
    shapes = [p[n].shape for n in SMALL]
    for n, dl, nm, nv in zip(SMALL, _unpack(delta, shapes), _unpack(new_m, shapes), _unpack(new_v, shapes)):
        outs[n] = (grads[n], dl, nm, nv)

    result = [loss, dx[None]]
    for k in range(4):
        result += [outs[n][k] for n in WEIGHTS]
    return tuple(result)
```

```python
import functools

import jax
import jax.numpy as jnp
from jax import lax
from jax.experimental import pallas as pl
from jax.experimental.pallas import tpu as pltpu

F32 = jnp.float32
BF16 = jnp.bfloat16
N_DEV = 8
N_CHIP = 4
EPS = 1e-6
CHUNK = 64
BLK = 128
XA_HEADS = 4
SB_HEADS_PER_STEP = 8
ADAM_LR = 0.001
ADAM_B1 = 0.9
ADAM_B2 = 0.999
ADAM_EPS = 1e-08
ADAM_WD = 0.01
ADAM_STEP = 10
VMEM_LIMIT_V7X = 56 * 1024 * 1024
STREAM_BLOCK_BYTES = 4 * 1024 * 1024
MESH = pl.DeviceIdType.MESH
ANY = pl.BlockSpec(memory_space=pl.ANY)


def _params(*sem):
    return pltpu.CompilerParams(dimension_semantics=sem, vmem_limit_bytes=VMEM_LIMIT_V7X)


def _tile(n, pref, mult=8):
    best = None
    for t in range(mult, min(n, pref) + 1, mult):
        if n % t == 0:
            best = t
    return best if best is not None else n


def _bf(x):
    return x if x.dtype == BF16 else x.astype(BF16)


def _dot(a, b, ca, cb):
    return lax.dot_general(_bf(a), _bf(b), (((ca,), (cb,)), ((), ())), preferred_element_type=F32)


def _rmsnorm(x, g):
    return (x * lax.rsqrt(jnp.mean(x * x, axis=-1, keepdims=True) + EPS)) * g


def _position():
    x, y, c = lax.axis_index("x"), lax.axis_index("y"), lax.axis_index("c")
    return x, y, c


def _all_gather(name, shards):
    n = len(shards)

    def body(*refs):
        ins, outs = refs[:n], refs[n:2 * n]
        send_sems, recv_sems, local_sems = refs[2 * n:]
        x, y, c = _position()
        me = 4 * x + 2 * y + c
        sibling = (x, y, 1 - c)
        chips = [(1 - x, y), (x, 1 - y), (1 - x, 1 - y)]

        def copy(a, k, block, to, src=None):
            dst = outs[a].at[block]
            return pltpu.make_async_remote_copy(
                src_ref=dst if src is None else src, dst_ref=dst, send_sem=send_sems.at[a, k],
                recv_sem=recv_sems.at[a, k], device_id=to, device_id_type=MESH)

        local = [pltpu.make_async_copy(ins[a], outs[a].at[me], local_sems.at[a]) for a in range(n)]
        for cp in local:
            cp.start()
        first = []
        for a in range(n):
            first.append(copy(a, 0, me, sibling, src=ins[a]))
            for j, chip in enumerate(chips):
                first.append(copy(a, 1 + j, me, (*chip, c), src=ins[a]))
        for cp in first:
            cp.start()
        passed = []
        for a in range(n):
            for j, (px, py) in enumerate(chips):
                block = 4 * px + 2 * py + c
                copy(a, 1 + j, block, sibling).wait_recv()
                forward = copy(a, 4 + j, block, sibling)
                forward.start()
                passed.append(forward)
        for a in range(n):
            copy(a, 0, 4 * x + 2 * y + (1 - c), sibling).wait_recv()
            for j, (px, py) in enumerate(chips):
                copy(a, 4 + j, 4 * px + 2 * py + (1 - c), sibling).wait_recv()
        for cp in first + passed:
            cp.wait_send()
        for cp in local:
            cp.wait()

    return pl.pallas_call(
        body, name=name,
        out_shape=[jax.ShapeDtypeStruct((N_DEV,) + s.shape, s.dtype) for s in shards],
        in_specs=[ANY] * n, out_specs=[ANY] * n,
        scratch_shapes=[pltpu.SemaphoreType.DMA((n, 7)), pltpu.SemaphoreType.DMA((n, 7)),
                        pltpu.SemaphoreType.DMA((n,))],
    )(*shards)


HBM = pl.BlockSpec(memory_space=pltpu.HBM)
SEM = pl.BlockSpec(memory_space=pltpu.SEMAPHORE)
DATAFLOW = pltpu.SideEffectType.DATAFLOW_SIDE_EFFECTING


def _remote(src, dst, send, recv, k, peer):
    return pltpu.make_async_remote_copy(src_ref=src, dst_ref=dst, send_sem=send.at[k], recv_sem=recv.at[k],
                                        device_id=peer, device_id_type=MESH)


def _gather_first_copies(n):
    def build(refs, send, recv):
        x, y, c = _position()
        me = 4 * x + 2 * y + c
        peers = [(x, y, 1 - c), (1 - x, y, c), (x, 1 - y, c), (1 - x, 1 - y, c)]
        return [_remote(refs[a].at[me], refs[a].at[me], send, recv, 4 * a + k, peer)
                for a in range(n) for k, peer in enumerate(peers)]
    return build, 4 * n


def _gather_relay_copies(n):
    def build(refs, send, recv):
        x, y, c = _position()
        blocks = [4 * px + 2 * py + c for px, py in ((1 - x, y), (x, 1 - y), (1 - x, 1 - y))]
        return [_remote(refs[a].at[blk], refs[a].at[blk], send, recv, 3 * a + j, (x, y, 1 - c))
                for a in range(n) for j, blk in enumerate(blocks)]
    return build, 3 * n


def _pair_copies(n):
    def build(refs, send, recv):
        x, y, c = _position()
        return [_remote(refs[a].at[k, 1 - c], refs[n + a].at[k], send, recv, N_CHIP * a + k, (x, y, 1 - c))
                for a in range(n) for k in range(N_CHIP)]
    return build, N_CHIP * n


def _chip_copies(n):
    def build(refs, send, recv):
        x, y, c = _position()
        chips = ((1 - x, y), (x, 1 - y), (1 - x, 1 - y))
        return [_remote(refs[a].at[2 * px + py], refs[n + a].at[j], send, recv, 3 * a + j, (px, py, c))
                for a in range(n) for j, (px, py) in enumerate(chips)]
    return build, 3 * n


def _split_start(name, buffers, copies, after=None):
    build, n_copies = copies
    nb = len(buffers)
    extra = [] if after is None else [after]

    def body(*refs):
        at = nb + len(extra)
        for cp in build(refs[:nb], refs[at], refs[at + 1]):
            cp.start()
        token = refs[at + 2 + nb]
        token[...] = jnp.zeros_like(token)

    outs = pl.pallas_call(
        body, name=name,
        out_shape=[pltpu.SemaphoreType.DMA((n_copies,)), pltpu.SemaphoreType.DMA((n_copies,))]
        + [pltpu.HBM(b.shape, b.dtype) for b in buffers] + [jax.ShapeDtypeStruct((8, BLK), F32)],
        in_specs=[HBM] * nb + [ANY] * len(extra),
        out_specs=[SEM, SEM] + [HBM] * nb + [pl.BlockSpec(memory_space=pltpu.VMEM)],
        input_output_aliases={i: 2 + i for i in range(nb)},
        compiler_params=pltpu.CompilerParams(has_side_effects=DATAFLOW),
    )(*[pltpu.with_memory_space_constraint(b, pltpu.HBM) for b in buffers], *extra)
    return (outs[0], outs[1]), list(outs[2:2 + nb]), outs[-1]


def _split_wait(name, buffers, sems, copies, after, then=None):
    build, _ = copies
    nb = len(buffers)
    n_next = 0 if then is None else then[1]
    first_out = nb + 3

    def body(*refs):
        for cp in build(refs[:nb], refs[nb], refs[nb + 1]):
            cp.wait()
        if then is not None:
            for cp in then[0](refs[:nb], refs[first_out], refs[first_out + 1]):
                cp.start()

    sem_shapes = [] if then is None else [pltpu.SemaphoreType.DMA((n_next,))] * 2
    outs = pl.pallas_call(
        body, name=name,
        out_shape=sem_shapes + [pltpu.HBM(b.shape, b.dtype) for b in buffers],
        in_specs=[HBM] * nb + [SEM, SEM, ANY],
        out_specs=[SEM] * len(sem_shapes) + [HBM] * nb,
        input_output_aliases={i: len(sem_shapes) + i for i in range(nb)},
        compiler_params=pltpu.CompilerParams(has_side_effects=DATAFLOW),
    )(*buffers, sems[0], sems[1], after)
    if then is None:
        return list(outs)
    return (outs[0], outs[1]), list(outs[2:])


def _pair_sum(name, partial, received, core):
    _, _, rows, cols = partial.shape
    tr = _tile(rows, max(8, STREAM_BLOCK_BYTES // (2 * cols)))

    def body(core_ref, p_ref, r_ref, o_ref):
        o_ref[...] = (p_ref[...].astype(F32) + r_ref[...].astype(F32)).astype(o_ref.dtype)

    return pl.pallas_call(
        body, name=name, out_shape=jax.ShapeDtypeStruct((N_CHIP, rows, cols), BF16),
        grid_spec=pltpu.PrefetchScalarGridSpec(
            num_scalar_prefetch=1, grid=(N_CHIP, rows // tr),
            in_specs=[pl.BlockSpec((None, None, tr, cols), lambda k, i, core: (k, core[0], i, 0)),
                      pl.BlockSpec((None, tr, cols), lambda k, i, core: (k, i, 0))],
            out_specs=pl.BlockSpec((None, tr, cols), lambda k, i, core: (k, i, 0))),
        compiler_params=_params("parallel", "parallel"),
    )(core, partial, received)


def _mm(name, a, b, out_shape, out_dtypes, grid, a_spec, b_spec, o_spec, ca, cb, k_steps=1, after=None):
    n_out = len(out_dtypes)
    extra = [] if after is None else [after]

    def body(a_ref, b_ref, *rest):
        rest = rest[len(extra):]
        o_refs = rest[:n_out]
        part = _dot(a_ref[...], b_ref[...], ca, cb)
        if k_steps == 1:
            for o in o_refs:
                o[...] = part.astype(o.dtype)
            return
        acc = rest[n_out]
        k = pl.program_id(len(grid) - 1)

        @pl.when(k == 0)
        def _():
            acc[...] = part

        @pl.when(k > 0)
        def _():
            acc[...] += part

        @pl.when(k == k_steps - 1)
        def _():
            for o in o_refs:
                o[...] = acc[...].astype(o.dtype)

    o_block = tuple(d for d in o_spec.block_shape if d is not None)
    sem = ("parallel",) * (len(grid) - 1) + (("arbitrary",) if k_steps > 1 else ("parallel",))
    out = pl.pallas_call(
        body, name=name, grid=grid,
        out_shape=[jax.ShapeDtypeStruct(out_shape, d) for d in out_dtypes],
        in_specs=[a_spec, b_spec] + [ANY] * len(extra), out_specs=[o_spec] * n_out,
        scratch_shapes=[pltpu.VMEM(o_block, F32)] if k_steps > 1 else [],
        compiler_params=_params(*sem),
    )(a, b, *extra)
    return out if n_out > 1 else out[0]


def _mm_nn_cs(name, a, w, out_dtypes=(F32,), tm=512, **kw):
    m, k = a.shape
    _, _, ns = w.shape
    tm = _tile(m, tm)
    return _mm(name, a, w, (m, N_DEV * ns), out_dtypes, (m // tm, N_DEV),
               pl.BlockSpec((tm, k), lambda i, j: (i, 0)),
               pl.BlockSpec((None, k, ns), lambda i, j: (j, 0, 0)),
               pl.BlockSpec((tm, ns), lambda i, j: (i, j)), 1, 0, **kw)


def _mm_nn(name, a, w, out_dtypes=(F32,), tm=512, tn=512, **kw):
    m, k = a.shape
    _, n = w.shape
    tm, tn = _tile(m, tm), _tile(n, tn, 128)
    return _mm(name, a, w, (m, n), out_dtypes, (m // tm, n // tn),
               pl.BlockSpec((tm, k), lambda i, j: (i, 0)),
               pl.BlockSpec((k, tn), lambda i, j: (0, j)),
               pl.BlockSpec((tm, tn), lambda i, j: (i, j)), 1, 0, **kw)


def _mm_nt_cs(name, a, w, tm=512, per_step=1, after=None):
    _, k, ns = w.shape
    m = a.shape[-2]
    tm = _tile(m, tm)
    steps = N_DEV // per_step
    extra = [] if after is None else [after]

    def body(a_ref, w_ref, *rest):
        o_ref = rest[len(extra)]
        part = _dot(a_ref[:, 0:ns], w_ref[0], 1, 1)
        for u in range(1, per_step):
            part = part + _dot(a_ref[:, u * ns:(u + 1) * ns], w_ref[u], 1, 1)
        j = pl.program_id(1)

        @pl.when(j == 0)
        def _():
            o_ref[...] = part

        @pl.when(j > 0)
        def _():
            o_ref[...] += part

    if a.ndim == 3:
        per_half = steps // 2
        a_spec = pl.BlockSpec((None, tm, per_step * ns), lambda i, j: (j // per_half, i, j % per_half))
    else:
        a_spec = pl.BlockSpec((tm, per_step * ns), lambda i, j: (i, j))
    return pl.pallas_call(
        body, name=name, grid=(m // tm, steps), out_shape=jax.ShapeDtypeStruct((m, k), F32),
        in_specs=[a_spec, pl.BlockSpec((per_step, k, ns), lambda i, j: (j, 0, 0))] + [ANY] * len(extra),
        out_specs=pl.BlockSpec((tm, k), lambda i, j: (i, 0)),
        compiler_params=_params("parallel", "arbitrary"),
    )(a, w, *extra)


def _mm_nt_cs_whole(name, a, w, out_dtype, tm=512):
    m, _ = a.shape
    _, k, ns = w.shape
    tm = _tile(m, tm)

    def body(a_ref, w_ref, o_ref):
        acc = _dot(a_ref[:, 0:ns], w_ref[0], 1, 1)
        for j in range(1, N_DEV):
            acc = acc + _dot(a_ref[:, j * ns:(j + 1) * ns], w_ref[j], 1, 1)
        o_ref[...] = acc.astype(o_ref.dtype)

    return pl.pallas_call(
        body, name=name, grid=(m // tm,), out_shape=jax.ShapeDtypeStruct((m, k), out_dtype),
        in_specs=[pl.BlockSpec((tm, N_DEV * ns), lambda i: (i, 0)),
                  pl.BlockSpec((N_DEV, k, ns), lambda i: (0, 0, 0))],
        out_specs=pl.BlockSpec((tm, k), lambda i: (i, 0)),
        compiler_params=_params("parallel"),
    )(a, w)


def _mm_nt(name, a, w, out_dtypes=(F32,), tm=512, tn=512, **kw):
    m, k = a.shape
    n, _ = w.shape
    tm, tn = _tile(m, tm), _tile(n, tn, 128)
    return _mm(name, a, w, (m, n), out_dtypes, (m // tm, n // tn),
               pl.BlockSpec((tm, k), lambda i, j: (i, 0)),
               pl.BlockSpec((tn, k), lambda i, j: (j, 0)),
               pl.BlockSpec((tm, tn), lambda i, j: (i, j)), 1, 1, **kw)


def _mm_tn_cs(name, a, g, tk=512, **kw):
    s, k = a.shape
    if g.ndim == 3:
        ns = g.shape[2] // (N_DEV // 2)
        g_spec = pl.BlockSpec((None, s, ns), lambda i, j: (j // (N_DEV // 2), 0, j % (N_DEV // 2)))
    else:
        ns = g.shape[1] // N_DEV
        g_spec = pl.BlockSpec((s, ns), lambda i, j: (0, j))
    tk = _tile(k, tk, 128)
    return _mm(name, a, g, (N_DEV, k, ns), (BF16,), (k // tk, N_DEV),
               pl.BlockSpec((s, tk), lambda i, j: (0, i)),
               g_spec,
               pl.BlockSpec((None, tk, ns), lambda i, j: (j, i, 0)), 0, 0, **kw)


def _mm_tn(name, a, g, tk=512, tn=1024, **kw):
    s, k = a.shape
    n = g.shape[1]
    tk, tn = _tile(k, tk, 128), _tile(n, tn, 128)
    return _mm(name, a, g, (k, n), (BF16,), (k // tk, n // tn),
               pl.BlockSpec((s, tk), lambda i, j: (0, i)),
               pl.BlockSpec((s, tn), lambda i, j: (0, j)),
               pl.BlockSpec((tk, tn), lambda i, j: (i, j)), 0, 0, **kw)


def _rowwise(name, fn, rows, bcast, outs, reds, tm, after=None):
    n_rows = rows[0][0].shape[0]
    tm = _tile(n_rows, tm)
    n_in, n_b, n_o, n_r = len(rows), len(bcast), len(outs), len(reds)
    extra = [] if after is None else [after]

    def body(*refs):
        ins = refs[:n_in + n_b]
        refs = refs[len(extra):]
        o_refs = refs[n_in + n_b:n_in + n_b + n_o]
        r_refs = refs[n_in + n_b + n_o:]
        vals = fn(*[r[...] for r in ins])
        for o, v in zip(o_refs, vals[:n_o]):
            o[...] = v.astype(o.dtype)
        first = pl.program_id(0) == 0
        for r, v in zip(r_refs, vals[n_o:]):
            @pl.when(first)
            def _(r=r, v=v):
                r[...] = v.astype(r.dtype)

            @pl.when(jnp.logical_not(first))
            def _(r=r, v=v):
                r[...] += v.astype(r.dtype)

    def whole(shape):
        zeros = (0,) * len(shape)
        return pl.BlockSpec(shape, lambda i: zeros)

    in_specs = [pl.BlockSpec((tm, w), functools.partial(lambda i, cb: (i, cb), cb=cb)) for _, w, cb in rows]
    in_specs += [whole(b.shape) for b in bcast] + [ANY] * len(extra)
    out_specs = [pl.BlockSpec((tm, w), lambda i: (i, 0)) for w, _ in outs]
    out_specs += [whole(s) for s, _ in reds]
    out_shape = [jax.ShapeDtypeStruct((n_rows, w), d) for w, d in outs]
    out_shape += [jax.ShapeDtypeStruct(s, d) for s, d in reds]
    return pl.pallas_call(
        body, name=name, grid=(n_rows // tm,), out_shape=out_shape, in_specs=in_specs,
        out_specs=out_specs, compiler_params=_params("arbitrary" if n_r else "parallel"),
    )(*[r[0] for r in rows], *bcast, *extra)


def _norm_fwd(name, x, g, after=None):
    d = x.shape[1]
    return _rowwise(name, lambda xt, gt: (_rmsnorm(xt, gt),), [(x, d, 0)], [g], [(d, BF16)], [], 256,
                    after=after)[0]


def _residual_norm_fwd(name, x, y, g_post, g_next):
    d = x.shape[1]

    def fn(xt, yt, gp, gn):
        new = xt + _rmsnorm(yt, gp)
        return new, _rmsnorm(new, gn)

    return _rowwise(name, fn, [(x, d, 0), (y, d, 0)], [g_post, g_next], [(d, F32), (d, BF16)], [], 256)


def _merge(z0, z1, z2, t0, t1, t2, b0, b1, b2):
    return (jax.nn.sigmoid(z0 + b0) * t0 + jax.nn.sigmoid(z1 + b1) * t1 + jax.nn.sigmoid(z2 + b2) * t2)


def _merge_fwd(name, zg, ts, b_gate):
    d = ts[0].shape[1]
    rows = [(zg, d, b) for b in range(3)] + [(t, d, 0) for t in ts]
    bias = [b_gate[:, b * d:(b + 1) * d] for b in range(3)]
    return _rowwise(name, lambda *v: (_merge(*v),), rows, bias, [(d, BF16)], [], 128)[0]


def _merge_bwd(name, dmerged, zg, ts, b_gate):
    d = ts[0].shape[1]
    rows = [(dmerged, d, 0)] + [(zg, d, b) for b in range(3)] + [(t, d, 0) for t in ts]
    bias = [b_gate[:, b * d:(b + 1) * d] for b in range(3)]

    def fn(dm, *v):
        _, vjp = jax.vjp(_merge, *v)
        dz0, dz1, dz2, dt0, dt1, dt2, db0, db1, db2 = vjp(dm)
        return (dt0, dt1, dt2, jnp.concatenate([dz0, dz1, dz2], axis=1),
                jnp.concatenate([db0, db1, db2], axis=1))

    return _rowwise(name, fn, rows, bias, [(d, BF16)] * 3 + [(3 * d, BF16)], [((1, 3 * d), F32)], 128)


def _loss_bwd(name, x, y, g_post, target):
    d = x.shape[1]

    def loss(xt, yt, gp, tt):
        err = xt + _rmsnorm(yt, gp) - tt
        return 0.5 * jnp.sum(jnp.mean(err * err, axis=-1))

    def fn(xt, yt, tt, gp):
        val, (dx, dy, dg) = jax.value_and_grad(loss, argnums=(0, 1, 2))(xt, yt, gp, tt)
        return dx, dy, jnp.full((1, BLK), val, F32), dg

    return _rowwise(name, fn, [(x, d, 0), (y, d, 0), (target, d, 0)], [g_post],
                    [(d, F32), (d, BF16)], [((1, BLK), F32), ((1, d), F32)], 256)


def _mid_bwd(name, dx_out, dh, x_mid, y, g_pre, g_post):
    d = dx_out.shape[1]

    def fn(dxo, dht, xm, yt, gpre, gpost):
        _, vjp_pre = jax.vjp(_rmsnorm, xm, gpre)
        dxm, dgpre = vjp_pre(dht)
        dxm = dxo + dxm
        _, vjp_post = jax.vjp(_rmsnorm, yt, gpost)
        dy, dgpost = vjp_post(dxm)
        return dxm, dy, dgpre, dgpost

    return _rowwise(name, fn, [(dx_out, d, 0), (dh, d, 0), (x_mid, d, 0), (y, d, 0)], [g_pre, g_post],
                    [(d, F32), (d, BF16)], [((1, d), F32), ((1, d), F32)], 256)


def _bottom_bwd(name, dx_mid, dh_a, dh_b, x, g_pre, y_prev=None, g_post_prev=None):
    d = dx_mid.shape[1]
    rows = [(dx_mid, d, 0), (dh_a, d, 0), (dh_b, d, 0), (x, d, 0)]
    if y_prev is None:
        def fn(dxm, da, db, xt, gpre):
            _, vjp_pre = jax.vjp(_rmsnorm, xt, gpre)
            dx, dgpre = vjp_pre(da + db)
            return dxm + dx, dgpre

        return _rowwise(name, fn, rows, [g_pre], [(d, F32)], [((1, d), F32)], 256)

    def fn2(dxm, da, db, xt, yt, gpre, gpost):
        _, vjp_pre = jax.vjp(_rmsnorm, xt, gpre)
        dx, dgpre = vjp_pre(da + db)
        dx = dxm + dx
        _, vjp_post = jax.vjp(_rmsnorm, yt, gpost)
        dy, dgpost = vjp_post(dx)
        return dx, dy, dgpre, dgpost

    return _rowwise(name, fn2, rows + [(y_prev, d, 0)], [g_pre, g_post_prev],
                    [(d, F32), (d, BF16)], [((1, d), F32), ((1, d), F32)], 256)


def _mem_norm_bwd(name, dmn, mem, g_mem):
    d = mem.shape[1]

    def fn(dt, mt, g):
        _, vjp = jax.vjp(_rmsnorm, mt, g)
        return (vjp(dt)[1],)

    return _rowwise(name, fn, [(dmn, d, 0), (mem, d, 0)], [g_mem], [], [((1, d), F32)], 256)[0]


def _split_dot(v, tri):
    hi = v.astype(BF16)
    lo = (v - hi.astype(F32)).astype(BF16)
    return (jnp.dot(hi, tri, preferred_element_type=F32) + jnp.dot(lo, tri, preferred_element_type=F32))


def _sb_scores(q, kb, strict=None):
    z = _dot(q, kb, 1, 1) * (BLK ** -0.5)
    soft = jnp.log(1.0 + jnp.exp(-jnp.abs(z)))
    log_beta = jnp.minimum(z, 0.0) - soft
    log_keep = jnp.minimum(-z, 0.0) - soft
    if strict is not None:
        log_keep = jnp.where(strict, log_keep, 0.0)
    return z, log_beta, log_keep


def _block_iotas():
    return (lax.broadcasted_iota(jnp.int32, (BLK, BLK), 0), lax.broadcasted_iota(jnp.int32, (BLK, BLK), 1))


def _sb_fwd(name, proj, heads):
    s = proj.shape[0]
    nq = s // BLK
    hb = _tile(heads, SB_HEADS_PER_STEP, 1)
    groups = heads // hb
    wide = hb * BLK

    def body(q_ref, k_ref, v_ref, o_ref, tot_ref, acc_ref):
        i = pl.program_id(1)
        row, col = _block_iotas()
        diag = col < row
        later = (row > col).astype(BF16)
        head_cols = [slice(hh * BLK, (hh + 1) * BLK) for hh in range(hb)]

        def tiles(off, runs, strict):
            scores = [_sb_scores(q_ref[:, c], k_ref[pl.ds(off, BLK), c], strict) for c in head_cols]
            suffixes = [_split_dot(sc[2], later) for sc in scores]
            new = []
            for hh, c in enumerate(head_cols):
                a = jnp.exp(scores[hh][1] + suffixes[hh] + runs[hh])
                if strict is not None:
                    a = jnp.where(strict, a, 0.0)
                acc_ref[:, c] += _dot(a, v_ref[pl.ds(off, BLK), c], 1, 0)
                new.append(runs[hh] + jnp.sum(scores[hh][2], axis=1, keepdims=True))
            return tuple(new)

        acc_ref[...] = jnp.zeros_like(acc_ref)
        runs = tiles(pl.multiple_of(i * BLK, BLK), (jnp.zeros((BLK, 1), F32),) * hb, diag)
        runs = lax.fori_loop(0, i, lambda t, r: tiles(pl.multiple_of((i - 1 - t) * BLK, BLK), r, None), runs)
        o_ref[...] = acc_ref[...].astype(o_ref.dtype)
        for hh in range(hb):
            tot_ref[:, hh:hh + 1] = runs[hh]

    return pl.pallas_call(
        body, name=name, grid=(groups, nq),
        out_shape=[jax.ShapeDtypeStruct((s, heads * BLK), BF16), jax.ShapeDtypeStruct((groups, s, hb), F32)],
        in_specs=[pl.BlockSpec((BLK, wide), lambda g, i: (i, g)),
                  pl.BlockSpec((s, wide), lambda g, i: (0, groups + g)),
                  pl.BlockSpec((s, wide), lambda g, i: (0, 2 * groups + g))],
        out_specs=[pl.BlockSpec((BLK, wide), lambda g, i: (i, g)),
                   pl.BlockSpec((None, BLK, hb), lambda g, i: (g, i, 0))],
        scratch_shapes=[pltpu.VMEM((BLK, wide), F32)],
        compiler_params=_params("parallel", "arbitrary"),
    )(proj, proj, proj)


def _sb_bwd(name, proj, do, tot, heads):
    s = proj.shape[0]
    nq = s // BLK
    groups, _, hb = tot.shape
    wide = hb * BLK

    def body(q_ref, k_ref, v_ref, do_ref, tot_ref, dq_ref, dk_ref, dv_ref, dq_acc, dk_acc, dv_acc):
        i = pl.program_id(1)
        row, col = _block_iotas()
        diag = col < row
        upto = (row <= col).astype(BF16)
        earlier = (row < col).astype(BF16)

        @pl.when(i == 0)
        def _():
            dk_acc[...] = jnp.zeros_like(dk_acc)
            dv_acc[...] = jnp.zeros_like(dv_acc)

        dq_acc[...] = jnp.zeros_like(dq_acc)

        head_cols = [slice(hh * BLK, (hh + 1) * BLK) for hh in range(hb)]

        def tiles(off, carry, strict):
            rows = pl.ds(off, BLK)
            scores = [_sb_scores(q_ref[:, c], k_ref[rows, c], strict) for c in head_cols]
            das = [_dot(do_ref[:, c], v_ref[rows, c], 1, 1) for c in head_cols]
            prefixes = [_split_dot(sc[2], upto) for sc in scores]
            dlas = []
            for hh, c in enumerate(head_cols):
                suffix = tot_ref[:, hh:hh + 1] - (prefixes[hh] + carry[2 * hh])
                a = jnp.exp(scores[hh][1] + suffix)
                if strict is not None:
                    a = jnp.where(strict, a, 0.0)
                dlas.append(das[hh] * a)
                dv_acc[rows, c] += _dot(a, do_ref[:, c], 0, 0)
            dkeeps = [_split_dot(dla, earlier) for dla in dlas]
            new = ()
            for hh, c in enumerate(head_cols):
                dkeep = dkeeps[hh] + carry[2 * hh + 1]
                if strict is not None:
                    dkeep = jnp.where(strict, dkeep, 0.0)
                sig = jnp.exp(scores[hh][1])
                dz = ((dlas[hh] * (1.0 - sig) - dkeep * sig) * (BLK ** -0.5)).astype(BF16)
                dk_acc[rows, c] += _dot(dz, q_ref[:, c], 0, 0)
                dq_acc[:, c] += _dot(dz, k_ref[rows, c], 1, 0)
                new += (carry[2 * hh] + jnp.sum(scores[hh][2], axis=1, keepdims=True),
                        carry[2 * hh + 1] + jnp.sum(dlas[hh], axis=1, keepdims=True))
            return new

        carry = lax.fori_loop(0, i, lambda jj, cr: tiles(pl.multiple_of(jj * BLK, BLK), cr, None),
                              (jnp.zeros((BLK, 1), F32),) * (2 * hb))
        tiles(pl.multiple_of(i * BLK, BLK), carry, diag)
        dq_ref[...] = dq_acc[...].astype(dq_ref.dtype)

        @pl.when(i == nq - 1)
        def _():
            dk_ref[...] = dk_acc[...].astype(dk_ref.dtype)
            dv_ref[...] = dv_acc[...].astype(dv_ref.dtype)

    shape = jax.ShapeDtypeStruct((s, heads * BLK), BF16)
    return pl.pallas_call(
        body, name=name, grid=(groups, nq), out_shape=[shape, shape, shape],
        in_specs=[pl.BlockSpec((BLK, wide), lambda g, i: (i, g)),
                  pl.BlockSpec((s, wide), lambda g, i: (0, groups + g)),
                  pl.BlockSpec((s, wide), lambda g, i: (0, 2 * groups + g)),
                  pl.BlockSpec((BLK, wide), lambda g, i: (i, g)),
                  pl.BlockSpec((None, BLK, hb), lambda g, i: (g, i, 0))],
        out_specs=[pl.BlockSpec((BLK, wide), lambda g, i: (i, g)),
                   pl.BlockSpec((s, wide), lambda g, i: (0, g)),
                   pl.BlockSpec((s, wide), lambda g, i: (0, g))],
        scratch_shapes=[pltpu.VMEM((BLK, wide), F32), pltpu.VMEM((s, wide), F32), pltpu.VMEM((s, wide), F32)],
        compiler_params=_params("parallel", "arbitrary"),
    )(proj, proj, proj, do, tot)


def _xa_probs(q, k, hd):
    z = _dot(q, k, 1, 1) * (hd ** -0.5)
    z = z - jnp.max(z, axis=-1, keepdims=True)
    e = jnp.exp(z)
    return e / jnp.sum(e, axis=-1, keepdims=True)


def _xa_fwd(name, proj, mem_kv, width):
    s = proj.shape[0]
    m = mem_kv.shape[0]
    hd = width // XA_HEADS
    tq = _tile(s, 256)
    q_first = 5 * XA_HEADS

    def body(q_ref, k_ref, v_ref, o_ref):
        p = _xa_probs(q_ref[...], k_ref[...], hd)
        o_ref[...] = _dot(p, v_ref[...], 1, 0).astype(o_ref.dtype)

    return pl.pallas_call(
        body, name=name, grid=(XA_HEADS, s // tq), out_shape=jax.ShapeDtypeStruct((s, width), BF16),
        in_specs=[pl.BlockSpec((tq, hd), lambda h, i: (i, q_first + h)),
                  pl.BlockSpec((m, hd), lambda h, i: (0, h)),
                  pl.BlockSpec((m, hd), lambda h, i: (0, XA_HEADS + h))],
        out_specs=pl.BlockSpec((tq, hd), lambda h, i: (i, h)),
        compiler_params=_params("parallel", "parallel"),
    )(proj, mem_kv, mem_kv)


def _xa_bwd(name, proj, mem_kv, do, width):
    s = proj.shape[0]
    m = mem_kv.shape[0]
    hd = width // XA_HEADS
    tq = _tile(s, 256)
    nq = s // tq
    q_first = 5 * XA_HEADS

    def body(q_ref, k_ref, v_ref, do_ref, dq_ref, dk_ref, dv_ref, dk_acc, dv_acc):
        i = pl.program_id(1)

        @pl.when(i == 0)
        def _():
            dk_acc[...] = jnp.zeros_like(dk_acc)
            dv_acc[...] = jnp.zeros_like(dv_acc)

        q, k, v, dout = q_ref[...], k_ref[...], v_ref[...], do_ref[...]
        p = _xa_probs(q, k, hd)
        dp = _dot(dout, v, 1, 1)
        dv_acc[...] += _dot(p, dout, 0, 0)
        dz = ((p * (dp - jnp.sum(dp * p, axis=-1, keepdims=True))) * (hd ** -0.5)).astype(BF16)
        dq_ref[...] = _dot(dz, k, 1, 0).astype(dq_ref.dtype)
        dk_acc[...] += _dot(dz, q, 0, 0)

        @pl.when(i == nq - 1)
        def _():
            dk_ref[...] = dk_acc[...].astype(dk_ref.dtype)
            dv_ref[...] = dv_acc[...].astype(dv_ref.dtype)

    dq, dk, dv = pl.pallas_call(
        body, name=name, grid=(XA_HEADS, nq),
        out_shape=[jax.ShapeDtypeStruct((s, width), BF16), jax.ShapeDtypeStruct((m, width), BF16),
                   jax.ShapeDtypeStruct((m, width), BF16)],
        in_specs=[pl.BlockSpec((tq, hd), lambda h, i: (i, q_first + h)),
                  pl.BlockSpec((m, hd), lambda h, i: (0, h)),
                  pl.BlockSpec((m, hd), lambda h, i: (0, XA_HEADS + h)),
                  pl.BlockSpec((tq, hd), lambda h, i: (i, h))],
        out_specs=[pl.BlockSpec((tq, hd), lambda h, i: (i, h)),
                   pl.BlockSpec((m, hd), lambda h, i: (0, h)),
                   pl.BlockSpec((m, hd), lambda h, i: (0, h))],
        scratch_shapes=[pltpu.VMEM((m, hd), F32), pltpu.VMEM((m, hd), F32)],
        compiler_params=_params("parallel", "arbitrary"),
    )(proj, mem_kv, mem_kv, do)
    return dq, dk, dv


def _gm_pointwise(u_in, v_in, g_vnorm):
    return jax.nn.gelu(u_in), _rmsnorm(jax.nn.gelu(v_in), g_vnorm)


def _gm_mask():
    row = lax.broadcasted_iota(jnp.int32, (BLK, BLK), 0)
    col = lax.broadcasted_iota(jnp.int32, (BLK, BLK), 1)
    return (col // CHUNK) <= (row // CHUNK)


def _gm_fwd(name, proj, g_vnorm, w_s, b_s, width):
    s = proj.shape[0]
    groups = width // BLK

    def body(u_ref, v_ref, g_ref, w_ref, b_ref, o_ref):
        u, vn = _gm_pointwise(u_ref[...], v_ref[...], g_ref[...])
        vn = vn.astype(BF16)
        mask = _gm_mask()
        for g in range(groups):
            cols = slice(g * BLK, (g + 1) * BLK)
            w = jnp.where(mask, w_ref[g], 0.0)
            mixed = _dot(w, vn[:, cols], 1, 0) + b_ref[g]
            o_ref[:, cols] = (u[:, cols] * mixed).astype(o_ref.dtype)

    return pl.pallas_call(
        body, name=name, grid=(s // BLK,), out_shape=jax.ShapeDtypeStruct((s, width), BF16),
        in_specs=[pl.BlockSpec((BLK, width), lambda c: (c, 3)), pl.BlockSpec((BLK, width), lambda c: (c, 4)),
                  pl.BlockSpec((1, width), lambda c: (0, 0)),
                  pl.BlockSpec((groups, BLK, BLK), lambda c: (0, 0, 0)),
                  pl.BlockSpec((groups, BLK, 1), lambda c: (0, 0, 0))],
        out_specs=pl.BlockSpec((BLK, width), lambda c: (c, 0)),
        compiler_params=_params("parallel"),
    )(proj, proj, g_vnorm, w_s, b_s)


def _gm_bwd(name, proj, do, g_vnorm, w_s, b_s, width):
    s = proj.shape[0]
    groups = width // BLK

    def body(u_ref, v_ref, do_ref, g_ref, w_ref, b_ref, du_ref, dv_ref, dw_ref, db_ref, dg_ref, dvn_buf):
        first = pl.program_id(0) == 0
        (u, vn), vjp = jax.vjp(_gm_pointwise, u_ref[...], v_ref[...], g_ref[...])
        vn16 = vn.astype(BF16)
        dout = do_ref[...]
        mask = _gm_mask()
        du_parts = []
        for g in range(groups):
            cols = slice(g * BLK, (g + 1) * BLK)
            w = jnp.where(mask, w_ref[g], 0.0)
            mixed = _dot(w, vn16[:, cols], 1, 0) + b_ref[g]
            du_parts.append(dout[:, cols] * mixed)
            dmixed = dout[:, cols] * u[:, cols]
            dw = jnp.where(mask, _dot(dmixed, vn16[:, cols], 1, 1), 0.0)
            db = jnp.sum(dmixed, axis=1, keepdims=True)
            dvn_buf[:, cols] = _dot(w, dmixed, 0, 0)

            @pl.when(first)
            def _(g=g, dw=dw, db=db):
                dw_ref[g] = dw
                db_ref[g] = db

            @pl.when(jnp.logical_not(first))
            def _(g=g, dw=dw, db=db):
                dw_ref[g] += dw
                db_ref[g] += db

        du_in, dv_in, dg = vjp((jnp.concatenate(du_parts, axis=1), dvn_buf[...]))
        du_ref[...] = du_in.astype(du_ref.dtype)
        dv_ref[...] = dv_in.astype(dv_ref.dtype)

        @pl.when(first)
        def _():
            dg_ref[...] = dg

        @pl.when(jnp.logical_not(first))
        def _():
            dg_ref[...] += dg

    shape = jax.ShapeDtypeStruct((s, width), BF16)
    return pl.pallas_call(
        body, name=name, grid=(s // BLK,),
        out_shape=[shape, shape, jax.ShapeDtypeStruct((groups, BLK, BLK), F32),
                   jax.ShapeDtypeStruct((groups, BLK, 1), F32), jax.ShapeDtypeStruct((1, width), F32)],
        in_specs=[pl.BlockSpec((BLK, width), lambda c: (c, 3)), pl.BlockSpec((BLK, width), lambda c: (c, 4)),
                  pl.BlockSpec((BLK, width), lambda c: (c, 0)),
                  pl.BlockSpec((1, width), lambda c: (0, 0)),
                  pl.BlockSpec((groups, BLK, BLK), lambda c: (0, 0, 0)),
                  pl.BlockSpec((groups, BLK, 1), lambda c: (0, 0, 0))],
        out_specs=[pl.BlockSpec((BLK, width), lambda c: (c, 0)), pl.BlockSpec((BLK, width), lambda c: (c, 0)),
                   pl.BlockSpec((groups, BLK, BLK), lambda c: (0, 0, 0)),
                   pl.BlockSpec((groups, BLK, 1), lambda c: (0, 0, 0)),
                   pl.BlockSpec((1, width), lambda c: (0, 0))],
        scratch_shapes=[pltpu.VMEM((BLK, width), F32)],
        compiler_params=_params("arbitrary"),
    )(proj, proj, do, g_vnorm, w_s, b_s)


def _shift_down(v, k):
    row = lax.broadcasted_iota(jnp.int32, v.shape, 0)
    return jnp.where(row >= k, pltpu.roll(v, k, axis=0), 0.0)


def _shift_up(v, k):
    n = v.shape[0]
    row = lax.broadcasted_iota(jnp.int32, v.shape, 0)
    return jnp.where(row < n - k, pltpu.roll(v, n - k, axis=0), 0.0)


def _conv(gate, w, b):
    return b + w[0:1] * _shift_down(gate, 2) + w[1:2] * _shift_down(gate, 1) + w[2:3] * gate


def _conv_fwd(name, up, conv_w, conv_b):
    s, f2 = up.shape
    f = f2 // 2
    tc = _tile(f, 256, 128)
    nf = f // tc

    def body(g_ref, v_ref, w_ref, b_ref, o_ref):
        o_ref[...] = (jax.nn.gelu(_conv(g_ref[...], w_ref[...], b_ref[...])) * v_ref[...]).astype(o_ref.dtype)

    return pl.pallas_call(
        body, name=name, grid=(nf,), out_shape=jax.ShapeDtypeStruct((s, f), BF16),
        in_specs=[pl.BlockSpec((s, tc), lambda j: (0, j)), pl.BlockSpec((s, tc), lambda j: (0, nf + j)),
                  pl.BlockSpec((3, tc), lambda j: (0, j)), pl.BlockSpec((1, tc), lambda j: (0, j))],
        out_specs=pl.BlockSpec((s, tc), lambda j: (0, j)),
        compiler_params=_params("parallel"),
    )(up, up, conv_w, conv_b)


def _conv_bwd(name, up, da, conv_w, conv_b):
    s, f2 = up.shape
    f = f2 // 2
    tc = _tile(f, 256, 128)
    nf = f // tc

    def body(g_ref, v_ref, da_ref, w_ref, b_ref, dup_ref, dw_ref, db_ref):
        gate, val, dact, w = g_ref[...], v_ref[...], da_ref[...], w_ref[...]
        act, vjp = jax.vjp(jax.nn.gelu, _conv(gate, w, b_ref[...]))
        dup_ref[1] = (dact * act).astype(dup_ref.dtype)
        dconv = vjp(dact * val)[0]
        dup_ref[0] = (w[2:3] * dconv + w[1:2] * _shift_up(dconv, 1) + w[0:1] * _shift_up(dconv, 2)
                      ).astype(dup_ref.dtype)
        dw_ref[0:1, :] = jnp.sum(dconv * _shift_down(gate, 2), axis=0, keepdims=True)
        dw_ref[1:2, :] = jnp.sum(dconv * _shift_down(gate, 1), axis=0, keepdims=True)
        dw_ref[2:3, :] = jnp.sum(dconv * gate, axis=0, keepdims=True)
        db_ref[...] = jnp.sum(dconv, axis=0, keepdims=True)

    return pl.pallas_call(
        body, name=name, grid=(nf,),
        out_shape=[jax.ShapeDtypeStruct((2, s, f), BF16), jax.ShapeDtypeStruct((3, f), F32),
                   jax.ShapeDtypeStruct((1, f), F32)],
        in_specs=[pl.BlockSpec((s, tc), lambda j: (0, j)), pl.BlockSpec((s, tc), lambda j: (0, nf + j)),
                  pl.BlockSpec((s, tc), lambda j: (0, j)),
                  pl.BlockSpec((3, tc), lambda j: (0, j)), pl.BlockSpec((1, tc), lambda j: (0, j))],
        out_specs=[pl.BlockSpec((2, s, tc), lambda j: (0, 0, j)),
                   pl.BlockSpec((3, tc), lambda j: (0, j)), pl.BlockSpec((1, tc), lambda j: (0, j))],
        compiler_params=_params("parallel"),
    )(up, up, da, conv_w, conv_b)


def _adamw(w, g, m, v):
    m = ADAM_B1 * m + (1.0 - ADAM_B1) * g
    v = ADAM_B2 * v + (1.0 - ADAM_B2) * jnp.square(g)
    m_hat = m / (1.0 - ADAM_B1 ** ADAM_STEP)
    v_hat = v / (1.0 - ADAM_B2 ** ADAM_STEP)
    delta = -ADAM_LR * (m_hat / (jnp.sqrt(v_hat) + ADAM_EPS) + ADAM_WD * w)
    return delta, m, v


def _update_shard(name, layer, w, m, v, own, received, chip, previous=None):
    _, rows, cols = w.shape
    tr = _tile(rows, 256)

    def body(chip_ref, w_ref, m_ref, v_ref, own_ref, rec_ref, *rest):
        g_ref, d_ref, nm_ref, nv_ref = rest[-4:]
        g = (own_ref[...].astype(F32) + rec_ref[0].astype(F32) + rec_ref[1].astype(F32)
             + rec_ref[2].astype(F32))
        delta, new_m, new_v = _adamw(w_ref[...], g, m_ref[...], v_ref[...])
        g_ref[...] = g
        d_ref[...] = delta
        nm_ref[...] = new_m
        nv_ref[...] = new_v

    layer_spec = pl.BlockSpec((None, tr, cols), lambda i, chip_ref: (layer, i, 0))
    in_specs = [layer_spec] * 3 + [pl.BlockSpec((None, tr, cols), lambda i, chip_ref: (chip_ref[0], i, 0)),
                                   pl.BlockSpec((3, tr, cols), lambda i, chip_ref: (0, i, 0))]
    args = [chip, w, m, v, own, received]
    aliases = {}
    if previous is not None:
        in_specs += [ANY] * 4
        aliases = {len(args) + k: k for k in range(4)}
        args += list(previous)
    return pl.pallas_call(
        body, name=name, out_shape=[jax.ShapeDtypeStruct(w.shape, F32)] * 4,
        grid_spec=pltpu.PrefetchScalarGridSpec(
            num_scalar_prefetch=1, grid=(rows // tr,), in_specs=in_specs, out_specs=[layer_spec] * 4),
        input_output_aliases=aliases, compiler_params=_params("parallel"),
    )(*args)


def _sum_devices(name, gathered):
    _, rows, cols = gathered.shape

    def body(g_ref, o_ref):
        total = g_ref[0]
        for d in range(1, N_DEV):
            total = total + g_ref[d]
        o_ref[...] = total

    return pl.pallas_call(
        body, name=name, out_shape=jax.ShapeDtypeStruct((rows, cols), F32),
        in_specs=[pl.BlockSpec((N_DEV, rows, cols), lambda: (0, 0, 0))],
        out_specs=pl.BlockSpec((rows, cols), lambda: (0, 0)),
        compiler_params=pltpu.CompilerParams(vmem_limit_bytes=VMEM_LIMIT_V7X),
    )(gathered)


def _update_small(name, w, g, m, v):
    def body(w_ref, g_ref, m_ref, v_ref, d_ref, nm_ref, nv_ref):
        delta, new_m, new_v = _adamw(w_ref[...], g_ref[...], m_ref[...], v_ref[...])
        d_ref[...] = delta
        nm_ref[...] = new_m
        nv_ref[...] = new_v

    spec = pl.BlockSpec(w.shape, lambda: (0, 0))
    return pl.pallas_call(
        body, name=name, out_shape=[jax.ShapeDtypeStruct(w.shape, F32)] * 3,
        in_specs=[spec] * 4, out_specs=[spec] * 3,
        compiler_params=pltpu.CompilerParams(vmem_limit_bytes=VMEM_LIMIT_V7X),
    )(w, g, m, v)


def _pack(arrays):
    flat = jnp.concatenate([a.reshape(-1) for a in arrays])
    pad = (-flat.shape[0]) % (8 * BLK)
    return jnp.pad(flat, (0, pad)).reshape(-1, BLK)


def _unpack(packed, shapes):
    flat = packed.reshape(-1)
    out, at = [], 0
    for shape in shapes:
        size = 1
        for d in shape:
            size *= d
        out.append(flat[at:at + size].reshape(shape))
        at += size
    return out


SHARDED = ("w_in", "w_mem_kv", "w_gate", "w_br_sb", "w_br_gm", "w_br_xa", "w_out", "w_up", "w_down")
SMALL = ("g_mix_pre", "g_vnorm", "w_s", "b_s", "g_mem", "b_gate", "g_mix_post", "g_ffn_pre", "conv_w",
         "conv_b", "g_ffn_post")
WEIGHTS = ("g_mix_pre", "w_in", "g_vnorm", "w_s", "b_s", "g_mem", "w_mem_kv", "w_gate", "b_gate", "w_br_sb",
           "w_br_gm", "w_br_xa", "w_out", "g_mix_post", "g_ffn_pre", "w_up", "conv_w", "conv_b", "w_down",
           "g_ffn_post")


ROW_SHARDED = ("w_mem_kv", "w_out", "w_down")
GATHER_GROUPS = (("w_in",), ("w_mem_kv", "w_gate"), ("w_br_sb", "w_br_gm", "w_br_xa", "w_out"), ("w_up", "w_down"))
REDUCE_GROUPS = (("w_down", "w_up"), ("w_out", "w_br_sb", "w_br_gm", "w_br_xa", "w_mem_kv", "w_gate"), ("w_in",))


def _cast_into_place(name, w, layer, me):
    _, rows, cols = w.shape
    tr = _tile(rows, max(8, STREAM_BLOCK_BYTES // (4 * cols)))

    def body(me_ref, w_ref, o_ref):
        o_ref[...] = w_ref[...].astype(o_ref.dtype)

    return pl.pallas_call(
        body, name=name, out_shape=jax.ShapeDtypeStruct((N_DEV, rows, cols), BF16),
        grid_spec=pltpu.PrefetchScalarGridSpec(
            num_scalar_prefetch=1, grid=(rows // tr,),
            in_specs=[pl.BlockSpec((None, tr, cols), lambda i, me_ref: (layer, i, 0))],
            out_specs=pl.BlockSpec((None, tr, cols), lambda i, me_ref: (me_ref[0], i, 0))),
        compiler_params=_params("parallel"),
    )(me, w)


class _Gather:
    def __init__(self, tag, names, places, after):
        self.tag, self.names, self.n = tag, names, len(places)
        self.sems, self.places, self.token = _split_start(
            f"gather_start_{tag}", places, _gather_first_copies(self.n), after)

    def relay(self, after):
        self.sems, self.places = _split_wait(
            f"gather_relay_{self.tag}", self.places, self.sems, _gather_first_copies(self.n), after,
            then=_gather_relay_copies(self.n))

    def finish(self, after):
        places = _split_wait(f"gather_finish_{self.tag}", self.places, self.sems,
                             _gather_relay_copies(self.n), after)
        full = dict(zip(self.names, places))
        for name in self.names:
            if name in ROW_SHARDED:
                full[name] = full[name].reshape(-1, full[name].shape[-1])
        return full


class _Reduce:
    def __init__(self, tag, names, partials):
        self.tag, self.names, self.n = tag, names, len(partials)
        blocks = [p.reshape((N_CHIP, 2, -1, p.shape[-1])) for p in partials]
        lands = [lax.empty((N_CHIP,) + b.shape[2:], b.dtype) for b in blocks]
        self.sems, self.buffers, self.token = _split_start(
            f"pair_start_{tag}", blocks + lands, _pair_copies(self.n))

    def middle(self, after, core):
        n = self.n
        got = _split_wait(f"pair_finish_{self.tag}", self.buffers, self.sems, _pair_copies(n), after)
        sums = [_pair_sum(f"pair_sum_{name}{self.tag}", got[a], got[n + a], core)
                for a, name in enumerate(self.names)]
        lands = [lax.empty((3,) + s.shape[1:], s.dtype) for s in sums]
        self.sems, self.buffers, self.token = _split_start(
            f"chip_start_{self.tag}", sums + lands, _chip_copies(n))

    def finish(self, after):
        n = self.n
        got = _split_wait(f"chip_finish_{self.tag}", self.buffers, self.sems, _chip_copies(n), after)
        return [(name, got[a], got[n + a]) for a, name in enumerate(self.names)]


def kernel(x, mem, g_mix_pre, w_in, g_vnorm, w_s, b_s, g_mem, w_mem_kv, w_gate, b_gate, w_br_sb, w_br_gm, w_br_xa, w_out, g_mix_post, g_ffn_pre, w_up, conv_w, conv_b, w_down, g_ffn_post, loss_target, m_g_mix_pre, m_w_in, m_g_vnorm, m_w_s, m_b_s, m_g_mem, m_w_mem_kv, m_w_gate, m_b_gate, m_w_br_sb, m_w_br_gm, m_w_br_xa, m_w_out, m_g_mix_post, m_g_ffn_pre, m_w_up, m_conv_w, m_conv_b, m_w_down, m_g_ffn_post, v_g_mix_pre, v_w_in, v_g_vnorm, v_w_s, v_b_s, v_g_mem, v_w_mem_kv, v_w_gate, v_b_gate, v_w_br_sb, v_w_br_gm, v_w_br_xa, v_w_out, v_g_mix_post, v_g_ffn_pre, v_w_up, v_conv_w, v_conv_b, v_w_down, v_g_ffn_post):
    p = dict(g_mix_pre=g_mix_pre, w_in=w_in, g_vnorm=g_vnorm, w_s=w_s, b_s=b_s, g_mem=g_mem, w_mem_kv=w_mem_kv,
             w_gate=w_gate, b_gate=b_gate, w_br_sb=w_br_sb, w_br_gm=w_br_gm, w_br_xa=w_br_xa, w_out=w_out,
             g_mix_post=g_mix_post, g_ffn_pre=g_ffn_pre, w_up=w_up, conv_w=conv_w, conv_b=conv_b, w_down=w_down,
             g_ffn_post=g_ffn_post)
    mom = dict(g_mix_pre=m_g_mix_pre, w_in=m_w_in, g_vnorm=m_g_vnorm, w_s=m_w_s, b_s=m_b_s, g_mem=m_g_mem,
               w_mem_kv=m_w_mem_kv, w_gate=m_w_gate, b_gate=m_b_gate, w_br_sb=m_w_br_sb, w_br_gm=m_w_br_gm,
               w_br_xa=m_w_br_xa, w_out=m_w_out, g_mix_post=m_g_mix_post, g_ffn_pre=m_g_ffn_pre, w_up=m_w_up,
               conv_w=m_conv_w, conv_b=m_conv_b, w_down=m_w_down, g_ffn_post=m_g_ffn_post)
    var = dict(g_mix_pre=v_g_mix_pre, w_in=v_w_in, g_vnorm=v_g_vnorm, w_s=v_w_s, b_s=v_b_s, g_mem=v_g_mem,
               w_mem_kv=v_w_mem_kv, w_gate=v_w_gate, b_gate=v_b_gate, w_br_sb=v_w_br_sb, w_br_gm=v_w_br_gm,
               w_br_xa=v_w_br_xa, w_out=v_w_out, g_mix_post=v_g_mix_post, g_ffn_pre=v_g_ffn_pre, w_up=v_w_up,
               conv_w=v_conv_w, conv_b=v_conv_b, w_down=v_w_down, g_ffn_post=v_g_ffn_post)
    depth = w_in.shape[0]
    x = x[0]
    mem = mem[0]
    target = loss_target[0]
    s, d = x.shape
    width = d // 2
    heads = width // BLK
    cx, cy, cc = lax.axis_index("x"), lax.axis_index("y"), lax.axis_index("c")
    core = cc.astype(jnp.int32).reshape(1)
    chip = (2 * cx + cy).astype(jnp.int32).reshape(1)
    me = 4 * cx + 2 * cy + cc
    me_index = me.astype(jnp.int32).reshape(1)

    conv_all = _all_gather("gather_conv_w", [conv_w])[0]
    conv_w_full = jnp.transpose(conv_all, (1, 2, 0, 3)).reshape(depth, 3, -1)
    gathers = {}
    token = conv_all
    for l in range(depth):
        for gi, names in enumerate(GATHER_GROUPS):
            places = [_cast_into_place(f"cast_{n}{l}", p[n], l, me_index) for n in names]
            gathers[l, gi] = _Gather(f"{l}{gi}", names, places, token)
            token = gathers[l, gi].token

    kept, gathered = [], []
    h = _norm_fwd("pre_norm0", x, g_mix_pre[0][None, :], after=token)
    gathers[0, 0].relay(h)
    for l in range(depth):
        row = lambda name: p[name][l][None, :]
        first, second, third, fourth = (gathers[l, gi] for gi in range(4))
        w = first.finish(h)
        proj, proj16 = _mm_nn_cs(f"proj{l}", h, w["w_in"], (F32, BF16), tm=1024)
        o_sb, sb_tot = _sb_fwd(f"sb_fwd{l}", proj16, heads)
        b_s3 = b_s[l][:, :, None]
        o_gm = _gm_fwd(f"gm_fwd{l}", proj, row("g_vnorm"), w_s[l], b_s3, width)
        second.relay(o_gm)
        mn = _norm_fwd(f"mem_norm{l}", mem, row("g_mem"))
        w.update(second.finish(mn))
        third.relay(mn)
        mem_kv = _mm_nn(f"mem_kv{l}", mn, w["w_mem_kv"], (BF16,))
        o_xa = _xa_fwd(f"xa_fwd{l}", proj16, mem_kv, width)
        zg = _mm_nn_cs(f"gates{l}", h, w["w_gate"], tm=1024)
        w.update(third.finish(zg))
        ts = [_mm_nn_cs(f"branch_{n}{l}", o, w[f"w_br_{n}"], tm=2048)
              for n, o in (("sb", o_sb), ("gm", o_gm), ("xa", o_xa))]
        merged = _merge_fwd(f"merge{l}", zg, ts, row("b_gate"))
        fourth.relay(merged)
        y1 = _mm_nn(f"out{l}", merged, w["w_out"], tm=1024, tn=1024)
        x1, h2 = _residual_norm_fwd(f"mix_post{l}", x, y1, row("g_mix_post"), row("g_ffn_pre"))
        w.update(fourth.finish(h2))
        up = _mm_nn_cs(f"up{l}", h2, w["w_up"])
        if l + 1 < depth:
            gathers[l + 1, 0].relay(up)
        act = _conv_fwd(f"conv_fwd{l}", up, conv_w_full[l], row("conv_b"))
        y2 = _mm_nn(f"down{l}", act, w["w_down"])
        kept.append(dict(x=x, h=h, proj=proj, proj16=proj16, zg=zg, mn=mn, mem_kv=mem_kv, o_sb=o_sb, o_gm=o_gm,
                         o_xa=o_xa, ts=ts, merged=merged, y1=y1, x1=x1, h2=h2, up=up, act=act, y2=y2, b_s=b_s3,
                         sb_tot=sb_tot))
        gathered.append(w)
        if l + 1 < depth:
            x, h = _residual_norm_fwd(f"ffn_post{l}", x1, y2, g_ffn_post[l][None, :], g_mix_pre[l + 1][None, :])

    top = kept[-1]
    dx, dy2, loss_part, dg_ffn_post = _loss_bwd("loss", top["x1"], top["y2"], g_ffn_post[depth - 1][None, :], target)
    loss = lax.psum(loss_part[0, 0], ("x", "y", "c"))
    small_grads = [dict() for _ in range(depth)]
    small_grads[depth - 1]["g_ffn_post"] = dg_ffn_post
    outs = {n: None for n in SHARDED}
    in_flight = []

    def reduce_finish(entry, after):
        layer = int(entry[0].tag[0])
        for name, own, received in entry[0].finish(after):
            outs[name] = _update_shard(f"update_{name}{layer}", layer, p[name], mom[name], var[name],
                                       own, received, chip, outs[name])
        in_flight.remove(entry)

    def reduce_start(new):
        for entry in in_flight:
            entry[2] += entry[1] == 2
        in_flight.append([new, 1, 0])
        for entry in [e for e in in_flight if e[2] >= 2]:
            reduce_finish(entry, new.token)
        return new.token

    def reduce_middle(after):
        for entry in [e for e in in_flight if e[1] == 1]:
            entry[0].middle(after, core)
            after = entry[0].token
            entry[1] = 2
        return after

    behind = None

    for l in reversed(range(depth)):
        k = kept[l]
        w = gathered[l]
        sg = small_grads[l]
        row = lambda name: p[name][l][None, :]
        partials = {}
        group = lambda gi: _Reduce(f"{l}{gi}", REDUCE_GROUPS[gi], [partials[n] for n in REDUCE_GROUPS[gi]])
        partials["w_down"] = _mm_tn(f"d_w_down{l}", k["act"], dy2, after=behind)
        dact = _mm_nt(f"d_act{l}", dy2, w["w_down"], tm=2048)
        dup, sg["conv_w"], sg["conv_b"] = _conv_bwd(f"conv_bwd{l}", k["up"], dact, conv_w_full[l], row("conv_b"))
        partials["w_up"] = _mm_tn_cs(f"d_w_up{l}", k["h2"], dup)
        behind = reduce_start(group(0))
        dh2 = _mm_nt_cs(f"d_h2{l}", dup, w["w_up"], per_step=2, after=behind)
        dx1, dy1, sg["g_ffn_pre"], sg["g_mix_post"] = _mid_bwd(f"mid_bwd{l}", dx, dh2, k["x1"], k["y1"],
                                                                row("g_ffn_pre"), row("g_mix_post"))
        behind = reduce_middle(dy1)
        partials["w_out"] = _mm_tn(f"d_w_out{l}", k["merged"], dy1, tk=1024, after=behind)
        dmerged = _mm_nt(f"d_merged{l}", dy1, w["w_out"], tm=1024, tn=1024)
        dt_sb, dt_gm, dt_xa, dzg, sg["b_gate"] = _merge_bwd(f"merge_bwd{l}", dmerged, k["zg"], k["ts"], row("b_gate"))
        do = {}
        for n, dt in (("sb", dt_sb), ("gm", dt_gm), ("xa", dt_xa)):
            partials[f"w_br_{n}"] = _mm_tn_cs(f"d_w_br_{n}{l}", k[f"o_{n}"], dt, tk=1024)
            do[n] = _mm_nt_cs_whole(f"d_o_{n}{l}", dt, w[f"w_br_{n}"], F32 if n == "gm" else BF16)
        dq_xa, dk_mem, dv_mem = _xa_bwd(f"xa_bwd{l}", k["proj16"], k["mem_kv"], do["xa"], width)
        dmem_kv = jnp.concatenate([dk_mem, dv_mem], axis=1)
        partials["w_mem_kv"] = _mm_tn(f"d_w_mem_kv{l}", k["mn"], dmem_kv)
        partials["w_gate"] = _mm_tn_cs(f"d_w_gate{l}", k["h"], dzg, tk=1024)
        behind = reduce_start(group(1))
        dh_gate = _mm_nt_cs(f"d_h_gate{l}", dzg, w["w_gate"], per_step=4, after=behind)
        behind = reduce_middle(dh_gate)
        dmn = _mm_nt(f"d_mn{l}", dmem_kv, w["w_mem_kv"], after=behind)
        sg["g_mem"] = _mem_norm_bwd(f"mem_norm_bwd{l}", dmn, mem, row("g_mem"))
        du, dvg, sg["w_s"], db_s, sg["g_vnorm"] = _gm_bwd(f"gm_bwd{l}", k["proj"], do["gm"], row("g_vnorm"),
                                                          p["w_s"][l], k["b_s"], width)
        sg["b_s"] = db_s[:, :, 0]
        dq, dk, dv = _sb_bwd(f"sb_bwd{l}", k["proj16"], do["sb"], k["sb_tot"], heads)
        dproj = jnp.concatenate([dq, dk, dv, du, dvg, dq_xa], axis=1)
        partials["w_in"] = _mm_tn_cs(f"d_w_in{l}", k["h"], dproj, tk=1024)
        behind = reduce_start(group(2))
        dh_in = _mm_nt_cs(f"d_h_in{l}", dproj, w["w_in"], per_step=4, after=behind)
        if l > 0:
            below = kept[l - 1]
            dx, dy2, sg["g_mix_pre"], small_grads[l - 1]["g_ffn_post"] = _bottom_bwd(
                f"bottom_bwd{l}", dx1, dh_in, dh_gate, k["x"], row("g_mix_pre"), below["y2"],
                p["g_ffn_post"][l - 1][None, :])
        else:
            dx, sg["g_mix_pre"] = _bottom_bwd(f"bottom_bwd{l}", dx1, dh_in, dh_gate, k["x"], row("g_mix_pre"))
        behind = reduce_middle(dx)

    for entry in list(in_flight):
        reduce_finish(entry, behind)

    small_shapes = [small_grads[l][n].shape for l in range(depth) for n in SMALL]
    packed = _pack([small_grads[l][n] for l in range(depth) for n in SMALL])
    total = _sum_devices("sum_small", _all_gather("gather_small", [packed])[0])
    flat = _unpack(total, small_shapes)
    grads = {}
    for i, n in enumerate(SMALL):
        g = jnp.stack([flat[l * len(SMALL) + i] for l in range(depth)]).reshape((depth,) + p[n].shape[1:]
                                                                                 if n != "conv_w" else (depth, 3, -1))
        if n == "conv_w":
            per = conv_w.shape[2]
            g = lax.dynamic_slice_in_dim(g, me * per, per, axis=2)
        grads[n] = g
    delta, new_m, new_v = _update_small(
        "update_small", _pack([p[n] for n in SMALL]), _pack([grads[n] for n in SMALL]),
        _pack([mom[n] for n in SMALL]), _pack([var[n] for n in SMALL]))
    shapes = [p[n].shape for n in SMALL]
    for n, dl, nm, nv in zip(SMALL, _unpack(delta, shapes), _unpack(new_m, shapes), _unpack(new_v, shapes)):
        outs[n] = (grads[n], dl, nm, nv)

    result = [loss, dx[None]]
    for k in range(4):
        result += [outs[n][k] for n in WEIGHTS]
    return tuple(result)
```

```python
import functools

import jax
import jax.numpy as jnp
from jax import lax
from jax.experimental import pallas as pl
from jax.experimental.pallas import tpu as pltpu

F32 = jnp.float32
BF16 = jnp.bfloat16
N_DEV = 8
N_CHIP = 4
EPS = 1e-6
CHUNK = 64
BLK = 128
XA_HEADS = 4
SB_HEADS_PER_STEP = 8
ADAM_LR = 0.001
ADAM_B1 = 0.9
ADAM_B2 = 0.999
ADAM_EPS = 1e-08
ADAM_WD = 0.01
ADAM_STEP = 10
VMEM_LIMIT_V7X = 56 * 1024 * 1024
STREAM_BLOCK_BYTES = 4 * 1024 * 1024
MESH = pl.DeviceIdType.MESH
ANY = pl.BlockSpec(memory_space=pl.ANY)


def _params(*sem):
    return pltpu.CompilerParams(dimension_semantics=sem, vmem_limit_bytes=VMEM_LIMIT_V7X)


def _tile(n, pref, mult=8):
    best = None
    for t in range(mult, min(n, pref) + 1, mult):
        if n % t == 0:
            best = t
    return best if best is not None else n


def _bf(x):
    return x if x.dtype == BF16 else x.astype(BF16)


def _dot(a, b, ca, cb):
    return lax.dot_general(_bf(a), _bf(b), (((ca,), (cb,)), ((), ())), preferred_element_type=F32)


def _rmsnorm(x, g):
    return (x * lax.rsqrt(jnp.mean(x * x, axis=-1, keepdims=True) + EPS)) * g


def _position():
    x, y, c = lax.axis_index("x"), lax.axis_index("y"), lax.axis_index("c")
    return x, y, c


def _all_gather(name, shards):
    n = len(shards)

    def body(*refs):
        ins, outs = refs[:n], refs[n:2 * n]
        send_sems, recv_sems, local_sems = refs[2 * n:]
        x, y, c = _position()
        me = 4 * x + 2 * y + c
        sibling = (x, y, 1 - c)
        chips = [(1 - x, y), (x, 1 - y), (1 - x, 1 - y)]

        def copy(a, k, block, to, src=None):
            dst = outs[a].at[block]
            return pltpu.make_async_remote_copy(
                src_ref=dst if src is None else src, dst_ref=dst, send_sem=send_sems.at[a, k],
                recv_sem=recv_sems.at[a, k], device_id=to, device_id_type=MESH)

        local = [pltpu.make_async_copy(ins[a], outs[a].at[me], local_sems.at[a]) for a in range(n)]
        for cp in local:
            cp.start()
        first = []
        for a in range(n):
            first.append(copy(a, 0, me, sibling, src=ins[a]))
            for j, chip in enumerate(chips):
                first.append(copy(a, 1 + j, me, (*chip, c), src=ins[a]))
        for cp in first:
            cp.start()
        passed = []
        for a in range(n):
            for j, (px, py) in enumerate(chips):
                block = 4 * px + 2 * py + c
                copy(a, 1 + j, block, sibling).wait_recv()
                forward = copy(a, 4 + j, block, sibling)
                forward.start()
                passed.append(forward)
        for a in range(n):
            copy(a, 0, 4 * x + 2 * y + (1 - c), sibling).wait_recv()
            for j, (px, py) in enumerate(chips):
                copy(a, 4 + j, 4 * px + 2 * py + (1 - c), sibling).wait_recv()
        for cp in first + passed:
            cp.wait_send()
        for cp in local:
            cp.wait()

    return pl.pallas_call(
        body, name=name,
        out_shape=[jax.ShapeDtypeStruct((N_DEV,) + s.shape, s.dtype) for s in shards],
        in_specs=[ANY] * n, out_specs=[ANY] * n,
        scratch_shapes=[pltpu.SemaphoreType.DMA((n, 7)), pltpu.SemaphoreType.DMA((n, 7)),
                        pltpu.SemaphoreType.DMA((n,))],
    )(*shards)


HBM = pl.BlockSpec(memory_space=pltpu.HBM)
SEM = pl.BlockSpec(memory_space=pltpu.SEMAPHORE)
DATAFLOW = pltpu.SideEffectType.DATAFLOW_SIDE_EFFECTING


def _remote(src, dst, send, recv, k, peer):
    return pltpu.make_async_remote_copy(src_ref=src, dst_ref=dst, send_sem=send.at[k], recv_sem=recv.at[k],
                                        device_id=peer, device_id_type=MESH)


def _gather_first_copies(n):
    def build(refs, send, recv):
        x, y, c = _position()
        me = 4 * x + 2 * y + c
        peers = [(x, y, 1 - c), (1 - x, y, c), (x, 1 - y, c), (1 - x, 1 - y, c)]
        return [_remote(refs[a].at[me], refs[a].at[me], send, recv, 4 * a + k, peer)
                for a in range(n) for k, peer in enumerate(peers)]
    return build, 4 * n


def _gather_relay_copies(n):
    def build(refs, send, recv):
        x, y, c = _position()
        blocks = [4 * px + 2 * py + c for px, py in ((1 - x, y), (x, 1 - y), (1 - x, 1 - y))]
        return [_remote(refs[a].at[blk], refs[a].at[blk], send, recv, 3 * a + j, (x, y, 1 - c))
                for a in range(n) for j, blk in enumerate(blocks)]
    return build, 3 * n


def _pair_copies(n):
    def build(refs, send, recv):
        x, y, c = _position()
        return [_remote(refs[a].at[k, 1 - c], refs[n + a].at[k], send, recv, N_CHIP * a + k, (x, y, 1 - c))
                for a in range(n) for k in range(N_CHIP)]
    return build, N_CHIP * n


def _chip_copies(n):
    def build(refs, send, recv):
        x, y, c = _position()
        chips = ((1 - x, y), (x, 1 - y), (1 - x, 1 - y))
        return [_remote(refs[a].at[2 * px + py], refs[n + a].at[j], send, recv, 3 * a + j, (px, py, c))
                for a in range(n) for j, (px, py) in enumerate(chips)]
    return build, 3 * n


def _split_start(name, buffers, copies, after=None):
    build, n_copies = copies
    nb = len(buffers)
    extra = [] if after is None else [after]

    def body(*refs):
        at = nb + len(extra)
        for cp in build(refs[:nb], refs[at], refs[at + 1]):
            cp.start()
        token = refs[at + 2 + nb]
        token[...] = jnp.zeros_like(token)

    outs = pl.pallas_call(
        body, name=name,
        out_shape=[pltpu.SemaphoreType.DMA((n_copies,)), pltpu.SemaphoreType.DMA((n_copies,))]
        + [pltpu.HBM(b.shape, b.dtype) for b in buffers] + [jax.ShapeDtypeStruct((8, BLK), F32)],
        in_specs=[HBM] * nb + [ANY] * len(extra),
        out_specs=[SEM, SEM] + [HBM] * nb + [pl.BlockSpec(memory_space=pltpu.VMEM)],
        input_output_aliases={i: 2 + i for i in range(nb)},
        compiler_params=pltpu.CompilerParams(has_side_effects=DATAFLOW),
    )(*[pltpu.with_memory_space_constraint(b, pltpu.HBM) for b in buffers], *extra)
    return (outs[0], outs[1]), list(outs[2:2 + nb]), outs[-1]


def _split_wait(name, buffers, sems, copies, after, then=None):
    build, _ = copies
    nb = len(buffers)
    n_next = 0 if then is None else then[1]
    first_out = nb + 3

    def body(*refs):
        for cp in build(refs[:nb], refs[nb], refs[nb + 1]):
            cp.wait()
        if then is not None:
            for cp in then[0](refs[:nb], refs[first_out], refs[first_out + 1]):
                cp.start()

    sem_shapes = [] if then is None else [pltpu.SemaphoreType.DMA((n_next,))] * 2
    outs = pl.pallas_call(
        body, name=name,
        out_shape=sem_shapes + [pltpu.HBM(b.shape, b.dtype) for b in buffers],
        in_specs=[HBM] * nb + [SEM, SEM, ANY],
        out_specs=[SEM] * len(sem_shapes) + [HBM] * nb,
        input_output_aliases={i: len(sem_shapes) + i for i in range(nb)},
        compiler_params=pltpu.CompilerParams(has_side_effects=DATAFLOW),
    )(*buffers, sems[0], sems[1], after)
    if then is None:
        return list(outs)
    return (outs[0], outs[1]), list(outs[2:])


def _pair_sum(name, partial, received, core):
    _, _, rows, cols = partial.shape
    tr = _tile(rows, max(8, STREAM_BLOCK_BYTES // (2 * cols)))

    def body(core_ref, p_ref, r_ref, o_ref):
        o_ref[...] = (p_ref[...].astype(F32) + r_ref[...].astype(F32)).astype(o_ref.dtype)

    return pl.pallas_call(
        body, name=name, out_shape=jax.ShapeDtypeStruct((N_CHIP, rows, cols), BF16),
        grid_spec=pltpu.PrefetchScalarGridSpec(
            num_scalar_prefetch=1, grid=(N_CHIP, rows // tr),
            in_specs=[pl.BlockSpec((None, None, tr, cols), lambda k, i, core: (k, core[0], i, 0)),
                      pl.BlockSpec((None, tr, cols), lambda k, i, core: (k, i, 0))],
            out_specs=pl.BlockSpec((None, tr, cols), lambda k, i, core: (k, i, 0))),
        compiler_params=_params("parallel", "parallel"),
    )(core, partial, received)


def _mm(name, a, b, out_shape, out_dtypes, grid, a_spec, b_spec, o_spec, ca, cb, k_steps=1, after=None):
    n_out = len(out_dtypes)
    extra = [] if after is None else [after]

    def body(a_ref, b_ref, *rest):
        rest = rest[len(extra):]
        o_refs = rest[:n_out]
        part = _dot(a_ref[...], b_ref[...], ca, cb)
        if k_steps == 1:
            for o in o_refs:
                o[...] = part.astype(o.dtype)
            return
        acc = rest[n_out]
        k = pl.program_id(len(grid) - 1)

        @pl.when(k == 0)
        def _():
            acc[...] = part

        @pl.when(k > 0)
        def _():
            acc[...] += part

        @pl.when(k == k_steps - 1)
        def _():
            for o in o_refs:
                o[...] = acc[...].astype(o.dtype)

    o_block = tuple(d for d in o_spec.block_shape if d is not None)
    sem = ("parallel",) * (len(grid) - 1) + (("arbitrary",) if k_steps > 1 else ("parallel",))
    out = pl.pallas_call(
        body, name=name, grid=grid,
        out_shape=[jax.ShapeDtypeStruct(out_shape, d) for d in out_dtypes],
        in_specs=[a_spec, b_spec] + [ANY] * len(extra), out_specs=[o_spec] * n_out,
        scratch_shapes=[pltpu.VMEM(o_block, F32)] if k_steps > 1 else [],
        compiler_params=_params(*sem),
    )(a, b, *extra)
    return out if n_out > 1 else out[0]


def _mm_nn_cs(name, a, w, out_dtypes=(F32,), tm=512, **kw):
    m, k = a.shape
    _, _, ns = w.shape
    tm = _tile(m, tm)
    return _mm(name, a, w, (m, N_DEV * ns), out_dtypes, (m // tm, N_DEV),
               pl.BlockSpec((tm, k), lambda i, j: (i, 0)),
               pl.BlockSpec((None, k, ns), lambda i, j: (j, 0, 0)),
               pl.BlockSpec((tm, ns), lambda i, j: (i, j)), 1, 0, **kw)


def _mm_nn(name, a, w, out_dtypes=(F32,), tm=512, tn=512, **kw):
    m, k = a.shape
    _, n = w.shape
    tm, tn = _tile(m, tm), _tile(n, tn, 128)
    return _mm(name, a, w, (m, n), out_dtypes, (m // tm, n // tn),
               pl.BlockSpec((tm, k), lambda i, j: (i, 0)),
               pl.BlockSpec((k, tn), lambda i, j: (0, j)),
               pl.BlockSpec((tm, tn), lambda i, j: (i, j)), 1, 0, **kw)


def _mm_nt_cs(name, a, w, tm=512, per_step=1, after=None):
    _, k, ns = w.shape
    parts = list(a) if isinstance(a, (tuple, list)) else [a]
    m = parts[0].shape[0]
    tm = _tile(m, tm)
    steps = N_DEV // per_step
    per_part = steps // len(parts)
    extra = [] if after is None else [after]

    def body(*refs):
        a_refs, w_ref, o_ref = refs[:len(parts)], refs[len(parts)], refs[len(parts) + 1 + len(extra)]
        j = pl.program_id(1)

        def contract(a_ref):
            part = _dot(a_ref[:, 0:ns], w_ref[0], 1, 1)
            for u in range(1, per_step):
                part = part + _dot(a_ref[:, u * ns:(u + 1) * ns], w_ref[u], 1, 1)

            @pl.when(j == 0)
            def _():
                o_ref[...] = part

            @pl.when(j > 0)
            def _():
                o_ref[...] += part

        for n, a_ref in enumerate(a_refs):
            pl.when(jnp.logical_and(j >= n * per_part, j < (n + 1) * per_part))(functools.partial(contract, a_ref))

    a_specs = [pl.BlockSpec((tm, per_step * ns),
                            functools.partial(lambda i, j, n: (i, jnp.clip(j - n * per_part, 0, per_part - 1)), n=n))
               for n in range(len(parts))]
    return pl.pallas_call(
        body, name=name, grid=(m // tm, steps), out_shape=jax.ShapeDtypeStruct((m, k), F32),
        in_specs=a_specs + [pl.BlockSpec((per_step, k, ns), lambda i, j: (j, 0, 0))] + [ANY] * len(extra),
        out_specs=pl.BlockSpec((tm, k), lambda i, j: (i, 0)),
        compiler_params=_params("parallel", "arbitrary"),
    )(*parts, w, *extra)


def _mm_nt_cs_whole(name, a, w, out_dtype, tm=512):
    m, _ = a.shape
    _, k, ns = w.shape
    tm = _tile(m, tm)

    def body(a_ref, w_ref, o_ref):
        acc = _dot(a_ref[:, 0:ns], w_ref[0], 1, 1)
        for j in range(1, N_DEV):
            acc = acc + _dot(a_ref[:, j * ns:(j + 1) * ns], w_ref[j], 1, 1)
        o_ref[...] = acc.astype(o_ref.dtype)

    return pl.pallas_call(
        body, name=name, grid=(m // tm,), out_shape=jax.ShapeDtypeStruct((m, k), out_dtype),
        in_specs=[pl.BlockSpec((tm, N_DEV * ns), lambda i: (i, 0)),
                  pl.BlockSpec((N_DEV, k, ns), lambda i: (0, 0, 0))],
        out_specs=pl.BlockSpec((tm, k), lambda i: (i, 0)),
        compiler_params=_params("parallel"),
    )(a, w)


def _mm_nt(name, a, w, out_dtypes=(F32,), tm=512, tn=512, **kw):
    m, k = a.shape
    n, _ = w.shape
    tm, tn = _tile(m, tm), _tile(n, tn, 128)
    return _mm(name, a, w, (m, n), out_dtypes, (m // tm, n // tn),
               pl.BlockSpec((tm, k), lambda i, j: (i, 0)),
               pl.BlockSpec((tn, k), lambda i, j: (j, 0)),
               pl.BlockSpec((tm, tn), lambda i, j: (i, j)), 1, 1, **kw)


def _mm_tn_cs(name, a, g, tk=512, after=None):
    s, k = a.shape
    parts = list(g) if isinstance(g, (tuple, list)) else [g]
    per_part = N_DEV // len(parts)
    ns = parts[0].shape[1] // per_part
    tk = _tile(k, tk, 128)
    extra = [] if after is None else [after]

    def body(a_ref, *rest):
        g_refs, o_ref = rest[:len(parts)], rest[len(parts) + len(extra)]
        j = pl.program_id(1)

        def shard(g_ref):
            o_ref[...] = _dot(a_ref[...], g_ref[...], 0, 0).astype(o_ref.dtype)

        for n, g_ref in enumerate(g_refs):
            pl.when(jnp.logical_and(j >= n * per_part, j < (n + 1) * per_part))(functools.partial(shard, g_ref))

    g_specs = [pl.BlockSpec((s, ns),
                            functools.partial(lambda i, j, n: (0, jnp.clip(j - n * per_part, 0, per_part - 1)), n=n))
               for n in range(len(parts))]
    return pl.pallas_call(
        body, name=name, grid=(k // tk, N_DEV), out_shape=jax.ShapeDtypeStruct((N_DEV, k, ns), BF16),
        in_specs=[pl.BlockSpec((s, tk), lambda i, j: (0, i))] + g_specs + [ANY] * len(extra),
        out_specs=pl.BlockSpec((None, tk, ns), lambda i, j: (j, i, 0)),
        compiler_params=_params("parallel", "arbitrary"),
    )(a, *parts, *extra)


def _mm_tn(name, a, g, tk=512, tn=1024, **kw):
    s, k = a.shape
    n = g.shape[1]
    tk, tn = _tile(k, tk, 128), _tile(n, tn, 128)
    return _mm(name, a, g, (k, n), (BF16,), (k // tk, n // tn),
               pl.BlockSpec((s, tk), lambda i, j: (0, i)),
               pl.BlockSpec((s, tn), lambda i, j: (0, j)),
               pl.BlockSpec((tk, tn), lambda i, j: (i, j)), 0, 0, **kw)


def _rowwise(name, fn, rows, bcast, outs, reds, tm, after=None):
    n_rows = rows[0][0].shape[0]
    tm = _tile(n_rows, tm)
    n_in, n_b, n_o, n_r = len(rows), len(bcast), len(outs), len(reds)
    extra = [] if after is None else [after]

    def body(*refs):
        ins = refs[:n_in + n_b]
        refs = refs[len(extra):]
        o_refs = refs[n_in + n_b:n_in + n_b + n_o]
        r_refs = refs[n_in + n_b + n_o:]
        vals = fn(*[r[...] for r in ins])
        for o, v in zip(o_refs, vals[:n_o]):
            o[...] = v.astype(o.dtype)
        first = pl.program_id(0) == 0
        for r, v in zip(r_refs, vals[n_o:]):
            @pl.when(first)
            def _(r=r, v=v):
                r[...] = v.astype(r.dtype)

            @pl.when(jnp.logical_not(first))
            def _(r=r, v=v):
                r[...] += v.astype(r.dtype)

    def whole(shape):
        zeros = (0,) * len(shape)
        return pl.BlockSpec(shape, lambda i: zeros)

    in_specs = [pl.BlockSpec((tm, w), functools.partial(lambda i, cb: (i, cb), cb=cb)) for _, w, cb in rows]
    in_specs += [whole(b.shape) for b in bcast] + [ANY] * len(extra)
    out_specs = [pl.BlockSpec((tm, w), lambda i: (i, 0)) for w, _ in outs]
    out_specs += [whole(s) for s, _ in reds]
    out_shape = [jax.ShapeDtypeStruct((n_rows, w), d) for w, d in outs]
    out_shape += [jax.ShapeDtypeStruct(s, d) for s, d in reds]
    return pl.pallas_call(
        body, name=name, grid=(n_rows // tm,), out_shape=out_shape, in_specs=in_specs,
        out_specs=out_specs, compiler_params=_params("arbitrary" if n_r else "parallel"),
    )(*[r[0] for r in rows], *bcast, *extra)


def _norm_fwd(name, x, g, after=None):
    d = x.shape[1]
    return _rowwise(name, lambda xt, gt: (_rmsnorm(xt, gt),), [(x, d, 0)], [g], [(d, BF16)], [], 256,
                    after=after)[0]


def _residual_norm_fwd(name, x, y, g_post, g_next):
    d = x.shape[1]

    def fn(xt, yt, gp, gn):
        new = xt + _rmsnorm(yt, gp)
        return new, _rmsnorm(new, gn)

    return _rowwise(name, fn, [(x, d, 0), (y, d, 0)], [g_post, g_next], [(d, F32), (d, BF16)], [], 256)


def _merge(z0, z1, z2, t0, t1, t2, b0, b1, b2):
    return (jax.nn.sigmoid(z0 + b0) * t0 + jax.nn.sigmoid(z1 + b1) * t1 + jax.nn.sigmoid(z2 + b2) * t2)


def _merge_fwd(name, zg, ts, b_gate):
    d = ts[0].shape[1]
    rows = [(zg, d, b) for b in range(3)] + [(t, d, 0) for t in ts]
    bias = [b_gate[:, b * d:(b + 1) * d] for b in range(3)]
    return _rowwise(name, lambda *v: (_merge(*v),), rows, bias, [(d, BF16)], [], 128)[0]


def _merge_bwd(name, dmerged, zg, ts, b_gate):
    d = ts[0].shape[1]
    rows = [(dmerged, d, 0)] + [(zg, d, b) for b in range(3)] + [(t, d, 0) for t in ts]
    bias = [b_gate[:, b * d:(b + 1) * d] for b in range(3)]

    def fn(dm, *v):
        _, vjp = jax.vjp(_merge, *v)
        dz0, dz1, dz2, dt0, dt1, dt2, db0, db1, db2 = vjp(dm)
        return (dt0, dt1, dt2, jnp.concatenate([dz0, dz1, dz2], axis=1),
                jnp.concatenate([db0, db1, db2], axis=1))

    return _rowwise(name, fn, rows, bias, [(d, BF16)] * 3 + [(3 * d, BF16)], [((1, 3 * d), F32)], 128)


def _loss_bwd(name, x, y, g_post, target):
    d = x.shape[1]

    def loss(xt, yt, gp, tt):
        err = xt + _rmsnorm(yt, gp) - tt
        return 0.5 * jnp.sum(jnp.mean(err * err, axis=-1))

    def fn(xt, yt, tt, gp):
        val, (dx, dy, dg) = jax.value_and_grad(loss, argnums=(0, 1, 2))(xt, yt, gp, tt)
        return dx, dy, jnp.full((1, BLK), val, F32), dg

    return _rowwise(name, fn, [(x, d, 0), (y, d, 0), (target, d, 0)], [g_post],
                    [(d, F32), (d, BF16)], [((1, BLK), F32), ((1, d), F32)], 256)


def _mid_bwd(name, dx_out, dh, x_mid, y, g_pre, g_post):
    d = dx_out.shape[1]

    def fn(dxo, dht, xm, yt, gpre, gpost):
        _, vjp_pre = jax.vjp(_rmsnorm, xm, gpre)
        dxm, dgpre = vjp_pre(dht)
        dxm = dxo + dxm
        _, vjp_post = jax.vjp(_rmsnorm, yt, gpost)
        dy, dgpost = vjp_post(dxm)
        return dxm, dy, dgpre, dgpost

    return _rowwise(name, fn, [(dx_out, d, 0), (dh, d, 0), (x_mid, d, 0), (y, d, 0)], [g_pre, g_post],
                    [(d, F32), (d, BF16)], [((1, d), F32), ((1, d), F32)], 256)


def _bottom_bwd(name, dx_mid, dh_a, dh_b, x, g_pre, y_prev=None, g_post_prev=None):
    d = dx_mid.shape[1]
    rows = [(dx_mid, d, 0), (dh_a, d, 0), (dh_b, d, 0), (x, d, 0)]
    if y_prev is None:
        def fn(dxm, da, db, xt, gpre):
            _, vjp_pre = jax.vjp(_rmsnorm, xt, gpre)
            dx, dgpre = vjp_pre(da + db)
            return dxm + dx, dgpre

        return _rowwise(name, fn, rows, [g_pre], [(d, F32)], [((1, d), F32)], 256)

    def fn2(dxm, da, db, xt, yt, gpre, gpost):
        _, vjp_pre = jax.vjp(_rmsnorm, xt, gpre)
        dx, dgpre = vjp_pre(da + db)
        dx = dxm + dx
        _, vjp_post = jax.vjp(_rmsnorm, yt, gpost)
        dy, dgpost = vjp_post(dx)
        return dx, dy, dgpre, dgpost

    return _rowwise(name, fn2, rows + [(y_prev, d, 0)], [g_pre, g_post_prev],
                    [(d, F32), (d, BF16)], [((1, d), F32), ((1, d), F32)], 256)


def _mem_norm_bwd(name, dmn, mem, g_mem):
    d = mem.shape[1]

    def fn(dt, mt, g):
        _, vjp = jax.vjp(_rmsnorm, mt, g)
        return (vjp(dt)[1],)

    return _rowwise(name, fn, [(dmn, d, 0), (mem, d, 0)], [g_mem], [], [((1, d), F32)], 256)[0]


def _split_dot(v, tri):
    hi = v.astype(BF16)
    lo = (v - hi.astype(F32)).astype(BF16)
    return (jnp.dot(hi, tri, preferred_element_type=F32) + jnp.dot(lo, tri, preferred_element_type=F32))


def _sb_scores(q, kb, strict=None):
    z = _dot(q, kb, 1, 1) * (BLK ** -0.5)
    soft = jnp.log(1.0 + jnp.exp(-jnp.abs(z)))
    log_beta = jnp.minimum(z, 0.0) - soft
    log_keep = jnp.minimum(-z, 0.0) - soft
    if strict is not None:
        log_keep = jnp.where(strict, log_keep, 0.0)
    return z, log_beta, log_keep


def _block_iotas():
    return (lax.broadcasted_iota(jnp.int32, (BLK, BLK), 0), lax.broadcasted_iota(jnp.int32, (BLK, BLK), 1))


def _sb_fwd(name, proj, heads):
    s = proj.shape[0]
    nq = s // BLK
    hb = _tile(heads, SB_HEADS_PER_STEP, 1)
    groups = heads // hb
    wide = hb * BLK

    def body(q_ref, k_ref, v_ref, o_ref, tot_ref, acc_ref):
        i = pl.program_id(1)
        row, col = _block_iotas()
        diag = col < row
        later = (row > col).astype(BF16)
        head_cols = [slice(hh * BLK, (hh + 1) * BLK) for hh in range(hb)]

        def tiles(off, runs, strict):
            scores = [_sb_scores(q_ref[:, c], k_ref[pl.ds(off, BLK), c], strict) for c in head_cols]
            suffixes = [_split_dot(sc[2], later) for sc in scores]
            new = []
            for hh, c in enumerate(head_cols):
                a = jnp.exp(scores[hh][1] + suffixes[hh] + runs[hh])
                if strict is not None:
                    a = jnp.where(strict, a, 0.0)
                acc_ref[:, c] += _dot(a, v_ref[pl.ds(off, BLK), c], 1, 0)
                new.append(runs[hh] + jnp.sum(scores[hh][2], axis=1, keepdims=True))
            return tuple(new)

        acc_ref[...] = jnp.zeros_like(acc_ref)
        runs = tiles(pl.multiple_of(i * BLK, BLK), (jnp.zeros((BLK, 1), F32),) * hb, diag)
        runs = lax.fori_loop(0, i, lambda t, r: tiles(pl.multiple_of((i - 1 - t) * BLK, BLK), r, None), runs)
        o_ref[...] = acc_ref[...].astype(o_ref.dtype)
        for hh in range(hb):
            tot_ref[:, hh:hh + 1] = runs[hh]

    return pl.pallas_call(
        body, name=name, grid=(groups, nq),
        out_shape=[jax.ShapeDtypeStruct((s, heads * BLK), BF16), jax.ShapeDtypeStruct((groups, s, hb), F32)],
        in_specs=[pl.BlockSpec((BLK, wide), lambda g, i: (i, g)),
                  pl.BlockSpec((s, wide), lambda g, i: (0, groups + g)),
                  pl.BlockSpec((s, wide), lambda g, i: (0, 2 * groups + g))],
        out_specs=[pl.BlockSpec((BLK, wide), lambda g, i: (i, g)),
                   pl.BlockSpec((None, BLK, hb), lambda g, i: (g, i, 0))],
        scratch_shapes=[pltpu.VMEM((BLK, wide), F32)],
        compiler_params=_params("parallel", "arbitrary"),
    )(proj, proj, proj)


def _sb_bwd(name, proj, do, tot, heads):
    s = proj.shape[0]
    nq = s // BLK
    groups, _, hb = tot.shape
    wide = hb * BLK

    def body(q_ref, k_ref, v_ref, do_ref, tot_ref, dq_ref, dk_ref, dv_ref, dq_acc, dk_acc, dv_acc):
        i = pl.program_id(1)
        row, col = _block_iotas()
        diag = col < row
        upto = (row <= col).astype(BF16)
        earlier = (row < col).astype(BF16)

        @pl.when(i == 0)
        def _():
            dk_acc[...] = jnp.zeros_like(dk_acc)
            dv_acc[...] = jnp.zeros_like(dv_acc)

        dq_acc[...] = jnp.zeros_like(dq_acc)

        head_cols = [slice(hh * BLK, (hh + 1) * BLK) for hh in range(hb)]

        def tiles(off, carry, strict):
            rows = pl.ds(off, BLK)
            scores = [_sb_scores(q_ref[:, c], k_ref[rows, c], strict) for c in head_cols]
            das = [_dot(do_ref[:, c], v_ref[rows, c], 1, 1) for c in head_cols]
            prefixes = [_split_dot(sc[2], upto) for sc in scores]
            dlas = []
            for hh, c in enumerate(head_cols):
                suffix = tot_ref[:, hh:hh + 1] - (prefixes[hh] + carry[2 * hh])
                a = jnp.exp(scores[hh][1] + suffix)
                if strict is not None:
                    a = jnp.where(strict, a, 0.0)
                dlas.append(das[hh] * a)
                dv_acc[rows, c] += _dot(a, do_ref[:, c], 0, 0)
            dkeeps = [_split_dot(dla, earlier) for dla in dlas]
            new = ()
            for hh, c in enumerate(head_cols):
                dkeep = dkeeps[hh] + carry[2 * hh + 1]
                if strict is not None:
                    dkeep = jnp.where(strict, dkeep, 0.0)
                sig = jnp.exp(scores[hh][1])
                dz = ((dlas[hh] * (1.0 - sig) - dkeep * sig) * (BLK ** -0.5)).astype(BF16)
                dk_acc[rows, c] += _dot(dz, q_ref[:, c], 0, 0)
                dq_acc[:, c] += _dot(dz, k_ref[rows, c], 1, 0)
                new += (carry[2 * hh] + jnp.sum(scores[hh][2], axis=1, keepdims=True),
                        carry[2 * hh + 1] + jnp.sum(dlas[hh], axis=1, keepdims=True))
            return new

        carry = lax.fori_loop(0, i, lambda jj, cr: tiles(pl.multiple_of(jj * BLK, BLK), cr, None),
                              (jnp.zeros((BLK, 1), F32),) * (2 * hb))
        tiles(pl.multiple_of(i * BLK, BLK), carry, diag)
        dq_ref[...] = dq_acc[...].astype(dq_ref.dtype)

        @pl.when(i == nq - 1)
        def _():
            dk_ref[...] = dk_acc[...].astype(dk_ref.dtype)
            dv_ref[...] = dv_acc[...].astype(dv_ref.dtype)

    shape = jax.ShapeDtypeStruct((s, heads * BLK), BF16)
    return pl.pallas_call(
        body, name=name, grid=(groups, nq), out_shape=[shape, shape, shape],
        in_specs=[pl.BlockSpec((BLK, wide), lambda g, i: (i, g)),
                  pl.BlockSpec((s, wide), lambda g, i: (0, groups + g)),
                  pl.BlockSpec((s, wide), lambda g, i: (0, 2 * groups + g)),
                  pl.BlockSpec((BLK, wide), lambda g, i: (i, g)),
                  pl.BlockSpec((None, BLK, hb), lambda g, i: (g, i, 0))],
        out_specs=[pl.BlockSpec((BLK, wide), lambda g, i: (i, g)),
                   pl.BlockSpec((s, wide), lambda g, i: (0, g)),
                   pl.BlockSpec((s, wide), lambda g, i: (0, g))],
        scratch_shapes=[pltpu.VMEM((BLK, wide), F32), pltpu.VMEM((s, wide), F32), pltpu.VMEM((s, wide), F32)],
        compiler_params=_params("parallel", "arbitrary"),
    )(proj, proj, proj, do, tot)


def _xa_probs(q, k, hd):
    z = _dot(q, k, 1, 1) * (hd ** -0.5)
    z = z - jnp.max(z, axis=-1, keepdims=True)
    e = jnp.exp(z)
    return e / jnp.sum(e, axis=-1, keepdims=True)


def _xa_fwd(name, proj, mem_kv, width):
    s = proj.shape[0]
    m = mem_kv.shape[0]
    hd = width // XA_HEADS
    tq = _tile(s, 256)
    q_first = 5 * XA_HEADS

    def body(q_ref, k_ref, v_ref, o_ref):
        p = _xa_probs(q_ref[...], k_ref[...], hd)
        o_ref[...] = _dot(p, v_ref[...], 1, 0).astype(o_ref.dtype)

    return pl.pallas_call(
        body, name=name, grid=(XA_HEADS, s // tq), out_shape=jax.ShapeDtypeStruct((s, width), BF16),
        in_specs=[pl.BlockSpec((tq, hd), lambda h, i: (i, q_first + h)),
                  pl.BlockSpec((m, hd), lambda h, i: (0, h)),
                  pl.BlockSpec((m, hd), lambda h, i: (0, XA_HEADS + h))],
        out_specs=pl.BlockSpec((tq, hd), lambda h, i: (i, h)),
        compiler_params=_params("parallel", "parallel"),
    )(proj, mem_kv, mem_kv)


def _xa_bwd(name, proj, mem_kv, do, width):
    s = proj.shape[0]
    m = mem_kv.shape[0]
    hd = width // XA_HEADS
    tq = _tile(s, 256)
    nq = s // tq
    q_first = 5 * XA_HEADS

    def body(q_ref, k_ref, v_ref, do_ref, dq_ref, dk_ref, dv_ref, dk_acc, dv_acc):
        i = pl.program_id(1)

        @pl.when(i == 0)
        def _():
            dk_acc[...] = jnp.zeros_like(dk_acc)
            dv_acc[...] = jnp.zeros_like(dv_acc)

        q, k, v, dout = q_ref[...], k_ref[...], v_ref[...], do_ref[...]
        p = _xa_probs(q, k, hd)
        dp = _dot(dout, v, 1, 1)
        dv_acc[...] += _dot(p, dout, 0, 0)
        dz = ((p * (dp - jnp.sum(dp * p, axis=-1, keepdims=True))) * (hd ** -0.5)).astype(BF16)
        dq_ref[...] = _dot(dz, k, 1, 0).astype(dq_ref.dtype)
        dk_acc[...] += _dot(dz, q, 0, 0)

        @pl.when(i == nq - 1)
        def _():
            dk_ref[...] = dk_acc[...].astype(dk_ref.dtype)
            dv_ref[...] = dv_acc[...].astype(dv_ref.dtype)

    dq, dk, dv = pl.pallas_call(
        body, name=name, grid=(XA_HEADS, nq),
        out_shape=[jax.ShapeDtypeStruct((s, width), BF16), jax.ShapeDtypeStruct((m, width), BF16),
                   jax.ShapeDtypeStruct((m, width), BF16)],
        in_specs=[pl.BlockSpec((tq, hd), lambda h, i: (i, q_first + h)),
                  pl.BlockSpec((m, hd), lambda h, i: (0, h)),
                  pl.BlockSpec((m, hd), lambda h, i: (0, XA_HEADS + h)),
                  pl.BlockSpec((tq, hd), lambda h, i: (i, h))],
        out_specs=[pl.BlockSpec((tq, hd), lambda h, i: (i, h)),
                   pl.BlockSpec((m, hd), lambda h, i: (0, h)),
                   pl.BlockSpec((m, hd), lambda h, i: (0, h))],
        scratch_shapes=[pltpu.VMEM((m, hd), F32), pltpu.VMEM((m, hd), F32)],
        compiler_params=_params("parallel", "arbitrary"),
    )(proj, mem_kv, mem_kv, do)
    return dq, dk, dv


def _gm_pointwise(u_in, v_in, g_vnorm):
    return jax.nn.gelu(u_in), _rmsnorm(jax.nn.gelu(v_in), g_vnorm)


def _gm_mask():
    row = lax.broadcasted_iota(jnp.int32, (BLK, BLK), 0)
    col = lax.broadcasted_iota(jnp.int32, (BLK, BLK), 1)
    return (col // CHUNK) <= (row // CHUNK)


def _gm_fwd(name, proj, g_vnorm, w_s, b_s, width):
    s = proj.shape[0]
    groups = width // BLK

    def body(u_ref, v_ref, g_ref, w_ref, b_ref, o_ref):
        u, vn = _gm_pointwise(u_ref[...], v_ref[...], g_ref[...])
        vn = vn.astype(BF16)
        mask = _gm_mask()
        for g in range(groups):
            cols = slice(g * BLK, (g + 1) * BLK)
            w = jnp.where(mask, w_ref[g], 0.0)
            mixed = _dot(w, vn[:, cols], 1, 0) + b_ref[g]
            o_ref[:, cols] = (u[:, cols] * mixed).astype(o_ref.dtype)

    return pl.pallas_call(
        body, name=name, grid=(s // BLK,), out_shape=jax.ShapeDtypeStruct((s, width), BF16),
        in_specs=[pl.BlockSpec((BLK, width), lambda c: (c, 3)), pl.BlockSpec((BLK, width), lambda c: (c, 4)),
                  pl.BlockSpec((1, width), lambda c: (0, 0)),
                  pl.BlockSpec((groups, BLK, BLK), lambda c: (0, 0, 0)),
                  pl.BlockSpec((groups, BLK, 1), lambda c: (0, 0, 0))],
        out_specs=pl.BlockSpec((BLK, width), lambda c: (c, 0)),
        compiler_params=_params("parallel"),
    )(proj, proj, g_vnorm, w_s, b_s)


def _gm_bwd(name, proj, do, g_vnorm, w_s, b_s, width):
    s = proj.shape[0]
    groups = width // BLK

    def body(u_ref, v_ref, do_ref, g_ref, w_ref, b_ref, du_ref, dv_ref, dw_ref, db_ref, dg_ref, dvn_buf):
        first = pl.program_id(0) == 0
        (u, vn), vjp = jax.vjp(_gm_pointwise, u_ref[...], v_ref[...], g_ref[...])
        vn16 = vn.astype(BF16)
        dout = do_ref[...]
        mask = _gm_mask()
        du_parts = []
        for g in range(groups):
            cols = slice(g * BLK, (g + 1) * BLK)
            w = jnp.where(mask, w_ref[g], 0.0)
            mixed = _dot(w, vn16[:, cols], 1, 0) + b_ref[g]
            du_parts.append(dout[:, cols] * mixed)
            dmixed = dout[:, cols] * u[:, cols]
            dw = jnp.where(mask, _dot(dmixed, vn16[:, cols], 1, 1), 0.0)
            db = jnp.sum(dmixed, axis=1, keepdims=True)
            dvn_buf[:, cols] = _dot(w, dmixed, 0, 0)

            @pl.when(first)
            def _(g=g, dw=dw, db=db):
                dw_ref[g] = dw
                db_ref[g] = db

            @pl.when(jnp.logical_not(first))
            def _(g=g, dw=dw, db=db):
                dw_ref[g] += dw
                db_ref[g] += db

        du_in, dv_in, dg = vjp((jnp.concatenate(du_parts, axis=1), dvn_buf[...]))
        du_ref[...] = du_in.astype(du_ref.dtype)
        dv_ref[...] = dv_in.astype(dv_ref.dtype)

        @pl.when(first)
        def _():
            dg_ref[...] = dg

        @pl.when(jnp.logical_not(first))
        def _():
            dg_ref[...] += dg

    shape = jax.ShapeDtypeStruct((s, width), BF16)
    return pl.pallas_call(
        body, name=name, grid=(s // BLK,),
        out_shape=[shape, shape, jax.ShapeDtypeStruct((groups, BLK, BLK), F32),
                   jax.ShapeDtypeStruct((groups, BLK, 1), F32), jax.ShapeDtypeStruct((1, width), F32)],
        in_specs=[pl.BlockSpec((BLK, width), lambda c: (c, 3)), pl.BlockSpec((BLK, width), lambda c: (c, 4)),
                  pl.BlockSpec((BLK, width), lambda c: (c, 0)),
                  pl.BlockSpec((1, width), lambda c: (0, 0)),
                  pl.BlockSpec((groups, BLK, BLK), lambda c: (0, 0, 0)),
                  pl.BlockSpec((groups, BLK, 1), lambda c: (0, 0, 0))],
        out_specs=[pl.BlockSpec((BLK, width), lambda c: (c, 0)), pl.BlockSpec((BLK, width), lambda c: (c, 0)),
                   pl.BlockSpec((groups, BLK, BLK), lambda c: (0, 0, 0)),
                   pl.BlockSpec((groups, BLK, 1), lambda c: (0, 0, 0)),
                   pl.BlockSpec((1, width), lambda c: (0, 0))],
        scratch_shapes=[pltpu.VMEM((BLK, width), F32)],
        compiler_params=_params("arbitrary"),
    )(proj, proj, do, g_vnorm, w_s, b_s)


def _shift_down(v, k):
    row = lax.broadcasted_iota(jnp.int32, v.shape, 0)
    return jnp.where(row >= k, pltpu.roll(v, k, axis=0), 0.0)


def _shift_up(v, k):
    n = v.shape[0]
    row = lax.broadcasted_iota(jnp.int32, v.shape, 0)
    return jnp.where(row < n - k, pltpu.roll(v, n - k, axis=0), 0.0)


def _conv(gate, w, b):
    return b + w[0:1] * _shift_down(gate, 2) + w[1:2] * _shift_down(gate, 1) + w[2:3] * gate


def _conv_fwd(name, up, conv_w, conv_b):
    s, f2 = up.shape
    f = f2 // 2
    tc = _tile(f, 256, 128)
    nf = f // tc

    def body(g_ref, v_ref, w_ref, b_ref, o_ref):
        o_ref[...] = (jax.nn.gelu(_conv(g_ref[...], w_ref[...], b_ref[...])) * v_ref[...]).astype(o_ref.dtype)

    return pl.pallas_call(
        body, name=name, grid=(nf,), out_shape=jax.ShapeDtypeStruct((s, f), BF16),
        in_specs=[pl.BlockSpec((s, tc), lambda j: (0, j)), pl.BlockSpec((s, tc), lambda j: (0, nf + j)),
                  pl.BlockSpec((3, tc), lambda j: (0, j)), pl.BlockSpec((1, tc), lambda j: (0, j))],
        out_specs=pl.BlockSpec((s, tc), lambda j: (0, j)),
        compiler_params=_params("parallel"),
    )(up, up, conv_w, conv_b)


def _conv_bwd(name, up, da, conv_w, conv_b):
    s, f2 = up.shape
    f = f2 // 2
    tc = _tile(f, 256, 128)
    nf = f // tc

    def body(g_ref, v_ref, da_ref, w_ref, b_ref, dg_ref, dv_ref, dw_ref, db_ref):
        gate, val, dact, w = g_ref[...], v_ref[...], da_ref[...], w_ref[...]
        act, vjp = jax.vjp(jax.nn.gelu, _conv(gate, w, b_ref[...]))
        dv_ref[...] = (dact * act).astype(dv_ref.dtype)
        dconv = vjp(dact * val)[0]
        dg_ref[...] = (w[2:3] * dconv + w[1:2] * _shift_up(dconv, 1) + w[0:1] * _shift_up(dconv, 2)
                       ).astype(dg_ref.dtype)
        dw_ref[0:1, :] = jnp.sum(dconv * _shift_down(gate, 2), axis=0, keepdims=True)
        dw_ref[1:2, :] = jnp.sum(dconv * _shift_down(gate, 1), axis=0, keepdims=True)
        dw_ref[2:3, :] = jnp.sum(dconv * gate, axis=0, keepdims=True)
        db_ref[...] = jnp.sum(dconv, axis=0, keepdims=True)

    shape = jax.ShapeDtypeStruct((s, f), BF16)
    return pl.pallas_call(
        body, name=name, grid=(nf,),
        out_shape=[shape, shape, jax.ShapeDtypeStruct((3, f), F32), jax.ShapeDtypeStruct((1, f), F32)],
        in_specs=[pl.BlockSpec((s, tc), lambda j: (0, j)), pl.BlockSpec((s, tc), lambda j: (0, nf + j)),
                  pl.BlockSpec((s, tc), lambda j: (0, j)),
                  pl.BlockSpec((3, tc), lambda j: (0, j)), pl.BlockSpec((1, tc), lambda j: (0, j))],
        out_specs=[pl.BlockSpec((s, tc), lambda j: (0, j)), pl.BlockSpec((s, tc), lambda j: (0, j)),
                   pl.BlockSpec((3, tc), lambda j: (0, j)), pl.BlockSpec((1, tc), lambda j: (0, j))],
        compiler_params=_params("parallel"),
    )(up, up, da, conv_w, conv_b)


def _adamw(w, g, m, v):
    m = ADAM_B1 * m + (1.0 - ADAM_B1) * g
    v = ADAM_B2 * v + (1.0 - ADAM_B2) * jnp.square(g)
    m_hat = m / (1.0 - ADAM_B1 ** ADAM_STEP)
    v_hat = v / (1.0 - ADAM_B2 ** ADAM_STEP)
    delta = -ADAM_LR * (m_hat / (jnp.sqrt(v_hat) + ADAM_EPS) + ADAM_WD * w)
    return delta, m, v


def _update_shard(name, layer, w, m, v, own, received, chip, previous=None):
    _, rows, cols = w.shape
    tr = _tile(rows, 256)

    def body(chip_ref, w_ref, m_ref, v_ref, own_ref, rec_ref, *rest):
        g_ref, d_ref, nm_ref, nv_ref = rest[-4:]
        g = (own_ref[...].astype(F32) + rec_ref[0].astype(F32) + rec_ref[1].astype(F32)
             + rec_ref[2].astype(F32))
        delta, new_m, new_v = _adamw(w_ref[...], g, m_ref[...], v_ref[...])
        g_ref[...] = g
        d_ref[...] = delta
        nm_ref[...] = new_m
        nv_ref[...] = new_v

    layer_spec = pl.BlockSpec((None, tr, cols), lambda i, chip_ref: (layer, i, 0))
    in_specs = [layer_spec] * 3 + [pl.BlockSpec((None, tr, cols), lambda i, chip_ref: (chip_ref[0], i, 0)),
                                   pl.BlockSpec((3, tr, cols), lambda i, chip_ref: (0, i, 0))]
    args = [chip, w, m, v, own, received]
    aliases = {}
    if previous is not None:
        in_specs += [ANY] * 4
        aliases = {len(args) + k: k for k in range(4)}
        args += list(previous)
    return pl.pallas_call(
        body, name=name, out_shape=[jax.ShapeDtypeStruct(w.shape, F32)] * 4,
        grid_spec=pltpu.PrefetchScalarGridSpec(
            num_scalar_prefetch=1, grid=(rows // tr,), in_specs=in_specs, out_specs=[layer_spec] * 4),
        input_output_aliases=aliases, compiler_params=_params("parallel"),
    )(*args)


def _sum_devices(name, gathered):
    _, rows, cols = gathered.shape

    def body(g_ref, o_ref):
        total = g_ref[0]
        for d in range(1, N_DEV):
            total = total + g_ref[d]
        o_ref[...] = total

    return pl.pallas_call(
        body, name=name, out_shape=jax.ShapeDtypeStruct((rows, cols), F32),
        in_specs=[pl.BlockSpec((N_DEV, rows, cols), lambda: (0, 0, 0))],
        out_specs=pl.BlockSpec((rows, cols), lambda: (0, 0)),
        compiler_params=pltpu.CompilerParams(vmem_limit_bytes=VMEM_LIMIT_V7X),
    )(gathered)


def _update_small(name, w, g, m, v):
    def body(w_ref, g_ref, m_ref, v_ref, d_ref, nm_ref, nv_ref):
        delta, new_m, new_v = _adamw(w_ref[...], g_ref[...], m_ref[...], v_ref[...])
        d_ref[...] = delta
        nm_ref[...] = new_m
        nv_ref[...] = new_v

    spec = pl.BlockSpec(w.shape, lambda: (0, 0))
    return pl.pallas_call(
        body, name=name, out_shape=[jax.ShapeDtypeStruct(w.shape, F32)] * 3,
        in_specs=[spec] * 4, out_specs=[spec] * 3,
        compiler_params=pltpu.CompilerParams(vmem_limit_bytes=VMEM_LIMIT_V7X),
    )(w, g, m, v)


def _pack(arrays):
    flat = jnp.concatenate([a.reshape(-1) for a in arrays])
    pad = (-flat.shape[0]) % (8 * BLK)
    return jnp.pad(flat, (0, pad)).reshape(-1, BLK)


def _unpack(packed, shapes):
    flat = packed.reshape(-1)
    out, at = [], 0
    for shape in shapes:
        size = 1
        for d in shape:
            size *= d
        out.append(flat[at:at + size].reshape(shape))
        at += size
    return out


SHARDED = ("w_in", "w_mem_kv", "w_gate", "w_br_sb", "w_br_gm", "w_br_xa", "w_out", "w_up", "w_down")
SMALL = ("g_mix_pre", "g_vnorm", "w_s", "b_s", "g_mem", "b_gate", "g_mix_post", "g_ffn_pre", "conv_w",
         "conv_b", "g_ffn_post")
WEIGHTS = ("g_mix_pre", "w_in", "g_vnorm", "w_s", "b_s", "g_mem", "w_mem_kv", "w_gate", "b_gate", "w_br_sb",
           "w_br_gm", "w_br_xa", "w_out", "g_mix_post", "g_ffn_pre", "w_up", "conv_w", "conv_b", "w_down",
           "g_ffn_post")


ROW_SHARDED = ("w_mem_kv", "w_out", "w_down")
GATHER_GROUPS = (("w_in",), ("w_mem_kv", "w_gate"), ("w_br_sb", "w_br_gm", "w_br_xa", "w_out"), ("w_up", "w_down"))
REDUCE_GROUPS = (("w_down", "w_up"), ("w_out", "w_br_sb", "w_br_gm", "w_br_xa", "w_mem_kv", "w_gate"), ("w_in",))


def _cast_into_place(name, w, layer, me):
    _, rows, cols = w.shape
    tr = _tile(rows, max(8, STREAM_BLOCK_BYTES // (4 * cols)))

    def body(me_ref, w_ref, o_ref):
        o_ref[...] = w_ref[...].astype(o_ref.dtype)

    return pl.pallas_call(
        body, name=name, out_shape=jax.ShapeDtypeStruct((N_DEV, rows, cols), BF16),
        grid_spec=pltpu.PrefetchScalarGridSpec(
            num_scalar_prefetch=1, grid=(rows // tr,),
            in_specs=[pl.BlockSpec((None, tr, cols), lambda i, me_ref: (layer, i, 0))],
            out_specs=pl.BlockSpec((None, tr, cols), lambda i, me_ref: (me_ref[0], i, 0))),
        compiler_params=_params("parallel"),
    )(me, w)


class _Gather:
    def __init__(self, tag, names, places, after):
        self.tag, self.names, self.n = tag, names, len(places)
        self.sems, self.places, self.token = _split_start(
            f"gather_start_{tag}", places, _gather_first_copies(self.n), after)

    def relay(self, after):
        self.sems, self.places = _split_wait(
            f"gather_relay_{self.tag}", self.places, self.sems, _gather_first_copies(self.n), after,
            then=_gather_relay_copies(self.n))

    def finish(self, after):
        places = _split_wait(f"gather_finish_{self.tag}", self.places, self.sems,
                             _gather_relay_copies(self.n), after)
        full = dict(zip(self.names, places))
        for name in self.names:
            if name in ROW_SHARDED:
                full[name] = full[name].reshape(-1, full[name].shape[-1])
        return full


class _Reduce:
    def __init__(self, tag, names, partials):
        self.tag, self.names, self.n = tag, names, len(partials)
        blocks = [p.reshape((N_CHIP, 2, -1, p.shape[-1])) for p in partials]
        lands = [lax.empty((N_CHIP,) + b.shape[2:], b.dtype) for b in blocks]
        self.sems, self.buffers, self.token = _split_start(
            f"pair_start_{tag}", blocks + lands, _pair_copies(self.n))

    def middle(self, after, core):
        n = self.n
        got = _split_wait(f"pair_finish_{self.tag}", self.buffers, self.sems, _pair_copies(n), after)
        sums = [_pair_sum(f"pair_sum_{name}{self.tag}", got[a], got[n + a], core)
                for a, name in enumerate(self.names)]
        lands = [lax.empty((3,) + s.shape[1:], s.dtype) for s in sums]
        self.sems, self.buffers, self.token = _split_start(
            f"chip_start_{self.tag}", sums + lands, _chip_copies(n))

    def finish(self, after):
        n = self.n
        got = _split_wait(f"chip_finish_{self.tag}", self.buffers, self.sems, _chip_copies(n), after)
        return [(name, got[a], got[n + a]) for a, name in enumerate(self.names)]


def kernel(x, mem, g_mix_pre, w_in, g_vnorm, w_s, b_s, g_mem, w_mem_kv, w_gate, b_gate, w_br_sb, w_br_gm, w_br_xa, w_out, g_mix_post, g_ffn_pre, w_up, conv_w, conv_b, w_down, g_ffn_post, loss_target, m_g_mix_pre, m_w_in, m_g_vnorm, m_w_s, m_b_s, m_g_mem, m_w_mem_kv, m_w_gate, m_b_gate, m_w_br_sb, m_w_br_gm, m_w_br_xa, m_w_out, m_g_mix_post, m_g_ffn_pre, m_w_up, m_conv_w, m_conv_b, m_w_down, m_g_ffn_post, v_g_mix_pre, v_w_in, v_g_vnorm, v_w_s, v_b_s, v_g_mem, v_w_mem_kv, v_w_gate, v_b_gate, v_w_br_sb, v_w_br_gm, v_w_br_xa, v_w_out, v_g_mix_post, v_g_ffn_pre, v_w_up, v_conv_w, v_conv_b, v_w_down, v_g_ffn_post):
    p = dict(g_mix_pre=g_mix_pre, w_in=w_in, g_vnorm=g_vnorm, w_s=w_s, b_s=b_s, g_mem=g_mem, w_mem_kv=w_mem_kv,
             w_gate=w_gate, b_gate=b_gate, w_br_sb=w_br_sb, w_br_gm=w_br_gm, w_br_xa=w_br_xa, w_out=w_out,
             g_mix_post=g_mix_post, g_ffn_pre=g_ffn_pre, w_up=w_up, conv_w=conv_w, conv_b=conv_b, w_down=w_down,
             g_ffn_post=g_ffn_post)
    mom = dict(g_mix_pre=m_g_mix_pre, w_in=m_w_in, g_vnorm=m_g_vnorm, w_s=m_w_s, b_s=m_b_s, g_mem=m_g_mem,
               w_mem_kv=m_w_mem_kv, w_gate=m_w_gate, b_gate=m_b_gate, w_br_sb=m_w_br_sb, w_br_gm=m_w_br_gm,
               w_br_xa=m_w_br_xa, w_out=m_w_out, g_mix_post=m_g_mix_post, g_ffn_pre=m_g_ffn_pre, w_up=m_w_up,
               conv_w=m_conv_w, conv_b=m_conv_b, w_down=m_w_down, g_ffn_post=m_g_ffn_post)
    var = dict(g_mix_pre=v_g_mix_pre, w_in=v_w_in, g_vnorm=v_g_vnorm, w_s=v_w_s, b_s=v_b_s, g_mem=v_g_mem,
               w_mem_kv=v_w_mem_kv, w_gate=v_w_gate, b_gate=v_b_gate, w_br_sb=v_w_br_sb, w_br_gm=v_w_br_gm,
               w_br_xa=v_w_br_xa, w_out=v_w_out, g_mix_post=v_g_mix_post, g_ffn_pre=v_g_ffn_pre, w_up=v_w_up,
               conv_w=v_conv_w, conv_b=v_conv_b, w_down=v_w_down, g_ffn_post=v_g_ffn_post)
    depth = w_in.shape[0]
    x = x[0]
    mem = mem[0]
    target = loss_target[0]
    s, d = x.shape
    width = d // 2
    heads = width // BLK
    cx, cy, cc = lax.axis_index("x"), lax.axis_index("y"), lax.axis_index("c")
    core = cc.astype(jnp.int32).reshape(1)
    chip = (2 * cx + cy).astype(jnp.int32).reshape(1)
    me = 4 * cx + 2 * cy + cc
    me_index = me.astype(jnp.int32).reshape(1)

    conv_all = _all_gather("gather_conv_w", [conv_w])[0]
    conv_w_full = jnp.transpose(conv_all, (1, 2, 0, 3)).reshape(depth, 3, -1)
    gathers = {}
    token = conv_all
    for l in range(depth):
        for gi, names in enumerate(GATHER_GROUPS):
            places = [_cast_into_place(f"cast_{n}{l}", p[n], l, me_index) for n in names]
            gathers[l, gi] = _Gather(f"{l}{gi}", names, places, token)
            token = gathers[l, gi].token

    kept, gathered = [], []
    h = _norm_fwd("pre_norm0", x, g_mix_pre[0][None, :], after=token)
    gathers[0, 0].relay(h)
    for l in range(depth):
        row = lambda name: p[name][l][None, :]
        first, second, third, fourth = (gathers[l, gi] for gi in range(4))
        w = first.finish(h)
        proj, proj16 = _mm_nn_cs(f"proj{l}", h, w["w_in"], (F32, BF16), tm=1024)
        o_sb, sb_tot = _sb_fwd(f"sb_fwd{l}", proj16, heads)
        b_s3 = b_s[l][:, :, None]
        o_gm = _gm_fwd(f"gm_fwd{l}", proj, row("g_vnorm"), w_s[l], b_s3, width)
        second.relay(o_gm)
        mn = _norm_fwd(f"mem_norm{l}", mem, row("g_mem"))
        w.update(second.finish(mn))
        third.relay(mn)
        mem_kv = _mm_nn(f"mem_kv{l}", mn, w["w_mem_kv"], (BF16,))
        o_xa = _xa_fwd(f"xa_fwd{l}", proj16, mem_kv, width)
        zg = _mm_nn_cs(f"gates{l}", h, w["w_gate"], tm=1024)
        w.update(third.finish(zg))
        ts = [_mm_nn_cs(f"branch_{n}{l}", o, w[f"w_br_{n}"], tm=2048)
              for n, o in (("sb", o_sb), ("gm", o_gm), ("xa", o_xa))]
        merged = _merge_fwd(f"merge{l}", zg, ts, row("b_gate"))
        fourth.relay(merged)
        y1 = _mm_nn(f"out{l}", merged, w["w_out"], tm=1024, tn=1024)
        x1, h2 = _residual_norm_fwd(f"mix_post{l}", x, y1, row("g_mix_post"), row("g_ffn_pre"))
        w.update(fourth.finish(h2))
        up = _mm_nn_cs(f"up{l}", h2, w["w_up"])
        if l + 1 < depth:
            gathers[l + 1, 0].relay(up)
        act = _conv_fwd(f"conv_fwd{l}", up, conv_w_full[l], row("conv_b"))
        y2 = _mm_nn(f"down{l}", act, w["w_down"])
        kept.append(dict(x=x, h=h, proj=proj, proj16=proj16, zg=zg, mn=mn, mem_kv=mem_kv, o_sb=o_sb, o_gm=o_gm,
                         o_xa=o_xa, ts=ts, merged=merged, y1=y1, x1=x1, h2=h2, up=up, act=act, y2=y2, b_s=b_s3,
                         sb_tot=sb_tot))
        gathered.append(w)
        if l + 1 < depth:
            x, h = _residual_norm_fwd(f"ffn_post{l}", x1, y2, g_ffn_post[l][None, :], g_mix_pre[l + 1][None, :])

    top = kept[-1]
    dx, dy2, loss_part, dg_ffn_post = _loss_bwd("loss", top["x1"], top["y2"], g_ffn_post[depth - 1][None, :], target)
    loss = lax.psum(loss_part[0, 0], ("x", "y", "c"))
    small_grads = [dict() for _ in range(depth)]
    small_grads[depth - 1]["g_ffn_post"] = dg_ffn_post
    outs = {n: None for n in SHARDED}
    in_flight = []

    def reduce_finish(entry, after):
        layer = int(entry[0].tag[0])
        for name, own, received in entry[0].finish(after):
            outs[name] = _update_shard(f"update_{name}{layer}", layer, p[name], mom[name], var[name],
                                       own, received, chip, outs[name])
        in_flight.remove(entry)

    def reduce_start(new):
        for entry in in_flight:
            entry[2] += entry[1] == 2
        in_flight.append([new, 1, 0])
        for entry in [e for e in in_flight if e[2] >= 2]:
            reduce_finish(entry, new.token)
        return new.token

    def reduce_middle(after):
        for entry in [e for e in in_flight if e[1] == 1]:
            entry[0].middle(after, core)
            after = entry[0].token
            entry[1] = 2
        return after

    behind = None

    for l in reversed(range(depth)):
        k = kept[l]
        w = gathered[l]
        sg = small_grads[l]
        row = lambda name: p[name][l][None, :]
        partials = {}
        group = lambda gi: _Reduce(f"{l}{gi}", REDUCE_GROUPS[gi], [partials[n] for n in REDUCE_GROUPS[gi]])
        partials["w_down"] = _mm_tn(f"d_w_down{l}", k["act"], dy2, after=behind)
        dact = _mm_nt(f"d_act{l}", dy2, w["w_down"], tm=2048)
        dgate, dval, sg["conv_w"], sg["conv_b"] = _conv_bwd(f"conv_bwd{l}", k["up"], dact, conv_w_full[l],
                                                             row("conv_b"))
        dup = (dgate, dval)
        partials["w_up"] = _mm_tn_cs(f"d_w_up{l}", k["h2"], dup)
        behind = reduce_start(group(0))
        dh2 = _mm_nt_cs(f"d_h2{l}", dup, w["w_up"], per_step=2, after=behind)
        dx1, dy1, sg["g_ffn_pre"], sg["g_mix_post"] = _mid_bwd(f"mid_bwd{l}", dx, dh2, k["x1"], k["y1"],
                                                                row("g_ffn_pre"), row("g_mix_post"))
        behind = reduce_middle(dy1)
        partials["w_out"] = _mm_tn(f"d_w_out{l}", k["merged"], dy1, tk=1024, after=behind)
        dmerged = _mm_nt(f"d_merged{l}", dy1, w["w_out"], tm=1024, tn=1024)
        dt_sb, dt_gm, dt_xa, dzg, sg["b_gate"] = _merge_bwd(f"merge_bwd{l}", dmerged, k["zg"], k["ts"], row("b_gate"))
        do = {}
        for n, dt in (("sb", dt_sb), ("gm", dt_gm), ("xa", dt_xa)):
            partials[f"w_br_{n}"] = _mm_tn_cs(f"d_w_br_{n}{l}", k[f"o_{n}"], dt, tk=1024)
            do[n] = _mm_nt_cs_whole(f"d_o_{n}{l}", dt, w[f"w_br_{n}"], F32 if n == "gm" else BF16)
        dq_xa, dk_mem, dv_mem = _xa_bwd(f"xa_bwd{l}", k["proj16"], k["mem_kv"], do["xa"], width)
        dmem_kv = jnp.concatenate([dk_mem, dv_mem], axis=1)
        partials["w_mem_kv"] = _mm_tn(f"d_w_mem_kv{l}", k["mn"], dmem_kv)
        partials["w_gate"] = _mm_tn_cs(f"d_w_gate{l}", k["h"], dzg, tk=1024)
        behind = reduce_start(group(1))
        dh_gate = _mm_nt_cs(f"d_h_gate{l}", dzg, w["w_gate"], per_step=4, after=behind)
        behind = reduce_middle(dh_gate)
        dmn = _mm_nt(f"d_mn{l}", dmem_kv, w["w_mem_kv"], after=behind)
        sg["g_mem"] = _mem_norm_bwd(f"mem_norm_bwd{l}", dmn, mem, row("g_mem"))
        du, dvg, sg["w_s"], db_s, sg["g_vnorm"] = _gm_bwd(f"gm_bwd{l}", k["proj"], do["gm"], row("g_vnorm"),
                                                          p["w_s"][l], k["b_s"], width)
        sg["b_s"] = db_s[:, :, 0]
        dq, dk, dv = _sb_bwd(f"sb_bwd{l}", k["proj16"], do["sb"], k["sb_tot"], heads)
        dproj = jnp.concatenate([dq, dk, dv, du, dvg, dq_xa], axis=1)
        partials["w_in"] = _mm_tn_cs(f"d_w_in{l}", k["h"], dproj, tk=1024)
        behind = reduce_start(group(2))
        dh_in = _mm_nt_cs(f"d_h_in{l}", dproj, w["w_in"], per_step=4, after=behind)
        if l > 0:
            below = kept[l - 1]
            dx, dy2, sg["g_mix_pre"], small_grads[l - 1]["g_ffn_post"] = _bottom_bwd(
                f"bottom_bwd{l}", dx1, dh_in, dh_gate, k["x"], row("g_mix_pre"), below["y2"],
                p["g_ffn_post"][l - 1][None, :])
        else:
            dx, sg["g_mix_pre"] = _bottom_bwd(f"bottom_bwd{l}", dx1, dh_in, dh_gate, k["x"], row("g_mix_pre"))
        behind = reduce_middle(dx)

    for entry in list(in_flight):
        reduce_finish(entry, behind)

    small_shapes = [small_grads[l][n].shape for l in range(depth) for n in SMALL]
    packed = _pack([small_grads[l][n] for l in range(depth) for n in SMALL])
    total = _sum_devices("sum_small", _all_gather("gather_small", [packed])[0])
    flat = _unpack(total, small_shapes)
    grads = {}
    for i, n in enumerate(SMALL):
        g = jnp.stack([flat[l * len(SMALL) + i] for l in range(depth)]).reshape((depth,) + p[n].shape[1:]
                                                                                 if n != "conv_w" else (depth, 3, -1))
        if n == "conv_w":
            per = conv_w.shape[2]
            g = lax.dynamic_slice_in_dim(g, me * per, per, axis=2)
        grads[n] = g
    delta, new_m, new_v = _update_small(
        "update_small", _pack([p[n] for n in SMALL]), _pack([grads[n] for n in SMALL]),
        _pack([mom[n] for n in SMALL]), _pack([var[n] for n in SMALL]))
    shapes = [p[n].shape for n in SMALL]
    for n, dl, nm, nv in zip(SMALL, _unpack(delta, shapes), _unpack(new_m, shapes), _unpack(new_v, shapes)):
        outs[n] = (grads[n], dl, nm, nv)

    result = [loss, dx[None]]
    for k in range(4):
        result += [outs[n][k] for n in WEIGHTS]
    return tuple(result)
```

```python
import functools

import jax
import jax.numpy as jnp
from jax import lax
from jax.experimental import pallas as pl
from jax.experimental.pallas import tpu as pltpu

F32 = jnp.float32
BF16 = jnp.bfloat16
N_DEV = 8
N_CHIP = 4
EPS = 1e-6
CHUNK = 64
BLK = 128
XA_HEADS = 4
SB_HEADS_PER_STEP = 8
ADAM_LR = 0.001
ADAM_B1 = 0.9
ADAM_B2 = 0.999
ADAM_EPS = 1e-08
ADAM_WD = 0.01
ADAM_STEP = 10
VMEM_LIMIT_V7X = 56 * 1024 * 1024
STREAM_BLOCK_BYTES = 4 * 1024 * 1024
MESH = pl.DeviceIdType.MESH
ANY = pl.BlockSpec(memory_space=pl.ANY)


def _params(*sem):
    return pltpu.CompilerParams(dimension_semantics=sem, vmem_limit_bytes=VMEM_LIMIT_V7X)


def _tile(n, pref, mult=8):
    best = None
    for t in range(mult, min(n, pref) + 1, mult):
        if n % t == 0:
            best = t
    return best if best is not None else n


def _bf(x):
    return x if x.dtype == BF16 else x.astype(BF16)


def _dot(a, b, ca, cb):
    return lax.dot_general(_bf(a), _bf(b), (((ca,), (cb,)), ((), ())), preferred_element_type=F32)


def _rmsnorm(x, g):
    return (x * lax.rsqrt(jnp.mean(x * x, axis=-1, keepdims=True) + EPS)) * g


def _position():
    x, y, c = lax.axis_index("x"), lax.axis_index("y"), lax.axis_index("c")
    return x, y, c


def _all_gather(name, shards):
    n = len(shards)

    def body(*refs):
        ins, outs = refs[:n], refs[n:2 * n]
        send_sems, recv_sems, local_sems = refs[2 * n:]
        x, y, c = _position()
        me = 4 * x + 2 * y + c
        sibling = (x, y, 1 - c)
        chips = [(1 - x, y), (x, 1 - y), (1 - x, 1 - y)]

        def copy(a, k, block, to, src=None):
            dst = outs[a].at[block]
            return pltpu.make_async_remote_copy(
                src_ref=dst if src is None else src, dst_ref=dst, send_sem=send_sems.at[a, k],
                recv_sem=recv_sems.at[a, k], device_id=to, device_id_type=MESH)

        local = [pltpu.make_async_copy(ins[a], outs[a].at[me], local_sems.at[a]) for a in range(n)]
        for cp in local:
            cp.start()
        first = []
        for a in range(n):
            first.append(copy(a, 0, me, sibling, src=ins[a]))
            for j, chip in enumerate(chips):
                first.append(copy(a, 1 + j, me, (*chip, c), src=ins[a]))
        for cp in first:
            cp.start()
        passed = []
        for a in range(n):
            for j, (px, py) in enumerate(chips):
                block = 4 * px + 2 * py + c
                copy(a, 1 + j, block, sibling).wait_recv()
                forward = copy(a, 4 + j, block, sibling)
                forward.start()
                passed.append(forward)
        for a in range(n):
            copy(a, 0, 4 * x + 2 * y + (1 - c), sibling).wait_recv()
            for j, (px, py) in enumerate(chips):
                copy(a, 4 + j, 4 * px + 2 * py + (1 - c), sibling).wait_recv()
        for cp in first + passed:
            cp.wait_send()
        for cp in local:
            cp.wait()

    return pl.pallas_call(
        body, name=name,
        out_shape=[jax.ShapeDtypeStruct((N_DEV,) + s.shape, s.dtype) for s in shards],
        in_specs=[ANY] * n, out_specs=[ANY] * n,
        scratch_shapes=[pltpu.SemaphoreType.DMA((n, 7)), pltpu.SemaphoreType.DMA((n, 7)),
                        pltpu.SemaphoreType.DMA((n,))],
    )(*shards)


HBM = pl.BlockSpec(memory_space=pltpu.HBM)
SEM = pl.BlockSpec(memory_space=pltpu.SEMAPHORE)
DATAFLOW = pltpu.SideEffectType.DATAFLOW_SIDE_EFFECTING


def _remote(src, dst, send, recv, k, peer):
    return pltpu.make_async_remote_copy(src_ref=src, dst_ref=dst, send_sem=send.at[k], recv_sem=recv.at[k],
                                        device_id=peer, device_id_type=MESH)


def _gather_first_copies(n):
    def build(refs, send, recv):
        x, y, c = _position()
        me = 4 * x + 2 * y + c
        peers = [(x, y, 1 - c), (1 - x, y, c), (x, 1 - y, c), (1 - x, 1 - y, c)]
        return [_remote(refs[a].at[me], refs[a].at[me], send, recv, 4 * a + k, peer)
                for a in range(n) for k, peer in enumerate(peers)]
    return build, 4 * n


def _gather_relay_copies(n):
    def build(refs, send, recv):
        x, y, c = _position()
        blocks = [4 * px + 2 * py + c for px, py in ((1 - x, y), (x, 1 - y), (1 - x, 1 - y))]
        return [_remote(refs[a].at[blk], refs[a].at[blk], send, recv, 3 * a + j, (x, y, 1 - c))
                for a in range(n) for j, blk in enumerate(blocks)]
    return build, 3 * n


def _pair_copies(n):
    def build(refs, send, recv):
        x, y, c = _position()
        return [_remote(refs[a].at[k, 1 - c], refs[n + a].at[k], send, recv, N_CHIP * a + k, (x, y, 1 - c))
                for a in range(n) for k in range(N_CHIP)]
    return build, N_CHIP * n


def _chip_copies(n):
    def build(refs, send, recv):
        x, y, c = _position()
        chips = ((1 - x, y), (x, 1 - y), (1 - x, 1 - y))
        return [_remote(refs[a].at[2 * px + py], refs[n + a].at[j], send, recv, 3 * a + j, (px, py, c))
                for a in range(n) for j, (px, py) in enumerate(chips)]
    return build, 3 * n


def _split_start(name, buffers, copies, after=None):
    build, n_copies = copies
    nb = len(buffers)
    extra = [] if after is None else [after]

    def body(*refs):
        at = nb + len(extra)
        for cp in build(refs[:nb], refs[at], refs[at + 1]):
            cp.start()
        token = refs[at + 2 + nb]
        token[...] = jnp.zeros_like(token)

    outs = pl.pallas_call(
        body, name=name,
        out_shape=[pltpu.SemaphoreType.DMA((n_copies,)), pltpu.SemaphoreType.DMA((n_copies,))]
        + [pltpu.HBM(b.shape, b.dtype) for b in buffers] + [jax.ShapeDtypeStruct((8, BLK), F32)],
        in_specs=[HBM] * nb + [ANY] * len(extra),
        out_specs=[SEM, SEM] + [HBM] * nb + [pl.BlockSpec(memory_space=pltpu.VMEM)],
        input_output_aliases={i: 2 + i for i in range(nb)},
        compiler_params=pltpu.CompilerParams(has_side_effects=DATAFLOW),
    )(*[pltpu.with_memory_space_constraint(b, pltpu.HBM) for b in buffers], *extra)
    return (outs[0], outs[1]), list(outs[2:2 + nb]), outs[-1]


def _split_wait(name, buffers, sems, copies, after, then=None):
    build, _ = copies
    nb = len(buffers)
    n_next = 0 if then is None else then[1]
    first_out = nb + 3

    def body(*refs):
        for cp in build(refs[:nb], refs[nb], refs[nb + 1]):
            cp.wait()
        if then is not None:
            for cp in then[0](refs[:nb], refs[first_out], refs[first_out + 1]):
                cp.start()

    sem_shapes = [] if then is None else [pltpu.SemaphoreType.DMA((n_next,))] * 2
    outs = pl.pallas_call(
        body, name=name,
        out_shape=sem_shapes + [pltpu.HBM(b.shape, b.dtype) for b in buffers],
        in_specs=[HBM] * nb + [SEM, SEM, ANY],
        out_specs=[SEM] * len(sem_shapes) + [HBM] * nb,
        input_output_aliases={i: len(sem_shapes) + i for i in range(nb)},
        compiler_params=pltpu.CompilerParams(has_side_effects=DATAFLOW),
    )(*buffers, sems[0], sems[1], after)
    if then is None:
        return list(outs)
    return (outs[0], outs[1]), list(outs[2:])


def _pair_sum(name, partial, received, core):
    _, _, rows, cols = partial.shape
    tr = _tile(rows, max(8, STREAM_BLOCK_BYTES // (2 * cols)))

    def body(core_ref, p_ref, r_ref, o_ref):
        o_ref[...] = (p_ref[...].astype(F32) + r_ref[...].astype(F32)).astype(o_ref.dtype)

    return pl.pallas_call(
        body, name=name, out_shape=jax.ShapeDtypeStruct((N_CHIP, rows, cols), BF16),
        grid_spec=pltpu.PrefetchScalarGridSpec(
            num_scalar_prefetch=1, grid=(N_CHIP, rows // tr),
            in_specs=[pl.BlockSpec((None, None, tr, cols), lambda k, i, core: (k, core[0], i, 0)),
                      pl.BlockSpec((None, tr, cols), lambda k, i, core: (k, i, 0))],
            out_specs=pl.BlockSpec((None, tr, cols), lambda k, i, core: (k, i, 0))),
        compiler_params=_params("parallel", "parallel"),
    )(core, partial, received)


def _mm(name, a, b, out_shape, out_dtypes, grid, a_spec, b_spec, o_spec, ca, cb, k_steps=1, after=None):
    n_out = len(out_dtypes)
    extra = [] if after is None else [after]

    def body(a_ref, b_ref, *rest):
        rest = rest[len(extra):]
        o_refs = rest[:n_out]
        part = _dot(a_ref[...], b_ref[...], ca, cb)
        if k_steps == 1:
            for o in o_refs:
                o[...] = part.astype(o.dtype)
            return
        acc = rest[n_out]
        k = pl.program_id(len(grid) - 1)

        @pl.when(k == 0)
        def _():
            acc[...] = part

        @pl.when(k > 0)
        def _():
            acc[...] += part

        @pl.when(k == k_steps - 1)
        def _():
            for o in o_refs:
                o[...] = acc[...].astype(o.dtype)

    o_block = tuple(d for d in o_spec.block_shape if d is not None)
    sem = ("parallel",) * (len(grid) - 1) + (("arbitrary",) if k_steps > 1 else ("parallel",))
    out = pl.pallas_call(
        body, name=name, grid=grid,
        out_shape=[jax.ShapeDtypeStruct(out_shape, d) for d in out_dtypes],
        in_specs=[a_spec, b_spec] + [ANY] * len(extra), out_specs=[o_spec] * n_out,
        scratch_shapes=[pltpu.VMEM(o_block, F32)] if k_steps > 1 else [],
        compiler_params=_params(*sem),
    )(a, b, *extra)
    return out if n_out > 1 else out[0]


def _mm_nn_cs(name, a, w, out_dtypes=(F32,), tm=512, **kw):
    m, k = a.shape
    _, _, ns = w.shape
    tm = _tile(m, tm)
    return _mm(name, a, w, (m, N_DEV * ns), out_dtypes, (m // tm, N_DEV),
               pl.BlockSpec((tm, k), lambda i, j: (i, 0)),
               pl.BlockSpec((None, k, ns), lambda i, j: (j, 0, 0)),
               pl.BlockSpec((tm, ns), lambda i, j: (i, j)), 1, 0, **kw)


def _mm_nn(name, a, w, out_dtypes=(F32,), tm=512, tn=512, **kw):
    m, k = a.shape
    _, n = w.shape
    tm, tn = _tile(m, tm), _tile(n, tn, 128)
    return _mm(name, a, w, (m, n), out_dtypes, (m // tm, n // tn),
               pl.BlockSpec((tm, k), lambda i, j: (i, 0)),
               pl.BlockSpec((k, tn), lambda i, j: (0, j)),
               pl.BlockSpec((tm, tn), lambda i, j: (i, j)), 1, 0, **kw)


def _mm_nt_cs(name, a, w, tm=512, per_step=1, after=None):
    _, k, ns = w.shape
    parts = list(a) if isinstance(a, (tuple, list)) else [a]
    m = parts[0].shape[0]
    tm = _tile(m, tm)
    steps = N_DEV // per_step
    per_part = steps // len(parts)
    extra = [] if after is None else [after]

    def body(*refs):
        a_refs, w_ref, o_ref = refs[:len(parts)], refs[len(parts)], refs[len(parts) + 1 + len(extra)]
        j = pl.program_id(1)

        def contract(a_ref):
            part = _dot(a_ref[:, 0:ns], w_ref[0], 1, 1)
            for u in range(1, per_step):
                part = part + _dot(a_ref[:, u * ns:(u + 1) * ns], w_ref[u], 1, 1)

            @pl.when(j == 0)
            def _():
                o_ref[...] = part

            @pl.when(j > 0)
            def _():
                o_ref[...] += part

        for n, a_ref in enumerate(a_refs):
            pl.when(jnp.logical_and(j >= n * per_part, j < (n + 1) * per_part))(functools.partial(contract, a_ref))

    a_specs = [pl.BlockSpec((tm, per_step * ns),
                            functools.partial(lambda i, j, n: (i, jnp.clip(j - n * per_part, 0, per_part - 1)), n=n))
               for n in range(len(parts))]
    return pl.pallas_call(
        body, name=name, grid=(m // tm, steps), out_shape=jax.ShapeDtypeStruct((m, k), F32),
        in_specs=a_specs + [pl.BlockSpec((per_step, k, ns), lambda i, j: (j, 0, 0))] + [ANY] * len(extra),
        out_specs=pl.BlockSpec((tm, k), lambda i, j: (i, 0)),
        compiler_params=_params("parallel", "arbitrary"),
    )(*parts, w, *extra)


def _mm_nt_cs_whole(name, a, w, out_dtype, tm=512):
    m, _ = a.shape
    _, k, ns = w.shape
    tm = _tile(m, tm)

    def body(a_ref, w_ref, o_ref):
        acc = _dot(a_ref[:, 0:ns], w_ref[0], 1, 1)
        for j in range(1, N_DEV):
            acc = acc + _dot(a_ref[:, j * ns:(j + 1) * ns], w_ref[j], 1, 1)
        o_ref[...] = acc.astype(o_ref.dtype)

    return pl.pallas_call(
        body, name=name, grid=(m // tm,), out_shape=jax.ShapeDtypeStruct((m, k), out_dtype),
        in_specs=[pl.BlockSpec((tm, N_DEV * ns), lambda i: (i, 0)),
                  pl.BlockSpec((N_DEV, k, ns), lambda i: (0, 0, 0))],
        out_specs=pl.BlockSpec((tm, k), lambda i: (i, 0)),
        compiler_params=_params("parallel"),
    )(a, w)


def _mm_nt(name, a, w, out_dtypes=(F32,), tm=512, tn=512, **kw):
    m, k = a.shape
    n, _ = w.shape
    tm, tn = _tile(m, tm), _tile(n, tn, 128)
    return _mm(name, a, w, (m, n), out_dtypes, (m // tm, n // tn),
               pl.BlockSpec((tm, k), lambda i, j: (i, 0)),
               pl.BlockSpec((tn, k), lambda i, j: (j, 0)),
               pl.BlockSpec((tm, tn), lambda i, j: (i, j)), 1, 1, **kw)


def _mm_tn_cs(name, a, g, tk=512, after=None):
    s, k = a.shape
    parts = list(g) if isinstance(g, (tuple, list)) else [g]
    per_part = N_DEV // len(parts)
    ns = parts[0].shape[1] // per_part
    tk = _tile(k, tk, 128)
    extra = [] if after is None else [after]

    def body(a_ref, *rest):
        g_refs, o_ref = rest[:len(parts)], rest[len(parts) + len(extra)]
        j = pl.program_id(1)

        def shard(g_ref):
            o_ref[...] = _dot(a_ref[...], g_ref[...], 0, 0).astype(o_ref.dtype)

        for n, g_ref in enumerate(g_refs):
            pl.when(jnp.logical_and(j >= n * per_part, j < (n + 1) * per_part))(functools.partial(shard, g_ref))

    g_specs = [pl.BlockSpec((s, ns),
                            functools.partial(lambda i, j, n: (0, jnp.clip(j - n * per_part, 0, per_part - 1)), n=n))
               for n in range(len(parts))]
    return pl.pallas_call(
        body, name=name, grid=(k // tk, N_DEV), out_shape=jax.ShapeDtypeStruct((N_DEV, k, ns), BF16),
        in_specs=[pl.BlockSpec((s, tk), lambda i, j: (0, i))] + g_specs + [ANY] * len(extra),
        out_specs=pl.BlockSpec((None, tk, ns), lambda i, j: (j, i, 0)),
        compiler_params=_params("parallel", "arbitrary"),
    )(a, *parts, *extra)


def _mm_tn(name, a, g, tk=512, tn=1024, **kw):
    s, k = a.shape
    n = g.shape[1]
    tk, tn = _tile(k, tk, 128), _tile(n, tn, 128)
    return _mm(name, a, g, (k, n), (BF16,), (k // tk, n // tn),
               pl.BlockSpec((s, tk), lambda i, j: (0, i)),
               pl.BlockSpec((s, tn), lambda i, j: (0, j)),
               pl.BlockSpec((tk, tn), lambda i, j: (i, j)), 0, 0, **kw)


def _rowwise(name, fn, rows, bcast, outs, reds, tm, after=None):
    n_rows = rows[0][0].shape[0]
    tm = _tile(n_rows, tm)
    n_in, n_b, n_o, n_r = len(rows), len(bcast), len(outs), len(reds)
    extra = [] if after is None else [after]

    def body(*refs):
        ins = refs[:n_in + n_b]
        refs = refs[len(extra):]
        o_refs = refs[n_in + n_b:n_in + n_b + n_o]
        r_refs = refs[n_in + n_b + n_o:]
        vals = fn(*[r[...] for r in ins])
        for o, v in zip(o_refs, vals[:n_o]):
            o[...] = v.astype(o.dtype)
        first = pl.program_id(0) == 0
        for r, v in zip(r_refs, vals[n_o:]):
            @pl.when(first)
            def _(r=r, v=v):
                r[...] = v.astype(r.dtype)

            @pl.when(jnp.logical_not(first))
            def _(r=r, v=v):
                r[...] += v.astype(r.dtype)

    def whole(shape):
        zeros = (0,) * len(shape)
        return pl.BlockSpec(shape, lambda i: zeros)

    in_specs = [pl.BlockSpec((tm, w), functools.partial(lambda i, cb: (i, cb), cb=cb)) for _, w, cb in rows]
    in_specs += [whole(b.shape) for b in bcast] + [ANY] * len(extra)
    out_specs = [pl.BlockSpec((tm, w), lambda i: (i, 0)) for w, _ in outs]
    out_specs += [whole(s) for s, _ in reds]
    out_shape = [jax.ShapeDtypeStruct((n_rows, w), d) for w, d in outs]
    out_shape += [jax.ShapeDtypeStruct(s, d) for s, d in reds]
    return pl.pallas_call(
        body, name=name, grid=(n_rows // tm,), out_shape=out_shape, in_specs=in_specs,
        out_specs=out_specs, compiler_params=_params("arbitrary" if n_r else "parallel"),
    )(*[r[0] for r in rows], *bcast, *extra)


def _norm_fwd(name, x, g, after=None):
    d = x.shape[1]
    return _rowwise(name, lambda xt, gt: (_rmsnorm(xt, gt),), [(x, d, 0)], [g], [(d, BF16)], [], 256,
                    after=after)[0]


def _residual_norm_fwd(name, x, y, g_post, g_next):
    d = x.shape[1]

    def fn(xt, yt, gp, gn):
        new = xt + _rmsnorm(yt, gp)
        return new, _rmsnorm(new, gn)

    return _rowwise(name, fn, [(x, d, 0), (y, d, 0)], [g_post, g_next], [(d, F32), (d, BF16)], [], 256)


def _merge(z0, z1, z2, t0, t1, t2, b0, b1, b2):
    return (jax.nn.sigmoid(z0 + b0) * t0 + jax.nn.sigmoid(z1 + b1) * t1 + jax.nn.sigmoid(z2 + b2) * t2)


def _merge_fwd(name, zg, ts, b_gate):
    d = ts[0].shape[1]
    rows = [(zg, d, b) for b in range(3)] + [(t, d, 0) for t in ts]
    bias = [b_gate[:, b * d:(b + 1) * d] for b in range(3)]
    return _rowwise(name, lambda *v: (_merge(*v),), rows, bias, [(d, BF16)], [], 128)[0]


def _merge_bwd(name, dmerged, zg, ts, b_gate):
    d = ts[0].shape[1]
    rows = [(dmerged, d, 0)] + [(zg, d, b) for b in range(3)] + [(t, d, 0) for t in ts]
    bias = [b_gate[:, b * d:(b + 1) * d] for b in range(3)]

    def fn(dm, *v):
        _, vjp = jax.vjp(_merge, *v)
        dz0, dz1, dz2, dt0, dt1, dt2, db0, db1, db2 = vjp(dm)
        return (dt0, dt1, dt2, jnp.concatenate([dz0, dz1, dz2], axis=1),
                jnp.concatenate([db0, db1, db2], axis=1))

    return _rowwise(name, fn, rows, bias, [(d, BF16)] * 3 + [(3 * d, BF16)], [((1, 3 * d), F32)], 128)


def _loss_bwd(name, x, y, g_post, target):
    d = x.shape[1]

    def loss(xt, yt, gp, tt):
        err = xt + _rmsnorm(yt, gp) - tt
        return 0.5 * jnp.sum(jnp.mean(err * err, axis=-1))

    def fn(xt, yt, tt, gp):
        val, (dx, dy, dg) = jax.value_and_grad(loss, argnums=(0, 1, 2))(xt, yt, gp, tt)
        return dx, dy, jnp.full((1, BLK), val, F32), dg

    return _rowwise(name, fn, [(x, d, 0), (y, d, 0), (target, d, 0)], [g_post],
                    [(d, F32), (d, BF16)], [((1, BLK), F32), ((1, d), F32)], 256)


def _mid_bwd(name, dx_out, dh, x_mid, y, g_pre, g_post):
    d = dx_out.shape[1]

    def fn(dxo, dht, xm, yt, gpre, gpost):
        _, vjp_pre = jax.vjp(_rmsnorm, xm, gpre)
        dxm, dgpre = vjp_pre(dht)
        dxm = dxo + dxm
        _, vjp_post = jax.vjp(_rmsnorm, yt, gpost)
        dy, dgpost = vjp_post(dxm)
        return dxm, dy, dgpre, dgpost

    return _rowwise(name, fn, [(dx_out, d, 0), (dh, d, 0), (x_mid, d, 0), (y, d, 0)], [g_pre, g_post],
                    [(d, F32), (d, BF16)], [((1, d), F32), ((1, d), F32)], 256)


def _bottom_bwd(name, dx_mid, dh_a, dh_b, x, g_pre, y_prev=None, g_post_prev=None):
    d = dx_mid.shape[1]
    rows = [(dx_mid, d, 0), (dh_a, d, 0), (dh_b, d, 0), (x, d, 0)]
    if y_prev is None:
        def fn(dxm, da, db, xt, gpre):
            _, vjp_pre = jax.vjp(_rmsnorm, xt, gpre)
            dx, dgpre = vjp_pre(da + db)
            return dxm + dx, dgpre

        return _rowwise(name, fn, rows, [g_pre], [(d, F32)], [((1, d), F32)], 256)

    def fn2(dxm, da, db, xt, yt, gpre, gpost):
        _, vjp_pre = jax.vjp(_rmsnorm, xt, gpre)
        dx, dgpre = vjp_pre(da + db)
        dx = dxm + dx
        _, vjp_post = jax.vjp(_rmsnorm, yt, gpost)
        dy, dgpost = vjp_post(dx)
        return dx, dy, dgpre, dgpost

    return _rowwise(name, fn2, rows + [(y_prev, d, 0)], [g_pre, g_post_prev],
                    [(d, F32), (d, BF16)], [((1, d), F32), ((1, d), F32)], 256)


def _mem_norm_bwd(name, dmn, mem, g_mem):
    d = mem.shape[1]

    def fn(dt, mt, g):
        _, vjp = jax.vjp(_rmsnorm, mt, g)
        return (vjp(dt)[1],)

    return _rowwise(name, fn, [(dmn, d, 0), (mem, d, 0)], [g_mem], [], [((1, d), F32)], 256)[0]


def _split_dot(v, tri):
    hi = v.astype(BF16)
    lo = (v - hi.astype(F32)).astype(BF16)
    return (jnp.dot(hi, tri, preferred_element_type=F32) + jnp.dot(lo, tri, preferred_element_type=F32))


def _sb_scores(q, kb, strict=None):
    z = _dot(q, kb, 1, 1) * (BLK ** -0.5)
    soft = jnp.log(1.0 + jnp.exp(-jnp.abs(z)))
    log_beta = jnp.minimum(z, 0.0) - soft
    log_keep = jnp.minimum(-z, 0.0) - soft
    if strict is not None:
        log_keep = jnp.where(strict, log_keep, 0.0)
    return z, log_beta, log_keep


def _block_iotas():
    return (lax.broadcasted_iota(jnp.int32, (BLK, BLK), 0), lax.broadcasted_iota(jnp.int32, (BLK, BLK), 1))


def _sb_fwd(name, proj, heads):
    s = proj.shape[0]
    nq = s // BLK
    hb = _tile(heads, SB_HEADS_PER_STEP, 1)
    groups = heads // hb
    wide = hb * BLK

    def body(q_ref, k_ref, v_ref, o_ref, tot_ref, acc_ref):
        i = pl.program_id(1)
        row, col = _block_iotas()
        diag = col < row
        later = (row > col).astype(BF16)
        head_cols = [slice(hh * BLK, (hh + 1) * BLK) for hh in range(hb)]

        def tiles(off, runs, strict):
            scores = [_sb_scores(q_ref[:, c], k_ref[pl.ds(off, BLK), c], strict) for c in head_cols]
            suffixes = [_split_dot(sc[2], later) for sc in scores]
            new = []
            for hh, c in enumerate(head_cols):
                a = jnp.exp(scores[hh][1] + suffixes[hh] + runs[hh])
                if strict is not None:
                    a = jnp.where(strict, a, 0.0)
                acc_ref[:, c] += _dot(a, v_ref[pl.ds(off, BLK), c], 1, 0)
                new.append(runs[hh] + jnp.sum(scores[hh][2], axis=1, keepdims=True))
            return tuple(new)

        acc_ref[...] = jnp.zeros_like(acc_ref)
        runs = tiles(pl.multiple_of(i * BLK, BLK), (jnp.zeros((BLK, 1), F32),) * hb, diag)
        runs = lax.fori_loop(0, i, lambda t, r: tiles(pl.multiple_of((i - 1 - t) * BLK, BLK), r, None), runs)
        o_ref[...] = acc_ref[...].astype(o_ref.dtype)
        for hh in range(hb):
            tot_ref[:, hh:hh + 1] = runs[hh]

    return pl.pallas_call(
        body, name=name, grid=(groups, nq),
        out_shape=[jax.ShapeDtypeStruct((s, heads * BLK), BF16), jax.ShapeDtypeStruct((groups, s, hb), F32)],
        in_specs=[pl.BlockSpec((BLK, wide), lambda g, i: (i, g)),
                  pl.BlockSpec((s, wide), lambda g, i: (0, groups + g)),
                  pl.BlockSpec((s, wide), lambda g, i: (0, 2 * groups + g))],
        out_specs=[pl.BlockSpec((BLK, wide), lambda g, i: (i, g)),
                   pl.BlockSpec((None, BLK, hb), lambda g, i: (g, i, 0))],
        scratch_shapes=[pltpu.VMEM((BLK, wide), F32)],
        compiler_params=_params("parallel", "arbitrary"),
    )(proj, proj, proj)


def _sb_bwd(name, proj, do, tot, heads):
    s = proj.shape[0]
    nq = s // BLK
    groups, _, hb = tot.shape
    wide = hb * BLK

    def body(q_ref, k_ref, v_ref, do_ref, tot_ref, dq_ref, dk_ref, dv_ref, dq_acc, dk_acc, dv_acc):
        i = pl.program_id(1)
        row, col = _block_iotas()
        diag = col < row
        upto = (row <= col).astype(BF16)
        earlier = (row < col).astype(BF16)

        @pl.when(i == 0)
        def _():
            dk_acc[...] = jnp.zeros_like(dk_acc)
            dv_acc[...] = jnp.zeros_like(dv_acc)

        dq_acc[...] = jnp.zeros_like(dq_acc)

        head_cols = [slice(hh * BLK, (hh + 1) * BLK) for hh in range(hb)]

        def tiles(off, carry, strict):
            rows = pl.ds(off, BLK)
            scores = [_sb_scores(q_ref[:, c], k_ref[rows, c], strict) for c in head_cols]
            das = [_dot(do_ref[:, c], v_ref[rows, c], 1, 1) for c in head_cols]
            prefixes = [_split_dot(sc[2], upto) for sc in scores]
            dlas = []
            for hh, c in enumerate(head_cols):
                suffix = tot_ref[:, hh:hh + 1] - (prefixes[hh] + carry[2 * hh])
                a = jnp.exp(scores[hh][1] + suffix)
                if strict is not None:
                    a = jnp.where(strict, a, 0.0)
                dlas.append(das[hh] * a)
                dv_acc[rows, c] += _dot(a, do_ref[:, c], 0, 0)
            dkeeps = [_split_dot(dla, earlier) for dla in dlas]
            new = ()
            for hh, c in enumerate(head_cols):
                dkeep = dkeeps[hh] + carry[2 * hh + 1]
                if strict is not None:
                    dkeep = jnp.where(strict, dkeep, 0.0)
                sig = jnp.exp(scores[hh][1])
                dz = ((dlas[hh] * (1.0 - sig) - dkeep * sig) * (BLK ** -0.5)).astype(BF16)
                dk_acc[rows, c] += _dot(dz, q_ref[:, c], 0, 0)
                dq_acc[:, c] += _dot(dz, k_ref[rows, c], 1, 0)
                new += (carry[2 * hh] + jnp.sum(scores[hh][2], axis=1, keepdims=True),
                        carry[2 * hh + 1] + jnp.sum(dlas[hh], axis=1, keepdims=True))
            return new

        carry = lax.fori_loop(0, i, lambda jj, cr: tiles(pl.multiple_of(jj * BLK, BLK), cr, None),
                              (jnp.zeros((BLK, 1), F32),) * (2 * hb))
        tiles(pl.multiple_of(i * BLK, BLK), carry, diag)
        dq_ref[...] = dq_acc[...].astype(dq_ref.dtype)

        @pl.when(i == nq - 1)
        def _():
            dk_ref[...] = dk_acc[...].astype(dk_ref.dtype)
            dv_ref[...] = dv_acc[...].astype(dv_ref.dtype)

    shape = jax.ShapeDtypeStruct((s, heads * BLK), BF16)
    return pl.pallas_call(
        body, name=name, grid=(groups, nq), out_shape=[shape, shape, shape],
        in_specs=[pl.BlockSpec((BLK, wide), lambda g, i: (i, g)),
                  pl.BlockSpec((s, wide), lambda g, i: (0, groups + g)),
                  pl.BlockSpec((s, wide), lambda g, i: (0, 2 * groups + g)),
                  pl.BlockSpec((BLK, wide), lambda g, i: (i, g)),
                  pl.BlockSpec((None, BLK, hb), lambda g, i: (g, i, 0))],
        out_specs=[pl.BlockSpec((BLK, wide), lambda g, i: (i, g)),
                   pl.BlockSpec((s, wide), lambda g, i: (0, g)),
                   pl.BlockSpec((s, wide), lambda g, i: (0, g))],
        scratch_shapes=[pltpu.VMEM((BLK, wide), F32), pltpu.VMEM((s, wide), F32), pltpu.VMEM((s, wide), F32)],
        compiler_params=_params("parallel", "arbitrary"),
    )(proj, proj, proj, do, tot)


def _xa_probs(q, k, hd):
    z = _dot(q, k, 1, 1) * (hd ** -0.5)
    z = z - jnp.max(z, axis=-1, keepdims=True)
    e = jnp.exp(z)
    return e / jnp.sum(e, axis=-1, keepdims=True)


def _xa_fwd(name, proj, mem_kv, width):
    s = proj.shape[0]
    m = mem_kv.shape[0]
    hd = width // XA_HEADS
    tq = _tile(s, 256)
    q_first = 5 * XA_HEADS

    def body(q_ref, k_ref, v_ref, o_ref):
        p = _xa_probs(q_ref[...], k_ref[...], hd)
        o_ref[...] = _dot(p, v_ref[...], 1, 0).astype(o_ref.dtype)

    return pl.pallas_call(
        body, name=name, grid=(XA_HEADS, s // tq), out_shape=jax.ShapeDtypeStruct((s, width), BF16),
        in_specs=[pl.BlockSpec((tq, hd), lambda h, i: (i, q_first + h)),
                  pl.BlockSpec((m, hd), lambda h, i: (0, h)),
                  pl.BlockSpec((m, hd), lambda h, i: (0, XA_HEADS + h))],
        out_specs=pl.BlockSpec((tq, hd), lambda h, i: (i, h)),
        compiler_params=_params("parallel", "parallel"),
    )(proj, mem_kv, mem_kv)


def _xa_bwd(name, proj, mem_kv, do, width):
    s = proj.shape[0]
    m = mem_kv.shape[0]
    hd = width // XA_HEADS
    tq = _tile(s, 256)
    nq = s // tq
    q_first = 5 * XA_HEADS

    def body(q_ref, k_ref, v_ref, do_ref, dq_ref, dk_ref, dv_ref, dk_acc, dv_acc):
        i = pl.program_id(1)

        @pl.when(i == 0)
        def _():
            dk_acc[...] = jnp.zeros_like(dk_acc)
            dv_acc[...] = jnp.zeros_like(dv_acc)

        q, k, v, dout = q_ref[...], k_ref[...], v_ref[...], do_ref[...]
        p = _xa_probs(q, k, hd)
        dp = _dot(dout, v, 1, 1)
        dv_acc[...] += _dot(p, dout, 0, 0)
        dz = ((p * (dp - jnp.sum(dp * p, axis=-1, keepdims=True))) * (hd ** -0.5)).astype(BF16)
        dq_ref[...] = _dot(dz, k, 1, 0).astype(dq_ref.dtype)
        dk_acc[...] += _dot(dz, q, 0, 0)

        @pl.when(i == nq - 1)
        def _():
            dk_ref[...] = dk_acc[...].astype(dk_ref.dtype)
            dv_ref[...] = dv_acc[...].astype(dv_ref.dtype)

    dq, dk, dv = pl.pallas_call(
        body, name=name, grid=(XA_HEADS, nq),
        out_shape=[jax.ShapeDtypeStruct((s, width), BF16), jax.ShapeDtypeStruct((m, width), BF16),
                   jax.ShapeDtypeStruct((m, width), BF16)],
        in_specs=[pl.BlockSpec((tq, hd), lambda h, i: (i, q_first + h)),
                  pl.BlockSpec((m, hd), lambda h, i: (0, h)),
                  pl.BlockSpec((m, hd), lambda h, i: (0, XA_HEADS + h)),
                  pl.BlockSpec((tq, hd), lambda h, i: (i, h))],
        out_specs=[pl.BlockSpec((tq, hd), lambda h, i: (i, h)),
                   pl.BlockSpec((m, hd), lambda h, i: (0, h)),
                   pl.BlockSpec((m, hd), lambda h, i: (0, h))],
        scratch_shapes=[pltpu.VMEM((m, hd), F32), pltpu.VMEM((m, hd), F32)],
        compiler_params=_params("parallel", "arbitrary"),
    )(proj, mem_kv, mem_kv, do)
    return dq, dk, dv


def _gm_pointwise(u_in, v_in, g_vnorm):
    return jax.nn.gelu(u_in), _rmsnorm(jax.nn.gelu(v_in), g_vnorm)


def _gm_mask():
    row = lax.broadcasted_iota(jnp.int32, (BLK, BLK), 0)
    col = lax.broadcasted_iota(jnp.int32, (BLK, BLK), 1)
    return (col // CHUNK) <= (row // CHUNK)


def _gm_fwd(name, proj, g_vnorm, w_s, b_s, width):
    s = proj.shape[0]
    groups = width // BLK

    def body(u_ref, v_ref, g_ref, w_ref, b_ref, o_ref):
        u, vn = _gm_pointwise(u_ref[...], v_ref[...], g_ref[...])
        vn = vn.astype(BF16)
        mask = _gm_mask()
        for g in range(groups):
            cols = slice(g * BLK, (g + 1) * BLK)
            w = jnp.where(mask, w_ref[g], 0.0)
            mixed = _dot(w, vn[:, cols], 1, 0) + b_ref[g]
            o_ref[:, cols] = (u[:, cols] * mixed).astype(o_ref.dtype)

    return pl.pallas_call(
        body, name=name, grid=(s // BLK,), out_shape=jax.ShapeDtypeStruct((s, width), BF16),
        in_specs=[pl.BlockSpec((BLK, width), lambda c: (c, 3)), pl.BlockSpec((BLK, width), lambda c: (c, 4)),
                  pl.BlockSpec((1, width), lambda c: (0, 0)),
                  pl.BlockSpec((groups, BLK, BLK), lambda c: (0, 0, 0)),
                  pl.BlockSpec((groups, BLK, 1), lambda c: (0, 0, 0))],
        out_specs=pl.BlockSpec((BLK, width), lambda c: (c, 0)),
        compiler_params=_params("parallel"),
    )(proj, proj, g_vnorm, w_s, b_s)


def _gm_bwd(name, proj, do, g_vnorm, w_s, b_s, width):
    s = proj.shape[0]
    groups = width // BLK

    def body(u_ref, v_ref, do_ref, g_ref, w_ref, b_ref, du_ref, dv_ref, dw_ref, db_ref, dg_ref, dvn_buf):
        first = pl.program_id(0) == 0
        (u, vn), vjp = jax.vjp(_gm_pointwise, u_ref[...], v_ref[...], g_ref[...])
        vn16 = vn.astype(BF16)
        dout = do_ref[...]
        mask = _gm_mask()
        du_parts = []
        for g in range(groups):
            cols = slice(g * BLK, (g + 1) * BLK)
            w = jnp.where(mask, w_ref[g], 0.0)
            mixed = _dot(w, vn16[:, cols], 1, 0) + b_ref[g]
            du_parts.append(dout[:, cols] * mixed)
            dmixed = dout[:, cols] * u[:, cols]
            dw = jnp.where(mask, _dot(dmixed, vn16[:, cols], 1, 1), 0.0)
            db = jnp.sum(dmixed, axis=1, keepdims=True)
            dvn_buf[:, cols] = _dot(w, dmixed, 0, 0)

            @pl.when(first)
            def _(g=g, dw=dw, db=db):
                dw_ref[g] = dw
                db_ref[g] = db

            @pl.when(jnp.logical_not(first))
            def _(g=g, dw=dw, db=db):
                dw_ref[g] += dw
                db_ref[g] += db

        du_in, dv_in, dg = vjp((jnp.concatenate(du_parts, axis=1), dvn_buf[...]))
        du_ref[...] = du_in.astype(du_ref.dtype)
        dv_ref[...] = dv_in.astype(dv_ref.dtype)

        @pl.when(first)
        def _():
            dg_ref[...] = dg

        @pl.when(jnp.logical_not(first))
        def _():
            dg_ref[...] += dg

    shape = jax.ShapeDtypeStruct((s, width), BF16)
    return pl.pallas_call(
        body, name=name, grid=(s // BLK,),
        out_shape=[shape, shape, jax.ShapeDtypeStruct((groups, BLK, BLK), F32),
                   jax.ShapeDtypeStruct((groups, BLK, 1), F32), jax.ShapeDtypeStruct((1, width), F32)],
        in_specs=[pl.BlockSpec((BLK, width), lambda c: (c, 3)), pl.BlockSpec((BLK, width), lambda c: (c, 4)),
                  pl.BlockSpec((BLK, width), lambda c: (c, 0)),
                  pl.BlockSpec((1, width), lambda c: (0, 0)),
                  pl.BlockSpec((groups, BLK, BLK), lambda c: (0, 0, 0)),
                  pl.BlockSpec((groups, BLK, 1), lambda c: (0, 0, 0))],
        out_specs=[pl.BlockSpec((BLK, width), lambda c: (c, 0)), pl.BlockSpec((BLK, width), lambda c: (c, 0)),
                   pl.BlockSpec((groups, BLK, BLK), lambda c: (0, 0, 0)),
                   pl.BlockSpec((groups, BLK, 1), lambda c: (0, 0, 0)),
                   pl.BlockSpec((1, width), lambda c: (0, 0))],
        scratch_shapes=[pltpu.VMEM((BLK, width), F32)],
        compiler_params=_params("arbitrary"),
    )(proj, proj, do, g_vnorm, w_s, b_s)


def _shift_down(v, k):
    row = lax.broadcasted_iota(jnp.int32, v.shape, 0)
    return jnp.where(row >= k, pltpu.roll(v, k, axis=0), 0.0)


def _shift_up(v, k):
    n = v.shape[0]
    row = lax.broadcasted_iota(jnp.int32, v.shape, 0)
    return jnp.where(row < n - k, pltpu.roll(v, n - k, axis=0), 0.0)


def _conv(gate, w, b):
    return b + w[0:1] * _shift_down(gate, 2) + w[1:2] * _shift_down(gate, 1) + w[2:3] * gate


def _conv_fwd(name, up, conv_w, conv_b):
    s, f2 = up.shape
    f = f2 // 2
    tc = _tile(f, 256, 128)
    nf = f // tc

    def body(g_ref, v_ref, w_ref, b_ref, o_ref):
        o_ref[...] = (jax.nn.gelu(_conv(g_ref[...], w_ref[...], b_ref[...])) * v_ref[...]).astype(o_ref.dtype)

    return pl.pallas_call(
        body, name=name, grid=(nf,), out_shape=jax.ShapeDtypeStruct((s, f), BF16),
        in_specs=[pl.BlockSpec((s, tc), lambda j: (0, j)), pl.BlockSpec((s, tc), lambda j: (0, nf + j)),
                  pl.BlockSpec((3, tc), lambda j: (0, j)), pl.BlockSpec((1, tc), lambda j: (0, j))],
        out_specs=pl.BlockSpec((s, tc), lambda j: (0, j)),
        compiler_params=_params("parallel"),
    )(up, up, conv_w, conv_b)


def _conv_bwd(name, up, da, conv_w, conv_b):
    s, f2 = up.shape
    f = f2 // 2
    tc = _tile(f, 256, 128)
    nf = f // tc

    def body(g_ref, v_ref, da_ref, w_ref, b_ref, dg_ref, dv_ref, dw_ref, db_ref):
        gate, val, dact, w = g_ref[...], v_ref[...], da_ref[...], w_ref[...]
        act, vjp = jax.vjp(jax.nn.gelu, _conv(gate, w, b_ref[...]))
        dv_ref[...] = (dact * act).astype(dv_ref.dtype)
        dconv = vjp(dact * val)[0]
        dg_ref[...] = (w[2:3] * dconv + w[1:2] * _shift_up(dconv, 1) + w[0:1] * _shift_up(dconv, 2)
                       ).astype(dg_ref.dtype)
        dw_ref[0:1, :] = jnp.sum(dconv * _shift_down(gate, 2), axis=0, keepdims=True)
        dw_ref[1:2, :] = jnp.sum(dconv * _shift_down(gate, 1), axis=0, keepdims=True)
        dw_ref[2:3, :] = jnp.sum(dconv * gate, axis=0, keepdims=True)
        db_ref[...] = jnp.sum(dconv, axis=0, keepdims=True)

    shape = jax.ShapeDtypeStruct((s, f), BF16)
    return pl.pallas_call(
        body, name=name, grid=(nf,),
        out_shape=[shape, shape, jax.ShapeDtypeStruct((3, f), F32), jax.ShapeDtypeStruct((1, f), F32)],
        in_specs=[pl.BlockSpec((s, tc), lambda j: (0, j)), pl.BlockSpec((s, tc), lambda j: (0, nf + j)),
                  pl.BlockSpec((s, tc), lambda j: (0, j)),
                  pl.BlockSpec((3, tc), lambda j: (0, j)), pl.BlockSpec((1, tc), lambda j: (0, j))],
        out_specs=[pl.BlockSpec((s, tc), lambda j: (0, j)), pl.BlockSpec((s, tc), lambda j: (0, j)),
                   pl.BlockSpec((3, tc), lambda j: (0, j)), pl.BlockSpec((1, tc), lambda j: (0, j))],
        compiler_params=_params("parallel"),
    )(up, up, da, conv_w, conv_b)


def _adamw(w, g, m, v):
    m = ADAM_B1 * m + (1.0 - ADAM_B1) * g
    v = ADAM_B2 * v + (1.0 - ADAM_B2) * jnp.square(g)
    m_hat = m / (1.0 - ADAM_B1 ** ADAM_STEP)
    v_hat = v / (1.0 - ADAM_B2 ** ADAM_STEP)
    delta = -ADAM_LR * (m_hat / (jnp.sqrt(v_hat) + ADAM_EPS) + ADAM_WD * w)
    return delta, m, v


def _update_shard(name, layer, w, m, v, own, received, chip, previous=None):
    _, rows, cols = w.shape
    tr = _tile(rows, 256)

    def body(chip_ref, w_ref, m_ref, v_ref, own_ref, rec_ref, *rest):
        g_ref, d_ref, nm_ref, nv_ref = rest[-4:]
        g = (own_ref[...].astype(F32) + rec_ref[0].astype(F32) + rec_ref[1].astype(F32)
             + rec_ref[2].astype(F32))
        delta, new_m, new_v = _adamw(w_ref[...], g, m_ref[...], v_ref[...])
        g_ref[...] = g
        d_ref[...] = delta
        nm_ref[...] = new_m
        nv_ref[...] = new_v

    layer_spec = pl.BlockSpec((None, tr, cols), lambda i, chip_ref: (layer, i, 0))
    in_specs = [layer_spec] * 3 + [pl.BlockSpec((None, tr, cols), lambda i, chip_ref: (chip_ref[0], i, 0)),
                                   pl.BlockSpec((3, tr, cols), lambda i, chip_ref: (0, i, 0))]
    args = [chip, w, m, v, own, received]
    aliases = {}
    if previous is not None:
        in_specs += [ANY] * 4
        aliases = {len(args) + k: k for k in range(4)}
        args += list(previous)
    return pl.pallas_call(
        body, name=name, out_shape=[jax.ShapeDtypeStruct(w.shape, F32)] * 4,
        grid_spec=pltpu.PrefetchScalarGridSpec(
            num_scalar_prefetch=1, grid=(rows // tr,), in_specs=in_specs, out_specs=[layer_spec] * 4),
        input_output_aliases=aliases, compiler_params=_params("parallel"),
    )(*args)


def _sum_devices(name, gathered):
    _, rows, cols = gathered.shape

    def body(g_ref, o_ref):
        total = g_ref[0]
        for d in range(1, N_DEV):
            total = total + g_ref[d]
        o_ref[...] = total

    return pl.pallas_call(
        body, name=name, out_shape=jax.ShapeDtypeStruct((rows, cols), F32),
        in_specs=[pl.BlockSpec((N_DEV, rows, cols), lambda: (0, 0, 0))],
        out_specs=pl.BlockSpec((rows, cols), lambda: (0, 0)),
        compiler_params=pltpu.CompilerParams(vmem_limit_bytes=VMEM_LIMIT_V7X),
    )(gathered)


def _update_small(name, w, g, m, v):
    def body(w_ref, g_ref, m_ref, v_ref, d_ref, nm_ref, nv_ref):
        delta, new_m, new_v = _adamw(w_ref[...], g_ref[...], m_ref[...], v_ref[...])
        d_ref[...] = delta
        nm_ref[...] = new_m
        nv_ref[...] = new_v

    spec = pl.BlockSpec(w.shape, lambda: (0, 0))
    return pl.pallas_call(
        body, name=name, out_shape=[jax.ShapeDtypeStruct(w.shape, F32)] * 3,
        in_specs=[spec] * 4, out_specs=[spec] * 3,
        compiler_params=pltpu.CompilerParams(vmem_limit_bytes=VMEM_LIMIT_V7X),
    )(w, g, m, v)


def _pack(arrays):
    flat = jnp.concatenate([a.reshape(-1) for a in arrays])
    pad = (-flat.shape[0]) % (8 * BLK)
    return jnp.pad(flat, (0, pad)).reshape(-1, BLK)


def _unpack(packed, shapes):
    flat = packed.reshape(-1)
    out, at = [], 0
    for shape in shapes:
        size = 1
        for d in shape:
            size *= d
        out.append(flat[at:at + size].reshape(shape))
        at += size
    return out


SHARDED = ("w_in", "w_mem_kv", "w_gate", "w_br_sb", "w_br_gm", "w_br_xa", "w_out", "w_up", "w_down")
SMALL = ("g_mix_pre", "g_vnorm", "w_s", "b_s", "g_mem", "b_gate", "g_mix_post", "g_ffn_pre", "conv_w",
         "conv_b", "g_ffn_post")
WEIGHTS = ("g_mix_pre", "w_in", "g_vnorm", "w_s", "b_s", "g_mem", "w_mem_kv", "w_gate", "b_gate", "w_br_sb",
           "w_br_gm", "w_br_xa", "w_out", "g_mix_post", "g_ffn_pre", "w_up", "conv_w", "conv_b", "w_down",
           "g_ffn_post")


ROW_SHARDED = ("w_mem_kv", "w_out", "w_down")
GATHER_GROUPS = (("w_in",), ("w_mem_kv", "w_gate"), ("w_br_sb", "w_br_gm", "w_br_xa", "w_out"), ("w_up",),
                 ("w_down",))
REDUCE_GROUPS = (("w_down", "w_up"), ("w_out", "w_br_sb", "w_br_gm", "w_br_xa", "w_mem_kv", "w_gate"), ("w_in",))


def _cast_into_place(name, w, layer, me):
    _, rows, cols = w.shape
    tr = _tile(rows, max(8, STREAM_BLOCK_BYTES // (4 * cols)))

    def body(me_ref, w_ref, o_ref):
        o_ref[...] = w_ref[...].astype(o_ref.dtype)

    return pl.pallas_call(
        body, name=name, out_shape=jax.ShapeDtypeStruct((N_DEV, rows, cols), BF16),
        grid_spec=pltpu.PrefetchScalarGridSpec(
            num_scalar_prefetch=1, grid=(rows // tr,),
            in_specs=[pl.BlockSpec((None, tr, cols), lambda i, me_ref: (layer, i, 0))],
            out_specs=pl.BlockSpec((None, tr, cols), lambda i, me_ref: (me_ref[0], i, 0))),
        compiler_params=_params("parallel"),
    )(me, w)


class _Gather:
    def __init__(self, tag, names, places, after):
        self.tag, self.names, self.n = tag, names, len(places)
        self.sems, self.places, self.token = _split_start(
            f"gather_start_{tag}", places, _gather_first_copies(self.n), after)

    def relay(self, after):
        self.sems, self.places = _split_wait(
            f"gather_relay_{self.tag}", self.places, self.sems, _gather_first_copies(self.n), after,
            then=_gather_relay_copies(self.n))

    def finish(self, after):
        places = _split_wait(f"gather_finish_{self.tag}", self.places, self.sems,
                             _gather_relay_copies(self.n), after)
        full = dict(zip(self.names, places))
        for name in self.names:
            if name in ROW_SHARDED:
                full[name] = full[name].reshape(-1, full[name].shape[-1])
        return full


class _Reduce:
    def __init__(self, tag, names, partials):
        self.tag, self.names, self.n = tag, names, len(partials)
        blocks = [p.reshape((N_CHIP, 2, -1, p.shape[-1])) for p in partials]
        lands = [lax.empty((N_CHIP,) + b.shape[2:], b.dtype) for b in blocks]
        self.sems, self.buffers, self.token = _split_start(
            f"pair_start_{tag}", blocks + lands, _pair_copies(self.n))

    def middle(self, after, core):
        n = self.n
        got = _split_wait(f"pair_finish_{self.tag}", self.buffers, self.sems, _pair_copies(n), after)
        sums = [_pair_sum(f"pair_sum_{name}{self.tag}", got[a], got[n + a], core)
                for a, name in enumerate(self.names)]
        lands = [lax.empty((3,) + s.shape[1:], s.dtype) for s in sums]
        self.sems, self.buffers, self.token = _split_start(
            f"chip_start_{self.tag}", sums + lands, _chip_copies(n))

    def finish(self, after):
        n = self.n
        got = _split_wait(f"chip_finish_{self.tag}", self.buffers, self.sems, _chip_copies(n), after)
        return [(name, got[a], got[n + a]) for a, name in enumerate(self.names)]


def kernel(x, mem, g_mix_pre, w_in, g_vnorm, w_s, b_s, g_mem, w_mem_kv, w_gate, b_gate, w_br_sb, w_br_gm, w_br_xa, w_out, g_mix_post, g_ffn_pre, w_up, conv_w, conv_b, w_down, g_ffn_post, loss_target, m_g_mix_pre, m_w_in, m_g_vnorm, m_w_s, m_b_s, m_g_mem, m_w_mem_kv, m_w_gate, m_b_gate, m_w_br_sb, m_w_br_gm, m_w_br_xa, m_w_out, m_g_mix_post, m_g_ffn_pre, m_w_up, m_conv_w, m_conv_b, m_w_down, m_g_ffn_post, v_g_mix_pre, v_w_in, v_g_vnorm, v_w_s, v_b_s, v_g_mem, v_w_mem_kv, v_w_gate, v_b_gate, v_w_br_sb, v_w_br_gm, v_w_br_xa, v_w_out, v_g_mix_post, v_g_ffn_pre, v_w_up, v_conv_w, v_conv_b, v_w_down, v_g_ffn_post):
    p = dict(g_mix_pre=g_mix_pre, w_in=w_in, g_vnorm=g_vnorm, w_s=w_s, b_s=b_s, g_mem=g_mem, w_mem_kv=w_mem_kv,
             w_gate=w_gate, b_gate=b_gate, w_br_sb=w_br_sb, w_br_gm=w_br_gm, w_br_xa=w_br_xa, w_out=w_out,
             g_mix_post=g_mix_post, g_ffn_pre=g_ffn_pre, w_up=w_up, conv_w=conv_w, conv_b=conv_b, w_down=w_down,
             g_ffn_post=g_ffn_post)
    mom = dict(g_mix_pre=m_g_mix_pre, w_in=m_w_in, g_vnorm=m_g_vnorm, w_s=m_w_s, b_s=m_b_s, g_mem=m_g_mem,
               w_mem_kv=m_w_mem_kv, w_gate=m_w_gate, b_gate=m_b_gate, w_br_sb=m_w_br_sb, w_br_gm=m_w_br_gm,
               w_br_xa=m_w_br_xa, w_out=m_w_out, g_mix_post=m_g_mix_post, g_ffn_pre=m_g_ffn_pre, w_up=m_w_up,
               conv_w=m_conv_w, conv_b=m_conv_b, w_down=m_w_down, g_ffn_post=m_g_ffn_post)
    var = dict(g_mix_pre=v_g_mix_pre, w_in=v_w_in, g_vnorm=v_g_vnorm, w_s=v_w_s, b_s=v_b_s, g_mem=v_g_mem,
               w_mem_kv=v_w_mem_kv, w_gate=v_w_gate, b_gate=v_b_gate, w_br_sb=v_w_br_sb, w_br_gm=v_w_br_gm,
               w_br_xa=v_w_br_xa, w_out=v_w_out, g_mix_post=v_g_mix_post, g_ffn_pre=v_g_ffn_pre, w_up=v_w_up,
               conv_w=v_conv_w, conv_b=v_conv_b, w_down=v_w_down, g_ffn_post=v_g_ffn_post)
    depth = w_in.shape[0]
    x = x[0]
    mem = mem[0]
    target = loss_target[0]
    s, d = x.shape
    width = d // 2
    heads = width // BLK
    cx, cy, cc = lax.axis_index("x"), lax.axis_index("y"), lax.axis_index("c")
    core = cc.astype(jnp.int32).reshape(1)
    chip = (2 * cx + cy).astype(jnp.int32).reshape(1)
    me = 4 * cx + 2 * cy + cc
    me_index = me.astype(jnp.int32).reshape(1)

    conv_all = _all_gather("gather_conv_w", [conv_w])[0]
    conv_w_full = jnp.transpose(conv_all, (1, 2, 0, 3)).reshape(depth, 3, -1)
    gathers = {}
    token = conv_all
    for l in range(depth):
        for gi, names in enumerate(GATHER_GROUPS):
            places = [_cast_into_place(f"cast_{n}{l}", p[n], l, me_index) for n in names]
            gathers[l, gi] = _Gather(f"{l}{gi}", names, places, token)
            token = gathers[l, gi].token

    kept, gathered = [], []
    h = _norm_fwd("pre_norm0", x, g_mix_pre[0][None, :], after=token)
    gathers[0, 0].relay(h)
    for l in range(depth):
        row = lambda name: p[name][l][None, :]
        first, second, third, fourth, fifth = (gathers[l, gi] for gi in range(len(GATHER_GROUPS)))
        w = first.finish(h)
        proj, proj16 = _mm_nn_cs(f"proj{l}", h, w["w_in"], (F32, BF16), tm=1024)
        o_sb, sb_tot = _sb_fwd(f"sb_fwd{l}", proj16, heads)
        b_s3 = b_s[l][:, :, None]
        o_gm = _gm_fwd(f"gm_fwd{l}", proj, row("g_vnorm"), w_s[l], b_s3, width)
        second.relay(o_gm)
        mn = _norm_fwd(f"mem_norm{l}", mem, row("g_mem"))
        w.update(second.finish(mn))
        third.relay(mn)
        mem_kv = _mm_nn(f"mem_kv{l}", mn, w["w_mem_kv"], (BF16,))
        o_xa = _xa_fwd(f"xa_fwd{l}", proj16, mem_kv, width)
        zg = _mm_nn_cs(f"gates{l}", h, w["w_gate"], tm=1024)
        w.update(third.finish(zg))
        ts = [_mm_nn_cs(f"branch_{n}{l}", o, w[f"w_br_{n}"], (BF16,), tm=2048)
              for n, o in (("sb", o_sb), ("gm", o_gm), ("xa", o_xa))]
        merged = _merge_fwd(f"merge{l}", zg, ts, row("b_gate"))
        fourth.relay(merged)
        y1 = _mm_nn(f"out{l}", merged, w["w_out"], tm=1024, tn=1024)
        x1, h2 = _residual_norm_fwd(f"mix_post{l}", x, y1, row("g_mix_post"), row("g_ffn_pre"))
        w.update(fourth.finish(h2))
        up = _mm_nn_cs(f"up{l}", h2, w["w_up"])
        fifth.relay(up)
        act = _conv_fwd(f"conv_fwd{l}", up, conv_w_full[l], row("conv_b"))
        w.update(fifth.finish(act))
        if l + 1 < depth:
            gathers[l + 1, 0].relay(act)
        y2 = _mm_nn(f"down{l}", act, w["w_down"])
        kept.append(dict(x=x, h=h, proj=proj, proj16=proj16, zg=zg, mn=mn, mem_kv=mem_kv, o_sb=o_sb, o_gm=o_gm,
                         o_xa=o_xa, ts=ts, merged=merged, y1=y1, x1=x1, h2=h2, up=up, act=act, y2=y2, b_s=b_s3,
                         sb_tot=sb_tot))
        gathered.append(w)
        if l + 1 < depth:
            x, h = _residual_norm_fwd(f"ffn_post{l}", x1, y2, g_ffn_post[l][None, :], g_mix_pre[l + 1][None, :])

    top = kept[-1]
    dx, dy2, loss_part, dg_ffn_post = _loss_bwd("loss", top["x1"], top["y2"], g_ffn_post[depth - 1][None, :], target)
    loss = lax.psum(loss_part[0, 0], ("x", "y", "c"))
    small_grads = [dict() for _ in range(depth)]
    small_grads[depth - 1]["g_ffn_post"] = dg_ffn_post
    outs = {n: None for n in SHARDED}
    in_flight = []

    def reduce_finish(entry, after):
        layer = int(entry[0].tag[0])
        for name, own, received in entry[0].finish(after):
            outs[name] = _update_shard(f"update_{name}{layer}", layer, p[name], mom[name], var[name],
                                       own, received, chip, outs[name])
        in_flight.remove(entry)

    def reduce_start(new):
        for entry in in_flight:
            entry[2] += entry[1] == 2
        in_flight.append([new, 1, 0])
        for entry in [e for e in in_flight if e[2] >= 2]:
            reduce_finish(entry, new.token)
        return new.token

    def reduce_middle(after):
        for entry in [e for e in in_flight if e[1] == 1]:
            entry[0].middle(after, core)
            after = entry[0].token
            entry[1] = 2
        return after

    behind = None

    for l in reversed(range(depth)):
        k = kept[l]
        w = gathered[l]
        sg = small_grads[l]
        row = lambda name: p[name][l][None, :]
        partials = {}
        group = lambda gi: _Reduce(f"{l}{gi}", REDUCE_GROUPS[gi], [partials[n] for n in REDUCE_GROUPS[gi]])
        partials["w_down"] = _mm_tn(f"d_w_down{l}", k["act"], dy2, after=behind)
        dact = _mm_nt(f"d_act{l}", dy2, w["w_down"], tm=2048)
        dgate, dval, sg["conv_w"], sg["conv_b"] = _conv_bwd(f"conv_bwd{l}", k["up"], dact, conv_w_full[l],
                                                             row("conv_b"))
        dup = (dgate, dval)
        partials["w_up"] = _mm_tn_cs(f"d_w_up{l}", k["h2"], dup)
        behind = reduce_start(group(0))
        dh2 = _mm_nt_cs(f"d_h2{l}", dup, w["w_up"], per_step=2, after=behind)
        dx1, dy1, sg["g_ffn_pre"], sg["g_mix_post"] = _mid_bwd(f"mid_bwd{l}", dx, dh2, k["x1"], k["y1"],
                                                                row("g_ffn_pre"), row("g_mix_post"))
        behind = reduce_middle(dy1)
        partials["w_out"] = _mm_tn(f"d_w_out{l}", k["merged"], dy1, tk=1024, after=behind)
        dmerged = _mm_nt(f"d_merged{l}", dy1, w["w_out"], tm=1024, tn=1024)
        dt_sb, dt_gm, dt_xa, dzg, sg["b_gate"] = _merge_bwd(f"merge_bwd{l}", dmerged, k["zg"], k["ts"], row("b_gate"))
        do = {}
        for n, dt in (("sb", dt_sb), ("gm", dt_gm), ("xa", dt_xa)):
            partials[f"w_br_{n}"] = _mm_tn_cs(f"d_w_br_{n}{l}", k[f"o_{n}"], dt, tk=1024)
            do[n] = _mm_nt_cs_whole(f"d_o_{n}{l}", dt, w[f"w_br_{n}"], F32 if n == "gm" else BF16)
        dq_xa, dk_mem, dv_mem = _xa_bwd(f"xa_bwd{l}", k["proj16"], k["mem_kv"], do["xa"], width)
        dmem_kv = jnp.concatenate([dk_mem, dv_mem], axis=1)
        partials["w_mem_kv"] = _mm_tn(f"d_w_mem_kv{l}", k["mn"], dmem_kv)
        partials["w_gate"] = _mm_tn_cs(f"d_w_gate{l}", k["h"], dzg, tk=1024)
        behind = reduce_start(group(1))
        dh_gate = _mm_nt_cs(f"d_h_gate{l}", dzg, w["w_gate"], per_step=4, after=behind)
        behind = reduce_middle(dh_gate)
        dmn = _mm_nt(f"d_mn{l}", dmem_kv, w["w_mem_kv"], after=behind)
        sg["g_mem"] = _mem_norm_bwd(f"mem_norm_bwd{l}", dmn, mem, row("g_mem"))
        du, dvg, sg["w_s"], db_s, sg["g_vnorm"] = _gm_bwd(f"gm_bwd{l}", k["proj"], do["gm"], row("g_vnorm"),
                                                          p["w_s"][l], k["b_s"], width)
        sg["b_s"] = db_s[:, :, 0]
        dq, dk, dv = _sb_bwd(f"sb_bwd{l}", k["proj16"], do["sb"], k["sb_tot"], heads)
        dproj = jnp.concatenate([dq, dk, dv, du, dvg, dq_xa], axis=1)
        partials["w_in"] = _mm_tn_cs(f"d_w_in{l}", k["h"], dproj, tk=1024)
        behind = reduce_start(group(2))
        dh_in = _mm_nt_cs(f"d_h_in{l}", dproj, w["w_in"], per_step=4, after=behind)
        if l > 0:
            below = kept[l - 1]
            dx, dy2, sg["g_mix_pre"], small_grads[l - 1]["g_ffn_post"] = _bottom_bwd(
                f"bottom_bwd{l}", dx1, dh_in, dh_gate, k["x"], row("g_mix_pre"), below["y2"],
                p["g_ffn_post"][l - 1][None, :])
        else:
            dx, sg["g_mix_pre"] = _bottom_bwd(f"bottom_bwd{l}", dx1, dh_in, dh_gate, k["x"], row("g_mix_pre"))
        if l > 0:
            behind = reduce_middle(dx)

    small_shapes = [small_grads[l][n].shape for l in range(depth) for n in SMALL]
    packed = _pack([small_grads[l][n] for l in range(depth) for n in SMALL])
    small_gathered = _all_gather("gather_small", [packed])[0]

    behind = reduce_middle(small_gathered)
    for entry in list(in_flight):
        reduce_finish(entry, behind)

    total = _sum_devices("sum_small", small_gathered)
    flat = _unpack(total, small_shapes)
    grads = {}
    for i, n in enumerate(SMALL):
        g = jnp.stack([flat[l * len(SMALL) + i] for l in range(depth)]).reshape((depth,) + p[n].shape[1:]
                                                                                 if n != "conv_w" else (depth, 3, -1))
        if n == "conv_w":
            per = conv_w.shape[2]
            g = lax.dynamic_slice_in_dim(g, me * per, per, axis=2)
        grads[n] = g
    delta, new_m, new_v = _update_small(
        "update_small", _pack([p[n] for n in SMALL]), _pack([grads[n] for n in SMALL]),
        _pack([mom[n] for n in SMALL]), _pack([var[n] for n in SMALL]))
    shapes = [p[n].shape for n in SMALL]
    for n, dl, nm, nv in zip(SMALL, _unpack(delta, shapes), _unpack(new_m, shapes), _unpack(new_v, shapes)):
        outs[n] = (grads[n], dl, nm, nv)

    result = [loss, dx[None]]
    for k in range(4):
        result += [outs[n][k] for n in WEIGHTS]
    return tuple(result)
```

```python
import functools

import jax
import jax.numpy as jnp
from jax import lax
from jax.experimental import pallas as pl
from jax.experimental.pallas import tpu as pltpu

F32 = jnp.float32
BF16 = jnp.bfloat16
N_DEV = 8
N_CHIP = 4
EPS = 1e-6
CHUNK = 64
BLK = 128
XA_HEADS = 4
SB_HEADS_PER_STEP = 8
ADAM_LR = 0.001
ADAM_B1 = 0.9
ADAM_B2 = 0.999
ADAM_EPS = 1e-08
ADAM_WD = 0.01
ADAM_STEP = 10
VMEM_LIMIT_V7X = 56 * 1024 * 1024
STREAM_BLOCK_BYTES = 4 * 1024 * 1024
MESH = pl.DeviceIdType.MESH
ANY = pl.BlockSpec(memory_space=pl.ANY)


def _params(*sem):
    return pltpu.CompilerParams(dimension_semantics=sem, vmem_limit_bytes=VMEM_LIMIT_V7X)


def _tile(n, pref, mult=8):
    best = None
    for t in range(mult, min(n, pref) + 1, mult):
        if n % t == 0:
            best = t
    return best if best is not None else n


def _bf(x):
    return x if x.dtype == BF16 else x.astype(BF16)


def _dot(a, b, ca, cb):
    return lax.dot_general(_bf(a), _bf(b), (((ca,), (cb,)), ((), ())), preferred_element_type=F32)


def _rmsnorm(x, g):
    return (x * lax.rsqrt(jnp.mean(x * x, axis=-1, keepdims=True) + EPS)) * g


def _position():
    x, y, c = lax.axis_index("x"), lax.axis_index("y"), lax.axis_index("c")
    return x, y, c


def _all_gather(name, shards):
    n = len(shards)

    def body(*refs):
        ins, outs = refs[:n], refs[n:2 * n]
        send_sems, recv_sems, local_sems = refs[2 * n:]
        x, y, c = _position()
        me = 4 * x + 2 * y + c
        sibling = (x, y, 1 - c)
        chips = [(1 - x, y), (x, 1 - y), (1 - x, 1 - y)]

        def copy(a, k, block, to, src=None):
            dst = outs[a].at[block]
            return pltpu.make_async_remote_copy(
                src_ref=dst if src is None else src, dst_ref=dst, send_sem=send_sems.at[a, k],
                recv_sem=recv_sems.at[a, k], device_id=to, device_id_type=MESH)

        local = [pltpu.make_async_copy(ins[a], outs[a].at[me], local_sems.at[a]) for a in range(n)]
        for cp in local:
            cp.start()
        first = []
        for a in range(n):
            first.append(copy(a, 0, me, sibling, src=ins[a]))
            for j, chip in enumerate(chips):
                first.append(copy(a, 1 + j, me, (*chip, c), src=ins[a]))
        for cp in first:
            cp.start()
        passed = []
        for a in range(n):
            for j, (px, py) in enumerate(chips):
                block = 4 * px + 2 * py + c
                copy(a, 1 + j, block, sibling).wait_recv()
                forward = copy(a, 4 + j, block, sibling)
                forward.start()
                passed.append(forward)
        for a in range(n):
            copy(a, 0, 4 * x + 2 * y + (1 - c), sibling).wait_recv()
            for j, (px, py) in enumerate(chips):
                copy(a, 4 + j, 4 * px + 2 * py + (1 - c), sibling).wait_recv()
        for cp in first + passed:
            cp.wait_send()
        for cp in local:
            cp.wait()

    return pl.pallas_call(
        body, name=name,
        out_shape=[jax.ShapeDtypeStruct((N_DEV,) + s.shape, s.dtype) for s in shards],
        in_specs=[ANY] * n, out_specs=[ANY] * n,
        scratch_shapes=[pltpu.SemaphoreType.DMA((n, 7)), pltpu.SemaphoreType.DMA((n, 7)),
                        pltpu.SemaphoreType.DMA((n,))],
    )(*shards)


HBM = pl.BlockSpec(memory_space=pltpu.HBM)
SEM = pl.BlockSpec(memory_space=pltpu.SEMAPHORE)
DATAFLOW = pltpu.SideEffectType.DATAFLOW_SIDE_EFFECTING


def _remote(src, dst, send, recv, k, peer):
    return pltpu.make_async_remote_copy(src_ref=src, dst_ref=dst, send_sem=send.at[k], recv_sem=recv.at[k],
                                        device_id=peer, device_id_type=MESH)


def _gather_first_copies(n):
    def build(refs, send, recv):
        x, y, c = _position()
        me = 4 * x + 2 * y + c
        peers = [(x, y, 1 - c), (1 - x, y, c), (x, 1 - y, c), (1 - x, 1 - y, c)]
        return [_remote(refs[a].at[me], refs[a].at[me], send, recv, 4 * a + k, peer)
                for a in range(n) for k, peer in enumerate(peers)]
    return build, 4 * n


def _gather_relay_copies(n):
    def build(refs, send, recv):
        x, y, c = _position()
        blocks = [4 * px + 2 * py + c for px, py in ((1 - x, y), (x, 1 - y), (1 - x, 1 - y))]
        return [_remote(refs[a].at[blk], refs[a].at[blk], send, recv, 3 * a + j, (x, y, 1 - c))
                for a in range(n) for j, blk in enumerate(blocks)]
    return build, 3 * n


def _pair_copies(n):
    def build(refs, send, recv):
        x, y, c = _position()
        return [_remote(refs[a].at[k, 1 - c], refs[n + a].at[k], send, recv, N_CHIP * a + k, (x, y, 1 - c))
                for a in range(n) for k in range(N_CHIP)]
    return build, N_CHIP * n


def _chip_copies(n):
    def build(refs, send, recv):
        x, y, c = _position()
        chips = ((1 - x, y), (x, 1 - y), (1 - x, 1 - y))
        return [_remote(refs[a].at[2 * px + py], refs[n + a].at[j], send, recv, 3 * a + j, (px, py, c))
                for a in range(n) for j, (px, py) in enumerate(chips)]
    return build, 3 * n


def _split_start(name, buffers, copies, after=None):
    build, n_copies = copies
    nb = len(buffers)
    extra = [] if after is None else [after]

    def body(*refs):
        at = nb + len(extra)
        for cp in build(refs[:nb], refs[at], refs[at + 1]):
            cp.start()
        token = refs[at + 2 + nb]
        token[...] = jnp.zeros_like(token)

    outs = pl.pallas_call(
        body, name=name,
        out_shape=[pltpu.SemaphoreType.DMA((n_copies,)), pltpu.SemaphoreType.DMA((n_copies,))]
        + [pltpu.HBM(b.shape, b.dtype) for b in buffers] + [jax.ShapeDtypeStruct((8, BLK), F32)],
        in_specs=[HBM] * nb + [ANY] * len(extra),
        out_specs=[SEM, SEM] + [HBM] * nb + [pl.BlockSpec(memory_space=pltpu.VMEM)],
        input_output_aliases={i: 2 + i for i in range(nb)},
        compiler_params=pltpu.CompilerParams(has_side_effects=DATAFLOW),
    )(*[pltpu.with_memory_space_constraint(b, pltpu.HBM) for b in buffers], *extra)
    return (outs[0], outs[1]), list(outs[2:2 + nb]), outs[-1]


def _split_wait(name, buffers, sems, copies, after, then=None):
    build, _ = copies
    nb = len(buffers)
    n_next = 0 if then is None else then[1]
    first_out = nb + 3

    def body(*refs):
        for cp in build(refs[:nb], refs[nb], refs[nb + 1]):
            cp.wait()
        if then is not None:
            for cp in then[0](refs[:nb], refs[first_out], refs[first_out + 1]):
                cp.start()

    sem_shapes = [] if then is None else [pltpu.SemaphoreType.DMA((n_next,))] * 2
    outs = pl.pallas_call(
        body, name=name,
        out_shape=sem_shapes + [pltpu.HBM(b.shape, b.dtype) for b in buffers],
        in_specs=[HBM] * nb + [SEM, SEM, ANY],
        out_specs=[SEM] * len(sem_shapes) + [HBM] * nb,
        input_output_aliases={i: len(sem_shapes) + i for i in range(nb)},
        compiler_params=pltpu.CompilerParams(has_side_effects=DATAFLOW),
    )(*buffers, sems[0], sems[1], after)
    if then is None:
        return list(outs)
    return (outs[0], outs[1]), list(outs[2:])


def _pair_sum(name, partial, received, core):
    _, _, rows, cols = partial.shape
    tr = _tile(rows, max(8, STREAM_BLOCK_BYTES // (2 * cols)))

    def body(core_ref, p_ref, r_ref, o_ref):
        o_ref[...] = (p_ref[...].astype(F32) + r_ref[...].astype(F32)).astype(o_ref.dtype)

    return pl.pallas_call(
        body, name=name, out_shape=jax.ShapeDtypeStruct((N_CHIP, rows, cols), BF16),
        grid_spec=pltpu.PrefetchScalarGridSpec(
            num_scalar_prefetch=1, grid=(N_CHIP, rows // tr),
            in_specs=[pl.BlockSpec((None, None, tr, cols), lambda k, i, core: (k, core[0], i, 0)),
                      pl.BlockSpec((None, tr, cols), lambda k, i, core: (k, i, 0))],
            out_specs=pl.BlockSpec((None, tr, cols), lambda k, i, core: (k, i, 0))),
        compiler_params=_params("parallel", "parallel"),
    )(core, partial, received)


def _mm(name, a, b, out_shape, out_dtypes, grid, a_spec, b_spec, o_spec, ca, cb, k_steps=1, after=None):
    n_out = len(out_dtypes)
    extra = [] if after is None else [after]

    def body(a_ref, b_ref, *rest):
        rest = rest[len(extra):]
        o_refs = rest[:n_out]
        part = _dot(a_ref[...], b_ref[...], ca, cb)
        if k_steps == 1:
            for o in o_refs:
                o[...] = part.astype(o.dtype)
            return
        acc = rest[n_out]
        k = pl.program_id(len(grid) - 1)

        @pl.when(k == 0)
        def _():
            acc[...] = part

        @pl.when(k > 0)
        def _():
            acc[...] += part

        @pl.when(k == k_steps - 1)
        def _():
            for o in o_refs:
                o[...] = acc[...].astype(o.dtype)

    o_block = tuple(d for d in o_spec.block_shape if d is not None)
    sem = ("parallel",) * (len(grid) - 1) + (("arbitrary",) if k_steps > 1 else ("parallel",))
    out = pl.pallas_call(
        body, name=name, grid=grid,
        out_shape=[jax.ShapeDtypeStruct(out_shape, d) for d in out_dtypes],
        in_specs=[a_spec, b_spec] + [ANY] * len(extra), out_specs=[o_spec] * n_out,
        scratch_shapes=[pltpu.VMEM(o_block, F32)] if k_steps > 1 else [],
        compiler_params=_params(*sem),
    )(a, b, *extra)
    return out if n_out > 1 else out[0]


def _mm_nn_cs(name, a, w, out_dtypes=(F32,), tm=512, **kw):
    m, k = a.shape
    _, _, ns = w.shape
    tm = _tile(m, tm)
    return _mm(name, a, w, (m, N_DEV * ns), out_dtypes, (m // tm, N_DEV),
               pl.BlockSpec((tm, k), lambda i, j: (i, 0)),
               pl.BlockSpec((None, k, ns), lambda i, j: (j, 0, 0)),
               pl.BlockSpec((tm, ns), lambda i, j: (i, j)), 1, 0, **kw)


def _mm_nn(name, a, w, out_dtypes=(F32,), tm=512, tn=512, **kw):
    m, k = a.shape
    _, n = w.shape
    tm, tn = _tile(m, tm), _tile(n, tn, 128)
    return _mm(name, a, w, (m, n), out_dtypes, (m // tm, n // tn),
               pl.BlockSpec((tm, k), lambda i, j: (i, 0)),
               pl.BlockSpec((k, tn), lambda i, j: (0, j)),
               pl.BlockSpec((tm, tn), lambda i, j: (i, j)), 1, 0, **kw)


def _mm_nt_cs(name, a, w, tm=512, per_step=1, after=None):
    _, k, ns = w.shape
    parts = list(a) if isinstance(a, (tuple, list)) else [a]
    m = parts[0].shape[0]
    tm = _tile(m, tm)
    steps = N_DEV // per_step
    per_part = steps // len(parts)
    extra = [] if after is None else [after]

    def body(*refs):
        a_refs, w_ref, o_ref = refs[:len(parts)], refs[len(parts)], refs[len(parts) + 1 + len(extra)]
        j = pl.program_id(1)

        def contract(a_ref):
            part = _dot(a_ref[:, 0:ns], w_ref[0], 1, 1)
            for u in range(1, per_step):
                part = part + _dot(a_ref[:, u * ns:(u + 1) * ns], w_ref[u], 1, 1)

            @pl.when(j == 0)
            def _():
                o_ref[...] = part

            @pl.when(j > 0)
            def _():
                o_ref[...] += part

        for n, a_ref in enumerate(a_refs):
            pl.when(jnp.logical_and(j >= n * per_part, j < (n + 1) * per_part))(functools.partial(contract, a_ref))

    a_specs = [pl.BlockSpec((tm, per_step * ns),
                            functools.partial(lambda i, j, n: (i, jnp.clip(j - n * per_part, 0, per_part - 1)), n=n))
               for n in range(len(parts))]
    return pl.pallas_call(
        body, name=name, grid=(m // tm, steps), out_shape=jax.ShapeDtypeStruct((m, k), F32),
        in_specs=a_specs + [pl.BlockSpec((per_step, k, ns), lambda i, j: (j, 0, 0))] + [ANY] * len(extra),
        out_specs=pl.BlockSpec((tm, k), lambda i, j: (i, 0)),
        compiler_params=_params("parallel", "arbitrary"),
    )(*parts, w, *extra)


def _mm_nt_cs_whole(name, a, w, out_dtype, tm=512):
    m, _ = a.shape
    _, k, ns = w.shape
    tm = _tile(m, tm)

    def body(a_ref, w_ref, o_ref):
        acc = _dot(a_ref[:, 0:ns], w_ref[0], 1, 1)
        for j in range(1, N_DEV):
            acc = acc + _dot(a_ref[:, j * ns:(j + 1) * ns], w_ref[j], 1, 1)
        o_ref[...] = acc.astype(o_ref.dtype)

    return pl.pallas_call(
        body, name=name, grid=(m // tm,), out_shape=jax.ShapeDtypeStruct((m, k), out_dtype),
        in_specs=[pl.BlockSpec((tm, N_DEV * ns), lambda i: (i, 0)),
                  pl.BlockSpec((N_DEV, k, ns), lambda i: (0, 0, 0))],
        out_specs=pl.BlockSpec((tm, k), lambda i: (i, 0)),
        compiler_params=_params("parallel"),
    )(a, w)


def _mm_nt(name, a, w, out_dtypes=(F32,), tm=512, tn=512, **kw):
    m, k = a.shape
    n, _ = w.shape
    tm, tn = _tile(m, tm), _tile(n, tn, 128)
    return _mm(name, a, w, (m, n), out_dtypes, (m // tm, n // tn),
               pl.BlockSpec((tm, k), lambda i, j: (i, 0)),
               pl.BlockSpec((tn, k), lambda i, j: (j, 0)),
               pl.BlockSpec((tm, tn), lambda i, j: (i, j)), 1, 1, **kw)


def _mm_tn_cs(name, a, g, tk=512, after=None):
    s, k = a.shape
    parts = list(g) if isinstance(g, (tuple, list)) else [g]
    per_part = N_DEV // len(parts)
    ns = parts[0].shape[1] // per_part
    tk = _tile(k, tk, 128)
    extra = [] if after is None else [after]

    def body(a_ref, *rest):
        g_refs, o_ref = rest[:len(parts)], rest[len(parts) + len(extra)]
        j = pl.program_id(1)

        def shard(g_ref):
            o_ref[...] = _dot(a_ref[...], g_ref[...], 0, 0).astype(o_ref.dtype)

        for n, g_ref in enumerate(g_refs):
            pl.when(jnp.logical_and(j >= n * per_part, j < (n + 1) * per_part))(functools.partial(shard, g_ref))

    g_specs = [pl.BlockSpec((s, ns),
                            functools.partial(lambda i, j, n: (0, jnp.clip(j - n * per_part, 0, per_part - 1)), n=n))
               for n in range(len(parts))]
    return pl.pallas_call(
        body, name=name, grid=(k // tk, N_DEV), out_shape=jax.ShapeDtypeStruct((N_DEV, k, ns), BF16),
        in_specs=[pl.BlockSpec((s, tk), lambda i, j: (0, i))] + g_specs + [ANY] * len(extra),
        out_specs=pl.BlockSpec((None, tk, ns), lambda i, j: (j, i, 0)),
        compiler_params=_params("parallel", "arbitrary"),
    )(a, *parts, *extra)


def _mm_tn(name, a, g, tk=512, tn=1024, **kw):
    s, k = a.shape
    n = g.shape[1]
    tk, tn = _tile(k, tk, 128), _tile(n, tn, 128)
    return _mm(name, a, g, (k, n), (BF16,), (k // tk, n // tn),
               pl.BlockSpec((s, tk), lambda i, j: (0, i)),
               pl.BlockSpec((s, tn), lambda i, j: (0, j)),
               pl.BlockSpec((tk, tn), lambda i, j: (i, j)), 0, 0, **kw)


def _rowwise(name, fn, rows, bcast, outs, reds, tm, after=None):
    n_rows = rows[0][0].shape[0]
    tm = _tile(n_rows, tm)
    n_in, n_b, n_o, n_r = len(rows), len(bcast), len(outs), len(reds)
    extra = [] if after is None else [after]

    def body(*refs):
        ins = refs[:n_in + n_b]
        refs = refs[len(extra):]
        o_refs = refs[n_in + n_b:n_in + n_b + n_o]
        r_refs = refs[n_in + n_b + n_o:]
        vals = fn(*[r[...] for r in ins])
        for o, v in zip(o_refs, vals[:n_o]):
            o[...] = v.astype(o.dtype)
        first = pl.program_id(0) == 0
        for r, v in zip(r_refs, vals[n_o:]):
            @pl.when(first)
            def _(r=r, v=v):
                r[...] = v.astype(r.dtype)

            @pl.when(jnp.logical_not(first))
            def _(r=r, v=v):
                r[...] += v.astype(r.dtype)

    def whole(shape):
        zeros = (0,) * len(shape)
        return pl.BlockSpec(shape, lambda i: zeros)

    in_specs = [pl.BlockSpec((tm, w), functools.partial(lambda i, cb: (i, cb), cb=cb)) for _, w, cb in rows]
    in_specs += [whole(b.shape) for b in bcast] + [ANY] * len(extra)
    out_specs = [pl.BlockSpec((tm, w), lambda i: (i, 0)) for w, _ in outs]
    out_specs += [whole(s) for s, _ in reds]
    out_shape = [jax.ShapeDtypeStruct((n_rows, w), d) for w, d in outs]
    out_shape += [jax.ShapeDtypeStruct(s, d) for s, d in reds]
    return pl.pallas_call(
        body, name=name, grid=(n_rows // tm,), out_shape=out_shape, in_specs=in_specs,
        out_specs=out_specs, compiler_params=_params("arbitrary" if n_r else "parallel"),
    )(*[r[0] for r in rows], *bcast, *extra)


def _norm_fwd(name, x, g, after=None):
    d = x.shape[1]
    return _rowwise(name, lambda xt, gt: (_rmsnorm(xt, gt),), [(x, d, 0)], [g], [(d, BF16)], [], 256,
                    after=after)[0]


def _residual_norm_fwd(name, x, y, g_post, g_next):
    d = x.shape[1]

    def fn(xt, yt, gp, gn):
        new = xt + _rmsnorm(yt, gp)
        return new, _rmsnorm(new, gn)

    return _rowwise(name, fn, [(x, d, 0), (y, d, 0)], [g_post, g_next], [(d, F32), (d, BF16)], [], 256)


def _merge(z0, z1, z2, t0, t1, t2, b0, b1, b2):
    return (jax.nn.sigmoid(z0 + b0) * t0 + jax.nn.sigmoid(z1 + b1) * t1 + jax.nn.sigmoid(z2 + b2) * t2)


def _merge_fwd(name, zg, ts, b_gate):
    d = ts[0].shape[1]
    rows = [(zg, d, b) for b in range(3)] + [(t, d, 0) for t in ts]
    bias = [b_gate[:, b * d:(b + 1) * d] for b in range(3)]
    return _rowwise(name, lambda *v: (_merge(*v),), rows, bias, [(d, BF16)], [], 128)[0]


def _merge_bwd(name, dmerged, zg, ts, b_gate):
    d = ts[0].shape[1]
    rows = [(dmerged, d, 0)] + [(zg, d, b) for b in range(3)] + [(t, d, 0) for t in ts]
    bias = [b_gate[:, b * d:(b + 1) * d] for b in range(3)]

    def fn(dm, *v):
        _, vjp = jax.vjp(_merge, *v)
        dz0, dz1, dz2, dt0, dt1, dt2, db0, db1, db2 = vjp(dm)
        return (dt0, dt1, dt2, jnp.concatenate([dz0, dz1, dz2], axis=1),
                jnp.concatenate([db0, db1, db2], axis=1))

    return _rowwise(name, fn, rows, bias, [(d, BF16)] * 3 + [(3 * d, BF16)], [((1, 3 * d), F32)], 128)


def _loss_bwd(name, x, y, g_post, target):
    d = x.shape[1]

    def loss(xt, yt, gp, tt):
        err = xt + _rmsnorm(yt, gp) - tt
        return 0.5 * jnp.sum(jnp.mean(err * err, axis=-1))

    def fn(xt, yt, tt, gp):
        val, (dx, dy, dg) = jax.value_and_grad(loss, argnums=(0, 1, 2))(xt, yt, gp, tt)
        return dx, dy, jnp.full((1, BLK), val, F32), dg

    return _rowwise(name, fn, [(x, d, 0), (y, d, 0), (target, d, 0)], [g_post],
                    [(d, F32), (d, BF16)], [((1, BLK), F32), ((1, d), F32)], 256)


def _mid_bwd(name, dx_out, dh, x_mid, y, g_pre, g_post):
    d = dx_out.shape[1]

    def fn(dxo, dht, xm, yt, gpre, gpost):
        _, vjp_pre = jax.vjp(_rmsnorm, xm, gpre)
        dxm, dgpre = vjp_pre(dht)
        dxm = dxo + dxm
        _, vjp_post = jax.vjp(_rmsnorm, yt, gpost)
        dy, dgpost = vjp_post(dxm)
        return dxm, dy, dgpre, dgpost

    return _rowwise(name, fn, [(dx_out, d, 0), (dh, d, 0), (x_mid, d, 0), (y, d, 0)], [g_pre, g_post],
                    [(d, F32), (d, BF16)], [((1, d), F32), ((1, d), F32)], 256)


def _bottom_bwd(name, dx_mid, dh_a, dh_b, x, g_pre, y_prev=None, g_post_prev=None):
    d = dx_mid.shape[1]
    rows = [(dx_mid, d, 0), (dh_a, d, 0), (dh_b, d, 0), (x, d, 0)]
    if y_prev is None:
        def fn(dxm, da, db, xt, gpre):
            _, vjp_pre = jax.vjp(_rmsnorm, xt, gpre)
            dx, dgpre = vjp_pre(da + db)
            return dxm + dx, dgpre

        return _rowwise(name, fn, rows, [g_pre], [(d, F32)], [((1, d), F32)], 256)

    def fn2(dxm, da, db, xt, yt, gpre, gpost):
        _, vjp_pre = jax.vjp(_rmsnorm, xt, gpre)
        dx, dgpre = vjp_pre(da + db)
        dx = dxm + dx
        _, vjp_post = jax.vjp(_rmsnorm, yt, gpost)
        dy, dgpost = vjp_post(dx)
        return dx, dy, dgpre, dgpost

    return _rowwise(name, fn2, rows + [(y_prev, d, 0)], [g_pre, g_post_prev],
                    [(d, F32), (d, BF16)], [((1, d), F32), ((1, d), F32)], 256)


def _mem_norm_bwd(name, dmn, mem, g_mem):
    d = mem.shape[1]

    def fn(dt, mt, g):
        _, vjp = jax.vjp(_rmsnorm, mt, g)
        return (vjp(dt)[1],)

    return _rowwise(name, fn, [(dmn, d, 0), (mem, d, 0)], [g_mem], [], [((1, d), F32)], 256)[0]


def _split_dot(v, tri):
    hi = v.astype(BF16)
    lo = (v - hi.astype(F32)).astype(BF16)
    return (jnp.dot(hi, tri, preferred_element_type=F32) + jnp.dot(lo, tri, preferred_element_type=F32))


def _sb_scores(q, kb, strict=None):
    z = _dot(q, kb, 1, 1) * (BLK ** -0.5)
    soft = jnp.log(1.0 + jnp.exp(-jnp.abs(z)))
    log_beta = jnp.minimum(z, 0.0) - soft
    log_keep = jnp.minimum(-z, 0.0) - soft
    if strict is not None:
        log_keep = jnp.where(strict, log_keep, 0.0)
    return z, log_beta, log_keep


def _block_iotas():
    return (lax.broadcasted_iota(jnp.int32, (BLK, BLK), 0), lax.broadcasted_iota(jnp.int32, (BLK, BLK), 1))


def _sb_fwd(name, proj, heads):
    s = proj.shape[0]
    nq = s // BLK
    hb = _tile(heads, SB_HEADS_PER_STEP, 1)
    groups = heads // hb
    wide = hb * BLK

    def body(q_ref, k_ref, v_ref, o_ref, tot_ref, acc_ref):
        i = pl.program_id(1)
        row, col = _block_iotas()
        diag = col < row
        later = (row > col).astype(BF16)
        head_cols = [slice(hh * BLK, (hh + 1) * BLK) for hh in range(hb)]

        def tiles(off, runs, strict):
            scores = [_sb_scores(q_ref[:, c], k_ref[pl.ds(off, BLK), c], strict) for c in head_cols]
            suffixes = [_split_dot(sc[2], later) for sc in scores]
            new = []
            for hh, c in enumerate(head_cols):
                a = jnp.exp(scores[hh][1] + suffixes[hh] + runs[hh])
                if strict is not None:
                    a = jnp.where(strict, a, 0.0)
                acc_ref[:, c] += _dot(a, v_ref[pl.ds(off, BLK), c], 1, 0)
                new.append(runs[hh] + jnp.sum(scores[hh][2], axis=1, keepdims=True))
            return tuple(new)

        acc_ref[...] = jnp.zeros_like(acc_ref)
        runs = tiles(pl.multiple_of(i * BLK, BLK), (jnp.zeros((BLK, 1), F32),) * hb, diag)
        runs = lax.fori_loop(0, i, lambda t, r: tiles(pl.multiple_of((i - 1 - t) * BLK, BLK), r, None), runs)
        o_ref[...] = acc_ref[...].astype(o_ref.dtype)
        for hh in range(hb):
            tot_ref[:, hh:hh + 1] = runs[hh]

    return pl.pallas_call(
        body, name=name, grid=(groups, nq),
        out_shape=[jax.ShapeDtypeStruct((s, heads * BLK), BF16), jax.ShapeDtypeStruct((groups, s, hb), F32)],
        in_specs=[pl.BlockSpec((BLK, wide), lambda g, i: (i, g)),
                  pl.BlockSpec((s, wide), lambda g, i: (0, groups + g)),
                  pl.BlockSpec((s, wide), lambda g, i: (0, 2 * groups + g))],
        out_specs=[pl.BlockSpec((BLK, wide), lambda g, i: (i, g)),
                   pl.BlockSpec((None, BLK, hb), lambda g, i: (g, i, 0))],
        scratch_shapes=[pltpu.VMEM((BLK, wide), F32)],
        compiler_params=_params("parallel", "arbitrary"),
    )(proj, proj, proj)


def _sb_bwd(name, proj, do, tot, heads, after=None):
    s = proj.shape[0]
    nq = s // BLK
    groups, _, hb = tot.shape
    wide = hb * BLK

    extra = [] if after is None else [after]

    def body(q_ref, k_ref, v_ref, do_ref, tot_ref, *rest):
        dq_ref, dk_ref, dv_ref, dq_acc, dk_acc, dv_acc = rest[len(extra):]
        i = pl.program_id(1)
        row, col = _block_iotas()
        diag = col < row
        upto = (row <= col).astype(BF16)
        earlier = (row < col).astype(BF16)

        @pl.when(i == 0)
        def _():
            dk_acc[...] = jnp.zeros_like(dk_acc)
            dv_acc[...] = jnp.zeros_like(dv_acc)

        dq_acc[...] = jnp.zeros_like(dq_acc)

        head_cols = [slice(hh * BLK, (hh + 1) * BLK) for hh in range(hb)]

        def tiles(off, carry, strict):
            rows = pl.ds(off, BLK)
            scores = [_sb_scores(q_ref[:, c], k_ref[rows, c], strict) for c in head_cols]
            das = [_dot(do_ref[:, c], v_ref[rows, c], 1, 1) for c in head_cols]
            prefixes = [_split_dot(sc[2], upto) for sc in scores]
            dlas = []
            for hh, c in enumerate(head_cols):
                suffix = tot_ref[:, hh:hh + 1] - (prefixes[hh] + carry[2 * hh])
                a = jnp.exp(scores[hh][1] + suffix)
                if strict is not None:
                    a = jnp.where(strict, a, 0.0)
                dlas.append(das[hh] * a)
                dv_acc[rows, c] += _dot(a, do_ref[:, c], 0, 0)
            dkeeps = [_split_dot(dla, earlier) for dla in dlas]
            new = ()
            for hh, c in enumerate(head_cols):
                dkeep = dkeeps[hh] + carry[2 * hh + 1]
                if strict is not None:
                    dkeep = jnp.where(strict, dkeep, 0.0)
                sig = jnp.exp(scores[hh][1])
                dz = ((dlas[hh] * (1.0 - sig) - dkeep * sig) * (BLK ** -0.5)).astype(BF16)
                dk_acc[rows, c] += _dot(dz, q_ref[:, c], 0, 0)
                dq_acc[:, c] += _dot(dz, k_ref[rows, c], 1, 0)
                new += (carry[2 * hh] + jnp.sum(scores[hh][2], axis=1, keepdims=True),
                        carry[2 * hh + 1] + jnp.sum(dlas[hh], axis=1, keepdims=True))
            return new

        carry = lax.fori_loop(0, i, lambda jj, cr: tiles(pl.multiple_of(jj * BLK, BLK), cr, None),
                              (jnp.zeros((BLK, 1), F32),) * (2 * hb))
        tiles(pl.multiple_of(i * BLK, BLK), carry, diag)
        dq_ref[...] = dq_acc[...].astype(dq_ref.dtype)

        @pl.when(i == nq - 1)
        def _():
            dk_ref[...] = dk_acc[...].astype(dk_ref.dtype)
            dv_ref[...] = dv_acc[...].astype(dv_ref.dtype)

    shape = jax.ShapeDtypeStruct((s, heads * BLK), BF16)
    return pl.pallas_call(
        body, name=name, grid=(groups, nq), out_shape=[shape, shape, shape],
        in_specs=[pl.BlockSpec((BLK, wide), lambda g, i: (i, g)),
                  pl.BlockSpec((s, wide), lambda g, i: (0, groups + g)),
                  pl.BlockSpec((s, wide), lambda g, i: (0, 2 * groups + g)),
                  pl.BlockSpec((BLK, wide), lambda g, i: (i, g)),
                  pl.BlockSpec((None, BLK, hb), lambda g, i: (g, i, 0))] + [ANY] * len(extra),
        out_specs=[pl.BlockSpec((BLK, wide), lambda g, i: (i, g)),
                   pl.BlockSpec((s, wide), lambda g, i: (0, g)),
                   pl.BlockSpec((s, wide), lambda g, i: (0, g))],
        scratch_shapes=[pltpu.VMEM((BLK, wide), F32), pltpu.VMEM((s, wide), F32), pltpu.VMEM((s, wide), F32)],
        compiler_params=_params("parallel", "arbitrary"),
    )(proj, proj, proj, do, tot, *extra)


def _xa_probs(q, k, hd):
    z = _dot(q, k, 1, 1) * (hd ** -0.5)
    z = z - jnp.max(z, axis=-1, keepdims=True)
    e = jnp.exp(z)
    return e / jnp.sum(e, axis=-1, keepdims=True)


def _xa_fwd(name, proj, mem_kv, width):
    s = proj.shape[0]
    m = mem_kv.shape[0]
    hd = width // XA_HEADS
    tq = _tile(s, 256)
    q_first = 5 * XA_HEADS

    def body(q_ref, k_ref, v_ref, o_ref):
        p = _xa_probs(q_ref[...], k_ref[...], hd)
        o_ref[...] = _dot(p, v_ref[...], 1, 0).astype(o_ref.dtype)

    return pl.pallas_call(
        body, name=name, grid=(XA_HEADS, s // tq), out_shape=jax.ShapeDtypeStruct((s, width), BF16),
        in_specs=[pl.BlockSpec((tq, hd), lambda h, i: (i, q_first + h)),
                  pl.BlockSpec((m, hd), lambda h, i: (0, h)),
                  pl.BlockSpec((m, hd), lambda h, i: (0, XA_HEADS + h))],
        out_specs=pl.BlockSpec((tq, hd), lambda h, i: (i, h)),
        compiler_params=_params("parallel", "parallel"),
    )(proj, mem_kv, mem_kv)


def _xa_bwd(name, proj, mem_kv, do, width):
    s = proj.shape[0]
    m = mem_kv.shape[0]
    hd = width // XA_HEADS
    tq = _tile(s, 256)
    nq = s // tq
    q_first = 5 * XA_HEADS

    def body(q_ref, k_ref, v_ref, do_ref, dq_ref, dk_ref, dv_ref, dk_acc, dv_acc):
        i = pl.program_id(1)

        @pl.when(i == 0)
        def _():
            dk_acc[...] = jnp.zeros_like(dk_acc)
            dv_acc[...] = jnp.zeros_like(dv_acc)

        q, k, v, dout = q_ref[...], k_ref[...], v_ref[...], do_ref[...]
        p = _xa_probs(q, k, hd)
        dp = _dot(dout, v, 1, 1)
        dv_acc[...] += _dot(p, dout, 0, 0)
        dz = ((p * (dp - jnp.sum(dp * p, axis=-1, keepdims=True))) * (hd ** -0.5)).astype(BF16)
        dq_ref[...] = _dot(dz, k, 1, 0).astype(dq_ref.dtype)
        dk_acc[...] += _dot(dz, q, 0, 0)

        @pl.when(i == nq - 1)
        def _():
            dk_ref[...] = dk_acc[...].astype(dk_ref.dtype)
            dv_ref[...] = dv_acc[...].astype(dv_ref.dtype)

    dq, dk, dv = pl.pallas_call(
        body, name=name, grid=(XA_HEADS, nq),
        out_shape=[jax.ShapeDtypeStruct((s, width), BF16), jax.ShapeDtypeStruct((m, width), BF16),
                   jax.ShapeDtypeStruct((m, width), BF16)],
        in_specs=[pl.BlockSpec((tq, hd), lambda h, i: (i, q_first + h)),
                  pl.BlockSpec((m, hd), lambda h, i: (0, h)),
                  pl.BlockSpec((m, hd), lambda h, i: (0, XA_HEADS + h)),
                  pl.BlockSpec((tq, hd), lambda h, i: (i, h))],
        out_specs=[pl.BlockSpec((tq, hd), lambda h, i: (i, h)),
                   pl.BlockSpec((m, hd), lambda h, i: (0, h)),
                   pl.BlockSpec((m, hd), lambda h, i: (0, h))],
        scratch_shapes=[pltpu.VMEM((m, hd), F32), pltpu.VMEM((m, hd), F32)],
        compiler_params=_params("parallel", "arbitrary"),
    )(proj, mem_kv, mem_kv, do)
    return dq, dk, dv


def _gm_pointwise(u_in, v_in, g_vnorm):
    return jax.nn.gelu(u_in), _rmsnorm(jax.nn.gelu(v_in), g_vnorm)


def _gm_mask():
    row = lax.broadcasted_iota(jnp.int32, (BLK, BLK), 0)
    col = lax.broadcasted_iota(jnp.int32, (BLK, BLK), 1)
    return (col // CHUNK) <= (row // CHUNK)


def _gm_fwd(name, proj, g_vnorm, w_s, b_s, width):
    s = proj.shape[0]
    groups = width // BLK

    def body(u_ref, v_ref, g_ref, w_ref, b_ref, o_ref):
        u, vn = _gm_pointwise(u_ref[...], v_ref[...], g_ref[...])
        vn = vn.astype(BF16)
        mask = _gm_mask()
        for g in range(groups):
            cols = slice(g * BLK, (g + 1) * BLK)
            w = jnp.where(mask, w_ref[g], 0.0)
            mixed = _dot(w, vn[:, cols], 1, 0) + b_ref[g]
            o_ref[:, cols] = (u[:, cols] * mixed).astype(o_ref.dtype)

    return pl.pallas_call(
        body, name=name, grid=(s // BLK,), out_shape=jax.ShapeDtypeStruct((s, width), BF16),
        in_specs=[pl.BlockSpec((BLK, width), lambda c: (c, 3)), pl.BlockSpec((BLK, width), lambda c: (c, 4)),
                  pl.BlockSpec((1, width), lambda c: (0, 0)),
                  pl.BlockSpec((groups, BLK, BLK), lambda c: (0, 0, 0)),
                  pl.BlockSpec((groups, BLK, 1), lambda c: (0, 0, 0))],
        out_specs=pl.BlockSpec((BLK, width), lambda c: (c, 0)),
        compiler_params=_params("parallel"),
    )(proj, proj, g_vnorm, w_s, b_s)


def _gm_bwd(name, proj, do, g_vnorm, w_s, b_s, width):
    s = proj.shape[0]
    groups = width // BLK

    def body(u_ref, v_ref, do_ref, g_ref, w_ref, b_ref, du_ref, dv_ref, dw_ref, db_ref, dg_ref, dvn_buf):
        first = pl.program_id(0) == 0
        (u, vn), vjp = jax.vjp(_gm_pointwise, u_ref[...], v_ref[...], g_ref[...])
        vn16 = vn.astype(BF16)
        dout = do_ref[...]
        mask = _gm_mask()
        du_parts = []
        for g in range(groups):
            cols = slice(g * BLK, (g + 1) * BLK)
            w = jnp.where(mask, w_ref[g], 0.0)
            mixed = _dot(w, vn16[:, cols], 1, 0) + b_ref[g]
            du_parts.append(dout[:, cols] * mixed)
            dmixed = dout[:, cols] * u[:, cols]
            dw = jnp.where(mask, _dot(dmixed, vn16[:, cols], 1, 1), 0.0)
            db = jnp.sum(dmixed, axis=1, keepdims=True)
            dvn_buf[:, cols] = _dot(w, dmixed, 0, 0)

            @pl.when(first)
            def _(g=g, dw=dw, db=db):
                dw_ref[g] = dw
                db_ref[g] = db

            @pl.when(jnp.logical_not(first))
            def _(g=g, dw=dw, db=db):
                dw_ref[g] += dw
                db_ref[g] += db

        du_in, dv_in, dg = vjp((jnp.concatenate(du_parts, axis=1), dvn_buf[...]))
        du_ref[...] = du_in.astype(du_ref.dtype)
        dv_ref[...] = dv_in.astype(dv_ref.dtype)

        @pl.when(first)
        def _():
            dg_ref[...] = dg

        @pl.when(jnp.logical_not(first))
        def _():
            dg_ref[...] += dg

    shape = jax.ShapeDtypeStruct((s, width), BF16)
    return pl.pallas_call(
        body, name=name, grid=(s // BLK,),
        out_shape=[shape, shape, jax.ShapeDtypeStruct((groups, BLK, BLK), F32),
                   jax.ShapeDtypeStruct((groups, BLK, 1), F32), jax.ShapeDtypeStruct((1, width), F32)],
        in_specs=[pl.BlockSpec((BLK, width), lambda c: (c, 3)), pl.BlockSpec((BLK, width), lambda c: (c, 4)),
                  pl.BlockSpec((BLK, width), lambda c: (c, 0)),
                  pl.BlockSpec((1, width), lambda c: (0, 0)),
                  pl.BlockSpec((groups, BLK, BLK), lambda c: (0, 0, 0)),
                  pl.BlockSpec((groups, BLK, 1), lambda c: (0, 0, 0))],
        out_specs=[pl.BlockSpec((BLK, width), lambda c: (c, 0)), pl.BlockSpec((BLK, width), lambda c: (c, 0)),
                   pl.BlockSpec((groups, BLK, BLK), lambda c: (0, 0, 0)),
                   pl.BlockSpec((groups, BLK, 1), lambda c: (0, 0, 0)),
                   pl.BlockSpec((1, width), lambda c: (0, 0))],
        scratch_shapes=[pltpu.VMEM((BLK, width), F32)],
        compiler_params=_params("arbitrary"),
    )(proj, proj, do, g_vnorm, w_s, b_s)


def _shift_down(v, k):
    row = lax.broadcasted_iota(jnp.int32, v.shape, 0)
    return jnp.where(row >= k, pltpu.roll(v, k, axis=0), 0.0)


def _shift_up(v, k):
    n = v.shape[0]
    row = lax.broadcasted_iota(jnp.int32, v.shape, 0)
    return jnp.where(row < n - k, pltpu.roll(v, n - k, axis=0), 0.0)


def _conv(gate, w, b):
    return b + w[0:1] * _shift_down(gate, 2) + w[1:2] * _shift_down(gate, 1) + w[2:3] * gate


GELU_C0 = 0.7978845608028654
GELU_C1 = 0.044715


def _gelu_and_slope(x):
    x2 = x * x
    t = jnp.tanh(x * (GELU_C0 + (GELU_C0 * GELU_C1) * x2))
    half = 0.5 * (1.0 + t)
    slope = half + (0.5 * x) * (1.0 - t * t) * (GELU_C0 + (3.0 * GELU_C0 * GELU_C1) * x2)
    return x * half, slope


def _conv_fwd(name, up, conv_w, conv_b):
    s, f2 = up.shape
    f = f2 // 2
    tc = _tile(f, 256, 128)
    nf = f // tc

    def body(g_ref, v_ref, w_ref, b_ref, o_ref):
        o_ref[...] = (jax.nn.gelu(_conv(g_ref[...], w_ref[...], b_ref[...])) * v_ref[...]).astype(o_ref.dtype)

    return pl.pallas_call(
        body, name=name, grid=(nf,), out_shape=jax.ShapeDtypeStruct((s, f), BF16),
        in_specs=[pl.BlockSpec((s, tc), lambda j: (0, j)), pl.BlockSpec((s, tc), lambda j: (0, nf + j)),
                  pl.BlockSpec((3, tc), lambda j: (0, j)), pl.BlockSpec((1, tc), lambda j: (0, j))],
        out_specs=pl.BlockSpec((s, tc), lambda j: (0, j)),
        compiler_params=_params("parallel"),
    )(up, up, conv_w, conv_b)


def _conv_bwd(name, up, da, conv_w, conv_b):
    s, f2 = up.shape
    f = f2 // 2
    tc = _tile(f, 256, 128)
    nf = f // tc

    def body(g_ref, v_ref, da_ref, w_ref, b_ref, dg_ref, dv_ref, dw_ref, db_ref):
        gate, val, dact, w = g_ref[...], v_ref[...], da_ref[...], w_ref[...]
        act, slope = _gelu_and_slope(_conv(gate, w, b_ref[...]))
        dv_ref[...] = (dact * act).astype(dv_ref.dtype)
        dconv = dact * val * slope
        dg_ref[...] = (w[2:3] * dconv + w[1:2] * _shift_up(dconv, 1) + w[0:1] * _shift_up(dconv, 2)
                       ).astype(dg_ref.dtype)
        dw_ref[0:1, :] = jnp.sum(dconv * _shift_down(gate, 2), axis=0, keepdims=True)
        dw_ref[1:2, :] = jnp.sum(dconv * _shift_down(gate, 1), axis=0, keepdims=True)
        dw_ref[2:3, :] = jnp.sum(dconv * gate, axis=0, keepdims=True)
        db_ref[...] = jnp.sum(dconv, axis=0, keepdims=True)

    shape = jax.ShapeDtypeStruct((s, f), BF16)
    return pl.pallas_call(
        body, name=name, grid=(nf,),
        out_shape=[shape, shape, jax.ShapeDtypeStruct((3, f), F32), jax.ShapeDtypeStruct((1, f), F32)],
        in_specs=[pl.BlockSpec((s, tc), lambda j: (0, j)), pl.BlockSpec((s, tc), lambda j: (0, nf + j)),
                  pl.BlockSpec((s, tc), lambda j: (0, j)),
                  pl.BlockSpec((3, tc), lambda j: (0, j)), pl.BlockSpec((1, tc), lambda j: (0, j))],
        out_specs=[pl.BlockSpec((s, tc), lambda j: (0, j)), pl.BlockSpec((s, tc), lambda j: (0, j)),
                   pl.BlockSpec((3, tc), lambda j: (0, j)), pl.BlockSpec((1, tc), lambda j: (0, j))],
        compiler_params=_params("parallel"),
    )(up, up, da, conv_w, conv_b)


def _adamw(w, g, m, v):
    m = ADAM_B1 * m + (1.0 - ADAM_B1) * g
    v = ADAM_B2 * v + (1.0 - ADAM_B2) * jnp.square(g)
    m_hat = m / (1.0 - ADAM_B1 ** ADAM_STEP)
    v_hat = v / (1.0 - ADAM_B2 ** ADAM_STEP)
    delta = -ADAM_LR * (m_hat / (jnp.sqrt(v_hat) + ADAM_EPS) + ADAM_WD * w)
    return delta, m, v


def _update_shard(name, layer, w, m, v, own, received, chip, previous=None):
    _, rows, cols = w.shape
    tr = _tile(rows, 256)

    def body(chip_ref, w_ref, m_ref, v_ref, own_ref, rec_ref, *rest):
        g_ref, d_ref, nm_ref, nv_ref = rest[-4:]
        g = (own_ref[...].astype(F32) + rec_ref[0].astype(F32) + rec_ref[1].astype(F32)
             + rec_ref[2].astype(F32))
        delta, new_m, new_v = _adamw(w_ref[...], g, m_ref[...], v_ref[...])
        g_ref[...] = g
        d_ref[...] = delta
        nm_ref[...] = new_m
        nv_ref[...] = new_v

    layer_spec = pl.BlockSpec((None, tr, cols), lambda i, chip_ref: (layer, i, 0))
    in_specs = [layer_spec] * 3 + [pl.BlockSpec((None, tr, cols), lambda i, chip_ref: (chip_ref[0], i, 0)),
                                   pl.BlockSpec((3, tr, cols), lambda i, chip_ref: (0, i, 0))]
    args = [chip, w, m, v, own, received]
    aliases = {}
    if previous is not None:
        in_specs += [ANY] * 4
        aliases = {len(args) + k: k for k in range(4)}
        args += list(previous)
    return pl.pallas_call(
        body, name=name, out_shape=[jax.ShapeDtypeStruct(w.shape, F32)] * 4,
        grid_spec=pltpu.PrefetchScalarGridSpec(
            num_scalar_prefetch=1, grid=(rows // tr,), in_specs=in_specs, out_specs=[layer_spec] * 4),
        input_output_aliases=aliases, compiler_params=_params("parallel"),
    )(*args)


def _sum_devices(name, gathered):
    _, rows, cols = gathered.shape

    def body(g_ref, o_ref):
        total = g_ref[0]
        for d in range(1, N_DEV):
            total = total + g_ref[d]
        o_ref[...] = total

    return pl.pallas_call(
        body, name=name, out_shape=jax.ShapeDtypeStruct((rows, cols), F32),
        in_specs=[pl.BlockSpec((N_DEV, rows, cols), lambda: (0, 0, 0))],
        out_specs=pl.BlockSpec((rows, cols), lambda: (0, 0)),
        compiler_params=pltpu.CompilerParams(vmem_limit_bytes=VMEM_LIMIT_V7X),
    )(gathered)


def _update_small(name, w, g, m, v):
    def body(w_ref, g_ref, m_ref, v_ref, d_ref, nm_ref, nv_ref):
        delta, new_m, new_v = _adamw(w_ref[...], g_ref[...], m_ref[...], v_ref[...])
        d_ref[...] = delta
        nm_ref[...] = new_m
        nv_ref[...] = new_v

    spec = pl.BlockSpec(w.shape, lambda: (0, 0))
    return pl.pallas_call(
        body, name=name, out_shape=[jax.ShapeDtypeStruct(w.shape, F32)] * 3,
        in_specs=[spec] * 4, out_specs=[spec] * 3,
        compiler_params=pltpu.CompilerParams(vmem_limit_bytes=VMEM_LIMIT_V7X),
    )(w, g, m, v)


def _pack(arrays):
    flat = jnp.concatenate([a.reshape(-1) for a in arrays])
    pad = (-flat.shape[0]) % (8 * BLK)
    return jnp.pad(flat, (0, pad)).reshape(-1, BLK)


def _unpack(packed, shapes):
    flat = packed.reshape(-1)
    out, at = [], 0
    for shape in shapes:
        size = 1
        for d in shape:
            size *= d
        out.append(flat[at:at + size].reshape(shape))
        at += size
    return out


SHARDED = ("w_in", "w_mem_kv", "w_gate", "w_br_sb", "w_br_gm", "w_br_xa", "w_out", "w_up", "w_down")
SMALL = ("g_mix_pre", "g_vnorm", "w_s", "b_s", "g_mem", "b_gate", "g_mix_post", "g_ffn_pre", "conv_w",
         "conv_b", "g_ffn_post")
WEIGHTS = ("g_mix_pre", "w_in", "g_vnorm", "w_s", "b_s", "g_mem", "w_mem_kv", "w_gate", "b_gate", "w_br_sb",
           "w_br_gm", "w_br_xa", "w_out", "g_mix_post", "g_ffn_pre", "w_up", "conv_w", "conv_b", "w_down",
           "g_ffn_post")


ROW_SHARDED = ("w_mem_kv", "w_out", "w_down")
GATHER_GROUPS = (("w_in",), ("w_mem_kv", "w_gate"), ("w_br_sb", "w_br_gm", "w_br_xa", "w_out"), ("w_up",),
                 ("w_down",))
REDUCE_GROUPS = (("w_down", "w_up"), ("w_out", "w_br_sb", "w_br_gm", "w_br_xa", "w_mem_kv", "w_gate"), ("w_in",))


def _cast_into_place(name, w, layer, me):
    _, rows, cols = w.shape
    tr = _tile(rows, max(8, STREAM_BLOCK_BYTES // (4 * cols)))

    def body(me_ref, w_ref, o_ref):
        o_ref[...] = w_ref[...].astype(o_ref.dtype)

    return pl.pallas_call(
        body, name=name, out_shape=jax.ShapeDtypeStruct((N_DEV, rows, cols), BF16),
        grid_spec=pltpu.PrefetchScalarGridSpec(
            num_scalar_prefetch=1, grid=(rows // tr,),
            in_specs=[pl.BlockSpec((None, tr, cols), lambda i, me_ref: (layer, i, 0))],
            out_specs=pl.BlockSpec((None, tr, cols), lambda i, me_ref: (me_ref[0], i, 0))),
        compiler_params=_params("parallel"),
    )(me, w)


class _Gather:
    def __init__(self, tag, names, places, after):
        self.tag, self.names, self.n = tag, names, len(places)
        self.sems, self.places, self.token = _split_start(
            f"gather_start_{tag}", places, _gather_first_copies(self.n), after)

    def relay(self, after):
        self.sems, self.places = _split_wait(
            f"gather_relay_{self.tag}", self.places, self.sems, _gather_first_copies(self.n), after,
            then=_gather_relay_copies(self.n))

    def finish(self, after):
        places = _split_wait(f"gather_finish_{self.tag}", self.places, self.sems,
                             _gather_relay_copies(self.n), after)
        full = dict(zip(self.names, places))
        for name in self.names:
            if name in ROW_SHARDED:
                full[name] = full[name].reshape(-1, full[name].shape[-1])
        return full


class _Reduce:
    def __init__(self, tag, names, partials):
        self.tag, self.names, self.n = tag, names, len(partials)
        blocks = [p.reshape((N_CHIP, 2, -1, p.shape[-1])) for p in partials]
        lands = [lax.empty((N_CHIP,) + b.shape[2:], b.dtype) for b in blocks]
        self.sems, self.buffers, self.token = _split_start(
            f"pair_start_{tag}", blocks + lands, _pair_copies(self.n))

    def middle(self, after, core):
        n = self.n
        got = _split_wait(f"pair_finish_{self.tag}", self.buffers, self.sems, _pair_copies(n), after)
        sums = [_pair_sum(f"pair_sum_{name}{self.tag}", got[a], got[n + a], core)
                for a, name in enumerate(self.names)]
        lands = [lax.empty((3,) + s.shape[1:], s.dtype) for s in sums]
        self.sems, self.buffers, self.token = _split_start(
            f"chip_start_{self.tag}", sums + lands, _chip_copies(n))

    def finish(self, after):
        n = self.n
        got = _split_wait(f"chip_finish_{self.tag}", self.buffers, self.sems, _chip_copies(n), after)
        return [(name, got[a], got[n + a]) for a, name in enumerate(self.names)]


def kernel(x, mem, g_mix_pre, w_in, g_vnorm, w_s, b_s, g_mem, w_mem_kv, w_gate, b_gate, w_br_sb, w_br_gm, w_br_xa, w_out, g_mix_post, g_ffn_pre, w_up, conv_w, conv_b, w_down, g_ffn_post, loss_target, m_g_mix_pre, m_w_in, m_g_vnorm, m_w_s, m_b_s, m_g_mem, m_w_mem_kv, m_w_gate, m_b_gate, m_w_br_sb, m_w_br_gm, m_w_br_xa, m_w_out, m_g_mix_post, m_g_ffn_pre, m_w_up, m_conv_w, m_conv_b, m_w_down, m_g_ffn_post, v_g_mix_pre, v_w_in, v_g_vnorm, v_w_s, v_b_s, v_g_mem, v_w_mem_kv, v_w_gate, v_b_gate, v_w_br_sb, v_w_br_gm, v_w_br_xa, v_w_out, v_g_mix_post, v_g_ffn_pre, v_w_up, v_conv_w, v_conv_b, v_w_down, v_g_ffn_post):
    p = dict(g_mix_pre=g_mix_pre, w_in=w_in, g_vnorm=g_vnorm, w_s=w_s, b_s=b_s, g_mem=g_mem, w_mem_kv=w_mem_kv,
             w_gate=w_gate, b_gate=b_gate, w_br_sb=w_br_sb, w_br_gm=w_br_gm, w_br_xa=w_br_xa, w_out=w_out,
             g_mix_post=g_mix_post, g_ffn_pre=g_ffn_pre, w_up=w_up, conv_w=conv_w, conv_b=conv_b, w_down=w_down,
             g_ffn_post=g_ffn_post)
    mom = dict(g_mix_pre=m_g_mix_pre, w_in=m_w_in, g_vnorm=m_g_vnorm, w_s=m_w_s, b_s=m_b_s, g_mem=m_g_mem,
               w_mem_kv=m_w_mem_kv, w_gate=m_w_gate, b_gate=m_b_gate, w_br_sb=m_w_br_sb, w_br_gm=m_w_br_gm,
               w_br_xa=m_w_br_xa, w_out=m_w_out, g_mix_post=m_g_mix_post, g_ffn_pre=m_g_ffn_pre, w_up=m_w_up,
               conv_w=m_conv_w, conv_b=m_conv_b, w_down=m_w_down, g_ffn_post=m_g_ffn_post)
    var = dict(g_mix_pre=v_g_mix_pre, w_in=v_w_in, g_vnorm=v_g_vnorm, w_s=v_w_s, b_s=v_b_s, g_mem=v_g_mem,
               w_mem_kv=v_w_mem_kv, w_gate=v_w_gate, b_gate=v_b_gate, w_br_sb=v_w_br_sb, w_br_gm=v_w_br_gm,
               w_br_xa=v_w_br_xa, w_out=v_w_out, g_mix_post=v_g_mix_post, g_ffn_pre=v_g_ffn_pre, w_up=v_w_up,
               conv_w=v_conv_w, conv_b=v_conv_b, w_down=v_w_down, g_ffn_post=v_g_ffn_post)
    depth = w_in.shape[0]
    x = x[0]
    mem = mem[0]
    target = loss_target[0]
    s, d = x.shape
    width = d // 2
    heads = width // BLK
    cx, cy, cc = lax.axis_index("x"), lax.axis_index("y"), lax.axis_index("c")
    core = cc.astype(jnp.int32).reshape(1)
    chip = (2 * cx + cy).astype(jnp.int32).reshape(1)
    me = 4 * cx + 2 * cy + cc
    me_index = me.astype(jnp.int32).reshape(1)

    conv_all = _all_gather("gather_conv_w", [conv_w])[0]
    conv_w_full = jnp.transpose(conv_all, (1, 2, 0, 3)).reshape(depth, 3, -1)
    gathers = {}
    token = conv_all
    for l in range(depth):
        for gi, names in enumerate(GATHER_GROUPS):
            places = [_cast_into_place(f"cast_{n}{l}", p[n], l, me_index) for n in names]
            gathers[l, gi] = _Gather(f"{l}{gi}", names, places, token)
            token = gathers[l, gi].token

    kept, gathered = [], []
    h = _norm_fwd("pre_norm0", x, g_mix_pre[0][None, :], after=token)
    gathers[0, 0].relay(h)
    for l in range(depth):
        row = lambda name: p[name][l][None, :]
        first, second, third, fourth, fifth = (gathers[l, gi] for gi in range(len(GATHER_GROUPS)))
        w = first.finish(h)
        proj, proj16 = _mm_nn_cs(f"proj{l}", h, w["w_in"], (F32, BF16), tm=1024)
        o_sb, sb_tot = _sb_fwd(f"sb_fwd{l}", proj16, heads)
        b_s3 = b_s[l][:, :, None]
        o_gm = _gm_fwd(f"gm_fwd{l}", proj, row("g_vnorm"), w_s[l], b_s3, width)
        second.relay(o_gm)
        mn = _norm_fwd(f"mem_norm{l}", mem, row("g_mem"))
        w.update(second.finish(mn))
        third.relay(mn)
        mem_kv = _mm_nn(f"mem_kv{l}", mn, w["w_mem_kv"], (BF16,))
        o_xa = _xa_fwd(f"xa_fwd{l}", proj16, mem_kv, width)
        zg = _mm_nn_cs(f"gates{l}", h, w["w_gate"], tm=1024)
        w.update(third.finish(zg))
        ts = [_mm_nn_cs(f"branch_{n}{l}", o, w[f"w_br_{n}"], (BF16,), tm=2048)
              for n, o in (("sb", o_sb), ("gm", o_gm), ("xa", o_xa))]
        merged = _merge_fwd(f"merge{l}", zg, ts, row("b_gate"))
        fourth.relay(merged)
        y1 = _mm_nn(f"out{l}", merged, w["w_out"], tm=1024, tn=1024)
        x1, h2 = _residual_norm_fwd(f"mix_post{l}", x, y1, row("g_mix_post"), row("g_ffn_pre"))
        w.update(fourth.finish(h2))
        up = _mm_nn_cs(f"up{l}", h2, w["w_up"])
        fifth.relay(up)
        act = _conv_fwd(f"conv_fwd{l}", up, conv_w_full[l], row("conv_b"))
        w.update(fifth.finish(act))
        if l + 1 < depth:
            gathers[l + 1, 0].relay(act)
        y2 = _mm_nn(f"down{l}", act, w["w_down"])
        kept.append(dict(x=x, h=h, proj=proj, proj16=proj16, zg=zg, mn=mn, mem_kv=mem_kv, o_sb=o_sb, o_gm=o_gm,
                         o_xa=o_xa, ts=ts, merged=merged, y1=y1, x1=x1, h2=h2, up=up, act=act, y2=y2, b_s=b_s3,
                         sb_tot=sb_tot))
        gathered.append(w)
        if l + 1 < depth:
            x, h = _residual_norm_fwd(f"ffn_post{l}", x1, y2, g_ffn_post[l][None, :], g_mix_pre[l + 1][None, :])

    top = kept[-1]
    dx, dy2, loss_part, dg_ffn_post = _loss_bwd("loss", top["x1"], top["y2"], g_ffn_post[depth - 1][None, :], target)
    loss = lax.psum(loss_part[0, 0], ("x", "y", "c"))
    small_grads = [dict() for _ in range(depth)]
    small_grads[depth - 1]["g_ffn_post"] = dg_ffn_post
    small_early = [(ll, n) for ll in range(depth) for n in SMALL if (ll, n) != (0, "g_mix_pre")]
    outs = {n: None for n in SHARDED}
    in_flight = []

    def reduce_finish(entry, after):
        layer = int(entry[0].tag[0])
        for name, own, received in entry[0].finish(after):
            outs[name] = _update_shard(f"update_{name}{layer}", layer, p[name], mom[name], var[name],
                                       own, received, chip, outs[name])
        in_flight.remove(entry)

    def reduce_start(new):
        for entry in in_flight:
            entry[2] += entry[1] == 2
        in_flight.append([new, 1, 0])
        for entry in [e for e in in_flight if e[2] >= 2]:
            reduce_finish(entry, new.token)
        return new.token

    def reduce_middle(after):
        for entry in [e for e in in_flight if e[1] == 1]:
            entry[0].middle(after, core)
            after = entry[0].token
            entry[1] = 2
        return after

    behind = None

    for l in reversed(range(depth)):
        k = kept[l]
        w = gathered[l]
        sg = small_grads[l]
        row = lambda name: p[name][l][None, :]
        partials = {}
        group = lambda gi: _Reduce(f"{l}{gi}", REDUCE_GROUPS[gi], [partials[n] for n in REDUCE_GROUPS[gi]])
        partials["w_down"] = _mm_tn(f"d_w_down{l}", k["act"], dy2, after=behind)
        dact = _mm_nt(f"d_act{l}", dy2, w["w_down"], tm=2048)
        dgate, dval, sg["conv_w"], sg["conv_b"] = _conv_bwd(f"conv_bwd{l}", k["up"], dact, conv_w_full[l],
                                                             row("conv_b"))
        dup = (dgate, dval)
        partials["w_up"] = _mm_tn_cs(f"d_w_up{l}", k["h2"], dup)
        behind = reduce_start(group(0))
        dh2 = _mm_nt_cs(f"d_h2{l}", dup, w["w_up"], per_step=2, after=behind)
        dx1, dy1, sg["g_ffn_pre"], sg["g_mix_post"] = _mid_bwd(f"mid_bwd{l}", dx, dh2, k["x1"], k["y1"],
                                                                row("g_ffn_pre"), row("g_mix_post"))
        behind = reduce_middle(dy1)
        partials["w_out"] = _mm_tn(f"d_w_out{l}", k["merged"], dy1, tk=1024, after=behind)
        dmerged = _mm_nt(f"d_merged{l}", dy1, w["w_out"], tm=1024, tn=1024)
        dt_sb, dt_gm, dt_xa, dzg, sg["b_gate"] = _merge_bwd(f"merge_bwd{l}", dmerged, k["zg"], k["ts"], row("b_gate"))
        do = {}
        for n, dt in (("sb", dt_sb), ("gm", dt_gm), ("xa", dt_xa)):
            partials[f"w_br_{n}"] = _mm_tn_cs(f"d_w_br_{n}{l}", k[f"o_{n}"], dt, tk=1024)
            do[n] = _mm_nt_cs_whole(f"d_o_{n}{l}", dt, w[f"w_br_{n}"], F32 if n == "gm" else BF16)
        dq_xa, dk_mem, dv_mem = _xa_bwd(f"xa_bwd{l}", k["proj16"], k["mem_kv"], do["xa"], width)
        dmem_kv = jnp.concatenate([dk_mem, dv_mem], axis=1)
        partials["w_mem_kv"] = _mm_tn(f"d_w_mem_kv{l}", k["mn"], dmem_kv)
        partials["w_gate"] = _mm_tn_cs(f"d_w_gate{l}", k["h"], dzg, tk=1024)
        behind = reduce_start(group(1))
        dh_gate = _mm_nt_cs(f"d_h_gate{l}", dzg, w["w_gate"], per_step=4, after=behind)
        behind = reduce_middle(dh_gate)
        dmn = _mm_nt(f"d_mn{l}", dmem_kv, w["w_mem_kv"], after=behind)
        sg["g_mem"] = _mem_norm_bwd(f"mem_norm_bwd{l}", dmn, mem, row("g_mem"))
        du, dvg, sg["w_s"], db_s, sg["g_vnorm"] = _gm_bwd(f"gm_bwd{l}", k["proj"], do["gm"], row("g_vnorm"),
                                                          p["w_s"][l], k["b_s"], width)
        sg["b_s"] = db_s[:, :, 0]
        small_token = None
        if l == 0:
            early_shapes = [small_grads[ll][n].shape for ll, n in small_early]
            packed = _pack([small_grads[ll][n] for ll, n in small_early])
            place = lax.dynamic_update_slice(lax.empty((N_DEV,) + packed.shape, F32), packed[None], (me, 0, 0))
            small_gather = _Gather("small", ("small",), [place], du)
            small_token = small_gather.token
        dq, dk, dv = _sb_bwd(f"sb_bwd{l}", k["proj16"], do["sb"], k["sb_tot"], heads, after=small_token)
        if l == 0:
            small_gather.relay(dq)
        dproj = jnp.concatenate([dq, dk, dv, du, dvg, dq_xa], axis=1)
        partials["w_in"] = _mm_tn_cs(f"d_w_in{l}", k["h"], dproj, tk=1024)
        behind = reduce_start(group(2))
        dh_in = _mm_nt_cs(f"d_h_in{l}", dproj, w["w_in"], per_step=4, after=behind)
        if l > 0:
            below = kept[l - 1]
            dx, dy2, sg["g_mix_pre"], small_grads[l - 1]["g_ffn_post"] = _bottom_bwd(
                f"bottom_bwd{l}", dx1, dh_in, dh_gate, k["x"], row("g_mix_pre"), below["y2"],
                p["g_ffn_post"][l - 1][None, :])
        else:
            dx, sg["g_mix_pre"] = _bottom_bwd(f"bottom_bwd{l}", dx1, dh_in, dh_gate, k["x"], row("g_mix_pre"))
        if l > 0:
            behind = reduce_middle(dx)

    late_gathered = _all_gather("gather_small", [_pack([small_grads[0]["g_mix_pre"]])])[0]
    behind = reduce_middle(late_gathered)
    for entry in list(in_flight):
        reduce_finish(entry, behind)

    early_total = _sum_devices("sum_small", small_gather.finish(behind)["small"])
    late_total = _sum_devices("sum_small_late", late_gathered)
    summed = dict(zip(small_early, _unpack(early_total, early_shapes)))
    summed[0, "g_mix_pre"] = _unpack(late_total, [small_grads[0]["g_mix_pre"].shape])[0]
    grads = {}
    for n in SMALL:
        g = jnp.stack([summed[l, n] for l in range(depth)]).reshape((depth,) + p[n].shape[1:]
                                                                    if n != "conv_w" else (depth, 3, -1))
        if n == "conv_w":
            per = conv_w.shape[2]
            g = lax.dynamic_slice_in_dim(g, me * per, per, axis=2)
        grads[n] = g
    delta, new_m, new_v = _update_small(
        "update_small", _pack([p[n] for n in SMALL]), _pack([grads[n] for n in SMALL]),
        _pack([mom[n] for n in SMALL]), _pack([var[n] for n in SMALL]))
    shapes = [p[n].shape for n in SMALL]
    for n, dl, nm, nv in zip(SMALL, _unpack(delta, shapes), _unpack(new_m, shapes), _unpack(new_v, shapes)):
        outs[n] = (grads[n], dl, nm, nv)

    result = [loss, dx[None]]
    for k in range(4):
        result += [outs[n][k] for n in WEIGHTS]
    return tuple(result)
```

```python
import functools

import jax
import jax.numpy as jnp
from jax import lax
from jax.experimental import pallas as pl
from jax.experimental.pallas import tpu as pltpu

F32 = jnp.float32
BF16 = jnp.bfloat16
N_DEV = 8
N_CHIP = 4
EPS = 1e-6
CHUNK = 64
BLK = 128
XA_HEADS = 4
SB_HEADS_PER_STEP = 8
ADAM_LR = 0.001
ADAM_B1 = 0.9
ADAM_B2 = 0.999
ADAM_EPS = 1e-08
ADAM_WD = 0.01
ADAM_STEP = 10
VMEM_LIMIT_V7X = 56 * 1024 * 1024
STREAM_BLOCK_BYTES = 4 * 1024 * 1024
MESH = pl.DeviceIdType.MESH
ANY = pl.BlockSpec(memory_space=pl.ANY)


def _params(*sem):
    return pltpu.CompilerParams(dimension_semantics=sem, vmem_limit_bytes=VMEM_LIMIT_V7X)


def _tile(n, pref, mult=8):
    best = None
    for t in range(mult, min(n, pref) + 1, mult):
        if n % t == 0:
            best = t
    return best if best is not None else n


def _bf(x):
    return x if x.dtype == BF16 else x.astype(BF16)


def _dot(a, b, ca, cb):
    return lax.dot_general(_bf(a), _bf(b), (((ca,), (cb,)), ((), ())), preferred_element_type=F32)


def _rmsnorm(x, g):
    return (x * lax.rsqrt(jnp.mean(x * x, axis=-1, keepdims=True) + EPS)) * g


def _position():
    x, y, c = lax.axis_index("x"), lax.axis_index("y"), lax.axis_index("c")
    return x, y, c


def _all_gather(name, shards):
    n = len(shards)

    def body(*refs):
        ins, outs = refs[:n], refs[n:2 * n]
        send_sems, recv_sems, local_sems = refs[2 * n:]
        x, y, c = _position()
        me = 4 * x + 2 * y + c
        sibling = (x, y, 1 - c)
        chips = [(1 - x, y), (x, 1 - y), (1 - x, 1 - y)]

        def copy(a, k, block, to, src=None):
            dst = outs[a].at[block]
            return pltpu.make_async_remote_copy(
                src_ref=dst if src is None else src, dst_ref=dst, send_sem=send_sems.at[a, k],
                recv_sem=recv_sems.at[a, k], device_id=to, device_id_type=MESH)

        local = [pltpu.make_async_copy(ins[a], outs[a].at[me], local_sems.at[a]) for a in range(n)]
        for cp in local:
            cp.start()
        first = []
        for a in range(n):
            first.append(copy(a, 0, me, sibling, src=ins[a]))
            for j, chip in enumerate(chips):
                first.append(copy(a, 1 + j, me, (*chip, c), src=ins[a]))
        for cp in first:
            cp.start()
        passed = []
        for a in range(n):
            for j, (px, py) in enumerate(chips):
                block = 4 * px + 2 * py + c
                copy(a, 1 + j, block, sibling).wait_recv()
                forward = copy(a, 4 + j, block, sibling)
                forward.start()
                passed.append(forward)
        for a in range(n):
            copy(a, 0, 4 * x + 2 * y + (1 - c), sibling).wait_recv()
            for j, (px, py) in enumerate(chips):
                copy(a, 4 + j, 4 * px + 2 * py + (1 - c), sibling).wait_recv()
        for cp in first + passed:
            cp.wait_send()
        for cp in local:
            cp.wait()

    return pl.pallas_call(
        body, name=name,
        out_shape=[jax.ShapeDtypeStruct((N_DEV,) + s.shape, s.dtype) for s in shards],
        in_specs=[ANY] * n, out_specs=[ANY] * n,
        scratch_shapes=[pltpu.SemaphoreType.DMA((n, 7)), pltpu.SemaphoreType.DMA((n, 7)),
                        pltpu.SemaphoreType.DMA((n,))],
    )(*shards)


HBM = pl.BlockSpec(memory_space=pltpu.HBM)
SEM = pl.BlockSpec(memory_space=pltpu.SEMAPHORE)
DATAFLOW = pltpu.SideEffectType.DATAFLOW_SIDE_EFFECTING


def _remote(src, dst, send, recv, k, peer):
    return pltpu.make_async_remote_copy(src_ref=src, dst_ref=dst, send_sem=send.at[k], recv_sem=recv.at[k],
                                        device_id=peer, device_id_type=MESH)


def _gather_first_copies(n):
    def build(refs, send, recv):
        x, y, c = _position()
        me = 4 * x + 2 * y + c
        peers = [(x, y, 1 - c), (1 - x, y, c), (x, 1 - y, c), (1 - x, 1 - y, c)]
        return [_remote(refs[a].at[me], refs[a].at[me], send, recv, 4 * a + k, peer)
                for a in range(n) for k, peer in enumerate(peers)]
    return build, 4 * n


def _gather_relay_copies(n):
    def build(refs, send, recv):
        x, y, c = _position()
        blocks = [4 * px + 2 * py + c for px, py in ((1 - x, y), (x, 1 - y), (1 - x, 1 - y))]
        return [_remote(refs[a].at[blk], refs[a].at[blk], send, recv, 3 * a + j, (x, y, 1 - c))
                for a in range(n) for j, blk in enumerate(blocks)]
    return build, 3 * n


def _pair_copies(n):
    def build(refs, send, recv):
        x, y, c = _position()
        return [_remote(refs[a].at[k, 1 - c], refs[n + a].at[k], send, recv, N_CHIP * a + k, (x, y, 1 - c))
                for a in range(n) for k in range(N_CHIP)]
    return build, N_CHIP * n


def _chip_copies(n):
    def build(refs, send, recv):
        x, y, c = _position()
        chips = ((1 - x, y), (x, 1 - y), (1 - x, 1 - y))
        return [_remote(refs[a].at[2 * px + py], refs[n + a].at[j], send, recv, 3 * a + j, (px, py, c))
                for a in range(n) for j, (px, py) in enumerate(chips)]
    return build, 3 * n


def _split_start(name, buffers, copies, after=None):
    build, n_copies = copies
    nb = len(buffers)
    extra = [] if after is None else [after]

    def body(*refs):
        at = nb + len(extra)
        for cp in build(refs[:nb], refs[at], refs[at + 1]):
            cp.start()
        token = refs[at + 2 + nb]
        token[...] = jnp.zeros_like(token)

    outs = pl.pallas_call(
        body, name=name,
        out_shape=[pltpu.SemaphoreType.DMA((n_copies,)), pltpu.SemaphoreType.DMA((n_copies,))]
        + [pltpu.HBM(b.shape, b.dtype) for b in buffers] + [jax.ShapeDtypeStruct((8, BLK), F32)],
        in_specs=[HBM] * nb + [ANY] * len(extra),
        out_specs=[SEM, SEM] + [HBM] * nb + [pl.BlockSpec(memory_space=pltpu.VMEM)],
        input_output_aliases={i: 2 + i for i in range(nb)},
        compiler_params=pltpu.CompilerParams(has_side_effects=DATAFLOW),
    )(*[pltpu.with_memory_space_constraint(b, pltpu.HBM) for b in buffers], *extra)
    return (outs[0], outs[1]), list(outs[2:2 + nb]), outs[-1]


def _split_wait(name, buffers, sems, copies, after, then=None):
    build, _ = copies
    nb = len(buffers)
    n_next = 0 if then is None else then[1]
    first_out = nb + 3

    def body(*refs):
        for cp in build(refs[:nb], refs[nb], refs[nb + 1]):
            cp.wait()
        if then is not None:
            for cp in then[0](refs[:nb], refs[first_out], refs[first_out + 1]):
                cp.start()

    sem_shapes = [] if then is None else [pltpu.SemaphoreType.DMA((n_next,))] * 2
    outs = pl.pallas_call(
        body, name=name,
        out_shape=sem_shapes + [pltpu.HBM(b.shape, b.dtype) for b in buffers],
        in_specs=[HBM] * nb + [SEM, SEM, ANY],
        out_specs=[SEM] * len(sem_shapes) + [HBM] * nb,
        input_output_aliases={i: len(sem_shapes) + i for i in range(nb)},
        compiler_params=pltpu.CompilerParams(has_side_effects=DATAFLOW),
    )(*buffers, sems[0], sems[1], after)
    if then is None:
        return list(outs)
    return (outs[0], outs[1]), list(outs[2:])


def _pair_sum(name, partial, received, core):
    _, _, rows, cols = partial.shape
    tr = _tile(rows, max(8, STREAM_BLOCK_BYTES // (2 * cols)))

    def body(core_ref, p_ref, r_ref, o_ref):
        o_ref[...] = (p_ref[...].astype(F32) + r_ref[...].astype(F32)).astype(o_ref.dtype)

    return pl.pallas_call(
        body, name=name, out_shape=jax.ShapeDtypeStruct((N_CHIP, rows, cols), BF16),
        grid_spec=pltpu.PrefetchScalarGridSpec(
            num_scalar_prefetch=1, grid=(N_CHIP, rows // tr),
            in_specs=[pl.BlockSpec((None, None, tr, cols), lambda k, i, core: (k, core[0], i, 0)),
                      pl.BlockSpec((None, tr, cols), lambda k, i, core: (k, i, 0))],
            out_specs=pl.BlockSpec((None, tr, cols), lambda k, i, core: (k, i, 0))),
        compiler_params=_params("parallel", "parallel"),
    )(core, partial, received)


def _mm(name, a, b, out_shape, out_dtypes, grid, a_spec, b_spec, o_spec, ca, cb, k_steps=1, after=None):
    n_out = len(out_dtypes)
    extra = [] if after is None else [after]

    def body(a_ref, b_ref, *rest):
        rest = rest[len(extra):]
        o_refs = rest[:n_out]
        part = _dot(a_ref[...], b_ref[...], ca, cb)
        if k_steps == 1:
            for o in o_refs:
                o[...] = part.astype(o.dtype)
            return
        acc = rest[n_out]
        k = pl.program_id(len(grid) - 1)

        @pl.when(k == 0)
        def _():
            acc[...] = part

        @pl.when(k > 0)
        def _():
            acc[...] += part

        @pl.when(k == k_steps - 1)
        def _():
            for o in o_refs:
                o[...] = acc[...].astype(o.dtype)

    o_block = tuple(d for d in o_spec.block_shape if d is not None)
    sem = ("parallel",) * (len(grid) - 1) + (("arbitrary",) if k_steps > 1 else ("parallel",))
    out = pl.pallas_call(
        body, name=name, grid=grid,
        out_shape=[jax.ShapeDtypeStruct(out_shape, d) for d in out_dtypes],
        in_specs=[a_spec, b_spec] + [ANY] * len(extra), out_specs=[o_spec] * n_out,
        scratch_shapes=[pltpu.VMEM(o_block, F32)] if k_steps > 1 else [],
        compiler_params=_params(*sem),
    )(a, b, *extra)
    return out if n_out > 1 else out[0]


def _mm_nn_cs(name, a, w, out_dtypes=(F32,), tm=512, **kw):
    m, k = a.shape
    _, _, ns = w.shape
    tm = _tile(m, tm)
    return _mm(name, a, w, (m, N_DEV * ns), out_dtypes, (m // tm, N_DEV),
               pl.BlockSpec((tm, k), lambda i, j: (i, 0)),
               pl.BlockSpec((None, k, ns), lambda i, j: (j, 0, 0)),
               pl.BlockSpec((tm, ns), lambda i, j: (i, j)), 1, 0, **kw)


def _mm_nn(name, a, w, out_dtypes=(F32,), tm=512, tn=512, **kw):
    m, k = a.shape
    _, n = w.shape
    tm, tn = _tile(m, tm), _tile(n, tn, 128)
    return _mm(name, a, w, (m, n), out_dtypes, (m // tm, n // tn),
               pl.BlockSpec((tm, k), lambda i, j: (i, 0)),
               pl.BlockSpec((k, tn), lambda i, j: (0, j)),
               pl.BlockSpec((tm, tn), lambda i, j: (i, j)), 1, 0, **kw)


def _mm_nt_cs(name, a, w, tm=512, per_step=1, after=None):
    _, k, ns = w.shape
    parts = list(a) if isinstance(a, (tuple, list)) else [a]
    m = parts[0].shape[0]
    tm = _tile(m, tm)
    steps = N_DEV // per_step
    per_part = steps // len(parts)
    extra = [] if after is None else [after]

    def body(*refs):
        a_refs, w_ref, o_ref = refs[:len(parts)], refs[len(parts)], refs[len(parts) + 1 + len(extra)]
        j = pl.program_id(1)

        def contract(a_ref):
            part = _dot(a_ref[:, 0:ns], w_ref[0], 1, 1)
            for u in range(1, per_step):
                part = part + _dot(a_ref[:, u * ns:(u + 1) * ns], w_ref[u], 1, 1)

            @pl.when(j == 0)
            def _():
                o_ref[...] = part

            @pl.when(j > 0)
            def _():
                o_ref[...] += part

        for n, a_ref in enumerate(a_refs):
            pl.when(jnp.logical_and(j >= n * per_part, j < (n + 1) * per_part))(functools.partial(contract, a_ref))

    a_specs = [pl.BlockSpec((tm, per_step * ns),
                            functools.partial(lambda i, j, n: (i, jnp.clip(j - n * per_part, 0, per_part - 1)), n=n))
               for n in range(len(parts))]
    return pl.pallas_call(
        body, name=name, grid=(m // tm, steps), out_shape=jax.ShapeDtypeStruct((m, k), F32),
        in_specs=a_specs + [pl.BlockSpec((per_step, k, ns), lambda i, j: (j, 0, 0))] + [ANY] * len(extra),
        out_specs=pl.BlockSpec((tm, k), lambda i, j: (i, 0)),
        compiler_params=_params("parallel", "arbitrary"),
    )(*parts, w, *extra)


def _mm_nt_cs_whole(name, a, w, out_dtype, tm=512):
    m, _ = a.shape
    _, k, ns = w.shape
    tm = _tile(m, tm)

    def body(a_ref, w_ref, o_ref):
        acc = _dot(a_ref[:, 0:ns], w_ref[0], 1, 1)
        for j in range(1, N_DEV):
            acc = acc + _dot(a_ref[:, j * ns:(j + 1) * ns], w_ref[j], 1, 1)
        o_ref[...] = acc.astype(o_ref.dtype)

    return pl.pallas_call(
        body, name=name, grid=(m // tm,), out_shape=jax.ShapeDtypeStruct((m, k), out_dtype),
        in_specs=[pl.BlockSpec((tm, N_DEV * ns), lambda i: (i, 0)),
                  pl.BlockSpec((N_DEV, k, ns), lambda i: (0, 0, 0))],
        out_specs=pl.BlockSpec((tm, k), lambda i: (i, 0)),
        compiler_params=_params("parallel"),
    )(a, w)


def _mm_nt(name, a, w, out_dtypes=(F32,), tm=512, tn=512, **kw):
    m, k = a.shape
    n, _ = w.shape
    tm, tn = _tile(m, tm), _tile(n, tn, 128)
    return _mm(name, a, w, (m, n), out_dtypes, (m // tm, n // tn),
               pl.BlockSpec((tm, k), lambda i, j: (i, 0)),
               pl.BlockSpec((tn, k), lambda i, j: (j, 0)),
               pl.BlockSpec((tm, tn), lambda i, j: (i, j)), 1, 1, **kw)


def _mm_tn_cs(name, a, g, tk=512, after=None):
    s, k = a.shape
    parts = list(g) if isinstance(g, (tuple, list)) else [g]
    per_part = N_DEV // len(parts)
    ns = parts[0].shape[1] // per_part
    tk = _tile(k, tk, 128)
    extra = [] if after is None else [after]

    def body(a_ref, *rest):
        g_refs, o_ref = rest[:len(parts)], rest[len(parts) + len(extra)]
        j = pl.program_id(1)

        def shard(g_ref):
            o_ref[...] = _dot(a_ref[...], g_ref[...], 0, 0).astype(o_ref.dtype)

        for n, g_ref in enumerate(g_refs):
            pl.when(jnp.logical_and(j >= n * per_part, j < (n + 1) * per_part))(functools.partial(shard, g_ref))

    g_specs = [pl.BlockSpec((s, ns),
                            functools.partial(lambda i, j, n: (0, jnp.clip(j - n * per_part, 0, per_part - 1)), n=n))
               for n in range(len(parts))]
    return pl.pallas_call(
        body, name=name, grid=(k // tk, N_DEV), out_shape=jax.ShapeDtypeStruct((N_DEV, k, ns), BF16),
        in_specs=[pl.BlockSpec((s, tk), lambda i, j: (0, i))] + g_specs + [ANY] * len(extra),
        out_specs=pl.BlockSpec((None, tk, ns), lambda i, j: (j, i, 0)),
        compiler_params=_params("parallel", "arbitrary"),
    )(a, *parts, *extra)


def _mm_tn(name, a, g, tk=512, tn=1024, **kw):
    s, k = a.shape
    n = g.shape[1]
    tk, tn = _tile(k, tk, 128), _tile(n, tn, 128)
    return _mm(name, a, g, (k, n), (BF16,), (k // tk, n // tn),
               pl.BlockSpec((s, tk), lambda i, j: (0, i)),
               pl.BlockSpec((s, tn), lambda i, j: (0, j)),
               pl.BlockSpec((tk, tn), lambda i, j: (i, j)), 0, 0, **kw)


def _rowwise(name, fn, rows, bcast, outs, reds, tm, after=None):
    n_rows = rows[0][0].shape[0]
    tm = _tile(n_rows, tm)
    n_in, n_b, n_o, n_r = len(rows), len(bcast), len(outs), len(reds)
    extra = [] if after is None else [after]

    def body(*refs):
        ins = refs[:n_in + n_b]
        refs = refs[len(extra):]
        o_refs = refs[n_in + n_b:n_in + n_b + n_o]
        r_refs = refs[n_in + n_b + n_o:]
        vals = fn(*[r[...] for r in ins])
        for o, v in zip(o_refs, vals[:n_o]):
            o[...] = v.astype(o.dtype)
        first = pl.program_id(0) == 0
        for r, v in zip(r_refs, vals[n_o:]):
            @pl.when(first)
            def _(r=r, v=v):
                r[...] = v.astype(r.dtype)

            @pl.when(jnp.logical_not(first))
            def _(r=r, v=v):
                r[...] += v.astype(r.dtype)

    def whole(shape):
        zeros = (0,) * len(shape)
        return pl.BlockSpec(shape, lambda i: zeros)

    in_specs = [pl.BlockSpec((tm, w), functools.partial(lambda i, cb: (i, cb), cb=cb)) for _, w, cb in rows]
    in_specs += [whole(b.shape) for b in bcast] + [ANY] * len(extra)
    out_specs = [pl.BlockSpec((tm, w), lambda i: (i, 0)) for w, _ in outs]
    out_specs += [whole(s) for s, _ in reds]
    out_shape = [jax.ShapeDtypeStruct((n_rows, w), d) for w, d in outs]
    out_shape += [jax.ShapeDtypeStruct(s, d) for s, d in reds]
    return pl.pallas_call(
        body, name=name, grid=(n_rows // tm,), out_shape=out_shape, in_specs=in_specs,
        out_specs=out_specs, compiler_params=_params("arbitrary" if n_r else "parallel"),
    )(*[r[0] for r in rows], *bcast, *extra)


def _norm_fwd(name, x, g, after=None):
    d = x.shape[1]
    return _rowwise(name, lambda xt, gt: (_rmsnorm(xt, gt),), [(x, d, 0)], [g], [(d, BF16)], [], 256,
                    after=after)[0]


def _residual_norm_fwd(name, x, y, g_post, g_next):
    d = x.shape[1]

    def fn(xt, yt, gp, gn):
        new = xt + _rmsnorm(yt, gp)
        return new, _rmsnorm(new, gn)

    return _rowwise(name, fn, [(x, d, 0), (y, d, 0)], [g_post, g_next], [(d, F32), (d, BF16)], [], 256)


def _merge(z0, z1, z2, t0, t1, t2, b0, b1, b2):
    return (jax.nn.sigmoid(z0 + b0) * t0 + jax.nn.sigmoid(z1 + b1) * t1 + jax.nn.sigmoid(z2 + b2) * t2)


def _merge_fwd(name, zg, ts, b_gate):
    d = ts[0].shape[1]
    rows = [(zg, d, b) for b in range(3)] + [(t, d, 0) for t in ts]
    bias = [b_gate[:, b * d:(b + 1) * d] for b in range(3)]
    return _rowwise(name, lambda *v: (_merge(*v),), rows, bias, [(d, BF16)], [], 128)[0]


def _merge_bwd(name, dmerged, zg, ts, b_gate):
    d = ts[0].shape[1]
    rows = [(dmerged, d, 0)] + [(zg, d, b) for b in range(3)] + [(t, d, 0) for t in ts]
    bias = [b_gate[:, b * d:(b + 1) * d] for b in range(3)]

    def fn(dm, *v):
        _, vjp = jax.vjp(_merge, *v)
        dz0, dz1, dz2, dt0, dt1, dt2, db0, db1, db2 = vjp(dm)
        return (dt0, dt1, dt2, jnp.concatenate([dz0, dz1, dz2], axis=1),
                jnp.concatenate([db0, db1, db2], axis=1))

    return _rowwise(name, fn, rows, bias, [(d, BF16)] * 3 + [(3 * d, BF16)], [((1, 3 * d), F32)], 128)


def _loss_bwd(name, x, y, g_post, target):
    d = x.shape[1]

    def loss(xt, yt, gp, tt):
        err = xt + _rmsnorm(yt, gp) - tt
        return 0.5 * jnp.sum(jnp.mean(err * err, axis=-1))

    def fn(xt, yt, tt, gp):
        val, (dx, dy, dg) = jax.value_and_grad(loss, argnums=(0, 1, 2))(xt, yt, gp, tt)
        return dx, dy, jnp.full((1, BLK), val, F32), dg

    return _rowwise(name, fn, [(x, d, 0), (y, d, 0), (target, d, 0)], [g_post],
                    [(d, F32), (d, BF16)], [((1, BLK), F32), ((1, d), F32)], 256)


def _mid_bwd(name, dx_out, dh, x_mid, y, g_pre, g_post):
    d = dx_out.shape[1]

    def fn(dxo, dht, xm, yt, gpre, gpost):
        _, vjp_pre = jax.vjp(_rmsnorm, xm, gpre)
        dxm, dgpre = vjp_pre(dht)
        dxm = dxo + dxm
        _, vjp_post = jax.vjp(_rmsnorm, yt, gpost)
        dy, dgpost = vjp_post(dxm)
        return dxm, dy, dgpre, dgpost

    return _rowwise(name, fn, [(dx_out, d, 0), (dh, d, 0), (x_mid, d, 0), (y, d, 0)], [g_pre, g_post],
                    [(d, F32), (d, BF16)], [((1, d), F32), ((1, d), F32)], 256)


def _bottom_bwd(name, dx_mid, dh_a, dh_b, x, g_pre, y_prev=None, g_post_prev=None):
    d = dx_mid.shape[1]
    rows = [(dx_mid, d, 0), (dh_a, d, 0), (dh_b, d, 0), (x, d, 0)]
    if y_prev is None:
        def fn(dxm, da, db, xt, gpre):
            _, vjp_pre = jax.vjp(_rmsnorm, xt, gpre)
            dx, dgpre = vjp_pre(da + db)
            return dxm + dx, dgpre

        return _rowwise(name, fn, rows, [g_pre], [(d, F32)], [((1, d), F32)], 256)

    def fn2(dxm, da, db, xt, yt, gpre, gpost):
        _, vjp_pre = jax.vjp(_rmsnorm, xt, gpre)
        dx, dgpre = vjp_pre(da + db)
        dx = dxm + dx
        _, vjp_post = jax.vjp(_rmsnorm, yt, gpost)
        dy, dgpost = vjp_post(dx)
        return dx, dy, dgpre, dgpost

    return _rowwise(name, fn2, rows + [(y_prev, d, 0)], [g_pre, g_post_prev],
                    [(d, F32), (d, BF16)], [((1, d), F32), ((1, d), F32)], 256)


def _mem_norm_bwd(name, dmn, mem, g_mem):
    d = mem.shape[1]

    def fn(dt, mt, g):
        _, vjp = jax.vjp(_rmsnorm, mt, g)
        return (vjp(dt)[1],)

    return _rowwise(name, fn, [(dmn, d, 0), (mem, d, 0)], [g_mem], [], [((1, d), F32)], 256)[0]


def _split_dot(v, tri):
    hi = v.astype(BF16)
    lo = (v - hi.astype(F32)).astype(BF16)
    return (jnp.dot(hi, tri, preferred_element_type=F32) + jnp.dot(lo, tri, preferred_element_type=F32))


def _sb_scores(q, kb, strict=None):
    z = _dot(q, kb, 1, 1) * (BLK ** -0.5)
    soft = jnp.log(1.0 + jnp.exp(-jnp.abs(z)))
    log_beta = jnp.minimum(z, 0.0) - soft
    log_keep = jnp.minimum(-z, 0.0) - soft
    if strict is not None:
        log_keep = jnp.where(strict, log_keep, 0.0)
    return z, log_beta, log_keep


def _block_iotas():
    return (lax.broadcasted_iota(jnp.int32, (BLK, BLK), 0), lax.broadcasted_iota(jnp.int32, (BLK, BLK), 1))


def _sb_fwd(name, proj, heads):
    s = proj.shape[0]
    nq = s // BLK
    hb = _tile(heads, SB_HEADS_PER_STEP, 1)
    groups = heads // hb
    wide = hb * BLK

    def body(q_ref, k_ref, v_ref, o_ref, tot_ref, acc_ref):
        i = pl.program_id(1)
        row, col = _block_iotas()
        diag = col < row
        later = (row > col).astype(BF16)
        head_cols = [slice(hh * BLK, (hh + 1) * BLK) for hh in range(hb)]

        def tiles(off, runs, strict):
            scores = [_sb_scores(q_ref[:, c], k_ref[pl.ds(off, BLK), c], strict) for c in head_cols]
            suffixes = [_split_dot(sc[2], later) for sc in scores]
            new = []
            for hh, c in enumerate(head_cols):
                a = jnp.exp(scores[hh][1] + suffixes[hh] + runs[hh])
                if strict is not None:
                    a = jnp.where(strict, a, 0.0)
                acc_ref[:, c] += _dot(a, v_ref[pl.ds(off, BLK), c], 1, 0)
                new.append(runs[hh] + jnp.sum(scores[hh][2], axis=1, keepdims=True))
            return tuple(new)

        acc_ref[...] = jnp.zeros_like(acc_ref)
        runs = tiles(pl.multiple_of(i * BLK, BLK), (jnp.zeros((BLK, 1), F32),) * hb, diag)
        runs = lax.fori_loop(0, i, lambda t, r: tiles(pl.multiple_of((i - 1 - t) * BLK, BLK), r, None), runs)
        o_ref[...] = acc_ref[...].astype(o_ref.dtype)
        for hh in range(hb):
            tot_ref[:, hh:hh + 1] = runs[hh]

    return pl.pallas_call(
        body, name=name, grid=(groups, nq),
        out_shape=[jax.ShapeDtypeStruct((s, heads * BLK), BF16), jax.ShapeDtypeStruct((groups, s, hb), F32)],
        in_specs=[pl.BlockSpec((BLK, wide), lambda g, i: (i, g)),
                  pl.BlockSpec((s, wide), lambda g, i: (0, groups + g)),
                  pl.BlockSpec((s, wide), lambda g, i: (0, 2 * groups + g))],
        out_specs=[pl.BlockSpec((BLK, wide), lambda g, i: (i, g)),
                   pl.BlockSpec((None, BLK, hb), lambda g, i: (g, i, 0))],
        scratch_shapes=[pltpu.VMEM((BLK, wide), F32)],
        compiler_params=_params("parallel", "arbitrary"),
    )(proj, proj, proj)


def _sb_bwd(name, proj, do, tot, heads, after=None):
    s = proj.shape[0]
    nq = s // BLK
    groups, _, hb = tot.shape
    wide = hb * BLK

    extra = [] if after is None else [after]

    def body(q_ref, k_ref, v_ref, do_ref, tot_ref, *rest):
        dq_ref, dk_ref, dv_ref, dq_acc, dk_acc, dv_acc = rest[len(extra):]
        i = pl.program_id(1)
        row, col = _block_iotas()
        diag = col < row
        upto = (row <= col).astype(BF16)
        earlier = (row < col).astype(BF16)

        @pl.when(i == 0)
        def _():
            dk_acc[...] = jnp.zeros_like(dk_acc)
            dv_acc[...] = jnp.zeros_like(dv_acc)

        dq_acc[...] = jnp.zeros_like(dq_acc)

        head_cols = [slice(hh * BLK, (hh + 1) * BLK) for hh in range(hb)]

        def tiles(off, carry, strict):
            rows = pl.ds(off, BLK)
            scores = [_sb_scores(q_ref[:, c], k_ref[rows, c], strict) for c in head_cols]
            das = [_dot(do_ref[:, c], v_ref[rows, c], 1, 1) for c in head_cols]
            prefixes = [_split_dot(sc[2], upto) for sc in scores]
            dlas = []
            for hh, c in enumerate(head_cols):
                suffix = tot_ref[:, hh:hh + 1] - (prefixes[hh] + carry[2 * hh])
                a = jnp.exp(scores[hh][1] + suffix)
                if strict is not None:
                    a = jnp.where(strict, a, 0.0)
                dlas.append(das[hh] * a)
                dv_acc[rows, c] += _dot(a, do_ref[:, c], 0, 0)
            dkeeps = [_split_dot(dla, earlier) for dla in dlas]
            new = ()
            for hh, c in enumerate(head_cols):
                dkeep = dkeeps[hh] + carry[2 * hh + 1]
                if strict is not None:
                    dkeep = jnp.where(strict, dkeep, 0.0)
                sig = jnp.exp(scores[hh][1])
                dz = ((dlas[hh] * (1.0 - sig) - dkeep * sig) * (BLK ** -0.5)).astype(BF16)
                dk_acc[rows, c] += _dot(dz, q_ref[:, c], 0, 0)
                dq_acc[:, c] += _dot(dz, k_ref[rows, c], 1, 0)
                new += (carry[2 * hh] + jnp.sum(scores[hh][2], axis=1, keepdims=True),
                        carry[2 * hh + 1] + jnp.sum(dlas[hh], axis=1, keepdims=True))
            return new

        carry = lax.fori_loop(0, i, lambda jj, cr: tiles(pl.multiple_of(jj * BLK, BLK), cr, None),
                              (jnp.zeros((BLK, 1), F32),) * (2 * hb))
        tiles(pl.multiple_of(i * BLK, BLK), carry, diag)
        dq_ref[...] = dq_acc[...].astype(dq_ref.dtype)

        @pl.when(i == nq - 1)
        def _():
            dk_ref[...] = dk_acc[...].astype(dk_ref.dtype)
            dv_ref[...] = dv_acc[...].astype(dv_ref.dtype)

    shape = jax.ShapeDtypeStruct((s, heads * BLK), BF16)
    return pl.pallas_call(
        body, name=name, grid=(groups, nq), out_shape=[shape, shape, shape],
        in_specs=[pl.BlockSpec((BLK, wide), lambda g, i: (i, g)),
                  pl.BlockSpec((s, wide), lambda g, i: (0, groups + g)),
                  pl.BlockSpec((s, wide), lambda g, i: (0, 2 * groups + g)),
                  pl.BlockSpec((BLK, wide), lambda g, i: (i, g)),
                  pl.BlockSpec((None, BLK, hb), lambda g, i: (g, i, 0))] + [ANY] * len(extra),
        out_specs=[pl.BlockSpec((BLK, wide), lambda g, i: (i, g)),
                   pl.BlockSpec((s, wide), lambda g, i: (0, g)),
                   pl.BlockSpec((s, wide), lambda g, i: (0, g))],
        scratch_shapes=[pltpu.VMEM((BLK, wide), F32), pltpu.VMEM((s, wide), F32), pltpu.VMEM((s, wide), F32)],
        compiler_params=_params("parallel", "arbitrary"),
    )(proj, proj, proj, do, tot, *extra)


def _xa_probs(q, k, hd):
    z = _dot(q, k, 1, 1) * (hd ** -0.5)
    z = z - jnp.max(z, axis=-1, keepdims=True)
    e = jnp.exp(z)
    return e / jnp.sum(e, axis=-1, keepdims=True)


def _xa_fwd(name, proj, mem_kv, width):
    s = proj.shape[0]
    m = mem_kv.shape[0]
    hd = width // XA_HEADS
    tq = _tile(s, 512)
    q_first = 5 * XA_HEADS

    def body(q_ref, k_ref, v_ref, o_ref):
        p = _xa_probs(q_ref[...], k_ref[...], hd)
        o_ref[...] = _dot(p, v_ref[...], 1, 0).astype(o_ref.dtype)

    return pl.pallas_call(
        body, name=name, grid=(XA_HEADS, s // tq), out_shape=jax.ShapeDtypeStruct((s, width), BF16),
        in_specs=[pl.BlockSpec((tq, hd), lambda h, i: (i, q_first + h)),
                  pl.BlockSpec((m, hd), lambda h, i: (0, h)),
                  pl.BlockSpec((m, hd), lambda h, i: (0, XA_HEADS + h))],
        out_specs=pl.BlockSpec((tq, hd), lambda h, i: (i, h)),
        compiler_params=_params("parallel", "parallel"),
    )(proj, mem_kv, mem_kv)


def _xa_bwd(name, proj, mem_kv, do, width):
    s = proj.shape[0]
    m = mem_kv.shape[0]
    hd = width // XA_HEADS
    tq = _tile(s, 512)
    nq = s // tq
    q_first = 5 * XA_HEADS

    def body(q_ref, k_ref, v_ref, do_ref, dq_ref, dk_ref, dv_ref, dk_acc, dv_acc):
        i = pl.program_id(1)

        @pl.when(i == 0)
        def _():
            dk_acc[...] = jnp.zeros_like(dk_acc)
            dv_acc[...] = jnp.zeros_like(dv_acc)

        q, k, v, dout = q_ref[...], k_ref[...], v_ref[...], do_ref[...]
        p = _xa_probs(q, k, hd)
        dp = _dot(dout, v, 1, 1)
        dv_acc[...] += _dot(p, dout, 0, 0)
        dz = ((p * (dp - jnp.sum(dp * p, axis=-1, keepdims=True))) * (hd ** -0.5)).astype(BF16)
        dq_ref[...] = _dot(dz, k, 1, 0).astype(dq_ref.dtype)
        dk_acc[...] += _dot(dz, q, 0, 0)

        @pl.when(i == nq - 1)
        def _():
            dk_ref[...] = dk_acc[...].astype(dk_ref.dtype)
            dv_ref[...] = dv_acc[...].astype(dv_ref.dtype)

    dq, dk, dv = pl.pallas_call(
        body, name=name, grid=(XA_HEADS, nq),
        out_shape=[jax.ShapeDtypeStruct((s, width), BF16), jax.ShapeDtypeStruct((m, width), BF16),
                   jax.ShapeDtypeStruct((m, width), BF16)],
        in_specs=[pl.BlockSpec((tq, hd), lambda h, i: (i, q_first + h)),
                  pl.BlockSpec((m, hd), lambda h, i: (0, h)),
                  pl.BlockSpec((m, hd), lambda h, i: (0, XA_HEADS + h)),
                  pl.BlockSpec((tq, hd), lambda h, i: (i, h))],
        out_specs=[pl.BlockSpec((tq, hd), lambda h, i: (i, h)),
                   pl.BlockSpec((m, hd), lambda h, i: (0, h)),
                   pl.BlockSpec((m, hd), lambda h, i: (0, h))],
        scratch_shapes=[pltpu.VMEM((m, hd), F32), pltpu.VMEM((m, hd), F32)],
        compiler_params=_params("parallel", "arbitrary"),
    )(proj, mem_kv, mem_kv, do)
    return dq, dk, dv


def _gm_pointwise(u_in, v_in, g_vnorm):
    return jax.nn.gelu(u_in), _rmsnorm(jax.nn.gelu(v_in), g_vnorm)


def _gm_mask():
    row = lax.broadcasted_iota(jnp.int32, (BLK, BLK), 0)
    col = lax.broadcasted_iota(jnp.int32, (BLK, BLK), 1)
    return (col // CHUNK) <= (row // CHUNK)


def _gm_fwd(name, proj, g_vnorm, w_s, b_s, width):
    s = proj.shape[0]
    groups = width // BLK

    def body(u_ref, v_ref, g_ref, w_ref, b_ref, o_ref):
        u, vn = _gm_pointwise(u_ref[...], v_ref[...], g_ref[...])
        vn = vn.astype(BF16)
        mask = _gm_mask()
        for g in range(groups):
            cols = slice(g * BLK, (g + 1) * BLK)
            w = jnp.where(mask, w_ref[g], 0.0)
            mixed = _dot(w, vn[:, cols], 1, 0) + b_ref[g]
            o_ref[:, cols] = (u[:, cols] * mixed).astype(o_ref.dtype)

    return pl.pallas_call(
        body, name=name, grid=(s // BLK,), out_shape=jax.ShapeDtypeStruct((s, width), BF16),
        in_specs=[pl.BlockSpec((BLK, width), lambda c: (c, 3)), pl.BlockSpec((BLK, width), lambda c: (c, 4)),
                  pl.BlockSpec((1, width), lambda c: (0, 0)),
                  pl.BlockSpec((groups, BLK, BLK), lambda c: (0, 0, 0)),
                  pl.BlockSpec((groups, BLK, 1), lambda c: (0, 0, 0))],
        out_specs=pl.BlockSpec((BLK, width), lambda c: (c, 0)),
        compiler_params=_params("parallel"),
    )(proj, proj, g_vnorm, w_s, b_s)


def _gm_bwd(name, proj, do, g_vnorm, w_s, b_s, width):
    s = proj.shape[0]
    groups = width // BLK

    def body(u_ref, v_ref, do_ref, g_ref, w_ref, b_ref, du_ref, dv_ref, dw_ref, db_ref, dg_ref, dvn_buf):
        first = pl.program_id(0) == 0
        (u, vn), vjp = jax.vjp(_gm_pointwise, u_ref[...], v_ref[...], g_ref[...])
        vn16 = vn.astype(BF16)
        dout = do_ref[...]
        mask = _gm_mask()
        du_parts = []
        for g in range(groups):
            cols = slice(g * BLK, (g + 1) * BLK)
            w = jnp.where(mask, w_ref[g], 0.0)
            mixed = _dot(w, vn16[:, cols], 1, 0) + b_ref[g]
            du_parts.append(dout[:, cols] * mixed)
            dmixed = dout[:, cols] * u[:, cols]
            dw = jnp.where(mask, _dot(dmixed, vn16[:, cols], 1, 1), 0.0)
            db = jnp.sum(dmixed, axis=1, keepdims=True)
            dvn_buf[:, cols] = _dot(w, dmixed, 0, 0)

            @pl.when(first)
            def _(g=g, dw=dw, db=db):
                dw_ref[g] = dw
                db_ref[g] = db

            @pl.when(jnp.logical_not(first))
            def _(g=g, dw=dw, db=db):
                dw_ref[g] += dw
                db_ref[g] += db

        du_in, dv_in, dg = vjp((jnp.concatenate(du_parts, axis=1), dvn_buf[...]))
        du_ref[...] = du_in.astype(du_ref.dtype)
        dv_ref[...] = dv_in.astype(dv_ref.dtype)

        @pl.when(first)
        def _():
            dg_ref[...] = dg

        @pl.when(jnp.logical_not(first))
        def _():
            dg_ref[...] += dg

    shape = jax.ShapeDtypeStruct((s, width), BF16)
    return pl.pallas_call(
        body, name=name, grid=(s // BLK,),
        out_shape=[shape, shape, jax.ShapeDtypeStruct((groups, BLK, BLK), F32),
                   jax.ShapeDtypeStruct((groups, BLK, 1), F32), jax.ShapeDtypeStruct((1, width), F32)],
        in_specs=[pl.BlockSpec((BLK, width), lambda c: (c, 3)), pl.BlockSpec((BLK, width), lambda c: (c, 4)),
                  pl.BlockSpec((BLK, width), lambda c: (c, 0)),
                  pl.BlockSpec((1, width), lambda c: (0, 0)),
                  pl.BlockSpec((groups, BLK, BLK), lambda c: (0, 0, 0)),
                  pl.BlockSpec((groups, BLK, 1), lambda c: (0, 0, 0))],
        out_specs=[pl.BlockSpec((BLK, width), lambda c: (c, 0)), pl.BlockSpec((BLK, width), lambda c: (c, 0)),
                   pl.BlockSpec((groups, BLK, BLK), lambda c: (0, 0, 0)),
                   pl.BlockSpec((groups, BLK, 1), lambda c: (0, 0, 0)),
                   pl.BlockSpec((1, width), lambda c: (0, 0))],
        scratch_shapes=[pltpu.VMEM((BLK, width), F32)],
        compiler_params=_params("arbitrary"),
    )(proj, proj, do, g_vnorm, w_s, b_s)


def _shift_down(v, k):
    row = lax.broadcasted_iota(jnp.int32, v.shape, 0)
    return jnp.where(row >= k, pltpu.roll(v, k, axis=0), 0.0)


def _shift_up(v, k):
    n = v.shape[0]
    row = lax.broadcasted_iota(jnp.int32, v.shape, 0)
    return jnp.where(row < n - k, pltpu.roll(v, n - k, axis=0), 0.0)


def _conv(gate, w, b):
    return b + w[0:1] * _shift_down(gate, 2) + w[1:2] * _shift_down(gate, 1) + w[2:3] * gate


GELU_C0 = 0.7978845608028654
GELU_C1 = 0.044715


def _gelu_and_slope(x):
    x2 = x * x
    t = jnp.tanh(x * (GELU_C0 + (GELU_C0 * GELU_C1) * x2))
    half = 0.5 * (1.0 + t)
    slope = half + (0.5 * x) * (1.0 - t * t) * (GELU_C0 + (3.0 * GELU_C0 * GELU_C1) * x2)
    return x * half, slope


def _conv_fwd(name, up, conv_w, conv_b):
    s, f2 = up.shape
    f = f2 // 2
    tc = _tile(f, 256, 128)
    nf = f // tc

    def body(g_ref, v_ref, w_ref, b_ref, o_ref):
        o_ref[...] = (jax.nn.gelu(_conv(g_ref[...], w_ref[...], b_ref[...])) * v_ref[...]).astype(o_ref.dtype)

    return pl.pallas_call(
        body, name=name, grid=(nf,), out_shape=jax.ShapeDtypeStruct((s, f), BF16),
        in_specs=[pl.BlockSpec((s, tc), lambda j: (0, j)), pl.BlockSpec((s, tc), lambda j: (0, nf + j)),
                  pl.BlockSpec((3, tc), lambda j: (0, j)), pl.BlockSpec((1, tc), lambda j: (0, j))],
        out_specs=pl.BlockSpec((s, tc), lambda j: (0, j)),
        compiler_params=_params("parallel"),
    )(up, up, conv_w, conv_b)


def _conv_bwd(name, up, da, conv_w, conv_b):
    s, f2 = up.shape
    f = f2 // 2
    tc = _tile(f, 256, 128)
    nf = f // tc

    def body(g_ref, v_ref, da_ref, w_ref, b_ref, dg_ref, dv_ref, dw_ref, db_ref):
        gate, val, dact, w = g_ref[...], v_ref[...], da_ref[...], w_ref[...]
        act, slope = _gelu_and_slope(_conv(gate, w, b_ref[...]))
        dv_ref[...] = (dact * act).astype(dv_ref.dtype)
        dconv = dact * val * slope
        dg_ref[...] = (w[2:3] * dconv + w[1:2] * _shift_up(dconv, 1) + w[0:1] * _shift_up(dconv, 2)
                       ).astype(dg_ref.dtype)
        dw_ref[0:1, :] = jnp.sum(dconv * _shift_down(gate, 2), axis=0, keepdims=True)
        dw_ref[1:2, :] = jnp.sum(dconv * _shift_down(gate, 1), axis=0, keepdims=True)
        dw_ref[2:3, :] = jnp.sum(dconv * gate, axis=0, keepdims=True)
        db_ref[...] = jnp.sum(dconv, axis=0, keepdims=True)

    shape = jax.ShapeDtypeStruct((s, f), BF16)
    return pl.pallas_call(
        body, name=name, grid=(nf,),
        out_shape=[shape, shape, jax.ShapeDtypeStruct((3, f), F32), jax.ShapeDtypeStruct((1, f), F32)],
        in_specs=[pl.BlockSpec((s, tc), lambda j: (0, j)), pl.BlockSpec((s, tc), lambda j: (0, nf + j)),
                  pl.BlockSpec((s, tc), lambda j: (0, j)),
                  pl.BlockSpec((3, tc), lambda j: (0, j)), pl.BlockSpec((1, tc), lambda j: (0, j))],
        out_specs=[pl.BlockSpec((s, tc), lambda j: (0, j)), pl.BlockSpec((s, tc), lambda j: (0, j)),
                   pl.BlockSpec((3, tc), lambda j: (0, j)), pl.BlockSpec((1, tc), lambda j: (0, j))],
        compiler_params=_params("parallel"),
    )(up, up, da, conv_w, conv_b)


def _adamw(w, g, m, v):
    m = ADAM_B1 * m + (1.0 - ADAM_B1) * g
    v = ADAM_B2 * v + (1.0 - ADAM_B2) * jnp.square(g)
    m_hat = m / (1.0 - ADAM_B1 ** ADAM_STEP)
    v_hat = v / (1.0 - ADAM_B2 ** ADAM_STEP)
    delta = -ADAM_LR * (m_hat / (jnp.sqrt(v_hat) + ADAM_EPS) + ADAM_WD * w)
    return delta, m, v


def _update_shard(name, layer, w, m, v, own, received, chip, previous=None):
    _, rows, cols = w.shape
    tr = _tile(rows, 256)

    def body(chip_ref, w_ref, m_ref, v_ref, own_ref, rec_ref, *rest):
        g_ref, d_ref, nm_ref, nv_ref = rest[-4:]
        g = (own_ref[...].astype(F32) + rec_ref[0].astype(F32) + rec_ref[1].astype(F32)
             + rec_ref[2].astype(F32))
        delta, new_m, new_v = _adamw(w_ref[...], g, m_ref[...], v_ref[...])
        g_ref[...] = g
        d_ref[...] = delta
        nm_ref[...] = new_m
        nv_ref[...] = new_v

    layer_spec = pl.BlockSpec((None, tr, cols), lambda i, chip_ref: (layer, i, 0))
    in_specs = [layer_spec] * 3 + [pl.BlockSpec((None, tr, cols), lambda i, chip_ref: (chip_ref[0], i, 0)),
                                   pl.BlockSpec((3, tr, cols), lambda i, chip_ref: (0, i, 0))]
    args = [chip, w, m, v, own, received]
    aliases = {}
    if previous is not None:
        in_specs += [ANY] * 4
        aliases = {len(args) + k: k for k in range(4)}
        args += list(previous)
    return pl.pallas_call(
        body, name=name, out_shape=[jax.ShapeDtypeStruct(w.shape, F32)] * 4,
        grid_spec=pltpu.PrefetchScalarGridSpec(
            num_scalar_prefetch=1, grid=(rows // tr,), in_specs=in_specs, out_specs=[layer_spec] * 4),
        input_output_aliases=aliases, compiler_params=_params("parallel"),
    )(*args)


def _sum_devices(name, gathered):
    _, rows, cols = gathered.shape

    def body(g_ref, o_ref):
        total = g_ref[0]
        for d in range(1, N_DEV):
            total = total + g_ref[d]
        o_ref[...] = total

    return pl.pallas_call(
        body, name=name, out_shape=jax.ShapeDtypeStruct((rows, cols), F32),
        in_specs=[pl.BlockSpec((N_DEV, rows, cols), lambda: (0, 0, 0))],
        out_specs=pl.BlockSpec((rows, cols), lambda: (0, 0)),
        compiler_params=pltpu.CompilerParams(vmem_limit_bytes=VMEM_LIMIT_V7X),
    )(gathered)


def _update_small(name, w, g, m, v):
    def body(w_ref, g_ref, m_ref, v_ref, d_ref, nm_ref, nv_ref):
        delta, new_m, new_v = _adamw(w_ref[...], g_ref[...], m_ref[...], v_ref[...])
        d_ref[...] = delta
        nm_ref[...] = new_m
        nv_ref[...] = new_v

    spec = pl.BlockSpec(w.shape, lambda: (0, 0))
    return pl.pallas_call(
        body, name=name, out_shape=[jax.ShapeDtypeStruct(w.shape, F32)] * 3,
        in_specs=[spec] * 4, out_specs=[spec] * 3,
        compiler_params=pltpu.CompilerParams(vmem_limit_bytes=VMEM_LIMIT_V7X),
    )(w, g, m, v)


def _pack(arrays):
    flat = jnp.concatenate([a.reshape(-1) for a in arrays])
    pad = (-flat.shape[0]) % (8 * BLK)
    return jnp.pad(flat, (0, pad)).reshape(-1, BLK)


def _unpack(packed, shapes):
    flat = packed.reshape(-1)
    out, at = [], 0
    for shape in shapes:
        size = 1
        for d in shape:
            size *= d
        out.append(flat[at:at + size].reshape(shape))
        at += size
    return out


SHARDED = ("w_in", "w_mem_kv", "w_gate", "w_br_sb", "w_br_gm", "w_br_xa", "w_out", "w_up", "w_down")
SMALL = ("g_mix_pre", "g_vnorm", "w_s", "b_s", "g_mem", "b_gate", "g_mix_post", "g_ffn_pre", "conv_w",
         "conv_b", "g_ffn_post")
WEIGHTS = ("g_mix_pre", "w_in", "g_vnorm", "w_s", "b_s", "g_mem", "w_mem_kv", "w_gate", "b_gate", "w_br_sb",
           "w_br_gm", "w_br_xa", "w_out", "g_mix_post", "g_ffn_pre", "w_up", "conv_w", "conv_b", "w_down",
           "g_ffn_post")


ROW_SHARDED = ("w_mem_kv", "w_out", "w_down")
GATHER_GROUPS = (("w_in",), ("w_mem_kv", "w_gate"), ("w_br_sb", "w_br_gm", "w_br_xa", "w_out"), ("w_up",),
                 ("w_down",))
REDUCE_GROUPS = (("w_down", "w_up"), ("w_out", "w_br_sb", "w_br_gm", "w_br_xa", "w_mem_kv", "w_gate"), ("w_in",))


def _cast_into_place(name, w, layer, me):
    _, rows, cols = w.shape
    tr = _tile(rows, max(8, STREAM_BLOCK_BYTES // (4 * cols)))

    def body(me_ref, w_ref, o_ref):
        o_ref[...] = w_ref[...].astype(o_ref.dtype)

    return pl.pallas_call(
        body, name=name, out_shape=jax.ShapeDtypeStruct((N_DEV, rows, cols), BF16),
        grid_spec=pltpu.PrefetchScalarGridSpec(
            num_scalar_prefetch=1, grid=(rows // tr,),
            in_specs=[pl.BlockSpec((None, tr, cols), lambda i, me_ref: (layer, i, 0))],
            out_specs=pl.BlockSpec((None, tr, cols), lambda i, me_ref: (me_ref[0], i, 0))),
        compiler_params=_params("parallel"),
    )(me, w)


class _Gather:
    def __init__(self, tag, names, places, after):
        self.tag, self.names, self.n = tag, names, len(places)
        self.sems, self.places, self.token = _split_start(
            f"gather_start_{tag}", places, _gather_first_copies(self.n), after)

    def relay(self, after):
        self.sems, self.places = _split_wait(
            f"gather_relay_{self.tag}", self.places, self.sems, _gather_first_copies(self.n), after,
            then=_gather_relay_copies(self.n))

    def finish(self, after):
        places = _split_wait(f"gather_finish_{self.tag}", self.places, self.sems,
                             _gather_relay_copies(self.n), after)
        full = dict(zip(self.names, places))
        for name in self.names:
            if name in ROW_SHARDED:
                full[name] = full[name].reshape(-1, full[name].shape[-1])
        return full


class _Reduce:
    def __init__(self, tag, names, partials):
        self.tag, self.names, self.n = tag, names, len(partials)
        blocks = [p.reshape((N_CHIP, 2, -1, p.shape[-1])) for p in partials]
        lands = [lax.empty((N_CHIP,) + b.shape[2:], b.dtype) for b in blocks]
        self.sems, self.buffers, self.token = _split_start(
            f"pair_start_{tag}", blocks + lands, _pair_copies(self.n))

    def middle(self, after, core):
        n = self.n
        got = _split_wait(f"pair_finish_{self.tag}", self.buffers, self.sems, _pair_copies(n), after)
        sums = [_pair_sum(f"pair_sum_{name}{self.tag}", got[a], got[n + a], core)
                for a, name in enumerate(self.names)]
        lands = [lax.empty((3,) + s.shape[1:], s.dtype) for s in sums]
        self.sems, self.buffers, self.token = _split_start(
            f"chip_start_{self.tag}", sums + lands, _chip_copies(n))

    def finish(self, after):
        n = self.n
        got = _split_wait(f"chip_finish_{self.tag}", self.buffers, self.sems, _chip_copies(n), after)
        return [(name, got[a], got[n + a]) for a, name in enumerate(self.names)]


def kernel(x, mem, g_mix_pre, w_in, g_vnorm, w_s, b_s, g_mem, w_mem_kv, w_gate, b_gate, w_br_sb, w_br_gm, w_br_xa, w_out, g_mix_post, g_ffn_pre, w_up, conv_w, conv_b, w_down, g_ffn_post, loss_target, m_g_mix_pre, m_w_in, m_g_vnorm, m_w_s, m_b_s, m_g_mem, m_w_mem_kv, m_w_gate, m_b_gate, m_w_br_sb, m_w_br_gm, m_w_br_xa, m_w_out, m_g_mix_post, m_g_ffn_pre, m_w_up, m_conv_w, m_conv_b, m_w_down, m_g_ffn_post, v_g_mix_pre, v_w_in, v_g_vnorm, v_w_s, v_b_s, v_g_mem, v_w_mem_kv, v_w_gate, v_b_gate, v_w_br_sb, v_w_br_gm, v_w_br_xa, v_w_out, v_g_mix_post, v_g_ffn_pre, v_w_up, v_conv_w, v_conv_b, v_w_down, v_g_ffn_post):
    p = dict(g_mix_pre=g_mix_pre, w_in=w_in, g_vnorm=g_vnorm, w_s=w_s, b_s=b_s, g_mem=g_mem, w_mem_kv=w_mem_kv,
             w_gate=w_gate, b_gate=b_gate, w_br_sb=w_br_sb, w_br_gm=w_br_gm, w_br_xa=w_br_xa, w_out=w_out,
             g_mix_post=g_mix_post, g_ffn_pre=g_ffn_pre, w_up=w_up, conv_w=conv_w, conv_b=conv_b, w_down=w_down,
             g_ffn_post=g_ffn_post)
    mom = dict(g_mix_pre=m_g_mix_pre, w_in=m_w_in, g_vnorm=m_g_vnorm, w_s=m_w_s, b_s=m_b_s, g_mem=m_g_mem,
               w_mem_kv=m_w_mem_kv, w_gate=m_w_gate, b_gate=m_b_gate, w_br_sb=m_w_br_sb, w_br_gm=m_w_br_gm,
               w_br_xa=m_w_br_xa, w_out=m_w_out, g_mix_post=m_g_mix_post, g_ffn_pre=m_g_ffn_pre, w_up=m_w_up,
               conv_w=m_conv_w, conv_b=m_conv_b, w_down=m_w_down, g_ffn_post=m_g_ffn_post)
    var = dict(g_mix_pre=v_g_mix_pre, w_in=v_w_in, g_vnorm=v_g_vnorm, w_s=v_w_s, b_s=v_b_s, g_mem=v_g_mem,
               w_mem_kv=v_w_mem_kv, w_gate=v_w_gate, b_gate=v_b_gate, w_br_sb=v_w_br_sb, w_br_gm=v_w_br_gm,
               w_br_xa=v_w_br_xa, w_out=v_w_out, g_mix_post=v_g_mix_post, g_ffn_pre=v_g_ffn_pre, w_up=v_w_up,
               conv_w=v_conv_w, conv_b=v_conv_b, w_down=v_w_down, g_ffn_post=v_g_ffn_post)
    depth = w_in.shape[0]
    x = x[0]
    mem = mem[0]
    target = loss_target[0]
    s, d = x.shape
    width = d // 2
    heads = width // BLK
    cx, cy, cc = lax.axis_index("x"), lax.axis_index("y"), lax.axis_index("c")
    core = cc.astype(jnp.int32).reshape(1)
    chip = (2 * cx + cy).astype(jnp.int32).reshape(1)
    me = 4 * cx + 2 * cy + cc
    me_index = me.astype(jnp.int32).reshape(1)

    conv_all = _all_gather("gather_conv_w", [conv_w])[0]
    conv_w_full = jnp.transpose(conv_all, (1, 2, 0, 3)).reshape(depth, 3, -1)
    gathers = {}
    token = conv_all
    for l in range(depth):
        for gi, names in enumerate(GATHER_GROUPS):
            places = [_cast_into_place(f"cast_{n}{l}", p[n], l, me_index) for n in names]
            gathers[l, gi] = _Gather(f"{l}{gi}", names, places, token)
            token = gathers[l, gi].token

    kept, gathered = [], []
    h = _norm_fwd("pre_norm0", x, g_mix_pre[0][None, :], after=token)
    gathers[0, 0].relay(h)
    for l in range(depth):
        row = lambda name: p[name][l][None, :]
        first, second, third, fourth, fifth = (gathers[l, gi] for gi in range(len(GATHER_GROUPS)))
        w = first.finish(h)
        proj, proj16 = _mm_nn_cs(f"proj{l}", h, w["w_in"], (F32, BF16), tm=1024)
        o_sb, sb_tot = _sb_fwd(f"sb_fwd{l}", proj16, heads)
        b_s3 = b_s[l][:, :, None]
        o_gm = _gm_fwd(f"gm_fwd{l}", proj, row("g_vnorm"), w_s[l], b_s3, width)
        second.relay(o_gm)
        mn = _norm_fwd(f"mem_norm{l}", mem, row("g_mem"))
        w.update(second.finish(mn))
        third.relay(mn)
        mem_kv = _mm_nn(f"mem_kv{l}", mn, w["w_mem_kv"], (BF16,))
        o_xa = _xa_fwd(f"xa_fwd{l}", proj16, mem_kv, width)
        zg = _mm_nn_cs(f"gates{l}", h, w["w_gate"], tm=1024)
        w.update(third.finish(zg))
        ts = [_mm_nn_cs(f"branch_{n}{l}", o, w[f"w_br_{n}"], (BF16,), tm=2048)
              for n, o in (("sb", o_sb), ("gm", o_gm), ("xa", o_xa))]
        merged = _merge_fwd(f"merge{l}", zg, ts, row("b_gate"))
        fourth.relay(merged)
        y1 = _mm_nn(f"out{l}", merged, w["w_out"], tm=1024, tn=1024)
        x1, h2 = _residual_norm_fwd(f"mix_post{l}", x, y1, row("g_mix_post"), row("g_ffn_pre"))
        w.update(fourth.finish(h2))
        up = _mm_nn_cs(f"up{l}", h2, w["w_up"], tm=1024)
        fifth.relay(up)
        act = _conv_fwd(f"conv_fwd{l}", up, conv_w_full[l], row("conv_b"))
        w.update(fifth.finish(act))
        if l + 1 < depth:
            gathers[l + 1, 0].relay(act)
        y2 = _mm_nn(f"down{l}", act, w["w_down"], tm=1024)
        kept.append(dict(x=x, h=h, proj=proj, proj16=proj16, zg=zg, mn=mn, mem_kv=mem_kv, o_sb=o_sb, o_gm=o_gm,
                         o_xa=o_xa, ts=ts, merged=merged, y1=y1, x1=x1, h2=h2, up=up, act=act, y2=y2, b_s=b_s3,
                         sb_tot=sb_tot))
        gathered.append(w)
        if l + 1 < depth:
            x, h = _residual_norm_fwd(f"ffn_post{l}", x1, y2, g_ffn_post[l][None, :], g_mix_pre[l + 1][None, :])

    top = kept[-1]
    dx, dy2, loss_part, dg_ffn_post = _loss_bwd("loss", top["x1"], top["y2"], g_ffn_post[depth - 1][None, :], target)
    loss = lax.psum(loss_part[0, 0], ("x", "y", "c"))
    small_grads = [dict() for _ in range(depth)]
    small_grads[depth - 1]["g_ffn_post"] = dg_ffn_post
    small_early = [(ll, n) for ll in range(depth) for n in SMALL if (ll, n) != (0, "g_mix_pre")]
    outs = {n: None for n in SHARDED}
    in_flight = []

    def reduce_finish(entry, after):
        layer = int(entry[0].tag[0])
        for name, own, received in entry[0].finish(after):
            outs[name] = _update_shard(f"update_{name}{layer}", layer, p[name], mom[name], var[name],
                                       own, received, chip, outs[name])
        in_flight.remove(entry)

    def reduce_start(new):
        for entry in in_flight:
            entry[2] += entry[1] == 2
        in_flight.append([new, 1, 0])
        for entry in [e for e in in_flight if e[2] >= 2]:
            reduce_finish(entry, new.token)
        return new.token

    def reduce_middle(after):
        for entry in [e for e in in_flight if e[1] == 1]:
            entry[0].middle(after, core)
            after = entry[0].token
            entry[1] = 2
        return after

    behind = None

    for l in reversed(range(depth)):
        k = kept[l]
        w = gathered[l]
        sg = small_grads[l]
        row = lambda name: p[name][l][None, :]
        partials = {}
        group = lambda gi: _Reduce(f"{l}{gi}", REDUCE_GROUPS[gi], [partials[n] for n in REDUCE_GROUPS[gi]])
        partials["w_down"] = _mm_tn(f"d_w_down{l}", k["act"], dy2, tn=2048, after=behind)
        dact = _mm_nt(f"d_act{l}", dy2, w["w_down"], tm=2048)
        dgate, dval, sg["conv_w"], sg["conv_b"] = _conv_bwd(f"conv_bwd{l}", k["up"], dact, conv_w_full[l],
                                                             row("conv_b"))
        dup = (dgate, dval)
        partials["w_up"] = _mm_tn_cs(f"d_w_up{l}", k["h2"], dup, tk=1024)
        behind = reduce_start(group(0))
        dh2 = _mm_nt_cs(f"d_h2{l}", dup, w["w_up"], per_step=2, after=behind)
        dx1, dy1, sg["g_ffn_pre"], sg["g_mix_post"] = _mid_bwd(f"mid_bwd{l}", dx, dh2, k["x1"], k["y1"],
                                                                row("g_ffn_pre"), row("g_mix_post"))
        behind = reduce_middle(dy1)
        partials["w_out"] = _mm_tn(f"d_w_out{l}", k["merged"], dy1, tk=1024, after=behind)
        dmerged = _mm_nt(f"d_merged{l}", dy1, w["w_out"], tm=1024, tn=1024)
        dt_sb, dt_gm, dt_xa, dzg, sg["b_gate"] = _merge_bwd(f"merge_bwd{l}", dmerged, k["zg"], k["ts"], row("b_gate"))
        do = {}
        for n, dt in (("sb", dt_sb), ("gm", dt_gm), ("xa", dt_xa)):
            partials[f"w_br_{n}"] = _mm_tn_cs(f"d_w_br_{n}{l}", k[f"o_{n}"], dt, tk=1024)
            do[n] = _mm_nt_cs_whole(f"d_o_{n}{l}", dt, w[f"w_br_{n}"], F32 if n == "gm" else BF16)
        dq_xa, dk_mem, dv_mem = _xa_bwd(f"xa_bwd{l}", k["proj16"], k["mem_kv"], do["xa"], width)
        dmem_kv = jnp.concatenate([dk_mem, dv_mem], axis=1)
        partials["w_mem_kv"] = _mm_tn(f"d_w_mem_kv{l}", k["mn"], dmem_kv)
        partials["w_gate"] = _mm_tn_cs(f"d_w_gate{l}", k["h"], dzg, tk=1024)
        behind = reduce_start(group(1))
        dh_gate = _mm_nt_cs(f"d_h_gate{l}", dzg, w["w_gate"], per_step=4, after=behind)
        behind = reduce_middle(dh_gate)
        dmn = _mm_nt(f"d_mn{l}", dmem_kv, w["w_mem_kv"], after=behind)
        sg["g_mem"] = _mem_norm_bwd(f"mem_norm_bwd{l}", dmn, mem, row("g_mem"))
        du, dvg, sg["w_s"], db_s, sg["g_vnorm"] = _gm_bwd(f"gm_bwd{l}", k["proj"], do["gm"], row("g_vnorm"),
                                                          p["w_s"][l], k["b_s"], width)
        sg["b_s"] = db_s[:, :, 0]
        small_token = None
        if l == 0:
            early_shapes = [small_grads[ll][n].shape for ll, n in small_early]
            packed = _pack([small_grads[ll][n] for ll, n in small_early])
            place = lax.dynamic_update_slice(lax.empty((N_DEV,) + packed.shape, F32), packed[None], (me, 0, 0))
            small_gather = _Gather("small", ("small",), [place], du)
            small_token = small_gather.token
        dq, dk, dv = _sb_bwd(f"sb_bwd{l}", k["proj16"], do["sb"], k["sb_tot"], heads, after=small_token)
        if l == 0:
            small_gather.relay(dq)
        dproj = jnp.concatenate([dq, dk, dv, du, dvg, dq_xa], axis=1)
        partials["w_in"] = _mm_tn_cs(f"d_w_in{l}", k["h"], dproj, tk=1024)
        behind = reduce_start(group(2))
        dh_in = _mm_nt_cs(f"d_h_in{l}", dproj, w["w_in"], per_step=4, after=behind)
        if l > 0:
            below = kept[l - 1]
            dx, dy2, sg["g_mix_pre"], small_grads[l - 1]["g_ffn_post"] = _bottom_bwd(
                f"bottom_bwd{l}", dx1, dh_in, dh_gate, k["x"], row("g_mix_pre"), below["y2"],
                p["g_ffn_post"][l - 1][None, :])
        else:
            dx, sg["g_mix_pre"] = _bottom_bwd(f"bottom_bwd{l}", dx1, dh_in, dh_gate, k["x"], row("g_mix_pre"))
        if l > 0:
            behind = reduce_middle(dx)

    late_gathered = _all_gather("gather_small", [_pack([small_grads[0]["g_mix_pre"]])])[0]
    behind = reduce_middle(late_gathered)
    for entry in list(in_flight):
        reduce_finish(entry, behind)

    early_total = _sum_devices("sum_small", small_gather.finish(behind)["small"])
    late_total = _sum_devices("sum_small_late", late_gathered)
    summed = dict(zip(small_early, _unpack(early_total, early_shapes)))
    summed[0, "g_mix_pre"] = _unpack(late_total, [small_grads[0]["g_mix_pre"].shape])[0]
    grads = {}
    for n in SMALL:
        g = jnp.stack([summed[l, n] for l in range(depth)]).reshape((depth,) + p[n].shape[1:]
                                                                    if n != "conv_w" else (depth, 3, -1))
        if n == "conv_w":
            per = conv_w.shape[2]
            g = lax.dynamic_slice_in_dim(g, me * per, per, axis=2)
        grads[n] = g
    delta, new_m, new_v = _update_small(
        "update_small", _pack([p[n] for n in SMALL]), _pack([grads[n] for n in SMALL]),
        _pack([mom[n] for n in SMALL]), _pack([var[n] for n in SMALL]))
    shapes = [p[n].shape for n in SMALL]
    for n, dl, nm, nv in zip(SMALL, _unpack(delta, shapes), _unpack(new_m, shapes), _unpack(new_v, shapes)):
        outs[n] = (grads[n], dl, nm, nv)

    result = [loss, dx[None]]
    for k in range(4):
        result += [outs[n][k] for n in WEIGHTS]
    return tuple(result)
```

```python
import functools

import jax
import jax.numpy as jnp
from jax import lax
from jax.experimental import pallas as pl
from jax.experimental.pallas import tpu as pltpu

F32 = jnp.float32
BF16 = jnp.bfloat16
N_DEV = 8
N_CHIP = 4
EPS = 1e-6
CHUNK = 64
BLK = 128
XA_HEADS = 4
SB_HEADS_PER_STEP = 8
ADAM_LR = 0.001
ADAM_B1 = 0.9
ADAM_B2 = 0.999
ADAM_EPS = 1e-08
ADAM_WD = 0.01
ADAM_STEP = 10
VMEM_LIMIT_V7X = 56 * 1024 * 1024
STREAM_BLOCK_BYTES = 4 * 1024 * 1024
MESH = pl.DeviceIdType.MESH
ANY = pl.BlockSpec(memory_space=pl.ANY)


def _params(*sem):
    return pltpu.CompilerParams(dimension_semantics=sem, vmem_limit_bytes=VMEM_LIMIT_V7X)


def _tile(n, pref, mult=8):
    best = None
    for t in range(mult, min(n, pref) + 1, mult):
        if n % t == 0:
            best = t
    return best if best is not None else n


def _bf(x):
    return x if x.dtype == BF16 else x.astype(BF16)


def _dot(a, b, ca, cb):
    return lax.dot_general(_bf(a), _bf(b), (((ca,), (cb,)), ((), ())), preferred_element_type=F32)


def _rmsnorm(x, g):
    return (x * lax.rsqrt(jnp.mean(x * x, axis=-1, keepdims=True) + EPS)) * g


def _position():
    x, y, c = lax.axis_index("x"), lax.axis_index("y"), lax.axis_index("c")
    return x, y, c


def _all_gather(name, shards):
    n = len(shards)

    def body(*refs):
        ins, outs = refs[:n], refs[n:2 * n]
        send_sems, recv_sems, local_sems = refs[2 * n:]
        x, y, c = _position()
        me = 4 * x + 2 * y + c
        sibling = (x, y, 1 - c)
        chips = [(1 - x, y), (x, 1 - y), (1 - x, 1 - y)]

        def copy(a, k, block, to, src=None):
            dst = outs[a].at[block]
            return pltpu.make_async_remote_copy(
                src_ref=dst if src is None else src, dst_ref=dst, send_sem=send_sems.at[a, k],
                recv_sem=recv_sems.at[a, k], device_id=to, device_id_type=MESH)

        local = [pltpu.make_async_copy(ins[a], outs[a].at[me], local_sems.at[a]) for a in range(n)]
        for cp in local:
            cp.start()
        first = []
        for a in range(n):
            first.append(copy(a, 0, me, sibling, src=ins[a]))
            for j, chip in enumerate(chips):
                first.append(copy(a, 1 + j, me, (*chip, c), src=ins[a]))
        for cp in first:
            cp.start()
        passed = []
        for a in range(n):
            for j, (px, py) in enumerate(chips):
                block = 4 * px + 2 * py + c
                copy(a, 1 + j, block, sibling).wait_recv()
                forward = copy(a, 4 + j, block, sibling)
                forward.start()
                passed.append(forward)
        for a in range(n):
            copy(a, 0, 4 * x + 2 * y + (1 - c), sibling).wait_recv()
            for j, (px, py) in enumerate(chips):
                copy(a, 4 + j, 4 * px + 2 * py + (1 - c), sibling).wait_recv()
        for cp in first + passed:
            cp.wait_send()
        for cp in local:
            cp.wait()

    return pl.pallas_call(
        body, name=name,
        out_shape=[jax.ShapeDtypeStruct((N_DEV,) + s.shape, s.dtype) for s in shards],
        in_specs=[ANY] * n, out_specs=[ANY] * n,
        scratch_shapes=[pltpu.SemaphoreType.DMA((n, 7)), pltpu.SemaphoreType.DMA((n, 7)),
                        pltpu.SemaphoreType.DMA((n,))],
    )(*shards)


HBM = pl.BlockSpec(memory_space=pltpu.HBM)
SEM = pl.BlockSpec(memory_space=pltpu.SEMAPHORE)
DATAFLOW = pltpu.SideEffectType.DATAFLOW_SIDE_EFFECTING


def _remote(src, dst, send, recv, k, peer):
    return pltpu.make_async_remote_copy(src_ref=src, dst_ref=dst, send_sem=send.at[k], recv_sem=recv.at[k],
                                        device_id=peer, device_id_type=MESH)


def _gather_first_copies(n):
    def build(refs, send, recv):
        x, y, c = _position()
        me = 4 * x + 2 * y + c
        peers = [(x, y, 1 - c), (1 - x, y, c), (x, 1 - y, c), (1 - x, 1 - y, c)]
        return [_remote(refs[a].at[me], refs[a].at[me], send, recv, 4 * a + k, peer)
                for a in range(n) for k, peer in enumerate(peers)]
    return build, 4 * n


def _gather_relay_copies(n):
    def build(refs, send, recv):
        x, y, c = _position()
        blocks = [4 * px + 2 * py + c for px, py in ((1 - x, y), (x, 1 - y), (1 - x, 1 - y))]
        return [_remote(refs[a].at[blk], refs[a].at[blk], send, recv, 3 * a + j, (x, y, 1 - c))
                for a in range(n) for j, blk in enumerate(blocks)]
    return build, 3 * n


def _pair_copies(n):
    def build(refs, send, recv):
        x, y, c = _position()
        return [_remote(refs[a].at[k, 1 - c], refs[n + a].at[k], send, recv, N_CHIP * a + k, (x, y, 1 - c))
                for a in range(n) for k in range(N_CHIP)]
    return build, N_CHIP * n


def _chip_copies(n):
    def build(refs, send, recv):
        x, y, c = _position()
        chips = ((1 - x, y), (x, 1 - y), (1 - x, 1 - y))
        return [_remote(refs[a].at[2 * px + py], refs[n + a].at[j], send, recv, 3 * a + j, (px, py, c))
                for a in range(n) for j, (px, py) in enumerate(chips)]
    return build, 3 * n


def _split_start(name, buffers, copies, after=None):
    build, n_copies = copies
    nb = len(buffers)
    extra = [] if after is None else [after]

    def body(*refs):
        at = nb + len(extra)
        for cp in build(refs[:nb], refs[at], refs[at + 1]):
            cp.start()
        token = refs[at + 2 + nb]
        token[...] = jnp.zeros_like(token)

    outs = pl.pallas_call(
        body, name=name,
        out_shape=[pltpu.SemaphoreType.DMA((n_copies,)), pltpu.SemaphoreType.DMA((n_copies,))]
        + [pltpu.HBM(b.shape, b.dtype) for b in buffers] + [jax.ShapeDtypeStruct((8, BLK), F32)],
        in_specs=[HBM] * nb + [ANY] * len(extra),
        out_specs=[SEM, SEM] + [HBM] * nb + [pl.BlockSpec(memory_space=pltpu.VMEM)],
        input_output_aliases={i: 2 + i for i in range(nb)},
        compiler_params=pltpu.CompilerParams(has_side_effects=DATAFLOW),
    )(*[pltpu.with_memory_space_constraint(b, pltpu.HBM) for b in buffers], *extra)
    return (outs[0], outs[1]), list(outs[2:2 + nb]), outs[-1]


def _split_wait(name, buffers, sems, copies, after, then=None):
    build, _ = copies
    nb = len(buffers)
    n_next = 0 if then is None else then[1]
    first_out = nb + 3

    def body(*refs):
        for cp in build(refs[:nb], refs[nb], refs[nb + 1]):
            cp.wait()
        if then is not None:
            for cp in then[0](refs[:nb], refs[first_out], refs[first_out + 1]):
                cp.start()

    sem_shapes = [] if then is None else [pltpu.SemaphoreType.DMA((n_next,))] * 2
    outs = pl.pallas_call(
        body, name=name,
        out_shape=sem_shapes + [pltpu.HBM(b.shape, b.dtype) for b in buffers],
        in_specs=[HBM] * nb + [SEM, SEM, ANY],
        out_specs=[SEM] * len(sem_shapes) + [HBM] * nb,
        input_output_aliases={i: len(sem_shapes) + i for i in range(nb)},
        compiler_params=pltpu.CompilerParams(has_side_effects=DATAFLOW),
    )(*buffers, sems[0], sems[1], after)
    if then is None:
        return list(outs)
    return (outs[0], outs[1]), list(outs[2:])


def _pair_sum(name, partial, received, core):
    _, _, rows, cols = partial.shape
    tr = _tile(rows, max(8, STREAM_BLOCK_BYTES // (2 * cols)))

    def body(core_ref, p_ref, r_ref, o_ref):
        o_ref[...] = (p_ref[...].astype(F32) + r_ref[...].astype(F32)).astype(o_ref.dtype)

    return pl.pallas_call(
        body, name=name, out_shape=jax.ShapeDtypeStruct((N_CHIP, rows, cols), BF16),
        grid_spec=pltpu.PrefetchScalarGridSpec(
            num_scalar_prefetch=1, grid=(N_CHIP, rows // tr),
            in_specs=[pl.BlockSpec((None, None, tr, cols), lambda k, i, core: (k, core[0], i, 0)),
                      pl.BlockSpec((None, tr, cols), lambda k, i, core: (k, i, 0))],
            out_specs=pl.BlockSpec((None, tr, cols), lambda k, i, core: (k, i, 0))),
        compiler_params=_params("parallel", "parallel"),
    )(core, partial, received)


def _mm(name, a, b, out_shape, out_dtypes, grid, a_spec, b_spec, o_spec, ca, cb, k_steps=1, after=None):
    n_out = len(out_dtypes)
    extra = [] if after is None else [after]

    def body(a_ref, b_ref, *rest):
        rest = rest[len(extra):]
        o_refs = rest[:n_out]
        part = _dot(a_ref[...], b_ref[...], ca, cb)
        if k_steps == 1:
            for o in o_refs:
                o[...] = part.astype(o.dtype)
            return
        acc = rest[n_out]
        k = pl.program_id(len(grid) - 1)

        @pl.when(k == 0)
        def _():
            acc[...] = part

        @pl.when(k > 0)
        def _():
            acc[...] += part

        @pl.when(k == k_steps - 1)
        def _():
            for o in o_refs:
                o[...] = acc[...].astype(o.dtype)

    o_block = tuple(d for d in o_spec.block_shape if d is not None)
    sem = ("parallel",) * (len(grid) - 1) + (("arbitrary",) if k_steps > 1 else ("parallel",))
    out = pl.pallas_call(
        body, name=name, grid=grid,
        out_shape=[jax.ShapeDtypeStruct(out_shape, d) for d in out_dtypes],
        in_specs=[a_spec, b_spec] + [ANY] * len(extra), out_specs=[o_spec] * n_out,
        scratch_shapes=[pltpu.VMEM(o_block, F32)] if k_steps > 1 else [],
        compiler_params=_params(*sem),
    )(a, b, *extra)
    return out if n_out > 1 else out[0]


def _mm_nn_cs(name, a, w, out_dtypes=(F32,), tm=512, **kw):
    m, k = a.shape
    _, _, ns = w.shape
    tm = _tile(m, tm)
    return _mm(name, a, w, (m, N_DEV * ns), out_dtypes, (m // tm, N_DEV),
               pl.BlockSpec((tm, k), lambda i, j: (i, 0)),
               pl.BlockSpec((None, k, ns), lambda i, j: (j, 0, 0)),
               pl.BlockSpec((tm, ns), lambda i, j: (i, j)), 1, 0, **kw)


def _mm_nn(name, a, w, out_dtypes=(F32,), tm=512, tn=512, **kw):
    m, k = a.shape
    _, n = w.shape
    tm, tn = _tile(m, tm), _tile(n, tn, 128)
    return _mm(name, a, w, (m, n), out_dtypes, (m // tm, n // tn),
               pl.BlockSpec((tm, k), lambda i, j: (i, 0)),
               pl.BlockSpec((k, tn), lambda i, j: (0, j)),
               pl.BlockSpec((tm, tn), lambda i, j: (i, j)), 1, 0, **kw)


def _mm_nt_cs(name, a, w, tm=512, per_step=1, after=None):
    _, k, ns = w.shape
    parts = list(a) if isinstance(a, (tuple, list)) else [a]
    m = parts[0].shape[0]
    tm = _tile(m, tm)
    steps = N_DEV // per_step
    per_part = steps // len(parts)
    extra = [] if after is None else [after]

    def body(*refs):
        a_refs, w_ref, o_ref = refs[:len(parts)], refs[len(parts)], refs[len(parts) + 1 + len(extra)]
        j = pl.program_id(1)

        def contract(a_ref):
            part = _dot(a_ref[:, 0:ns], w_ref[0], 1, 1)
            for u in range(1, per_step):
                part = part + _dot(a_ref[:, u * ns:(u + 1) * ns], w_ref[u], 1, 1)

            @pl.when(j == 0)
            def _():
                o_ref[...] = part

            @pl.when(j > 0)
            def _():
                o_ref[...] += part

        for n, a_ref in enumerate(a_refs):
            pl.when(jnp.logical_and(j >= n * per_part, j < (n + 1) * per_part))(functools.partial(contract, a_ref))

    a_specs = [pl.BlockSpec((tm, per_step * ns),
                            functools.partial(lambda i, j, n: (i, jnp.clip(j - n * per_part, 0, per_part - 1)), n=n))
               for n in range(len(parts))]
    return pl.pallas_call(
        body, name=name, grid=(m // tm, steps), out_shape=jax.ShapeDtypeStruct((m, k), F32),
        in_specs=a_specs + [pl.BlockSpec((per_step, k, ns), lambda i, j: (j, 0, 0))] + [ANY] * len(extra),
        out_specs=pl.BlockSpec((tm, k), lambda i, j: (i, 0)),
        compiler_params=_params("parallel", "arbitrary"),
    )(*parts, w, *extra)


def _mm_nt_cs_whole(name, a, w, out_dtype, tm=512):
    m, _ = a.shape
    _, k, ns = w.shape
    tm = _tile(m, tm)

    def body(a_ref, w_ref, o_ref):
        acc = _dot(a_ref[:, 0:ns], w_ref[0], 1, 1)
        for j in range(1, N_DEV):
            acc = acc + _dot(a_ref[:, j * ns:(j + 1) * ns], w_ref[j], 1, 1)
        o_ref[...] = acc.astype(o_ref.dtype)

    return pl.pallas_call(
        body, name=name, grid=(m // tm,), out_shape=jax.ShapeDtypeStruct((m, k), out_dtype),
        in_specs=[pl.BlockSpec((tm, N_DEV * ns), lambda i: (i, 0)),
                  pl.BlockSpec((N_DEV, k, ns), lambda i: (0, 0, 0))],
        out_specs=pl.BlockSpec((tm, k), lambda i: (i, 0)),
        compiler_params=_params("parallel"),
    )(a, w)


def _mm_nt(name, a, w, out_dtypes=(F32,), tm=512, tn=512, **kw):
    m, k = a.shape
    n, _ = w.shape
    tm, tn = _tile(m, tm), _tile(n, tn, 128)
    return _mm(name, a, w, (m, n), out_dtypes, (m // tm, n // tn),
               pl.BlockSpec((tm, k), lambda i, j: (i, 0)),
               pl.BlockSpec((tn, k), lambda i, j: (j, 0)),
               pl.BlockSpec((tm, tn), lambda i, j: (i, j)), 1, 1, **kw)


def _mm_tn_cs(name, a, g, tk=512, after=None):
    s, k = a.shape
    parts = list(g) if isinstance(g, (tuple, list)) else [g]
    per_part = N_DEV // len(parts)
    ns = parts[0].shape[1] // per_part
    tk = _tile(k, tk, 128)
    extra = [] if after is None else [after]

    def body(a_ref, *rest):
        g_refs, o_ref = rest[:len(parts)], rest[len(parts) + len(extra)]
        j = pl.program_id(1)

        def shard(g_ref):
            o_ref[...] = _dot(a_ref[...], g_ref[...], 0, 0).astype(o_ref.dtype)

        for n, g_ref in enumerate(g_refs):
            pl.when(jnp.logical_and(j >= n * per_part, j < (n + 1) * per_part))(functools.partial(shard, g_ref))

    g_specs = [pl.BlockSpec((s, ns),
                            functools.partial(lambda i, j, n: (0, jnp.clip(j - n * per_part, 0, per_part - 1)), n=n))
               for n in range(len(parts))]
    return pl.pallas_call(
        body, name=name, grid=(k // tk, N_DEV), out_shape=jax.ShapeDtypeStruct((N_DEV, k, ns), BF16),
        in_specs=[pl.BlockSpec((s, tk), lambda i, j: (0, i))] + g_specs + [ANY] * len(extra),
        out_specs=pl.BlockSpec((None, tk, ns), lambda i, j: (j, i, 0)),
        compiler_params=_params("parallel", "arbitrary"),
    )(a, *parts, *extra)


def _mm_tn(name, a, g, tk=512, tn=1024, **kw):
    s, k = a.shape
    n = g.shape[1]
    tk, tn = _tile(k, tk, 128), _tile(n, tn, 128)
    return _mm(name, a, g, (k, n), (BF16,), (k // tk, n // tn),
               pl.BlockSpec((s, tk), lambda i, j: (0, i)),
               pl.BlockSpec((s, tn), lambda i, j: (0, j)),
               pl.BlockSpec((tk, tn), lambda i, j: (i, j)), 0, 0, **kw)


def _rowwise(name, fn, rows, bcast, outs, reds, tm, after=None):
    n_rows = rows[0][0].shape[0]
    tm = _tile(n_rows, tm)
    n_in, n_b, n_o, n_r = len(rows), len(bcast), len(outs), len(reds)
    extra = [] if after is None else [after]

    def body(*refs):
        ins = refs[:n_in + n_b]
        refs = refs[len(extra):]
        o_refs = refs[n_in + n_b:n_in + n_b + n_o]
        r_refs = refs[n_in + n_b + n_o:]
        vals = fn(*[r[...] for r in ins])
        for o, v in zip(o_refs, vals[:n_o]):
            o[...] = v.astype(o.dtype)
        first = pl.program_id(0) == 0
        for r, v in zip(r_refs, vals[n_o:]):
            @pl.when(first)
            def _(r=r, v=v):
                r[...] = v.astype(r.dtype)

            @pl.when(jnp.logical_not(first))
            def _(r=r, v=v):
                r[...] += v.astype(r.dtype)

    def whole(shape):
        zeros = (0,) * len(shape)
        return pl.BlockSpec(shape, lambda i: zeros)

    in_specs = [pl.BlockSpec((tm, w), functools.partial(lambda i, cb: (i, cb), cb=cb)) for _, w, cb in rows]
    in_specs += [whole(b.shape) for b in bcast] + [ANY] * len(extra)
    out_specs = [pl.BlockSpec((tm, w), lambda i: (i, 0)) for w, _ in outs]
    out_specs += [whole(s) for s, _ in reds]
    out_shape = [jax.ShapeDtypeStruct((n_rows, w), d) for w, d in outs]
    out_shape += [jax.ShapeDtypeStruct(s, d) for s, d in reds]
    return pl.pallas_call(
        body, name=name, grid=(n_rows // tm,), out_shape=out_shape, in_specs=in_specs,
        out_specs=out_specs, compiler_params=_params("arbitrary" if n_r else "parallel"),
    )(*[r[0] for r in rows], *bcast, *extra)


def _norm_fwd(name, x, g, after=None):
    d = x.shape[1]
    return _rowwise(name, lambda xt, gt: (_rmsnorm(xt, gt),), [(x, d, 0)], [g], [(d, BF16)], [], 256,
                    after=after)[0]


def _residual_norm_fwd(name, x, y, g_post, g_next):
    d = x.shape[1]

    def fn(xt, yt, gp, gn):
        new = xt + _rmsnorm(yt, gp)
        return new, _rmsnorm(new, gn)

    return _rowwise(name, fn, [(x, d, 0), (y, d, 0)], [g_post, g_next], [(d, F32), (d, BF16)], [], 256)


def _merge(z0, z1, z2, t0, t1, t2, b0, b1, b2):
    return (jax.nn.sigmoid(z0 + b0) * t0 + jax.nn.sigmoid(z1 + b1) * t1 + jax.nn.sigmoid(z2 + b2) * t2)


def _merge_fwd(name, zg, ts, b_gate):
    d = ts[0].shape[1]
    rows = [(zg, d, b) for b in range(3)] + [(t, d, 0) for t in ts]
    bias = [b_gate[:, b * d:(b + 1) * d] for b in range(3)]
    return _rowwise(name, lambda *v: (_merge(*v),), rows, bias, [(d, BF16)], [], 128)[0]


def _merge_bwd(name, dmerged, zg, ts, b_gate):
    d = ts[0].shape[1]
    rows = [(dmerged, d, 0)] + [(zg, d, b) for b in range(3)] + [(t, d, 0) for t in ts]
    bias = [b_gate[:, b * d:(b + 1) * d] for b in range(3)]

    def fn(dm, *v):
        _, vjp = jax.vjp(_merge, *v)
        dz0, dz1, dz2, dt0, dt1, dt2, db0, db1, db2 = vjp(dm)
        return (dt0, dt1, dt2, jnp.concatenate([dz0, dz1, dz2], axis=1),
                jnp.concatenate([db0, db1, db2], axis=1))

    return _rowwise(name, fn, rows, bias, [(d, BF16)] * 3 + [(3 * d, BF16)], [((1, 3 * d), F32)], 128)


def _loss_bwd(name, x, y, g_post, target):
    d = x.shape[1]

    def loss(xt, yt, gp, tt):
        err = xt + _rmsnorm(yt, gp) - tt
        return 0.5 * jnp.sum(jnp.mean(err * err, axis=-1))

    def fn(xt, yt, tt, gp):
        val, (dx, dy, dg) = jax.value_and_grad(loss, argnums=(0, 1, 2))(xt, yt, gp, tt)
        return dx, dy, jnp.full((1, BLK), val, F32), dg

    return _rowwise(name, fn, [(x, d, 0), (y, d, 0), (target, d, 0)], [g_post],
                    [(d, F32), (d, BF16)], [((1, BLK), F32), ((1, d), F32)], 256)


def _mid_bwd(name, dx_out, dh, x_mid, y, g_pre, g_post):
    d = dx_out.shape[1]

    def fn(dxo, dht, xm, yt, gpre, gpost):
        _, vjp_pre = jax.vjp(_rmsnorm, xm, gpre)
        dxm, dgpre = vjp_pre(dht)
        dxm = dxo + dxm
        _, vjp_post = jax.vjp(_rmsnorm, yt, gpost)
        dy, dgpost = vjp_post(dxm)
        return dxm, dy, dgpre, dgpost

    return _rowwise(name, fn, [(dx_out, d, 0), (dh, d, 0), (x_mid, d, 0), (y, d, 0)], [g_pre, g_post],
                    [(d, F32), (d, BF16)], [((1, d), F32), ((1, d), F32)], 256)


def _bottom_bwd(name, dx_mid, dh_a, dh_b, x, g_pre, y_prev=None, g_post_prev=None):
    d = dx_mid.shape[1]
    rows = [(dx_mid, d, 0), (dh_a, d, 0), (dh_b, d, 0), (x, d, 0)]
    if y_prev is None:
        def fn(dxm, da, db, xt, gpre):
            _, vjp_pre = jax.vjp(_rmsnorm, xt, gpre)
            dx, dgpre = vjp_pre(da + db)
            return dxm + dx, dgpre

        return _rowwise(name, fn, rows, [g_pre], [(d, F32)], [((1, d), F32)], 256)

    def fn2(dxm, da, db, xt, yt, gpre, gpost):
        _, vjp_pre = jax.vjp(_rmsnorm, xt, gpre)
        dx, dgpre = vjp_pre(da + db)
        dx = dxm + dx
        _, vjp_post = jax.vjp(_rmsnorm, yt, gpost)
        dy, dgpost = vjp_post(dx)
        return dx, dy, dgpre, dgpost

    return _rowwise(name, fn2, rows + [(y_prev, d, 0)], [g_pre, g_post_prev],
                    [(d, F32), (d, BF16)], [((1, d), F32), ((1, d), F32)], 256)


def _mem_norm_bwd(name, dmn, mem, g_mem):
    d = mem.shape[1]

    def fn(dt, mt, g):
        _, vjp = jax.vjp(_rmsnorm, mt, g)
        return (vjp(dt)[1],)

    return _rowwise(name, fn, [(dmn, d, 0), (mem, d, 0)], [g_mem], [], [((1, d), F32)], 256)[0]


def _split_dot(v, tri):
    hi = v.astype(BF16)
    lo = (v - hi.astype(F32)).astype(BF16)
    return (jnp.dot(hi, tri, preferred_element_type=F32) + jnp.dot(lo, tri, preferred_element_type=F32))


def _sb_scores(q, kb, strict=None):
    z = _dot(q, kb, 1, 1) * (BLK ** -0.5)
    soft = jnp.log(1.0 + jnp.exp(-jnp.abs(z)))
    log_beta = jnp.minimum(z, 0.0) - soft
    log_keep = jnp.minimum(-z, 0.0) - soft
    if strict is not None:
        log_keep = jnp.where(strict, log_keep, 0.0)
    return z, log_beta, log_keep


def _block_iotas():
    return (lax.broadcasted_iota(jnp.int32, (BLK, BLK), 0), lax.broadcasted_iota(jnp.int32, (BLK, BLK), 1))


def _sb_fwd(name, proj, heads):
    s = proj.shape[0]
    nq = s // BLK
    hb = _tile(heads, SB_HEADS_PER_STEP, 1)
    groups = heads // hb
    wide = hb * BLK

    def body(q_ref, k_ref, v_ref, o_ref, tot_ref, acc_ref):
        i = pl.program_id(1)
        row, col = _block_iotas()
        diag = col < row
        later = (row > col).astype(BF16)
        head_cols = [slice(hh * BLK, (hh + 1) * BLK) for hh in range(hb)]

        def tiles(off, runs, strict):
            scores = [_sb_scores(q_ref[:, c], k_ref[pl.ds(off, BLK), c], strict) for c in head_cols]
            suffixes = [_split_dot(sc[2], later) for sc in scores]
            new = []
            for hh, c in enumerate(head_cols):
                a = jnp.exp(scores[hh][1] + suffixes[hh] + runs[hh])
                if strict is not None:
                    a = jnp.where(strict, a, 0.0)
                acc_ref[:, c] += _dot(a, v_ref[pl.ds(off, BLK), c], 1, 0)
                new.append(runs[hh] + jnp.sum(scores[hh][2], axis=1, keepdims=True))
            return tuple(new)

        acc_ref[...] = jnp.zeros_like(acc_ref)
        runs = tiles(pl.multiple_of(i * BLK, BLK), (jnp.zeros((BLK, 1), F32),) * hb, diag)
        runs = lax.fori_loop(0, i, lambda t, r: tiles(pl.multiple_of((i - 1 - t) * BLK, BLK), r, None), runs)
        o_ref[...] = acc_ref[...].astype(o_ref.dtype)
        for hh in range(hb):
            tot_ref[:, hh:hh + 1] = runs[hh]

    return pl.pallas_call(
        body, name=name, grid=(groups, nq),
        out_shape=[jax.ShapeDtypeStruct((s, heads * BLK), BF16), jax.ShapeDtypeStruct((groups, s, hb), F32)],
        in_specs=[pl.BlockSpec((BLK, wide), lambda g, i: (i, g)),
                  pl.BlockSpec((s, wide), lambda g, i: (0, groups + g)),
                  pl.BlockSpec((s, wide), lambda g, i: (0, 2 * groups + g))],
        out_specs=[pl.BlockSpec((BLK, wide), lambda g, i: (i, g)),
                   pl.BlockSpec((None, BLK, hb), lambda g, i: (g, i, 0))],
        scratch_shapes=[pltpu.VMEM((BLK, wide), F32)],
        compiler_params=_params("parallel", "arbitrary"),
    )(proj, proj, proj)


def _sb_bwd(name, proj, do, tot, heads, after=None):
    s = proj.shape[0]
    nq = s // BLK
    groups, _, hb = tot.shape
    wide = hb * BLK

    extra = [] if after is None else [after]

    def body(q_ref, k_ref, v_ref, do_ref, tot_ref, *rest):
        dq_ref, dk_ref, dv_ref, dq_acc, dk_acc, dv_acc = rest[len(extra):]
        i = pl.program_id(1)
        row, col = _block_iotas()
        diag = col < row
        upto = (row <= col).astype(BF16)
        earlier = (row < col).astype(BF16)

        @pl.when(i == 0)
        def _():
            dk_acc[...] = jnp.zeros_like(dk_acc)
            dv_acc[...] = jnp.zeros_like(dv_acc)

        dq_acc[...] = jnp.zeros_like(dq_acc)

        head_cols = [slice(hh * BLK, (hh + 1) * BLK) for hh in range(hb)]

        def tiles(off, carry, strict):
            rows = pl.ds(off, BLK)
            scores = [_sb_scores(q_ref[:, c], k_ref[rows, c], strict) for c in head_cols]
            das = [_dot(do_ref[:, c], v_ref[rows, c], 1, 1) for c in head_cols]
            prefixes = [_split_dot(sc[2], upto) for sc in scores]
            dlas = []
            for hh, c in enumerate(head_cols):
                suffix = tot_ref[:, hh:hh + 1] - (prefixes[hh] + carry[2 * hh])
                a = jnp.exp(scores[hh][1] + suffix)
                if strict is not None:
                    a = jnp.where(strict, a, 0.0)
                dlas.append(das[hh] * a)
                dv_acc[rows, c] += _dot(a, do_ref[:, c], 0, 0)
            dkeeps = [_split_dot(dla, earlier) for dla in dlas]
            new = ()
            for hh, c in enumerate(head_cols):
                dkeep = dkeeps[hh] + carry[2 * hh + 1]
                if strict is not None:
                    dkeep = jnp.where(strict, dkeep, 0.0)
                sig = jnp.exp(scores[hh][1])
                dz = ((dlas[hh] * (1.0 - sig) - dkeep * sig) * (BLK ** -0.5)).astype(BF16)
                dk_acc[rows, c] += _dot(dz, q_ref[:, c], 0, 0)
                dq_acc[:, c] += _dot(dz, k_ref[rows, c], 1, 0)
                new += (carry[2 * hh] + jnp.sum(scores[hh][2], axis=1, keepdims=True),
                        carry[2 * hh + 1] + jnp.sum(dlas[hh], axis=1, keepdims=True))
            return new

        carry = lax.fori_loop(0, i, lambda jj, cr: tiles(pl.multiple_of(jj * BLK, BLK), cr, None),
                              (jnp.zeros((BLK, 1), F32),) * (2 * hb))
        tiles(pl.multiple_of(i * BLK, BLK), carry, diag)
        dq_ref[...] = dq_acc[...].astype(dq_ref.dtype)

        @pl.when(i == nq - 1)
        def _():
            dk_ref[...] = dk_acc[...].astype(dk_ref.dtype)
            dv_ref[...] = dv_acc[...].astype(dv_ref.dtype)

    shape = jax.ShapeDtypeStruct((s, heads * BLK), BF16)
    return pl.pallas_call(
        body, name=name, grid=(groups, nq), out_shape=[shape, shape, shape],
        in_specs=[pl.BlockSpec((BLK, wide), lambda g, i: (i, g)),
                  pl.BlockSpec((s, wide), lambda g, i: (0, groups + g)),
                  pl.BlockSpec((s, wide), lambda g, i: (0, 2 * groups + g)),
                  pl.BlockSpec((BLK, wide), lambda g, i: (i, g)),
                  pl.BlockSpec((None, BLK, hb), lambda g, i: (g, i, 0))] + [ANY] * len(extra),
        out_specs=[pl.BlockSpec((BLK, wide), lambda g, i: (i, g)),
                   pl.BlockSpec((s, wide), lambda g, i: (0, g)),
                   pl.BlockSpec((s, wide), lambda g, i: (0, g))],
        scratch_shapes=[pltpu.VMEM((BLK, wide), F32), pltpu.VMEM((s, wide), F32), pltpu.VMEM((s, wide), F32)],
        compiler_params=_params("parallel", "arbitrary"),
    )(proj, proj, proj, do, tot, *extra)


def _xa_probs(q, k, hd):
    z = _dot(q, k, 1, 1) * (hd ** -0.5)
    z = z - jnp.max(z, axis=-1, keepdims=True)
    e = jnp.exp(z)
    return e / jnp.sum(e, axis=-1, keepdims=True)


def _xa_fwd(name, proj, mem_kv, width):
    s = proj.shape[0]
    m = mem_kv.shape[0]
    hd = width // XA_HEADS
    tq = _tile(s, 512)
    q_first = 5 * XA_HEADS

    def body(q_ref, k_ref, v_ref, o_ref):
        p = _xa_probs(q_ref[...], k_ref[...], hd)
        o_ref[...] = _dot(p, v_ref[...], 1, 0).astype(o_ref.dtype)

    return pl.pallas_call(
        body, name=name, grid=(XA_HEADS, s // tq), out_shape=jax.ShapeDtypeStruct((s, width), BF16),
        in_specs=[pl.BlockSpec((tq, hd), lambda h, i: (i, q_first + h)),
                  pl.BlockSpec((m, hd), lambda h, i: (0, h)),
                  pl.BlockSpec((m, hd), lambda h, i: (0, XA_HEADS + h))],
        out_specs=pl.BlockSpec((tq, hd), lambda h, i: (i, h)),
        compiler_params=_params("parallel", "parallel"),
    )(proj, mem_kv, mem_kv)


def _xa_bwd(name, proj, mem_kv, do, width):
    s = proj.shape[0]
    m = mem_kv.shape[0]
    hd = width // XA_HEADS
    tq = _tile(s, 512)
    nq = s // tq
    q_first = 5 * XA_HEADS

    def body(q_ref, k_ref, v_ref, do_ref, dq_ref, dk_ref, dv_ref, dk_acc, dv_acc):
        i = pl.program_id(1)

        @pl.when(i == 0)
        def _():
            dk_acc[...] = jnp.zeros_like(dk_acc)
            dv_acc[...] = jnp.zeros_like(dv_acc)

        q, k, v, dout = q_ref[...], k_ref[...], v_ref[...], do_ref[...]
        p = _xa_probs(q, k, hd)
        dp = _dot(dout, v, 1, 1)
        dv_acc[...] += _dot(p, dout, 0, 0)
        dz = ((p * (dp - jnp.sum(dp * p, axis=-1, keepdims=True))) * (hd ** -0.5)).astype(BF16)
        dq_ref[...] = _dot(dz, k, 1, 0).astype(dq_ref.dtype)
        dk_acc[...] += _dot(dz, q, 0, 0)

        @pl.when(i == nq - 1)
        def _():
            dk_ref[...] = dk_acc[...].astype(dk_ref.dtype)
            dv_ref[...] = dv_acc[...].astype(dv_ref.dtype)

    dq, dk, dv = pl.pallas_call(
        body, name=name, grid=(XA_HEADS, nq),
        out_shape=[jax.ShapeDtypeStruct((s, width), BF16), jax.ShapeDtypeStruct((m, width), BF16),
                   jax.ShapeDtypeStruct((m, width), BF16)],
        in_specs=[pl.BlockSpec((tq, hd), lambda h, i: (i, q_first + h)),
                  pl.BlockSpec((m, hd), lambda h, i: (0, h)),
                  pl.BlockSpec((m, hd), lambda h, i: (0, XA_HEADS + h)),
                  pl.BlockSpec((tq, hd), lambda h, i: (i, h))],
        out_specs=[pl.BlockSpec((tq, hd), lambda h, i: (i, h)),
                   pl.BlockSpec((m, hd), lambda h, i: (0, h)),
                   pl.BlockSpec((m, hd), lambda h, i: (0, h))],
        scratch_shapes=[pltpu.VMEM((m, hd), F32), pltpu.VMEM((m, hd), F32)],
        compiler_params=_params("parallel", "arbitrary"),
    )(proj, mem_kv, mem_kv, do)
    return dq, dk, dv


def _gm_pointwise(u_in, v_in, g_vnorm):
    return jax.nn.gelu(u_in), _rmsnorm(jax.nn.gelu(v_in), g_vnorm)


def _gm_mask():
    row = lax.broadcasted_iota(jnp.int32, (BLK, BLK), 0)
    col = lax.broadcasted_iota(jnp.int32, (BLK, BLK), 1)
    return (col // CHUNK) <= (row // CHUNK)


def _gm_fwd(name, proj, g_vnorm, w_s, b_s, width):
    s = proj.shape[0]
    groups = width // BLK

    def body(u_ref, v_ref, g_ref, w_ref, b_ref, o_ref):
        u, vn = _gm_pointwise(u_ref[...], v_ref[...], g_ref[...])
        vn = vn.astype(BF16)
        mask = _gm_mask()
        for g in range(groups):
            cols = slice(g * BLK, (g + 1) * BLK)
            w = jnp.where(mask, w_ref[g], 0.0)
            mixed = _dot(w, vn[:, cols], 1, 0) + b_ref[g]
            o_ref[:, cols] = (u[:, cols] * mixed).astype(o_ref.dtype)

    return pl.pallas_call(
        body, name=name, grid=(s // BLK,), out_shape=jax.ShapeDtypeStruct((s, width), BF16),
        in_specs=[pl.BlockSpec((BLK, width), lambda c: (c, 3)), pl.BlockSpec((BLK, width), lambda c: (c, 4)),
                  pl.BlockSpec((1, width), lambda c: (0, 0)),
                  pl.BlockSpec((groups, BLK, BLK), lambda c: (0, 0, 0)),
                  pl.BlockSpec((groups, BLK, 1), lambda c: (0, 0, 0))],
        out_specs=pl.BlockSpec((BLK, width), lambda c: (c, 0)),
        compiler_params=_params("parallel"),
    )(proj, proj, g_vnorm, w_s, b_s)


def _gm_bwd(name, proj, do, g_vnorm, w_s, b_s, width):
    s = proj.shape[0]
    groups = width // BLK

    def body(u_ref, v_ref, do_ref, g_ref, w_ref, b_ref, du_ref, dv_ref, dw_ref, db_ref, dg_ref, dvn_buf):
        first = pl.program_id(0) == 0
        (u, vn), vjp = jax.vjp(_gm_pointwise, u_ref[...], v_ref[...], g_ref[...])
        vn16 = vn.astype(BF16)
        dout = do_ref[...]
        mask = _gm_mask()
        du_parts = []
        for g in range(groups):
            cols = slice(g * BLK, (g + 1) * BLK)
            w = jnp.where(mask, w_ref[g], 0.0)
            mixed = _dot(w, vn16[:, cols], 1, 0) + b_ref[g]
            du_parts.append(dout[:, cols] * mixed)
            dmixed = dout[:, cols] * u[:, cols]
            dw = jnp.where(mask, _dot(dmixed, vn16[:, cols], 1, 1), 0.0)
            db = jnp.sum(dmixed, axis=1, keepdims=True)
            dvn_buf[:, cols] = _dot(w, dmixed, 0, 0)

            @pl.when(first)
            def _(g=g, dw=dw, db=db):
                dw_ref[g] = dw
                db_ref[g] = db

            @pl.when(jnp.logical_not(first))
            def _(g=g, dw=dw, db=db):
                dw_ref[g] += dw
                db_ref[g] += db

        du_in, dv_in, dg = vjp((jnp.concatenate(du_parts, axis=1), dvn_buf[...]))
        du_ref[...] = du_in.astype(du_ref.dtype)
        dv_ref[...] = dv_in.astype(dv_ref.dtype)

        @pl.when(first)
        def _():
            dg_ref[...] = dg

        @pl.when(jnp.logical_not(first))
        def _():
            dg_ref[...] += dg

    shape = jax.ShapeDtypeStruct((s, width), BF16)
    return pl.pallas_call(
        body, name=name, grid=(s // BLK,),
        out_shape=[shape, shape, jax.ShapeDtypeStruct((groups, BLK, BLK), F32),
                   jax.ShapeDtypeStruct((groups, BLK, 1), F32), jax.ShapeDtypeStruct((1, width), F32)],
        in_specs=[pl.BlockSpec((BLK, width), lambda c: (c, 3)), pl.BlockSpec((BLK, width), lambda c: (c, 4)),
                  pl.BlockSpec((BLK, width), lambda c: (c, 0)),
                  pl.BlockSpec((1, width), lambda c: (0, 0)),
                  pl.BlockSpec((groups, BLK, BLK), lambda c: (0, 0, 0)),
                  pl.BlockSpec((groups, BLK, 1), lambda c: (0, 0, 0))],
        out_specs=[pl.BlockSpec((BLK, width), lambda c: (c, 0)), pl.BlockSpec((BLK, width), lambda c: (c, 0)),
                   pl.BlockSpec((groups, BLK, BLK), lambda c: (0, 0, 0)),
                   pl.BlockSpec((groups, BLK, 1), lambda c: (0, 0, 0)),
                   pl.BlockSpec((1, width), lambda c: (0, 0))],
        scratch_shapes=[pltpu.VMEM((BLK, width), F32)],
        compiler_params=_params("arbitrary"),
    )(proj, proj, do, g_vnorm, w_s, b_s)


def _shift_down(v, k):
    row = lax.broadcasted_iota(jnp.int32, v.shape, 0)
    return jnp.where(row >= k, pltpu.roll(v, k, axis=0), 0.0)


def _shift_up(v, k):
    n = v.shape[0]
    row = lax.broadcasted_iota(jnp.int32, v.shape, 0)
    return jnp.where(row < n - k, pltpu.roll(v, n - k, axis=0), 0.0)


def _conv(gate, w, b):
    return b + w[0:1] * _shift_down(gate, 2) + w[1:2] * _shift_down(gate, 1) + w[2:3] * gate


GELU_C0 = 0.7978845608028654
GELU_C1 = 0.044715


def _gelu_and_slope(x):
    x2 = x * x
    t = jnp.tanh(x * (GELU_C0 + (GELU_C0 * GELU_C1) * x2))
    half = 0.5 * (1.0 + t)
    slope = half + (0.5 * x) * (1.0 - t * t) * (GELU_C0 + (3.0 * GELU_C0 * GELU_C1) * x2)
    return x * half, slope


def _conv_fwd(name, up, conv_w, conv_b):
    s, f2 = up.shape
    f = f2 // 2
    tc = _tile(f, 256, 128)
    nf = f // tc

    def body(g_ref, v_ref, w_ref, b_ref, o_ref):
        o_ref[...] = (jax.nn.gelu(_conv(g_ref[...], w_ref[...], b_ref[...])) * v_ref[...]).astype(o_ref.dtype)

    return pl.pallas_call(
        body, name=name, grid=(nf,), out_shape=jax.ShapeDtypeStruct((s, f), BF16),
        in_specs=[pl.BlockSpec((s, tc), lambda j: (0, j)), pl.BlockSpec((s, tc), lambda j: (0, nf + j)),
                  pl.BlockSpec((3, tc), lambda j: (0, j)), pl.BlockSpec((1, tc), lambda j: (0, j))],
        out_specs=pl.BlockSpec((s, tc), lambda j: (0, j)),
        compiler_params=_params("parallel"),
    )(up, up, conv_w, conv_b)


def _conv_bwd(name, up, da, conv_w, conv_b):
    s, f2 = up.shape
    f = f2 // 2
    tc = _tile(f, 256, 128)
    nf = f // tc

    def body(g_ref, v_ref, da_ref, w_ref, b_ref, dg_ref, dv_ref, dw_ref, db_ref):
        gate, val, dact, w = g_ref[...], v_ref[...], da_ref[...], w_ref[...]
        act, slope = _gelu_and_slope(_conv(gate, w, b_ref[...]))
        dv_ref[...] = (dact * act).astype(dv_ref.dtype)
        dconv = dact * val * slope
        dg_ref[...] = (w[2:3] * dconv + w[1:2] * _shift_up(dconv, 1) + w[0:1] * _shift_up(dconv, 2)
                       ).astype(dg_ref.dtype)
        dw_ref[0:1, :] = jnp.sum(dconv * _shift_down(gate, 2), axis=0, keepdims=True)
        dw_ref[1:2, :] = jnp.sum(dconv * _shift_down(gate, 1), axis=0, keepdims=True)
        dw_ref[2:3, :] = jnp.sum(dconv * gate, axis=0, keepdims=True)
        db_ref[...] = jnp.sum(dconv, axis=0, keepdims=True)

    shape = jax.ShapeDtypeStruct((s, f), BF16)
    return pl.pallas_call(
        body, name=name, grid=(nf,),
        out_shape=[shape, shape, jax.ShapeDtypeStruct((3, f), F32), jax.ShapeDtypeStruct((1, f), F32)],
        in_specs=[pl.BlockSpec((s, tc), lambda j: (0, j)), pl.BlockSpec((s, tc), lambda j: (0, nf + j)),
                  pl.BlockSpec((s, tc), lambda j: (0, j)),
                  pl.BlockSpec((3, tc), lambda j: (0, j)), pl.BlockSpec((1, tc), lambda j: (0, j))],
        out_specs=[pl.BlockSpec((s, tc), lambda j: (0, j)), pl.BlockSpec((s, tc), lambda j: (0, j)),
                   pl.BlockSpec((3, tc), lambda j: (0, j)), pl.BlockSpec((1, tc), lambda j: (0, j))],
        compiler_params=_params("parallel"),
    )(up, up, da, conv_w, conv_b)


def _adamw(w, g, m, v):
    m = ADAM_B1 * m + (1.0 - ADAM_B1) * g
    v = ADAM_B2 * v + (1.0 - ADAM_B2) * jnp.square(g)
    m_hat = m / (1.0 - ADAM_B1 ** ADAM_STEP)
    v_hat = v / (1.0 - ADAM_B2 ** ADAM_STEP)
    delta = -ADAM_LR * (m_hat / (jnp.sqrt(v_hat) + ADAM_EPS) + ADAM_WD * w)
    return delta, m, v


def _update_shard(name, layer, w, m, v, own, received, chip, previous=None):
    _, rows, cols = w.shape
    tr = _tile(rows, max(8, STREAM_BLOCK_BYTES // (8 * cols)))

    def body(chip_ref, w_ref, m_ref, v_ref, own_ref, rec_ref, *rest):
        g_ref, d_ref, nm_ref, nv_ref = rest[-4:]
        g = (own_ref[...].astype(F32) + rec_ref[0].astype(F32) + rec_ref[1].astype(F32)
             + rec_ref[2].astype(F32))
        delta, new_m, new_v = _adamw(w_ref[...], g, m_ref[...], v_ref[...])
        g_ref[...] = g
        d_ref[...] = delta
        nm_ref[...] = new_m
        nv_ref[...] = new_v

    layer_spec = pl.BlockSpec((None, tr, cols), lambda i, chip_ref: (layer, i, 0))
    in_specs = [layer_spec] * 3 + [pl.BlockSpec((None, tr, cols), lambda i, chip_ref: (chip_ref[0], i, 0)),
                                   pl.BlockSpec((3, tr, cols), lambda i, chip_ref: (0, i, 0))]
    args = [chip, w, m, v, own, received]
    aliases = {}
    if previous is not None:
        in_specs += [ANY] * 4
        aliases = {len(args) + k: k for k in range(4)}
        args += list(previous)
    return pl.pallas_call(
        body, name=name, out_shape=[jax.ShapeDtypeStruct(w.shape, F32)] * 4,
        grid_spec=pltpu.PrefetchScalarGridSpec(
            num_scalar_prefetch=1, grid=(rows // tr,), in_specs=in_specs, out_specs=[layer_spec] * 4),
        input_output_aliases=aliases, compiler_params=_params("parallel"),
    )(*args)


def _sum_devices(name, gathered):
    _, rows, cols = gathered.shape

    def body(g_ref, o_ref):
        total = g_ref[0]
        for d in range(1, N_DEV):
            total = total + g_ref[d]
        o_ref[...] = total

    return pl.pallas_call(
        body, name=name, out_shape=jax.ShapeDtypeStruct((rows, cols), F32),
        in_specs=[pl.BlockSpec((N_DEV, rows, cols), lambda: (0, 0, 0))],
        out_specs=pl.BlockSpec((rows, cols), lambda: (0, 0)),
        compiler_params=pltpu.CompilerParams(vmem_limit_bytes=VMEM_LIMIT_V7X),
    )(gathered)


def _update_small(name, w, g, m, v):
    def body(w_ref, g_ref, m_ref, v_ref, d_ref, nm_ref, nv_ref):
        delta, new_m, new_v = _adamw(w_ref[...], g_ref[...], m_ref[...], v_ref[...])
        d_ref[...] = delta
        nm_ref[...] = new_m
        nv_ref[...] = new_v

    spec = pl.BlockSpec(w.shape, lambda: (0, 0))
    return pl.pallas_call(
        body, name=name, out_shape=[jax.ShapeDtypeStruct(w.shape, F32)] * 3,
        in_specs=[spec] * 4, out_specs=[spec] * 3,
        compiler_params=pltpu.CompilerParams(vmem_limit_bytes=VMEM_LIMIT_V7X),
    )(w, g, m, v)


def _pack(arrays):
    flat = jnp.concatenate([a.reshape(-1) for a in arrays])
    pad = (-flat.shape[0]) % (8 * BLK)
    return jnp.pad(flat, (0, pad)).reshape(-1, BLK)


def _unpack(packed, shapes):
    flat = packed.reshape(-1)
    out, at = [], 0
    for shape in shapes:
        size = 1
        for d in shape:
            size *= d
        out.append(flat[at:at + size].reshape(shape))
        at += size
    return out


SHARDED = ("w_in", "w_mem_kv", "w_gate", "w_br_sb", "w_br_gm", "w_br_xa", "w_out", "w_up", "w_down")
SMALL = ("g_mix_pre", "g_vnorm", "w_s", "b_s", "g_mem", "b_gate", "g_mix_post", "g_ffn_pre", "conv_w",
         "conv_b", "g_ffn_post")
WEIGHTS = ("g_mix_pre", "w_in", "g_vnorm", "w_s", "b_s", "g_mem", "w_mem_kv", "w_gate", "b_gate", "w_br_sb",
           "w_br_gm", "w_br_xa", "w_out", "g_mix_post", "g_ffn_pre", "w_up", "conv_w", "conv_b", "w_down",
           "g_ffn_post")


ROW_SHARDED = ("w_mem_kv", "w_out", "w_down")
GATHER_GROUPS = (("w_in",), ("w_mem_kv", "w_gate"), ("w_br_sb", "w_br_gm", "w_br_xa", "w_out"), ("w_up",),
                 ("w_down",))
REDUCE_GROUPS = (("w_down", "w_up"), ("w_out", "w_br_sb", "w_br_gm", "w_br_xa", "w_mem_kv", "w_gate"), ("w_in",))


def _cast_into_place(name, w, layer, me):
    _, rows, cols = w.shape
    tr = _tile(rows, max(8, STREAM_BLOCK_BYTES // (4 * cols)))

    def body(me_ref, w_ref, o_ref):
        o_ref[...] = w_ref[...].astype(o_ref.dtype)

    return pl.pallas_call(
        body, name=name, out_shape=jax.ShapeDtypeStruct((N_DEV, rows, cols), BF16),
        grid_spec=pltpu.PrefetchScalarGridSpec(
            num_scalar_prefetch=1, grid=(rows // tr,),
            in_specs=[pl.BlockSpec((None, tr, cols), lambda i, me_ref: (layer, i, 0))],
            out_specs=pl.BlockSpec((None, tr, cols), lambda i, me_ref: (me_ref[0], i, 0))),
        compiler_params=_params("parallel"),
    )(me, w)


class _Gather:
    def __init__(self, tag, names, places, after):
        self.tag, self.names, self.n = tag, names, len(places)
        self.sems, self.places, self.token = _split_start(
            f"gather_start_{tag}", places, _gather_first_copies(self.n), after)

    def relay(self, after):
        self.sems, self.places = _split_wait(
            f"gather_relay_{self.tag}", self.places, self.sems, _gather_first_copies(self.n), after,
            then=_gather_relay_copies(self.n))

    def finish(self, after):
        places = _split_wait(f"gather_finish_{self.tag}", self.places, self.sems,
                             _gather_relay_copies(self.n), after)
        full = dict(zip(self.names, places))
        for name in self.names:
            if name in ROW_SHARDED:
                full[name] = full[name].reshape(-1, full[name].shape[-1])
        return full


class _Reduce:
    def __init__(self, tag, names, partials):
        self.tag, self.names, self.n = tag, names, len(partials)
        blocks = [p.reshape((N_CHIP, 2, -1, p.shape[-1])) for p in partials]
        lands = [lax.empty((N_CHIP,) + b.shape[2:], b.dtype) for b in blocks]
        self.sems, self.buffers, self.token = _split_start(
            f"pair_start_{tag}", blocks + lands, _pair_copies(self.n))

    def middle(self, after, core):
        n = self.n
        got = _split_wait(f"pair_finish_{self.tag}", self.buffers, self.sems, _pair_copies(n), after)
        sums = [_pair_sum(f"pair_sum_{name}{self.tag}", got[a], got[n + a], core)
                for a, name in enumerate(self.names)]
        lands = [lax.empty((3,) + s.shape[1:], s.dtype) for s in sums]
        self.sems, self.buffers, self.token = _split_start(
            f"chip_start_{self.tag}", sums + lands, _chip_copies(n))

    def finish(self, after):
        n = self.n
        got = _split_wait(f"chip_finish_{self.tag}", self.buffers, self.sems, _chip_copies(n), after)
        return [(name, got[a], got[n + a]) for a, name in enumerate(self.names)]


def kernel(x, mem, g_mix_pre, w_in, g_vnorm, w_s, b_s, g_mem, w_mem_kv, w_gate, b_gate, w_br_sb, w_br_gm, w_br_xa, w_out, g_mix_post, g_ffn_pre, w_up, conv_w, conv_b, w_down, g_ffn_post, loss_target, m_g_mix_pre, m_w_in, m_g_vnorm, m_w_s, m_b_s, m_g_mem, m_w_mem_kv, m_w_gate, m_b_gate, m_w_br_sb, m_w_br_gm, m_w_br_xa, m_w_out, m_g_mix_post, m_g_ffn_pre, m_w_up, m_conv_w, m_conv_b, m_w_down, m_g_ffn_post, v_g_mix_pre, v_w_in, v_g_vnorm, v_w_s, v_b_s, v_g_mem, v_w_mem_kv, v_w_gate, v_b_gate, v_w_br_sb, v_w_br_gm, v_w_br_xa, v_w_out, v_g_mix_post, v_g_ffn_pre, v_w_up, v_conv_w, v_conv_b, v_w_down, v_g_ffn_post):
    p = dict(g_mix_pre=g_mix_pre, w_in=w_in, g_vnorm=g_vnorm, w_s=w_s, b_s=b_s, g_mem=g_mem, w_mem_kv=w_mem_kv,
             w_gate=w_gate, b_gate=b_gate, w_br_sb=w_br_sb, w_br_gm=w_br_gm, w_br_xa=w_br_xa, w_out=w_out,
             g_mix_post=g_mix_post, g_ffn_pre=g_ffn_pre, w_up=w_up, conv_w=conv_w, conv_b=conv_b, w_down=w_down,
             g_ffn_post=g_ffn_post)
    mom = dict(g_mix_pre=m_g_mix_pre, w_in=m_w_in, g_vnorm=m_g_vnorm, w_s=m_w_s, b_s=m_b_s, g_mem=m_g_mem,
               w_mem_kv=m_w_mem_kv, w_gate=m_w_gate, b_gate=m_b_gate, w_br_sb=m_w_br_sb, w_br_gm=m_w_br_gm,
               w_br_xa=m_w_br_xa, w_out=m_w_out, g_mix_post=m_g_mix_post, g_ffn_pre=m_g_ffn_pre, w_up=m_w_up,
               conv_w=m_conv_w, conv_b=m_conv_b, w_down=m_w_down, g_ffn_post=m_g_ffn_post)
    var = dict(g_mix_pre=v_g_mix_pre, w_in=v_w_in, g_vnorm=v_g_vnorm, w_s=v_w_s, b_s=v_b_s, g_mem=v_g_mem,
               w_mem_kv=v_w_mem_kv, w_gate=v_w_gate, b_gate=v_b_gate, w_br_sb=v_w_br_sb, w_br_gm=v_w_br_gm,
               w_br_xa=v_w_br_xa, w_out=v_w_out, g_mix_post=v_g_mix_post, g_ffn_pre=v_g_ffn_pre, w_up=v_w_up,
               conv_w=v_conv_w, conv_b=v_conv_b, w_down=v_w_down, g_ffn_post=v_g_ffn_post)
    depth = w_in.shape[0]
    x = x[0]
    mem = mem[0]
    target = loss_target[0]
    s, d = x.shape
    width = d // 2
    heads = width // BLK
    cx, cy, cc = lax.axis_index("x"), lax.axis_index("y"), lax.axis_index("c")
    core = cc.astype(jnp.int32).reshape(1)
    chip = (2 * cx + cy).astype(jnp.int32).reshape(1)
    me = 4 * cx + 2 * cy + cc
    me_index = me.astype(jnp.int32).reshape(1)

    conv_all = _all_gather("gather_conv_w", [conv_w])[0]
    conv_w_full = jnp.transpose(conv_all, (1, 2, 0, 3)).reshape(depth, 3, -1)
    gathers = {}
    token = conv_all
    for l in range(depth):
        for gi, names in enumerate(GATHER_GROUPS):
            places = [_cast_into_place(f"cast_{n}{l}", p[n], l, me_index) for n in names]
            gathers[l, gi] = _Gather(f"{l}{gi}", names, places, token)
            token = gathers[l, gi].token

    kept, gathered = [], []
    h = _norm_fwd("pre_norm0", x, g_mix_pre[0][None, :], after=token)
    gathers[0, 0].relay(h)
    for l in range(depth):
        row = lambda name: p[name][l][None, :]
        first, second, third, fourth, fifth = (gathers[l, gi] for gi in range(len(GATHER_GROUPS)))
        w = first.finish(h)
        proj, proj16 = _mm_nn_cs(f"proj{l}", h, w["w_in"], (F32, BF16), tm=2048)
        o_sb, sb_tot = _sb_fwd(f"sb_fwd{l}", proj16, heads)
        b_s3 = b_s[l][:, :, None]
        o_gm = _gm_fwd(f"gm_fwd{l}", proj, row("g_vnorm"), w_s[l], b_s3, width)
        second.relay(o_gm)
        mn = _norm_fwd(f"mem_norm{l}", mem, row("g_mem"))
        w.update(second.finish(mn))
        third.relay(mn)
        mem_kv = _mm_nn(f"mem_kv{l}", mn, w["w_mem_kv"], (BF16,))
        o_xa = _xa_fwd(f"xa_fwd{l}", proj16, mem_kv, width)
        zg = _mm_nn_cs(f"gates{l}", h, w["w_gate"], tm=2048)
        w.update(third.finish(zg))
        ts = [_mm_nn_cs(f"branch_{n}{l}", o, w[f"w_br_{n}"], (BF16,), tm=2048)
              for n, o in (("sb", o_sb), ("gm", o_gm), ("xa", o_xa))]
        merged = _merge_fwd(f"merge{l}", zg, ts, row("b_gate"))
        fourth.relay(merged)
        y1 = _mm_nn(f"out{l}", merged, w["w_out"], tm=1024, tn=1024)
        x1, h2 = _residual_norm_fwd(f"mix_post{l}", x, y1, row("g_mix_post"), row("g_ffn_pre"))
        w.update(fourth.finish(h2))
        up = _mm_nn_cs(f"up{l}", h2, w["w_up"], tm=1024)
        fifth.relay(up)
        act = _conv_fwd(f"conv_fwd{l}", up, conv_w_full[l], row("conv_b"))
        w.update(fifth.finish(act))
        if l + 1 < depth:
            gathers[l + 1, 0].relay(act)
        y2 = _mm_nn(f"down{l}", act, w["w_down"], tm=1024)
        kept.append(dict(x=x, h=h, proj=proj, proj16=proj16, zg=zg, mn=mn, mem_kv=mem_kv, o_sb=o_sb, o_gm=o_gm,
                         o_xa=o_xa, ts=ts, merged=merged, y1=y1, x1=x1, h2=h2, up=up, act=act, y2=y2, b_s=b_s3,
                         sb_tot=sb_tot))
        gathered.append(w)
        if l + 1 < depth:
            x, h = _residual_norm_fwd(f"ffn_post{l}", x1, y2, g_ffn_post[l][None, :], g_mix_pre[l + 1][None, :])

    top = kept[-1]
    dx, dy2, loss_part, dg_ffn_post = _loss_bwd("loss", top["x1"], top["y2"], g_ffn_post[depth - 1][None, :], target)
    loss = lax.psum(loss_part[0, 0], ("x", "y", "c"))
    small_grads = [dict() for _ in range(depth)]
    small_grads[depth - 1]["g_ffn_post"] = dg_ffn_post
    small_early = [(ll, n) for ll in range(depth) for n in SMALL if (ll, n) != (0, "g_mix_pre")]
    outs = {n: None for n in SHARDED}
    in_flight = []

    def reduce_finish(entry, after):
        layer = int(entry[0].tag[0])
        for name, own, received in entry[0].finish(after):
            outs[name] = _update_shard(f"update_{name}{layer}", layer, p[name], mom[name], var[name],
                                       own, received, chip, outs[name])
        in_flight.remove(entry)

    def reduce_start(new):
        for entry in in_flight:
            entry[2] += entry[1] == 2
        in_flight.append([new, 1, 0])
        for entry in [e for e in in_flight if e[2] >= 2]:
            reduce_finish(entry, new.token)
        return new.token

    def reduce_middle(after):
        for entry in [e for e in in_flight if e[1] == 1]:
            entry[0].middle(after, core)
            after = entry[0].token
            entry[1] = 2
        return after

    behind = None

    for l in reversed(range(depth)):
        k = kept[l]
        w = gathered[l]
        sg = small_grads[l]
        row = lambda name: p[name][l][None, :]
        partials = {}
        group = lambda gi: _Reduce(f"{l}{gi}", REDUCE_GROUPS[gi], [partials[n] for n in REDUCE_GROUPS[gi]])
        partials["w_down"] = _mm_tn(f"d_w_down{l}", k["act"], dy2, tn=2048, after=behind)
        dact = _mm_nt(f"d_act{l}", dy2, w["w_down"], tm=2048)
        dgate, dval, sg["conv_w"], sg["conv_b"] = _conv_bwd(f"conv_bwd{l}", k["up"], dact, conv_w_full[l],
                                                             row("conv_b"))
        dup = (dgate, dval)
        partials["w_up"] = _mm_tn_cs(f"d_w_up{l}", k["h2"], dup, tk=1024)
        behind = reduce_start(group(0))
        dh2 = _mm_nt_cs(f"d_h2{l}", dup, w["w_up"], per_step=2, after=behind)
        dx1, dy1, sg["g_ffn_pre"], sg["g_mix_post"] = _mid_bwd(f"mid_bwd{l}", dx, dh2, k["x1"], k["y1"],
                                                                row("g_ffn_pre"), row("g_mix_post"))
        behind = reduce_middle(dy1)
        partials["w_out"] = _mm_tn(f"d_w_out{l}", k["merged"], dy1, tk=1024, after=behind)
        dmerged = _mm_nt(f"d_merged{l}", dy1, w["w_out"], tm=1024, tn=1024)
        dt_sb, dt_gm, dt_xa, dzg, sg["b_gate"] = _merge_bwd(f"merge_bwd{l}", dmerged, k["zg"], k["ts"], row("b_gate"))
        do = {}
        for n, dt in (("sb", dt_sb), ("gm", dt_gm), ("xa", dt_xa)):
            partials[f"w_br_{n}"] = _mm_tn_cs(f"d_w_br_{n}{l}", k[f"o_{n}"], dt, tk=1024)
            do[n] = _mm_nt_cs_whole(f"d_o_{n}{l}", dt, w[f"w_br_{n}"], F32 if n == "gm" else BF16)
        dq_xa, dk_mem, dv_mem = _xa_bwd(f"xa_bwd{l}", k["proj16"], k["mem_kv"], do["xa"], width)
        dmem_kv = jnp.concatenate([dk_mem, dv_mem], axis=1)
        partials["w_mem_kv"] = _mm_tn(f"d_w_mem_kv{l}", k["mn"], dmem_kv)
        partials["w_gate"] = _mm_tn_cs(f"d_w_gate{l}", k["h"], dzg, tk=2048)
        behind = reduce_start(group(1))
        dh_gate = _mm_nt_cs(f"d_h_gate{l}", dzg, w["w_gate"], per_step=4, after=behind)
        behind = reduce_middle(dh_gate)
        dmn = _mm_nt(f"d_mn{l}", dmem_kv, w["w_mem_kv"], after=behind)
        sg["g_mem"] = _mem_norm_bwd(f"mem_norm_bwd{l}", dmn, mem, row("g_mem"))
        du, dvg, sg["w_s"], db_s, sg["g_vnorm"] = _gm_bwd(f"gm_bwd{l}", k["proj"], do["gm"], row("g_vnorm"),
                                                          p["w_s"][l], k["b_s"], width)
        sg["b_s"] = db_s[:, :, 0]
        small_token = None
        if l == 0:
            early_shapes = [small_grads[ll][n].shape for ll, n in small_early]
            packed = _pack([small_grads[ll][n] for ll, n in small_early])
            place = lax.dynamic_update_slice(lax.empty((N_DEV,) + packed.shape, F32), packed[None], (me, 0, 0))
            small_gather = _Gather("small", ("small",), [place], du)
            small_token = small_gather.token
        dq, dk, dv = _sb_bwd(f"sb_bwd{l}", k["proj16"], do["sb"], k["sb_tot"], heads, after=small_token)
        if l == 0:
            small_gather.relay(dq)
        dproj = jnp.concatenate([dq, dk, dv, du, dvg, dq_xa], axis=1)
        partials["w_in"] = _mm_tn_cs(f"d_w_in{l}", k["h"], dproj, tk=2048)
        behind = reduce_start(group(2))
        dh_in = _mm_nt_cs(f"d_h_in{l}", dproj, w["w_in"], per_step=4, after=behind)
        if l > 0:
            below = kept[l - 1]
            dx, dy2, sg["g_mix_pre"], small_grads[l - 1]["g_ffn_post"] = _bottom_bwd(
                f"bottom_bwd{l}", dx1, dh_in, dh_gate, k["x"], row("g_mix_pre"), below["y2"],
                p["g_ffn_post"][l - 1][None, :])
        else:
            dx, sg["g_mix_pre"] = _bottom_bwd(f"bottom_bwd{l}", dx1, dh_in, dh_gate, k["x"], row("g_mix_pre"))
        if l > 0:
            behind = reduce_middle(dx)

    late_gathered = _all_gather("gather_small", [_pack([small_grads[0]["g_mix_pre"]])])[0]
    behind = reduce_middle(late_gathered)
    for entry in list(in_flight):
        reduce_finish(entry, behind)

    early_total = _sum_devices("sum_small", small_gather.finish(behind)["small"])
    late_total = _sum_devices("sum_small_late", late_gathered)
    summed = dict(zip(small_early, _unpack(early_total, early_shapes)))
    summed[0, "g_mix_pre"] = _unpack(late_total, [small_grads[0]["g_mix_pre"].shape])[0]
    grads = {}
    for n in SMALL:
        g = jnp.stack([summed[l, n] for l in range(depth)]).reshape((depth,) + p[n].shape[1:]
                                                                    if n != "conv_w" else (depth, 3, -1))
        if n == "conv_w":
            per = conv_w.shape[2]
            g = lax.dynamic_slice_in_dim(g, me * per, per, axis=2)
        grads[n] = g
    delta, new_m, new_v = _update_small(
        "update_small", _pack([p[n] for n in SMALL]), _pack([grads[n] for n in SMALL]),
        _pack([mom[n] for n in SMALL]), _pack([var[n] for n in SMALL]))
    shapes = [p[n].shape for n in SMALL]
    for n, dl, nm, nv in zip(SMALL, _unpack(delta, shapes), _unpack(new_m, shapes), _unpack(new_v, shapes)):
        outs[n] = (grads[n], dl, nm, nv)

    result = [loss, dx[None]]
    for k in range(4):
        result += [outs[n][k] for n in WEIGHTS]
    return tuple(result)
```

```python
import functools

import jax
import jax.numpy as jnp
from jax import lax
from jax.experimental import pallas as pl
from jax.experimental.pallas import tpu as pltpu

F32 = jnp.float32
BF16 = jnp.bfloat16
N_DEV = 8
N_CHIP = 4
EPS = 1e-6
CHUNK = 64
BLK = 128
XA_HEADS = 4
SB_HEADS_PER_STEP = 8
ADAM_LR = 0.001
ADAM_B1 = 0.9
ADAM_B2 = 0.999
ADAM_EPS = 1e-08
ADAM_WD = 0.01
ADAM_STEP = 10
VMEM_LIMIT_V7X = 56 * 1024 * 1024
STREAM_BLOCK_BYTES = 4 * 1024 * 1024
MESH = pl.DeviceIdType.MESH
ANY = pl.BlockSpec(memory_space=pl.ANY)


def _params(*sem):
    return pltpu.CompilerParams(dimension_semantics=sem, vmem_limit_bytes=VMEM_LIMIT_V7X)


def _tile(n, pref, mult=8):
    best = None
    for t in range(mult, min(n, pref) + 1, mult):
        if n % t == 0:
            best = t
    return best if best is not None else n


def _bf(x):
    return x if x.dtype == BF16 else x.astype(BF16)


def _dot(a, b, ca, cb):
    return lax.dot_general(_bf(a), _bf(b), (((ca,), (cb,)), ((), ())), preferred_element_type=F32)


def _rmsnorm(x, g):
    return (x * lax.rsqrt(jnp.mean(x * x, axis=-1, keepdims=True) + EPS)) * g


def _position():
    x, y, c = lax.axis_index("x"), lax.axis_index("y"), lax.axis_index("c")
    return x, y, c


def _all_gather(name, shards):
    n = len(shards)

    def body(*refs):
        ins, outs = refs[:n], refs[n:2 * n]
        send_sems, recv_sems, local_sems = refs[2 * n:]
        x, y, c = _position()
        me = 4 * x + 2 * y + c
        sibling = (x, y, 1 - c)
        chips = [(1 - x, y), (x, 1 - y), (1 - x, 1 - y)]

        def copy(a, k, block, to, src=None):
            dst = outs[a].at[block]
            return pltpu.make_async_remote_copy(
                src_ref=dst if src is None else src, dst_ref=dst, send_sem=send_sems.at[a, k],
                recv_sem=recv_sems.at[a, k], device_id=to, device_id_type=MESH)

        local = [pltpu.make_async_copy(ins[a], outs[a].at[me], local_sems.at[a]) for a in range(n)]
        for cp in local:
            cp.start()
        first = []
        for a in range(n):
            first.append(copy(a, 0, me, sibling, src=ins[a]))
            for j, chip in enumerate(chips):
                first.append(copy(a, 1 + j, me, (*chip, c), src=ins[a]))
        for cp in first:
            cp.start()
        passed = []
        for a in range(n):
            for j, (px, py) in enumerate(chips):
                block = 4 * px + 2 * py + c
                copy(a, 1 + j, block, sibling).wait_recv()
                forward = copy(a, 4 + j, block, sibling)
                forward.start()
                passed.append(forward)
        for a in range(n):
            copy(a, 0, 4 * x + 2 * y + (1 - c), sibling).wait_recv()
            for j, (px, py) in enumerate(chips):
                copy(a, 4 + j, 4 * px + 2 * py + (1 - c), sibling).wait_recv()
        for cp in first + passed:
            cp.wait_send()
        for cp in local:
            cp.wait()

    return pl.pallas_call(
        body, name=name,
        out_shape=[jax.ShapeDtypeStruct((N_DEV,) + s.shape, s.dtype) for s in shards],
        in_specs=[ANY] * n, out_specs=[ANY] * n,
        scratch_shapes=[pltpu.SemaphoreType.DMA((n, 7)), pltpu.SemaphoreType.DMA((n, 7)),
                        pltpu.SemaphoreType.DMA((n,))],
    )(*shards)


HBM = pl.BlockSpec(memory_space=pltpu.HBM)
SEM = pl.BlockSpec(memory_space=pltpu.SEMAPHORE)
DATAFLOW = pltpu.SideEffectType.DATAFLOW_SIDE_EFFECTING


def _remote(src, dst, send, recv, k, peer):
    return pltpu.make_async_remote_copy(src_ref=src, dst_ref=dst, send_sem=send.at[k], recv_sem=recv.at[k],
                                        device_id=peer, device_id_type=MESH)


def _gather_first_copies(n):
    def build(refs, send, recv):
        x, y, c = _position()
        me = 4 * x + 2 * y + c
        peers = [(x, y, 1 - c), (1 - x, y, c), (x, 1 - y, c), (1 - x, 1 - y, c)]
        return [_remote(refs[a].at[me], refs[a].at[me], send, recv, 4 * a + k, peer)
                for a in range(n) for k, peer in enumerate(peers)]
    return build, 4 * n


def _gather_relay_copies(n):
    def build(refs, send, recv):
        x, y, c = _position()
        blocks = [4 * px + 2 * py + c for px, py in ((1 - x, y), (x, 1 - y), (1 - x, 1 - y))]
        return [_remote(refs[a].at[blk], refs[a].at[blk], send, recv, 3 * a + j, (x, y, 1 - c))
                for a in range(n) for j, blk in enumerate(blocks)]
    return build, 3 * n


def _pair_copies(n):
    def build(refs, send, recv):
        x, y, c = _position()
        return [_remote(refs[a].at[k, 1 - c], refs[n + a].at[k], send, recv, N_CHIP * a + k, (x, y, 1 - c))
                for a in range(n) for k in range(N_CHIP)]
    return build, N_CHIP * n


def _chip_copies(n):
    def build(refs, send, recv):
        x, y, c = _position()
        chips = ((1 - x, y), (x, 1 - y), (1 - x, 1 - y))
        return [_remote(refs[a].at[2 * px + py], refs[n + a].at[j], send, recv, 3 * a + j, (px, py, c))
                for a in range(n) for j, (px, py) in enumerate(chips)]
    return build, 3 * n


def _split_start(name, buffers, copies, after=None):
    build, n_copies = copies
    nb = len(buffers)
    extra = [] if after is None else [after]

    def body(*refs):
        at = nb + len(extra)
        for cp in build(refs[:nb], refs[at], refs[at + 1]):
            cp.start()
        token = refs[at + 2 + nb]
        token[...] = jnp.zeros_like(token)

    outs = pl.pallas_call(
        body, name=name,
        out_shape=[pltpu.SemaphoreType.DMA((n_copies,)), pltpu.SemaphoreType.DMA((n_copies,))]
        + [pltpu.HBM(b.shape, b.dtype) for b in buffers] + [jax.ShapeDtypeStruct((8, BLK), F32)],
        in_specs=[HBM] * nb + [ANY] * len(extra),
        out_specs=[SEM, SEM] + [HBM] * nb + [pl.BlockSpec(memory_space=pltpu.VMEM)],
        input_output_aliases={i: 2 + i for i in range(nb)},
        compiler_params=pltpu.CompilerParams(has_side_effects=DATAFLOW),
    )(*[pltpu.with_memory_space_constraint(b, pltpu.HBM) for b in buffers], *extra)
    return (outs[0], outs[1]), list(outs[2:2 + nb]), outs[-1]


def _split_wait(name, buffers, sems, copies, after, then=None):
    build, _ = copies
    nb = len(buffers)
    n_next = 0 if then is None else then[1]
    first_out = nb + 3

    def body(*refs):
        for cp in build(refs[:nb], refs[nb], refs[nb + 1]):
            cp.wait()
        if then is not None:
            for cp in then[0](refs[:nb], refs[first_out], refs[first_out + 1]):
                cp.start()

    sem_shapes = [] if then is None else [pltpu.SemaphoreType.DMA((n_next,))] * 2
    outs = pl.pallas_call(
        body, name=name,
        out_shape=sem_shapes + [pltpu.HBM(b.shape, b.dtype) for b in buffers],
        in_specs=[HBM] * nb + [SEM, SEM, ANY],
        out_specs=[SEM] * len(sem_shapes) + [HBM] * nb,
        input_output_aliases={i: len(sem_shapes) + i for i in range(nb)},
        compiler_params=pltpu.CompilerParams(has_side_effects=DATAFLOW),
    )(*buffers, sems[0], sems[1], after)
    if then is None:
        return list(outs)
    return (outs[0], outs[1]), list(outs[2:])


def _pair_sum(name, partial, received, core):
    _, _, rows, cols = partial.shape
    tr = _tile(rows, max(8, STREAM_BLOCK_BYTES // (2 * cols)))

    def body(core_ref, p_ref, r_ref, o_ref):
        o_ref[...] = (p_ref[...].astype(F32) + r_ref[...].astype(F32)).astype(o_ref.dtype)

    return pl.pallas_call(
        body, name=name, out_shape=jax.ShapeDtypeStruct((N_CHIP, rows, cols), BF16),
        grid_spec=pltpu.PrefetchScalarGridSpec(
            num_scalar_prefetch=1, grid=(N_CHIP, rows // tr),
            in_specs=[pl.BlockSpec((None, None, tr, cols), lambda k, i, core: (k, core[0], i, 0)),
                      pl.BlockSpec((None, tr, cols), lambda k, i, core: (k, i, 0))],
            out_specs=pl.BlockSpec((None, tr, cols), lambda k, i, core: (k, i, 0))),
        compiler_params=_params("parallel", "parallel"),
    )(core, partial, received)


def _mm(name, a, b, out_shape, out_dtypes, grid, a_spec, b_spec, o_spec, ca, cb, k_steps=1, after=None):
    n_out = len(out_dtypes)
    extra = [] if after is None else [after]

    def body(a_ref, b_ref, *rest):
        rest = rest[len(extra):]
        o_refs = rest[:n_out]
        part = _dot(a_ref[...], b_ref[...], ca, cb)
        if k_steps == 1:
            for o in o_refs:
                o[...] = part.astype(o.dtype)
            return
        acc = rest[n_out]
        k = pl.program_id(len(grid) - 1)

        @pl.when(k == 0)
        def _():
            acc[...] = part

        @pl.when(k > 0)
        def _():
            acc[...] += part

        @pl.when(k == k_steps - 1)
        def _():
            for o in o_refs:
                o[...] = acc[...].astype(o.dtype)

    o_block = tuple(d for d in o_spec.block_shape if d is not None)
    sem = ("parallel",) * (len(grid) - 1) + (("arbitrary",) if k_steps > 1 else ("parallel",))
    out = pl.pallas_call(
        body, name=name, grid=grid,
        out_shape=[jax.ShapeDtypeStruct(out_shape, d) for d in out_dtypes],
        in_specs=[a_spec, b_spec] + [ANY] * len(extra), out_specs=[o_spec] * n_out,
        scratch_shapes=[pltpu.VMEM(o_block, F32)] if k_steps > 1 else [],
        compiler_params=_params(*sem),
    )(a, b, *extra)
    return out if n_out > 1 else out[0]


def _mm_nn_cs(name, a, w, out_dtypes=(F32,), tm=512, **kw):
    m, k = a.shape
    _, _, ns = w.shape
    tm = _tile(m, tm)
    return _mm(name, a, w, (m, N_DEV * ns), out_dtypes, (m // tm, N_DEV),
               pl.BlockSpec((tm, k), lambda i, j: (i, 0)),
               pl.BlockSpec((None, k, ns), lambda i, j: (j, 0, 0)),
               pl.BlockSpec((tm, ns), lambda i, j: (i, j)), 1, 0, **kw)


def _mm_nn(name, a, w, out_dtypes=(F32,), tm=512, tn=512, **kw):
    m, k = a.shape
    _, n = w.shape
    tm, tn = _tile(m, tm), _tile(n, tn, 128)
    return _mm(name, a, w, (m, n), out_dtypes, (m // tm, n // tn),
               pl.BlockSpec((tm, k), lambda i, j: (i, 0)),
               pl.BlockSpec((k, tn), lambda i, j: (0, j)),
               pl.BlockSpec((tm, tn), lambda i, j: (i, j)), 1, 0, **kw)


def _mm_nt_cs(name, a, w, tm=512, per_step=1, after=None):
    _, k, ns = w.shape
    parts = list(a) if isinstance(a, (tuple, list)) else [a]
    m = parts[0].shape[0]
    tm = _tile(m, tm)
    steps = N_DEV // per_step
    per_part = steps // len(parts)
    extra = [] if after is None else [after]

    def body(*refs):
        a_refs, w_ref, o_ref = refs[:len(parts)], refs[len(parts)], refs[len(parts) + 1 + len(extra)]
        j = pl.program_id(1)

        def contract(a_ref):
            part = _dot(a_ref[:, 0:ns], w_ref[0], 1, 1)
            for u in range(1, per_step):
                part = part + _dot(a_ref[:, u * ns:(u + 1) * ns], w_ref[u], 1, 1)

            @pl.when(j == 0)
            def _():
                o_ref[...] = part

            @pl.when(j > 0)
            def _():
                o_ref[...] += part

        for n, a_ref in enumerate(a_refs):
            pl.when(jnp.logical_and(j >= n * per_part, j < (n + 1) * per_part))(functools.partial(contract, a_ref))

    a_specs = [pl.BlockSpec((tm, per_step * ns),
                            functools.partial(lambda i, j, n: (i, jnp.clip(j - n * per_part, 0, per_part - 1)), n=n))
               for n in range(len(parts))]
    return pl.pallas_call(
        body, name=name, grid=(m // tm, steps), out_shape=jax.ShapeDtypeStruct((m, k), F32),
        in_specs=a_specs + [pl.BlockSpec((per_step, k, ns), lambda i, j: (j, 0, 0))] + [ANY] * len(extra),
        out_specs=pl.BlockSpec((tm, k), lambda i, j: (i, 0)),
        compiler_params=_params("parallel", "arbitrary"),
    )(*parts, w, *extra)


def _mm_nt_cs_whole(name, a, w, out_dtype, tm=512):
    m, _ = a.shape
    _, k, ns = w.shape
    tm = _tile(m, tm)

    def body(a_ref, w_ref, o_ref):
        acc = _dot(a_ref[:, 0:ns], w_ref[0], 1, 1)
        for j in range(1, N_DEV):
            acc = acc + _dot(a_ref[:, j * ns:(j + 1) * ns], w_ref[j], 1, 1)
        o_ref[...] = acc.astype(o_ref.dtype)

    return pl.pallas_call(
        body, name=name, grid=(m // tm,), out_shape=jax.ShapeDtypeStruct((m, k), out_dtype),
        in_specs=[pl.BlockSpec((tm, N_DEV * ns), lambda i: (i, 0)),
                  pl.BlockSpec((N_DEV, k, ns), lambda i: (0, 0, 0))],
        out_specs=pl.BlockSpec((tm, k), lambda i: (i, 0)),
        compiler_params=_params("parallel"),
    )(a, w)


def _mm_nt(name, a, w, out_dtypes=(F32,), tm=512, tn=512, **kw):
    m, k = a.shape
    n, _ = w.shape
    tm, tn = _tile(m, tm), _tile(n, tn, 128)
    return _mm(name, a, w, (m, n), out_dtypes, (m // tm, n // tn),
               pl.BlockSpec((tm, k), lambda i, j: (i, 0)),
               pl.BlockSpec((tn, k), lambda i, j: (j, 0)),
               pl.BlockSpec((tm, tn), lambda i, j: (i, j)), 1, 1, **kw)


def _mm_tn_cs(name, a, g, tk=512, after=None):
    s, k = a.shape
    parts = list(g) if isinstance(g, (tuple, list)) else [g]
    per_part = N_DEV // len(parts)
    ns = parts[0].shape[1] // per_part
    tk = _tile(k, tk, 128)
    extra = [] if after is None else [after]

    def body(a_ref, *rest):
        g_refs, o_ref = rest[:len(parts)], rest[len(parts) + len(extra)]
        j = pl.program_id(1)

        def shard(g_ref):
            o_ref[...] = _dot(a_ref[...], g_ref[...], 0, 0).astype(o_ref.dtype)

        for n, g_ref in enumerate(g_refs):
            pl.when(jnp.logical_and(j >= n * per_part, j < (n + 1) * per_part))(functools.partial(shard, g_ref))

    g_specs = [pl.BlockSpec((s, ns),
                            functools.partial(lambda i, j, n: (0, jnp.clip(j - n * per_part, 0, per_part - 1)), n=n))
               for n in range(len(parts))]
    return pl.pallas_call(
        body, name=name, grid=(k // tk, N_DEV), out_shape=jax.ShapeDtypeStruct((N_DEV, k, ns), BF16),
        in_specs=[pl.BlockSpec((s, tk), lambda i, j: (0, i))] + g_specs + [ANY] * len(extra),
        out_specs=pl.BlockSpec((None, tk, ns), lambda i, j: (j, i, 0)),
        compiler_params=_params("parallel", "arbitrary"),
    )(a, *parts, *extra)


def _mm_tn(name, a, g, tk=512, tn=1024, **kw):
    s, k = a.shape
    n = g.shape[1]
    tk, tn = _tile(k, tk, 128), _tile(n, tn, 128)
    return _mm(name, a, g, (k, n), (BF16,), (k // tk, n // tn),
               pl.BlockSpec((s, tk), lambda i, j: (0, i)),
               pl.BlockSpec((s, tn), lambda i, j: (0, j)),
               pl.BlockSpec((tk, tn), lambda i, j: (i, j)), 0, 0, **kw)


def _rowwise(name, fn, rows, bcast, outs, reds, tm, after=None):
    n_rows = rows[0][0].shape[0]
    tm = _tile(n_rows, tm)
    n_in, n_b, n_o, n_r = len(rows), len(bcast), len(outs), len(reds)
    extra = [] if after is None else [after]

    def body(*refs):
        ins = refs[:n_in + n_b]
        refs = refs[len(extra):]
        o_refs = refs[n_in + n_b:n_in + n_b + n_o]
        r_refs = refs[n_in + n_b + n_o:]
        vals = fn(*[r[...] for r in ins])
        for o, v in zip(o_refs, vals[:n_o]):
            o[...] = v.astype(o.dtype)
        first = pl.program_id(0) == 0
        for r, v in zip(r_refs, vals[n_o:]):
            @pl.when(first)
            def _(r=r, v=v):
                r[...] = v.astype(r.dtype)

            @pl.when(jnp.logical_not(first))
            def _(r=r, v=v):
                r[...] += v.astype(r.dtype)

    def whole(shape):
        zeros = (0,) * len(shape)
        return pl.BlockSpec(shape, lambda i: zeros)

    in_specs = [pl.BlockSpec((tm, w), functools.partial(lambda i, cb: (i, cb), cb=cb)) for _, w, cb in rows]
    in_specs += [whole(b.shape) for b in bcast] + [ANY] * len(extra)
    out_specs = [pl.BlockSpec((tm, w), lambda i: (i, 0)) for w, _ in outs]
    out_specs += [whole(s) for s, _ in reds]
    out_shape = [jax.ShapeDtypeStruct((n_rows, w), d) for w, d in outs]
    out_shape += [jax.ShapeDtypeStruct(s, d) for s, d in reds]
    return pl.pallas_call(
        body, name=name, grid=(n_rows // tm,), out_shape=out_shape, in_specs=in_specs,
        out_specs=out_specs, compiler_params=_params("arbitrary" if n_r else "parallel"),
    )(*[r[0] for r in rows], *bcast, *extra)


def _norm_fwd(name, x, g, after=None):
    d = x.shape[1]
    return _rowwise(name, lambda xt, gt: (_rmsnorm(xt, gt),), [(x, d, 0)], [g], [(d, BF16)], [], 256,
                    after=after)[0]


def _residual_norm_fwd(name, x, y, g_post, g_next):
    d = x.shape[1]

    def fn(xt, yt, gp, gn):
        new = xt + _rmsnorm(yt, gp)
        return new, _rmsnorm(new, gn)

    return _rowwise(name, fn, [(x, d, 0), (y, d, 0)], [g_post, g_next], [(d, F32), (d, BF16)], [], 256)


def _merge(z0, z1, z2, t0, t1, t2, b0, b1, b2):
    return (jax.nn.sigmoid(z0 + b0) * t0 + jax.nn.sigmoid(z1 + b1) * t1 + jax.nn.sigmoid(z2 + b2) * t2)


def _merge_fwd(name, zg, ts, b_gate):
    d = ts[0].shape[1]
    rows = [(zg, d, b) for b in range(3)] + [(t, d, 0) for t in ts]
    bias = [b_gate[:, b * d:(b + 1) * d] for b in range(3)]
    return _rowwise(name, lambda *v: (_merge(*v),), rows, bias, [(d, BF16)], [], 128)[0]


def _merge_bwd(name, dmerged, zg, ts, b_gate):
    d = ts[0].shape[1]
    rows = [(dmerged, d, 0)] + [(zg, d, b) for b in range(3)] + [(t, d, 0) for t in ts]
    bias = [b_gate[:, b * d:(b + 1) * d] for b in range(3)]

    def fn(dm, *v):
        _, vjp = jax.vjp(_merge, *v)
        dz0, dz1, dz2, dt0, dt1, dt2, db0, db1, db2 = vjp(dm)
        return (dt0, dt1, dt2, jnp.concatenate([dz0, dz1, dz2], axis=1),
                jnp.concatenate([db0, db1, db2], axis=1))

    return _rowwise(name, fn, rows, bias, [(d, BF16)] * 3 + [(3 * d, BF16)], [((1, 3 * d), F32)], 128)


def _loss_bwd(name, x, y, g_post, target):
    d = x.shape[1]

    def loss(xt, yt, gp, tt):
        err = xt + _rmsnorm(yt, gp) - tt
        return 0.5 * jnp.sum(jnp.mean(err * err, axis=-1))

    def fn(xt, yt, tt, gp):
        val, (dx, dy, dg) = jax.value_and_grad(loss, argnums=(0, 1, 2))(xt, yt, gp, tt)
        return dx, dy, jnp.full((1, BLK), val, F32), dg

    return _rowwise(name, fn, [(x, d, 0), (y, d, 0), (target, d, 0)], [g_post],
                    [(d, F32), (d, BF16)], [((1, BLK), F32), ((1, d), F32)], 256)


def _mid_bwd(name, dx_out, dh, x_mid, y, g_pre, g_post):
    d = dx_out.shape[1]

    def fn(dxo, dht, xm, yt, gpre, gpost):
        _, vjp_pre = jax.vjp(_rmsnorm, xm, gpre)
        dxm, dgpre = vjp_pre(dht)
        dxm = dxo + dxm
        _, vjp_post = jax.vjp(_rmsnorm, yt, gpost)
        dy, dgpost = vjp_post(dxm)
        return dxm, dy, dgpre, dgpost

    return _rowwise(name, fn, [(dx_out, d, 0), (dh, d, 0), (x_mid, d, 0), (y, d, 0)], [g_pre, g_post],
                    [(d, F32), (d, BF16)], [((1, d), F32), ((1, d), F32)], 128)


def _bottom_bwd(name, dx_mid, dh_a, dh_b, x, g_pre, y_prev=None, g_post_prev=None):
    d = dx_mid.shape[1]
    rows = [(dx_mid, d, 0), (dh_a, d, 0), (dh_b, d, 0), (x, d, 0)]
    if y_prev is None:
        def fn(dxm, da, db, xt, gpre):
            _, vjp_pre = jax.vjp(_rmsnorm, xt, gpre)
            dx, dgpre = vjp_pre(da + db)
            return dxm + dx, dgpre

        return _rowwise(name, fn, rows, [g_pre], [(d, F32)], [((1, d), F32)], 128)

    def fn2(dxm, da, db, xt, yt, gpre, gpost):
        _, vjp_pre = jax.vjp(_rmsnorm, xt, gpre)
        dx, dgpre = vjp_pre(da + db)
        dx = dxm + dx
        _, vjp_post = jax.vjp(_rmsnorm, yt, gpost)
        dy, dgpost = vjp_post(dx)
        return dx, dy, dgpre, dgpost

    return _rowwise(name, fn2, rows + [(y_prev, d, 0)], [g_pre, g_post_prev],
                    [(d, F32), (d, BF16)], [((1, d), F32), ((1, d), F32)], 128)


def _mem_norm_bwd(name, dmn, mem, g_mem):
    d = mem.shape[1]

    def fn(dt, mt, g):
        _, vjp = jax.vjp(_rmsnorm, mt, g)
        return (vjp(dt)[1],)

    return _rowwise(name, fn, [(dmn, d, 0), (mem, d, 0)], [g_mem], [], [((1, d), F32)], 256)[0]


def _split_dot(v, tri):
    hi = v.astype(BF16)
    lo = (v - hi.astype(F32)).astype(BF16)
    return (jnp.dot(hi, tri, preferred_element_type=F32) + jnp.dot(lo, tri, preferred_element_type=F32))


def _sb_scores(q, kb, strict=None):
    z = _dot(q, kb, 1, 1) * (BLK ** -0.5)
    soft = jnp.log(1.0 + jnp.exp(-jnp.abs(z)))
    log_beta = jnp.minimum(z, 0.0) - soft
    log_keep = jnp.minimum(-z, 0.0) - soft
    if strict is not None:
        log_keep = jnp.where(strict, log_keep, 0.0)
    return z, log_beta, log_keep


def _block_iotas():
    return (lax.broadcasted_iota(jnp.int32, (BLK, BLK), 0), lax.broadcasted_iota(jnp.int32, (BLK, BLK), 1))


def _sb_fwd(name, proj, heads):
    s = proj.shape[0]
    nq = s // BLK
    hb = _tile(heads, SB_HEADS_PER_STEP, 1)
    groups = heads // hb
    wide = hb * BLK

    def body(q_ref, k_ref, v_ref, o_ref, tot_ref, acc_ref):
        i = pl.program_id(1)
        row, col = _block_iotas()
        diag = col < row
        later = (row > col).astype(BF16)
        head_cols = [slice(hh * BLK, (hh + 1) * BLK) for hh in range(hb)]

        def tiles(off, runs, strict):
            scores = [_sb_scores(q_ref[:, c], k_ref[pl.ds(off, BLK), c], strict) for c in head_cols]
            suffixes = [_split_dot(sc[2], later) for sc in scores]
            new = []
            for hh, c in enumerate(head_cols):
                a = jnp.exp(scores[hh][1] + suffixes[hh] + runs[hh])
                if strict is not None:
                    a = jnp.where(strict, a, 0.0)
                acc_ref[:, c] += _dot(a, v_ref[pl.ds(off, BLK), c], 1, 0)
                new.append(runs[hh] + jnp.sum(scores[hh][2], axis=1, keepdims=True))
            return tuple(new)

        acc_ref[...] = jnp.zeros_like(acc_ref)
        runs = tiles(pl.multiple_of(i * BLK, BLK), (jnp.zeros((BLK, 1), F32),) * hb, diag)
        runs = lax.fori_loop(0, i, lambda t, r: tiles(pl.multiple_of((i - 1 - t) * BLK, BLK), r, None), runs)
        o_ref[...] = acc_ref[...].astype(o_ref.dtype)
        for hh in range(hb):
            tot_ref[:, hh:hh + 1] = runs[hh]

    return pl.pallas_call(
        body, name=name, grid=(groups, nq),
        out_shape=[jax.ShapeDtypeStruct((s, heads * BLK), BF16), jax.ShapeDtypeStruct((groups, s, hb), F32)],
        in_specs=[pl.BlockSpec((BLK, wide), lambda g, i: (i, g)),
                  pl.BlockSpec((s, wide), lambda g, i: (0, groups + g)),
                  pl.BlockSpec((s, wide), lambda g, i: (0, 2 * groups + g))],
        out_specs=[pl.BlockSpec((BLK, wide), lambda g, i: (i, g)),
                   pl.BlockSpec((None, BLK, hb), lambda g, i: (g, i, 0))],
        scratch_shapes=[pltpu.VMEM((BLK, wide), F32)],
        compiler_params=_params("parallel", "arbitrary"),
    )(proj, proj, proj)


def _sb_bwd(name, proj, do, tot, heads, after=None):
    s = proj.shape[0]
    nq = s // BLK
    groups, _, hb = tot.shape
    wide = hb * BLK

    extra = [] if after is None else [after]

    def body(q_ref, k_ref, v_ref, do_ref, tot_ref, *rest):
        dq_ref, dk_ref, dv_ref, dq_acc, dk_acc, dv_acc = rest[len(extra):]
        i = pl.program_id(1)
        row, col = _block_iotas()
        diag = col < row
        upto = (row <= col).astype(BF16)
        earlier = (row < col).astype(BF16)

        @pl.when(i == 0)
        def _():
            dk_acc[...] = jnp.zeros_like(dk_acc)
            dv_acc[...] = jnp.zeros_like(dv_acc)

        dq_acc[...] = jnp.zeros_like(dq_acc)

        head_cols = [slice(hh * BLK, (hh + 1) * BLK) for hh in range(hb)]

        def tiles(off, carry, strict):
            rows = pl.ds(off, BLK)
            scores = [_sb_scores(q_ref[:, c], k_ref[rows, c], strict) for c in head_cols]
            das = [_dot(do_ref[:, c], v_ref[rows, c], 1, 1) for c in head_cols]
            prefixes = [_split_dot(sc[2], upto) for sc in scores]
            dlas = []
            for hh, c in enumerate(head_cols):
                suffix = tot_ref[:, hh:hh + 1] - (prefixes[hh] + carry[2 * hh])
                a = jnp.exp(scores[hh][1] + suffix)
                if strict is not None:
                    a = jnp.where(strict, a, 0.0)
                dlas.append(das[hh] * a)
                dv_acc[rows, c] += _dot(a, do_ref[:, c], 0, 0)
            dkeeps = [_split_dot(dla, earlier) for dla in dlas]
            new = ()
            for hh, c in enumerate(head_cols):
                dkeep = dkeeps[hh] + carry[2 * hh + 1]
                if strict is not None:
                    dkeep = jnp.where(strict, dkeep, 0.0)
                sig = jnp.exp(scores[hh][1])
                dz = ((dlas[hh] * (1.0 - sig) - dkeep * sig) * (BLK ** -0.5)).astype(BF16)
                dk_acc[rows, c] += _dot(dz, q_ref[:, c], 0, 0)
                dq_acc[:, c] += _dot(dz, k_ref[rows, c], 1, 0)
                new += (carry[2 * hh] + jnp.sum(scores[hh][2], axis=1, keepdims=True),
                        carry[2 * hh + 1] + jnp.sum(dlas[hh], axis=1, keepdims=True))
            return new

        carry = lax.fori_loop(0, i, lambda jj, cr: tiles(pl.multiple_of(jj * BLK, BLK), cr, None),
                              (jnp.zeros((BLK, 1), F32),) * (2 * hb))
        tiles(pl.multiple_of(i * BLK, BLK), carry, diag)
        dq_ref[...] = dq_acc[...].astype(dq_ref.dtype)

        @pl.when(i == nq - 1)
        def _():
            dk_ref[...] = dk_acc[...].astype(dk_ref.dtype)
            dv_ref[...] = dv_acc[...].astype(dv_ref.dtype)

    shape = jax.ShapeDtypeStruct((s, heads * BLK), BF16)
    return pl.pallas_call(
        body, name=name, grid=(groups, nq), out_shape=[shape, shape, shape],
        in_specs=[pl.BlockSpec((BLK, wide), lambda g, i: (i, g)),
                  pl.BlockSpec((s, wide), lambda g, i: (0, groups + g)),
                  pl.BlockSpec((s, wide), lambda g, i: (0, 2 * groups + g)),
                  pl.BlockSpec((BLK, wide), lambda g, i: (i, g)),
                  pl.BlockSpec((None, BLK, hb), lambda g, i: (g, i, 0))] + [ANY] * len(extra),
        out_specs=[pl.BlockSpec((BLK, wide), lambda g, i: (i, g)),
                   pl.BlockSpec((s, wide), lambda g, i: (0, g)),
                   pl.BlockSpec((s, wide), lambda g, i: (0, g))],
        scratch_shapes=[pltpu.VMEM((BLK, wide), F32), pltpu.VMEM((s, wide), F32), pltpu.VMEM((s, wide), F32)],
        compiler_params=_params("parallel", "arbitrary"),
    )(proj, proj, proj, do, tot, *extra)


def _xa_probs(q, k, hd):
    z = _dot(q, k, 1, 1) * (hd ** -0.5)
    z = z - jnp.max(z, axis=-1, keepdims=True)
    e = jnp.exp(z)
    return e / jnp.sum(e, axis=-1, keepdims=True)


def _xa_fwd(name, proj, mem_kv, width):
    s = proj.shape[0]
    m = mem_kv.shape[0]
    hd = width // XA_HEADS
    tq = _tile(s, 512)
    q_first = 5 * XA_HEADS

    def body(q_ref, k_ref, v_ref, o_ref):
        p = _xa_probs(q_ref[...], k_ref[...], hd)
        o_ref[...] = _dot(p, v_ref[...], 1, 0).astype(o_ref.dtype)

    return pl.pallas_call(
        body, name=name, grid=(XA_HEADS, s // tq), out_shape=jax.ShapeDtypeStruct((s, width), BF16),
        in_specs=[pl.BlockSpec((tq, hd), lambda h, i: (i, q_first + h)),
                  pl.BlockSpec((m, hd), lambda h, i: (0, h)),
                  pl.BlockSpec((m, hd), lambda h, i: (0, XA_HEADS + h))],
        out_specs=pl.BlockSpec((tq, hd), lambda h, i: (i, h)),
        compiler_params=_params("parallel", "parallel"),
    )(proj, mem_kv, mem_kv)


def _xa_bwd(name, proj, mem_kv, do, width):
    s = proj.shape[0]
    m = mem_kv.shape[0]
    hd = width // XA_HEADS
    tq = _tile(s, 512)
    nq = s // tq
    q_first = 5 * XA_HEADS

    def body(q_ref, k_ref, v_ref, do_ref, dq_ref, dk_ref, dv_ref, dk_acc, dv_acc):
        i = pl.program_id(1)

        @pl.when(i == 0)
        def _():
            dk_acc[...] = jnp.zeros_like(dk_acc)
            dv_acc[...] = jnp.zeros_like(dv_acc)

        q, k, v, dout = q_ref[...], k_ref[...], v_ref[...], do_ref[...]
        p = _xa_probs(q, k, hd)
        dp = _dot(dout, v, 1, 1)
        dv_acc[...] += _dot(p, dout, 0, 0)
        dz = ((p * (dp - jnp.sum(dp * p, axis=-1, keepdims=True))) * (hd ** -0.5)).astype(BF16)
        dq_ref[...] = _dot(dz, k, 1, 0).astype(dq_ref.dtype)
        dk_acc[...] += _dot(dz, q, 0, 0)

        @pl.when(i == nq - 1)
        def _():
            dk_ref[...] = dk_acc[...].astype(dk_ref.dtype)
            dv_ref[...] = dv_acc[...].astype(dv_ref.dtype)

    dq, dk, dv = pl.pallas_call(
        body, name=name, grid=(XA_HEADS, nq),
        out_shape=[jax.ShapeDtypeStruct((s, width), BF16), jax.ShapeDtypeStruct((m, width), BF16),
                   jax.ShapeDtypeStruct((m, width), BF16)],
        in_specs=[pl.BlockSpec((tq, hd), lambda h, i: (i, q_first + h)),
                  pl.BlockSpec((m, hd), lambda h, i: (0, h)),
                  pl.BlockSpec((m, hd), lambda h, i: (0, XA_HEADS + h)),
                  pl.BlockSpec((tq, hd), lambda h, i: (i, h))],
        out_specs=[pl.BlockSpec((tq, hd), lambda h, i: (i, h)),
                   pl.BlockSpec((m, hd), lambda h, i: (0, h)),
                   pl.BlockSpec((m, hd), lambda h, i: (0, h))],
        scratch_shapes=[pltpu.VMEM((m, hd), F32), pltpu.VMEM((m, hd), F32)],
        compiler_params=_params("parallel", "arbitrary"),
    )(proj, mem_kv, mem_kv, do)
    return dq, dk, dv


def _gm_pointwise(u_in, v_in, g_vnorm):
    return jax.nn.gelu(u_in), _rmsnorm(jax.nn.gelu(v_in), g_vnorm)


def _gm_mask():
    row = lax.broadcasted_iota(jnp.int32, (BLK, BLK), 0)
    col = lax.broadcasted_iota(jnp.int32, (BLK, BLK), 1)
    return (col // CHUNK) <= (row // CHUNK)


def _gm_fwd(name, proj, g_vnorm, w_s, b_s, width):
    s = proj.shape[0]
    groups = width // BLK

    def body(u_ref, v_ref, g_ref, w_ref, b_ref, o_ref):
        u, vn = _gm_pointwise(u_ref[...], v_ref[...], g_ref[...])
        vn = vn.astype(BF16)
        mask = _gm_mask()
        for g in range(groups):
            cols = slice(g * BLK, (g + 1) * BLK)
            w = jnp.where(mask, w_ref[g], 0.0)
            mixed = _dot(w, vn[:, cols], 1, 0) + b_ref[g]
            o_ref[:, cols] = (u[:, cols] * mixed).astype(o_ref.dtype)

    return pl.pallas_call(
        body, name=name, grid=(s // BLK,), out_shape=jax.ShapeDtypeStruct((s, width), BF16),
        in_specs=[pl.BlockSpec((BLK, width), lambda c: (c, 3)), pl.BlockSpec((BLK, width), lambda c: (c, 4)),
                  pl.BlockSpec((1, width), lambda c: (0, 0)),
                  pl.BlockSpec((groups, BLK, BLK), lambda c: (0, 0, 0)),
                  pl.BlockSpec((groups, BLK, 1), lambda c: (0, 0, 0))],
        out_specs=pl.BlockSpec((BLK, width), lambda c: (c, 0)),
        compiler_params=_params("parallel"),
    )(proj, proj, g_vnorm, w_s, b_s)


def _gm_bwd(name, proj, do, g_vnorm, w_s, b_s, width):
    s = proj.shape[0]
    groups = width // BLK

    def body(u_ref, v_ref, do_ref, g_ref, w_ref, b_ref, du_ref, dv_ref, dw_ref, db_ref, dg_ref, dvn_buf):
        first = pl.program_id(0) == 0
        (u, vn), vjp = jax.vjp(_gm_pointwise, u_ref[...], v_ref[...], g_ref[...])
        vn16 = vn.astype(BF16)
        dout = do_ref[...]
        mask = _gm_mask()
        du_parts = []
        for g in range(groups):
            cols = slice(g * BLK, (g + 1) * BLK)
            w = jnp.where(mask, w_ref[g], 0.0)
            mixed = _dot(w, vn16[:, cols], 1, 0) + b_ref[g]
            du_parts.append(dout[:, cols] * mixed)
            dmixed = dout[:, cols] * u[:, cols]
            dw = jnp.where(mask, _dot(dmixed, vn16[:, cols], 1, 1), 0.0)
            db = jnp.sum(dmixed, axis=1, keepdims=True)
            dvn_buf[:, cols] = _dot(w, dmixed, 0, 0)

            @pl.when(first)
            def _(g=g, dw=dw, db=db):
                dw_ref[g] = dw
                db_ref[g] = db

            @pl.when(jnp.logical_not(first))
            def _(g=g, dw=dw, db=db):
                dw_ref[g] += dw
                db_ref[g] += db

        du_in, dv_in, dg = vjp((jnp.concatenate(du_parts, axis=1), dvn_buf[...]))
        du_ref[...] = du_in.astype(du_ref.dtype)
        dv_ref[...] = dv_in.astype(dv_ref.dtype)

        @pl.when(first)
        def _():
            dg_ref[...] = dg

        @pl.when(jnp.logical_not(first))
        def _():
            dg_ref[...] += dg

    shape = jax.ShapeDtypeStruct((s, width), BF16)
    return pl.pallas_call(
        body, name=name, grid=(s // BLK,),
        out_shape=[shape, shape, jax.ShapeDtypeStruct((groups, BLK, BLK), F32),
                   jax.ShapeDtypeStruct((groups, BLK, 1), F32), jax.ShapeDtypeStruct((1, width), F32)],
        in_specs=[pl.BlockSpec((BLK, width), lambda c: (c, 3)), pl.BlockSpec((BLK, width), lambda c: (c, 4)),
                  pl.BlockSpec((BLK, width), lambda c: (c, 0)),
                  pl.BlockSpec((1, width), lambda c: (0, 0)),
                  pl.BlockSpec((groups, BLK, BLK), lambda c: (0, 0, 0)),
                  pl.BlockSpec((groups, BLK, 1), lambda c: (0, 0, 0))],
        out_specs=[pl.BlockSpec((BLK, width), lambda c: (c, 0)), pl.BlockSpec((BLK, width), lambda c: (c, 0)),
                   pl.BlockSpec((groups, BLK, BLK), lambda c: (0, 0, 0)),
                   pl.BlockSpec((groups, BLK, 1), lambda c: (0, 0, 0)),
                   pl.BlockSpec((1, width), lambda c: (0, 0))],
        scratch_shapes=[pltpu.VMEM((BLK, width), F32)],
        compiler_params=_params("arbitrary"),
    )(proj, proj, do, g_vnorm, w_s, b_s)


def _shift_down(v, k):
    row = lax.broadcasted_iota(jnp.int32, v.shape, 0)
    return jnp.where(row >= k, pltpu.roll(v, k, axis=0), 0.0)


def _shift_up(v, k):
    n = v.shape[0]
    row = lax.broadcasted_iota(jnp.int32, v.shape, 0)
    return jnp.where(row < n - k, pltpu.roll(v, n - k, axis=0), 0.0)


def _conv(gate, w, b):
    return b + w[0:1] * _shift_down(gate, 2) + w[1:2] * _shift_down(gate, 1) + w[2:3] * gate


GELU_C0 = 0.7978845608028654
GELU_C1 = 0.044715


def _gelu_and_slope(x):
    x2 = x * x
    t = jnp.tanh(x * (GELU_C0 + (GELU_C0 * GELU_C1) * x2))
    half = 0.5 * (1.0 + t)
    slope = half + (0.5 * x) * (1.0 - t * t) * (GELU_C0 + (3.0 * GELU_C0 * GELU_C1) * x2)
    return x * half, slope


def _conv_fwd(name, up, conv_w, conv_b):
    s, f2 = up.shape
    f = f2 // 2
    tc = _tile(f, 256, 128)
    nf = f // tc

    def body(g_ref, v_ref, w_ref, b_ref, o_ref):
        o_ref[...] = (jax.nn.gelu(_conv(g_ref[...], w_ref[...], b_ref[...])) * v_ref[...]).astype(o_ref.dtype)

    return pl.pallas_call(
        body, name=name, grid=(nf,), out_shape=jax.ShapeDtypeStruct((s, f), BF16),
        in_specs=[pl.BlockSpec((s, tc), lambda j: (0, j)), pl.BlockSpec((s, tc), lambda j: (0, nf + j)),
                  pl.BlockSpec((3, tc), lambda j: (0, j)), pl.BlockSpec((1, tc), lambda j: (0, j))],
        out_specs=pl.BlockSpec((s, tc), lambda j: (0, j)),
        compiler_params=_params("parallel"),
    )(up, up, conv_w, conv_b)


def _conv_bwd(name, up, da, conv_w, conv_b):
    s, f2 = up.shape
    f = f2 // 2
    tc = _tile(f, 256, 128)
    nf = f // tc

    def body(g_ref, v_ref, da_ref, w_ref, b_ref, dg_ref, dv_ref, dw_ref, db_ref):
        gate, val, dact, w = g_ref[...], v_ref[...], da_ref[...], w_ref[...]
        act, slope = _gelu_and_slope(_conv(gate, w, b_ref[...]))
        dv_ref[...] = (dact * act).astype(dv_ref.dtype)
        dconv = dact * val * slope
        dg_ref[...] = (w[2:3] * dconv + w[1:2] * _shift_up(dconv, 1) + w[0:1] * _shift_up(dconv, 2)
                       ).astype(dg_ref.dtype)
        dw_ref[0:1, :] = jnp.sum(dconv * _shift_down(gate, 2), axis=0, keepdims=True)
        dw_ref[1:2, :] = jnp.sum(dconv * _shift_down(gate, 1), axis=0, keepdims=True)
        dw_ref[2:3, :] = jnp.sum(dconv * gate, axis=0, keepdims=True)
        db_ref[...] = jnp.sum(dconv, axis=0, keepdims=True)

    shape = jax.ShapeDtypeStruct((s, f), BF16)
    return pl.pallas_call(
        body, name=name, grid=(nf,),
        out_shape=[shape, shape, jax.ShapeDtypeStruct((3, f), F32), jax.ShapeDtypeStruct((1, f), F32)],
        in_specs=[pl.BlockSpec((s, tc), lambda j: (0, j)), pl.BlockSpec((s, tc), lambda j: (0, nf + j)),
                  pl.BlockSpec((s, tc), lambda j: (0, j)),
                  pl.BlockSpec((3, tc), lambda j: (0, j)), pl.BlockSpec((1, tc), lambda j: (0, j))],
        out_specs=[pl.BlockSpec((s, tc), lambda j: (0, j)), pl.BlockSpec((s, tc), lambda j: (0, j)),
                   pl.BlockSpec((3, tc), lambda j: (0, j)), pl.BlockSpec((1, tc), lambda j: (0, j))],
        compiler_params=_params("parallel"),
    )(up, up, da, conv_w, conv_b)


def _adamw(w, g, m, v):
    m = ADAM_B1 * m + (1.0 - ADAM_B1) * g
    v = ADAM_B2 * v + (1.0 - ADAM_B2) * jnp.square(g)
    m_hat = m / (1.0 - ADAM_B1 ** ADAM_STEP)
    v_hat = v / (1.0 - ADAM_B2 ** ADAM_STEP)
    delta = -ADAM_LR * (m_hat / (jnp.sqrt(v_hat) + ADAM_EPS) + ADAM_WD * w)
    return delta, m, v


def _update_shard(name, layer, w, m, v, own, received, chip, previous=None):
    _, rows, cols = w.shape
    tr = _tile(rows, 256)

    def body(chip_ref, w_ref, m_ref, v_ref, own_ref, rec_ref, *rest):
        g_ref, d_ref, nm_ref, nv_ref = rest[-4:]
        g = (own_ref[...].astype(F32) + rec_ref[0].astype(F32) + rec_ref[1].astype(F32)
             + rec_ref[2].astype(F32))
        delta, new_m, new_v = _adamw(w_ref[...], g, m_ref[...], v_ref[...])
        g_ref[...] = g
        d_ref[...] = delta
        nm_ref[...] = new_m
        nv_ref[...] = new_v

    layer_spec = pl.BlockSpec((None, tr, cols), lambda i, chip_ref: (layer, i, 0))
    in_specs = [layer_spec] * 3 + [pl.BlockSpec((None, tr, cols), lambda i, chip_ref: (chip_ref[0], i, 0)),
                                   pl.BlockSpec((3, tr, cols), lambda i, chip_ref: (0, i, 0))]
    args = [chip, w, m, v, own, received]
    aliases = {}
    if previous is not None:
        in_specs += [ANY] * 4
        aliases = {len(args) + k: k for k in range(4)}
        args += list(previous)
    return pl.pallas_call(
        body, name=name, out_shape=[jax.ShapeDtypeStruct(w.shape, F32)] * 4,
        grid_spec=pltpu.PrefetchScalarGridSpec(
            num_scalar_prefetch=1, grid=(rows // tr,), in_specs=in_specs, out_specs=[layer_spec] * 4),
        input_output_aliases=aliases, compiler_params=_params("parallel"),
    )(*args)


def _sum_devices(name, gathered):
    _, rows, cols = gathered.shape

    def body(g_ref, o_ref):
        total = g_ref[0]
        for d in range(1, N_DEV):
            total = total + g_ref[d]
        o_ref[...] = total

    return pl.pallas_call(
        body, name=name, out_shape=jax.ShapeDtypeStruct((rows, cols), F32),
        in_specs=[pl.BlockSpec((N_DEV, rows, cols), lambda: (0, 0, 0))],
        out_specs=pl.BlockSpec((rows, cols), lambda: (0, 0)),
        compiler_params=pltpu.CompilerParams(vmem_limit_bytes=VMEM_LIMIT_V7X),
    )(gathered)


def _update_small(name, w, g, m, v):
    def body(w_ref, g_ref, m_ref, v_ref, d_ref, nm_ref, nv_ref):
        delta, new_m, new_v = _adamw(w_ref[...], g_ref[...], m_ref[...], v_ref[...])
        d_ref[...] = delta
        nm_ref[...] = new_m
        nv_ref[...] = new_v

    spec = pl.BlockSpec(w.shape, lambda: (0, 0))
    return pl.pallas_call(
        body, name=name, out_shape=[jax.ShapeDtypeStruct(w.shape, F32)] * 3,
        in_specs=[spec] * 4, out_specs=[spec] * 3,
        compiler_params=pltpu.CompilerParams(vmem_limit_bytes=VMEM_LIMIT_V7X),
    )(w, g, m, v)


def _pack(arrays):
    flat = jnp.concatenate([a.reshape(-1) for a in arrays])
    pad = (-flat.shape[0]) % (8 * BLK)
    return jnp.pad(flat, (0, pad)).reshape(-1, BLK)


def _unpack(packed, shapes):
    flat = packed.reshape(-1)
    out, at = [], 0
    for shape in shapes:
        size = 1
        for d in shape:
            size *= d
        out.append(flat[at:at + size].reshape(shape))
        at += size
    return out


SHARDED = ("w_in", "w_mem_kv", "w_gate", "w_br_sb", "w_br_gm", "w_br_xa", "w_out", "w_up", "w_down")
SMALL = ("g_mix_pre", "g_vnorm", "w_s", "b_s", "g_mem", "b_gate", "g_mix_post", "g_ffn_pre", "conv_w",
         "conv_b", "g_ffn_post")
WEIGHTS = ("g_mix_pre", "w_in", "g_vnorm", "w_s", "b_s", "g_mem", "w_mem_kv", "w_gate", "b_gate", "w_br_sb",
           "w_br_gm", "w_br_xa", "w_out", "g_mix_post", "g_ffn_pre", "w_up", "conv_w", "conv_b", "w_down",
           "g_ffn_post")


ROW_SHARDED = ("w_mem_kv", "w_out", "w_down")
GATHER_GROUPS = (("w_in",), ("w_mem_kv", "w_gate"), ("w_br_sb", "w_br_gm", "w_br_xa", "w_out"), ("w_up",),
                 ("w_down",))
REDUCE_GROUPS = (("w_down", "w_up"), ("w_out", "w_br_sb", "w_br_gm", "w_br_xa", "w_mem_kv", "w_gate"), ("w_in",))


def _cast_into_place(name, w, layer, me):
    _, rows, cols = w.shape
    tr = _tile(rows, max(8, STREAM_BLOCK_BYTES // (4 * cols)))

    def body(me_ref, w_ref, o_ref):
        o_ref[...] = w_ref[...].astype(o_ref.dtype)

    return pl.pallas_call(
        body, name=name, out_shape=jax.ShapeDtypeStruct((N_DEV, rows, cols), BF16),
        grid_spec=pltpu.PrefetchScalarGridSpec(
            num_scalar_prefetch=1, grid=(rows // tr,),
            in_specs=[pl.BlockSpec((None, tr, cols), lambda i, me_ref: (layer, i, 0))],
            out_specs=pl.BlockSpec((None, tr, cols), lambda i, me_ref: (me_ref[0], i, 0))),
        compiler_params=_params("parallel"),
    )(me, w)


class _Gather:
    def __init__(self, tag, names, places, after):
        self.tag, self.names, self.n = tag, names, len(places)
        self.sems, self.places, self.token = _split_start(
            f"gather_start_{tag}", places, _gather_first_copies(self.n), after)

    def relay(self, after):
        self.sems, self.places = _split_wait(
            f"gather_relay_{self.tag}", self.places, self.sems, _gather_first_copies(self.n), after,
            then=_gather_relay_copies(self.n))

    def finish(self, after):
        places = _split_wait(f"gather_finish_{self.tag}", self.places, self.sems,
                             _gather_relay_copies(self.n), after)
        full = dict(zip(self.names, places))
        for name in self.names:
            if name in ROW_SHARDED:
                full[name] = full[name].reshape(-1, full[name].shape[-1])
        return full


class _Reduce:
    def __init__(self, tag, names, partials):
        self.tag, self.names, self.n = tag, names, len(partials)
        blocks = [p.reshape((N_CHIP, 2, -1, p.shape[-1])) for p in partials]
        lands = [lax.empty((N_CHIP,) + b.shape[2:], b.dtype) for b in blocks]
        self.sems, self.buffers, self.token = _split_start(
            f"pair_start_{tag}", blocks + lands, _pair_copies(self.n))

    def middle(self, after, core):
        n = self.n
        got = _split_wait(f"pair_finish_{self.tag}", self.buffers, self.sems, _pair_copies(n), after)
        sums = [_pair_sum(f"pair_sum_{name}{self.tag}", got[a], got[n + a], core)
                for a, name in enumerate(self.names)]
        lands = [lax.empty((3,) + s.shape[1:], s.dtype) for s in sums]
        self.sems, self.buffers, self.token = _split_start(
            f"chip_start_{self.tag}", sums + lands, _chip_copies(n))

    def finish(self, after):
        n = self.n
        got = _split_wait(f"chip_finish_{self.tag}", self.buffers, self.sems, _chip_copies(n), after)
        return [(name, got[a], got[n + a]) for a, name in enumerate(self.names)]


def kernel(x, mem, g_mix_pre, w_in, g_vnorm, w_s, b_s, g_mem, w_mem_kv, w_gate, b_gate, w_br_sb, w_br_gm, w_br_xa, w_out, g_mix_post, g_ffn_pre, w_up, conv_w, conv_b, w_down, g_ffn_post, loss_target, m_g_mix_pre, m_w_in, m_g_vnorm, m_w_s, m_b_s, m_g_mem, m_w_mem_kv, m_w_gate, m_b_gate, m_w_br_sb, m_w_br_gm, m_w_br_xa, m_w_out, m_g_mix_post, m_g_ffn_pre, m_w_up, m_conv_w, m_conv_b, m_w_down, m_g_ffn_post, v_g_mix_pre, v_w_in, v_g_vnorm, v_w_s, v_b_s, v_g_mem, v_w_mem_kv, v_w_gate, v_b_gate, v_w_br_sb, v_w_br_gm, v_w_br_xa, v_w_out, v_g_mix_post, v_g_ffn_pre, v_w_up, v_conv_w, v_conv_b, v_w_down, v_g_ffn_post):
    p = dict(g_mix_pre=g_mix_pre, w_in=w_in, g_vnorm=g_vnorm, w_s=w_s, b_s=b_s, g_mem=g_mem, w_mem_kv=w_mem_kv,
             w_gate=w_gate, b_gate=b_gate, w_br_sb=w_br_sb, w_br_gm=w_br_gm, w_br_xa=w_br_xa, w_out=w_out,
             g_mix_post=g_mix_post, g_ffn_pre=g_ffn_pre, w_up=w_up, conv_w=conv_w, conv_b=conv_b, w_down=w_down,
             g_ffn_post=g_ffn_post)
    mom = dict(g_mix_pre=m_g_mix_pre, w_in=m_w_in, g_vnorm=m_g_vnorm, w_s=m_w_s, b_s=m_b_s, g_mem=m_g_mem,
               w_mem_kv=m_w_mem_kv, w_gate=m_w_gate, b_gate=m_b_gate, w_br_sb=m_w_br_sb, w_br_gm=m_w_br_gm,
               w_br_xa=m_w_br_xa, w_out=m_w_out, g_mix_post=m_g_mix_post, g_ffn_pre=m_g_ffn_pre, w_up=m_w_up,
               conv_w=m_conv_w, conv_b=m_conv_b, w_down=m_w_down, g_ffn_post=m_g_ffn_post)
    var = dict(g_mix_pre=v_g_mix_pre, w_in=v_w_in, g_vnorm=v_g_vnorm, w_s=v_w_s, b_s=v_b_s, g_mem=v_g_mem,
               w_mem_kv=v_w_mem_kv, w_gate=v_w_gate, b_gate=v_b_gate, w_br_sb=v_w_br_sb, w_br_gm=v_w_br_gm,
               w_br_xa=v_w_br_xa, w_out=v_w_out, g_mix_post=v_g_mix_post, g_ffn_pre=v_g_ffn_pre, w_up=v_w_up,
               conv_w=v_conv_w, conv_b=v_conv_b, w_down=v_w_down, g_ffn_post=v_g_ffn_post)
    depth = w_in.shape[0]
    x = x[0]
    mem = mem[0]
    target = loss_target[0]
    s, d = x.shape
    width = d // 2
    heads = width // BLK
    cx, cy, cc = lax.axis_index("x"), lax.axis_index("y"), lax.axis_index("c")
    core = cc.astype(jnp.int32).reshape(1)
    chip = (2 * cx + cy).astype(jnp.int32).reshape(1)
    me = 4 * cx + 2 * cy + cc
    me_index = me.astype(jnp.int32).reshape(1)

    conv_all = _all_gather("gather_conv_w", [conv_w])[0]
    conv_w_full = jnp.transpose(conv_all, (1, 2, 0, 3)).reshape(depth, 3, -1)
    gathers = {}
    token = conv_all
    for l in range(depth):
        for gi, names in enumerate(GATHER_GROUPS):
            places = [_cast_into_place(f"cast_{n}{l}", p[n], l, me_index) for n in names]
            gathers[l, gi] = _Gather(f"{l}{gi}", names, places, token)
            token = gathers[l, gi].token

    kept, gathered = [], []
    h = _norm_fwd("pre_norm0", x, g_mix_pre[0][None, :], after=token)
    gathers[0, 0].relay(h)
    for l in range(depth):
        row = lambda name: p[name][l][None, :]
        first, second, third, fourth, fifth = (gathers[l, gi] for gi in range(len(GATHER_GROUPS)))
        w = first.finish(h)
        proj, proj16 = _mm_nn_cs(f"proj{l}", h, w["w_in"], (F32, BF16), tm=1024)
        o_sb, sb_tot = _sb_fwd(f"sb_fwd{l}", proj16, heads)
        b_s3 = b_s[l][:, :, None]
        o_gm = _gm_fwd(f"gm_fwd{l}", proj, row("g_vnorm"), w_s[l], b_s3, width)
        second.relay(o_gm)
        mn = _norm_fwd(f"mem_norm{l}", mem, row("g_mem"))
        w.update(second.finish(mn))
        third.relay(mn)
        mem_kv = _mm_nn(f"mem_kv{l}", mn, w["w_mem_kv"], (BF16,))
        o_xa = _xa_fwd(f"xa_fwd{l}", proj16, mem_kv, width)
        zg = _mm_nn_cs(f"gates{l}", h, w["w_gate"], tm=1024)
        w.update(third.finish(zg))
        ts = [_mm_nn_cs(f"branch_{n}{l}", o, w[f"w_br_{n}"], (BF16,), tm=2048)
              for n, o in (("sb", o_sb), ("gm", o_gm), ("xa", o_xa))]
        merged = _merge_fwd(f"merge{l}", zg, ts, row("b_gate"))
        fourth.relay(merged)
        y1 = _mm_nn(f"out{l}", merged, w["w_out"], tm=1024, tn=1024)
        x1, h2 = _residual_norm_fwd(f"mix_post{l}", x, y1, row("g_mix_post"), row("g_ffn_pre"))
        w.update(fourth.finish(h2))
        up = _mm_nn_cs(f"up{l}", h2, w["w_up"], tm=1024)
        fifth.relay(up)
        act = _conv_fwd(f"conv_fwd{l}", up, conv_w_full[l], row("conv_b"))
        w.update(fifth.finish(act))
        if l + 1 < depth:
            gathers[l + 1, 0].relay(act)
        y2 = _mm_nn(f"down{l}", act, w["w_down"], tm=1024)
        kept.append(dict(x=x, h=h, proj=proj, proj16=proj16, zg=zg, mn=mn, mem_kv=mem_kv, o_sb=o_sb, o_gm=o_gm,
                         o_xa=o_xa, ts=ts, merged=merged, y1=y1, x1=x1, h2=h2, up=up, act=act, y2=y2, b_s=b_s3,
                         sb_tot=sb_tot))
        gathered.append(w)
        if l + 1 < depth:
            x, h = _residual_norm_fwd(f"ffn_post{l}", x1, y2, g_ffn_post[l][None, :], g_mix_pre[l + 1][None, :])

    top = kept[-1]
    dx, dy2, loss_part, dg_ffn_post = _loss_bwd("loss", top["x1"], top["y2"], g_ffn_post[depth - 1][None, :], target)
    loss = lax.psum(loss_part[0, 0], ("x", "y", "c"))
    small_grads = [dict() for _ in range(depth)]
    small_grads[depth - 1]["g_ffn_post"] = dg_ffn_post
    small_early = [(ll, n) for ll in range(depth) for n in SMALL if (ll, n) != (0, "g_mix_pre")]
    outs = {n: None for n in SHARDED}
    in_flight = []

    def reduce_finish(entry, after):
        layer = int(entry[0].tag[0])
        for name, own, received in entry[0].finish(after):
            outs[name] = _update_shard(f"update_{name}{layer}", layer, p[name], mom[name], var[name],
                                       own, received, chip, outs[name])
        in_flight.remove(entry)

    def reduce_start(new):
        for entry in in_flight:
            entry[2] += entry[1] == 2
        in_flight.append([new, 1, 0])
        for entry in [e for e in in_flight if e[2] >= 2]:
            reduce_finish(entry, new.token)
        return new.token

    def reduce_middle(after):
        for entry in [e for e in in_flight if e[1] == 1]:
            entry[0].middle(after, core)
            after = entry[0].token
            entry[1] = 2
        return after

    behind = None

    for l in reversed(range(depth)):
        k = kept[l]
        w = gathered[l]
        sg = small_grads[l]
        row = lambda name: p[name][l][None, :]
        partials = {}
        group = lambda gi: _Reduce(f"{l}{gi}", REDUCE_GROUPS[gi], [partials[n] for n in REDUCE_GROUPS[gi]])
        partials["w_down"] = _mm_tn(f"d_w_down{l}", k["act"], dy2, tn=2048, after=behind)
        dact = _mm_nt(f"d_act{l}", dy2, w["w_down"], tm=2048)
        dgate, dval, sg["conv_w"], sg["conv_b"] = _conv_bwd(f"conv_bwd{l}", k["up"], dact, conv_w_full[l],
                                                             row("conv_b"))
        dup = (dgate, dval)
        partials["w_up"] = _mm_tn_cs(f"d_w_up{l}", k["h2"], dup, tk=1024)
        behind = reduce_start(group(0))
        dh2 = _mm_nt_cs(f"d_h2{l}", dup, w["w_up"], per_step=2, after=behind)
        dx1, dy1, sg["g_ffn_pre"], sg["g_mix_post"] = _mid_bwd(f"mid_bwd{l}", dx, dh2, k["x1"], k["y1"],
                                                                row("g_ffn_pre"), row("g_mix_post"))
        behind = reduce_middle(dy1)
        partials["w_out"] = _mm_tn(f"d_w_out{l}", k["merged"], dy1, tk=1024, after=behind)
        dmerged = _mm_nt(f"d_merged{l}", dy1, w["w_out"], tm=1024, tn=1024)
        dt_sb, dt_gm, dt_xa, dzg, sg["b_gate"] = _merge_bwd(f"merge_bwd{l}", dmerged, k["zg"], k["ts"], row("b_gate"))
        do = {}
        for n, dt in (("sb", dt_sb), ("gm", dt_gm), ("xa", dt_xa)):
            partials[f"w_br_{n}"] = _mm_tn_cs(f"d_w_br_{n}{l}", k[f"o_{n}"], dt, tk=1024)
            do[n] = _mm_nt_cs_whole(f"d_o_{n}{l}", dt, w[f"w_br_{n}"], F32 if n == "gm" else BF16)
        dq_xa, dk_mem, dv_mem = _xa_bwd(f"xa_bwd{l}", k["proj16"], k["mem_kv"], do["xa"], width)
        dmem_kv = jnp.concatenate([dk_mem, dv_mem], axis=1)
        partials["w_mem_kv"] = _mm_tn(f"d_w_mem_kv{l}", k["mn"], dmem_kv)
        partials["w_gate"] = _mm_tn_cs(f"d_w_gate{l}", k["h"], dzg, tk=1024)
        behind = reduce_start(group(1))
        dh_gate = _mm_nt_cs(f"d_h_gate{l}", dzg, w["w_gate"], per_step=4, after=behind)
        behind = reduce_middle(dh_gate)
        dmn = _mm_nt(f"d_mn{l}", dmem_kv, w["w_mem_kv"], after=behind)
        sg["g_mem"] = _mem_norm_bwd(f"mem_norm_bwd{l}", dmn, mem, row("g_mem"))
        du, dvg, sg["w_s"], db_s, sg["g_vnorm"] = _gm_bwd(f"gm_bwd{l}", k["proj"], do["gm"], row("g_vnorm"),
                                                          p["w_s"][l], k["b_s"], width)
        sg["b_s"] = db_s[:, :, 0]
        small_token = None
        if l == 0:
            early_shapes = [small_grads[ll][n].shape for ll, n in small_early]
            packed = _pack([small_grads[ll][n] for ll, n in small_early])
            place = lax.dynamic_update_slice(lax.empty((N_DEV,) + packed.shape, F32), packed[None], (me, 0, 0))
            small_gather = _Gather("small", ("small",), [place], du)
            small_token = small_gather.token
        dq, dk, dv = _sb_bwd(f"sb_bwd{l}", k["proj16"], do["sb"], k["sb_tot"], heads, after=small_token)
        if l == 0:
            small_gather.relay(dq)
        dproj = jnp.concatenate([dq, dk, dv, du, dvg, dq_xa], axis=1)
        partials["w_in"] = _mm_tn_cs(f"d_w_in{l}", k["h"], dproj, tk=1024)
        behind = reduce_start(group(2))
        dh_in = _mm_nt_cs(f"d_h_in{l}", dproj, w["w_in"], per_step=4, after=behind)
        if l > 0:
            below = kept[l - 1]
            dx, dy2, sg["g_mix_pre"], small_grads[l - 1]["g_ffn_post"] = _bottom_bwd(
                f"bottom_bwd{l}", dx1, dh_in, dh_gate, k["x"], row("g_mix_pre"), below["y2"],
                p["g_ffn_post"][l - 1][None, :])
        else:
            dx, sg["g_mix_pre"] = _bottom_bwd(f"bottom_bwd{l}", dx1, dh_in, dh_gate, k["x"], row("g_mix_pre"))
        if l > 0:
            behind = reduce_middle(dx)

    late_gathered = _all_gather("gather_small", [_pack([small_grads[0]["g_mix_pre"]])])[0]
    behind = reduce_middle(late_gathered)
    for entry in list(in_flight):
        reduce_finish(entry, behind)

    early_total = _sum_devices("sum_small", small_gather.finish(behind)["small"])
    late_total = _sum_devices("sum_small_late", late_gathered)
    summed = dict(zip(small_early, _unpack(early_total, early_shapes)))
    summed[0, "g_mix_pre"] = _unpack(late_total, [small_grads[0]["g_mix_pre"].shape])[0]
    grads = {}
    for n in SMALL:
        g = jnp.stack([summed[l, n] for l in range(depth)]).reshape((depth,) + p[n].shape[1:]
                                                                    if n != "conv_w" else (depth, 3, -1))
        if n == "conv_w":
            per = conv_w.shape[2]
            g = lax.dynamic_slice_in_dim(g, me * per, per, axis=2)
        grads[n] = g
    delta, new_m, new_v = _update_small(
        "update_small", _pack([p[n] for n in SMALL]), _pack([grads[n] for n in SMALL]),
        _pack([mom[n] for n in SMALL]), _pack([var[n] for n in SMALL]))
    shapes = [p[n].shape for n in SMALL]
    for n, dl, nm, nv in zip(SMALL, _unpack(delta, shapes), _unpack(new_m, shapes), _unpack(new_v, shapes)):
        outs[n] = (grads[n], dl, nm, nv)

    result = [loss, dx[None]]
    for k in range(4):
        result += [outs[n][k] for n in WEIGHTS]
    return tuple(result)
```
